```python
import math
import jax, jax.numpy as jnp
from jax import lax
import numpy as np

D_MODEL = 1024
BATCH = 8
SEQ = 4096
DEPTH = 1

MIX_WIDTH = D_MODEL
HEAD_DIM = 64
RWKV_WIDTH = MIX_WIDTH // 2
ATTN_WIDTH = MIX_WIDTH - RWKV_WIDTH
RWKV_HEADS = RWKV_WIDTH // HEAD_DIM
ATTN_HEADS = ATTN_WIDTH // HEAD_DIM
DECAY_RANK = 64
ICLR_RANK = 64
DILATION_PATTERNS = ((128, 1), (512, 4), (2048, 16))
ATTN_BLOCK = 128
ROPE_THETA = 500000.0
ROPE_DIM = HEAD_DIM // 4
NORM_EPS = 1e-6
GN_EPS = 64e-5
SHIFT_WIDTH = 3 * RWKV_WIDTH + DECAY_RANK + ICLR_RANK
IN_WIDTH = SHIFT_WIDTH + RWKV_WIDTH + 4 * ATTN_WIDTH

kernel_name = "hymba_rwkv7_dilated_swa_layer"


def _rmsnorm(x, g):
    xf = x.astype(jnp.float32)
    y = xf * lax.rsqrt(jnp.mean(xf * xf, axis=-1, keepdims=True) + NORM_EPS)
    return (y * g.astype(jnp.float32)).astype(x.dtype)


def _token_shift(p):
    return jnp.pad(p, ((0, 0), (1, 0), (0, 0)))[:, :-1]


def _rwkv7_scan(r, w, k, v, kk, b):
    bsz, _, nh, n = r.shape

    def step(S, inp):
        r_t, w_t, k_t, v_t, kk_t, b_t = inp
        sa = jnp.einsum('bhvk,bhk->bhv', S, -kk_t)
        S = (S * w_t[:, :, None, :] + sa[..., None] * b_t[:, :, None, :]
             + v_t[..., None] * k_t[:, :, None, :])
        y = jnp.einsum('bhvk,bhk->bhv', S, r_t)
        return S, y

    xs = (jnp.moveaxis(r, 1, 0), jnp.moveaxis(w, 1, 0), jnp.moveaxis(k, 1, 0),
          jnp.moveaxis(v, 1, 0), jnp.moveaxis(kk, 1, 0), jnp.moveaxis(b, 1, 0))
    S0 = jnp.zeros((bsz, nh, n, n), jnp.float32)
    _, ys = lax.scan(step, S0, xs)
    return jnp.moveaxis(ys, 0, 1)


def _rwkv7_mixer(p_r, p_k, p_v, p_w, p_a, decay_base, decay_up, iclr_base,
                 iclr_up, key_norm_scale, key_iclr_mix, bonus, gn_gain, gn_bias):
    bsz, t, _ = p_r.shape
    f32 = jnp.float32
    heads = lambda a: a.astype(f32).reshape(bsz, t, RWKV_HEADS, HEAD_DIM)
    decay_logit = -jax.nn.softplus(-(decay_base.astype(f32)
                                     + jnp.tanh(p_w.astype(f32)) @ decay_up.astype(f32))) - 0.5
    w = jnp.exp(-jnp.exp(decay_logit))
    a = jax.nn.sigmoid(iclr_base.astype(f32) + p_a.astype(f32) @ iclr_up.astype(f32))
    kk = heads(p_k * key_norm_scale)
    kk = kk / jnp.maximum(jnp.sqrt(jnp.sum(kk * kk, axis=-1, keepdims=True)), 1e-12)
    k = p_k.astype(f32) * (1.0 + (a - 1.0) * key_iclr_mix.astype(f32))
    r_h, k_h, v_h, w_h, a_h = heads(p_r), heads(k), heads(p_v), heads(w), heads(a)
    y = _rwkv7_scan(r_h, w_h, k_h, v_h, kk, kk * a_h)
    mu = jnp.mean(y, axis=-1, keepdims=True)
    var = jnp.mean(jnp.square(y - mu), axis=-1, keepdims=True)
    y = (y - mu) * lax.rsqrt(var + GN_EPS)
    y = y.reshape(bsz, t, RWKV_WIDTH) * gn_gain.astype(f32) + gn_bias.astype(f32)
    bonus_term = jnp.sum(r_h * k_h * bonus.astype(f32), axis=-1, keepdims=True) * v_h
    return y + bonus_term.reshape(bsz, t, RWKV_WIDTH)


def _rope_partial(t, pos):
    half = ROPE_DIM // 2
    inv = ROPE_THETA ** (-jnp.arange(half, dtype=jnp.float32) * 2.0 / ROPE_DIM)
    ang = pos.astype(jnp.float32)[:, None] * inv[None, :]
    cos = jnp.cos(ang)[None, :, None, :].astype(t.dtype)
    sin = jnp.sin(ang)[None, :, None, :].astype(t.dtype)
    x1, x2 = t[..., :half], t[..., half:ROPE_DIM]
    return jnp.concatenate([x1 * cos - x2 * sin, x1 * sin + x2 * cos, t[..., ROPE_DIM:]], axis=-1)


def _banded_attention(q, k, v, span):
    g, L, n = q.shape
    blk = ATTN_BLOCK
    nb = -(-L // blk)
    lp = nb * blk
    qb = jnp.pad(q, ((0, 0), (0, lp - L), (0, 0))).reshape(g, nb, blk, n)
    kf = jnp.pad(k, ((0, 0), (blk, lp - L), (0, 0)))
    vf = jnp.pad(v, ((0, 0), (blk, lp - L), (0, 0)))
    kw = jnp.concatenate([kf[:, :lp].reshape(g, nb, blk, n), kf[:, blk:].reshape(g, nb, blk, n)], axis=2)
    vw = jnp.concatenate([vf[:, :lp].reshape(g, nb, blk, n), vf[:, blk:].reshape(g, nb, blk, n)], axis=2)
    s = jnp.einsum('gnqd,gnkd->gnqk', qb, kw).astype(jnp.float32) * (n ** -0.5)
    i = jnp.arange(blk)[:, None]
    j = jnp.arange(2 * blk)[None, :]
    dist = i + blk - j
    bidx = jnp.arange(nb)[:, None, None]
    mask = (dist >= 0) & (dist <= span) & (bidx * blk + j - blk >= 0)
    s = jnp.where(mask, s, -jnp.inf)
    m = jnp.max(s, axis=-1, keepdims=True)
    p = jnp.exp(s - m)
    l = jnp.sum(p, axis=-1, keepdims=True)
    o = jnp.einsum('gnqk,gnkd->gnqd', (p / l).astype(v.dtype), vw)
    lse = (m + jnp.log(l))[..., 0]
    return o.reshape(g, lp, n)[:, :L], lse.reshape(g, lp)[:, :L]


def _dilated_attention(q, k, v):
    bsz, t, nh, n = q.shape
    outs, lses = [], []
    for window, dil in DILATION_PATTERNS:
        L = t // dil
        split = lambda a: a.reshape(bsz, L, dil, nh, n).transpose(0, 2, 3, 1, 4).reshape(bsz * dil * nh, L, n)
        o, lse = _banded_attention(split(q), split(k), split(v), window // dil)
        outs.append(o.reshape(bsz, dil, nh, L, n).transpose(0, 3, 1, 2, 4).reshape(bsz, t, nh, n))
        lses.append(lse.reshape(bsz, dil, nh, L).transpose(0, 3, 1, 2).reshape(bsz, t, nh))
    wts = jax.nn.softmax(jnp.stack(lses, axis=0), axis=0)
    out = jnp.sum(wts[..., None] * jnp.stack(outs, axis=0).astype(jnp.float32), axis=0)
    return out.astype(q.dtype)


def setup_inputs(seed: int = 0) -> dict:
    key = jax.random.key(seed)
    ks = jax.random.split(key, 16)
    f32 = jnp.float32
    nrm = lambda kk, shape: jax.random.normal(kk, shape, f32)
    x = nrm(ks[0], (BATCH, SEQ, D_MODEL))
    norm_gain = 1.0 + 0.02 * nrm(ks[1], (DEPTH, D_MODEL))
    w_in = nrm(ks[2], (DEPTH, D_MODEL, IN_WIDTH)) * D_MODEL ** -0.5
    shift_mix = jax.random.uniform(ks[3], (DEPTH, SHIFT_WIDTH), f32)
    decay_base = jax.random.uniform(ks[4], (DEPTH, RWKV_WIDTH), f32, minval=-4.0, maxval=1.0)
    decay_up = nrm(ks[5], (DEPTH, DECAY_RANK, RWKV_WIDTH)) * 0.5 * DECAY_RANK ** -0.5
    iclr_base = 0.1 * nrm(ks[6], (DEPTH, RWKV_WIDTH))
    iclr_up = nrm(ks[7], (DEPTH, ICLR_RANK, RWKV_WIDTH)) * 0.5 * ICLR_RANK ** -0.5
    key_norm_scale = 0.85 + 0.05 * nrm(ks[8], (DEPTH, RWKV_WIDTH))
    key_iclr_mix = 1.0 + 0.05 * nrm(ks[9], (DEPTH, RWKV_WIDTH))
    bonus = 0.1 * nrm(ks[10], (DEPTH, RWKV_HEADS, HEAD_DIM))
    gn_gain = 1.0 + 0.02 * nrm(ks[11], (DEPTH, RWKV_WIDTH))
    gn_bias = 0.02 * nrm(ks[12], (DEPTH, RWKV_WIDTH))
    w_out = nrm(ks[13], (DEPTH, MIX_WIDTH, D_MODEL)) * MIX_WIDTH ** -0.5
    final_gain = 1.0 + 0.02 * nrm(ks[14], (D_MODEL,))
    return {"x": x, "norm_gain": norm_gain, "w_in": w_in, "shift_mix": shift_mix,
            "decay_base": decay_base, "decay_up": decay_up, "iclr_base": iclr_base,
            "iclr_up": iclr_up, "key_norm_scale": key_norm_scale, "key_iclr_mix": key_iclr_mix,
            "bonus": bonus, "gn_gain": gn_gain, "gn_bias": gn_bias, "w_out": w_out,
            "final_gain": final_gain}


def reference(x, norm_gain, w_in, shift_mix, decay_base, decay_up, iclr_base, iclr_up,
              key_norm_scale, key_iclr_mix, bonus, gn_gain, gn_bias, w_out, final_gain):
    bsz, t, _ = x.shape
    pos = jnp.arange(t, dtype=jnp.int32)
    c_a, c_b = RWKV_WIDTH, ATTN_WIDTH
    for layer in range(DEPTH):
        h = _rmsnorm(x, norm_gain[layer])
        p = h @ w_in[layer]
        ps = p[..., :SHIFT_WIDTH]
        ps = ps + (_token_shift(ps) - ps) * shift_mix[layer]
        p_r = ps[..., :c_a]
        p_k = ps[..., c_a:2 * c_a]
        p_v = ps[..., 2 * c_a:3 * c_a]
        p_w = ps[..., 3 * c_a:3 * c_a + DECAY_RANK]
        p_a = ps[..., 3 * c_a + DECAY_RANK:SHIFT_WIDTH]
        o0 = SHIFT_WIDTH
        z_a = p[..., o0:o0 + c_a]
        o1 = o0 + c_a
        q = p[..., o1:o1 + c_b].reshape(bsz, t, ATTN_HEADS, HEAD_DIM)
        k = p[..., o1 + c_b:o1 + 2 * c_b].reshape(bsz, t, ATTN_HEADS, HEAD_DIM)
        v = p[..., o1 + 2 * c_b:o1 + 3 * c_b].reshape(bsz, t, ATTN_HEADS, HEAD_DIM)
        z_b = p[..., o1 + 3 * c_b:o1 + 4 * c_b]
        y_a = _rwkv7_mixer(p_r, p_k, p_v, p_w, p_a, decay_base[layer], decay_up[layer],
                           iclr_base[layer], iclr_up[layer], key_norm_scale[layer],
                           key_iclr_mix[layer], bonus[layer], gn_gain[layer], gn_bias[layer])
        y_a = y_a.astype(x.dtype) * jax.nn.silu(z_a)
        y_b = _dilated_attention(_rope_partial(q, pos), _rope_partial(k, pos), v)
        y_b = y_b.reshape(bsz, t, c_b) * jax.nn.silu(z_b)
        x = x + jnp.concatenate([y_a, y_b], axis=-1) @ w_out[layer]
    return _rmsnorm(x, final_gain)
```

```python
import functools

import jax
import jax.numpy as jnp
from jax import lax
from jax.experimental import pallas as pl
from jax.experimental.pallas import tpu as pltpu

HEAD_DIM = 64
LANES = 128
SUBLANES = 8
DECAY_RANK = 64
ICLR_RANK = 64
DILATION_PATTERNS = ((128, 1), (512, 4), (2048, 16))
ATTN_BLOCK = 128
ROPE_THETA = 500000.0
ROPE_DIM = HEAD_DIM // 4
NORM_EPS = 1e-6
GN_EPS = 64e-5
CHUNK = 64
NEG_BIG = -1e30
VMEM_LIMIT = 56 * 1024 * 1024

F32 = jnp.float32
BF16 = jnp.bfloat16


def _dot(a, b, **kw):
    return jnp.dot(a, b, preferred_element_type=F32, **kw)


def _dot_nt(a, b):
    return lax.dot_general(a, b, (((1,), (1,)), ((), ())), preferred_element_type=F32)


def _dot_tn(a, b):
    return lax.dot_general(a, b, (((0,), (0,)), ((), ())), preferred_element_type=F32)


def _in_proj_kernel(x_ref, g_ref, w_ref, mix_ref, cos_ref, sa_ref, sb_ref, o_ref, carry_ref,
                    *, segs, shift_w, rope_lo, rope_hi, tiles_per_seq):
    i = pl.program_id(0)

    @pl.when(i == 0)
    def _():
        carry_ref[...] = jnp.zeros(carry_ref.shape, F32)

    x = x_ref[...]
    h = x * lax.rsqrt(jnp.mean(x * x, axis=-1, keepdims=True) + NORM_EPS) * g_ref[...]
    hb = h.astype(BF16)
    tm = x.shape[0]
    first = (i % tiles_per_seq) == 0
    row0 = lax.broadcasted_iota(jnp.int32, (tm, 1), 0) == 0
    for lo, hi in segs:
        p = _dot(hb, w_ref[:, lo:hi])
        if hi <= shift_w:
            old = jnp.where(first, 0.0, carry_ref[0:1, lo:hi])
            carry_ref[0:1, lo:hi] = p[tm - 1:tm, :]
            prev = jnp.where(row0, old, pltpu.roll(p, 1, axis=0))
            p = p + (prev - p) * mix_ref[:, lo:hi]
            o_ref[:, lo:hi] = p
        elif lo >= rope_lo and hi <= rope_hi:
            for j in range(lo, hi, LANES):
                t = p[:, j - lo:j - lo + LANES]
                t = (t * cos_ref[...] + pltpu.roll(t, LANES - ROPE_DIM // 2, axis=1) * sa_ref[...]
                     + pltpu.roll(t, ROPE_DIM // 2, axis=1) * sb_ref[...])
                o_ref[:, j:j + LANES] = t
        else:
            o_ref[:, lo:hi] = p


def _tri_inverse(nmat, is_h0):
    c = nmat.shape[0]
    ng = c // SUBLANES
    row = lax.broadcasted_iota(jnp.int32, (SUBLANES, LANES), 0)
    col = lax.broadcasted_iota(jnp.int32, (SUBLANES, LANES), 1) % HEAD_DIM
    xs = [jnp.where(col == row + g * SUBLANES, 1.0, 0.0).astype(F32) for g in range(ng)]
    ns = [nmat[g * SUBLANES:(g + 1) * SUBLANES, :] for g in range(ng)]
    for s in range(c - 1):
        gs, rs = divmod(s, SUBLANES)
        xrow = xs[gs][rs:rs + 1, :]
        for g in range((s + 1) // SUBLANES, ng):
            coef = jnp.where(is_h0, ns[g][:, s:s + 1], ns[g][:, HEAD_DIM + s:HEAD_DIM + s + 1])
            xs[g] = xs[g] - coef * xrow
    return jnp.concatenate(xs, axis=0)


def _rwkv_kernel(r_ref, k_ref, v_ref, wa_ref, z_ref, up_ref, dbase_ref, abase_ref, kns_ref,
                 kim_ref, bonus_ref, gng_ref, gnb_ref, o_ref):
    t_len = r_ref.shape[1]
    c = CHUNK
    lane = lax.broadcasted_iota(jnp.int32, (c, LANES), 1)
    is_h0 = lane < HEAD_DIM
    is_h0_row = is_h0[0:SUBLANES]
    trow = lax.broadcasted_iota(jnp.int32, (c, LANES), 0)
    scol = lane % HEAD_DIM
    strict = trow > scol
    incl = trow >= scol
    ltri = (lax.broadcasted_iota(jnp.int32, (c, c), 0)
            >= lax.broadcasted_iota(jnp.int32, (c, c), 1)).astype(F32)
    sq_r = lax.broadcasted_iota(jnp.int32, (LANES, LANES), 0) // HEAD_DIM
    sq_c = lax.broadcasted_iota(jnp.int32, (LANES, LANES), 1) // HEAD_DIM
    same_head = sq_r == sq_c

    dbase, abase = dbase_ref[...], abase_ref[...]
    kns, kim, bonus = kns_ref[...], kim_ref[...], bonus_ref[...]
    gng, gnb = gng_ref[...], gnb_ref[...]
    up = up_ref[...]

    def head_sum(x):
        s0 = jnp.sum(jnp.where(is_h0, x, 0.0), axis=1, keepdims=True)
        s1 = jnp.sum(jnp.where(is_h0, 0.0, x), axis=1, keepdims=True)
        return jnp.where(is_h0, s0, s1)

    def block_diag(x):
        return jnp.concatenate([jnp.where(is_h0, x, 0.0), jnp.where(is_h0, 0.0, x)], axis=0)

    def chunk(ci, state):
        sl = pl.ds(pl.multiple_of(ci * c, c), c)
        r = r_ref[0, sl, :]
        k = k_ref[0, sl, :]
        v = v_ref[0, sl, :]
        wa = wa_ref[0, sl, :]
        lin = _dot(jnp.where(is_h0, jnp.tanh(wa), wa), up, precision=lax.Precision.HIGHEST)
        logit = -jax.nn.softplus(-(dbase + lin[:, :LANES])) - 0.5
        logw = -jnp.exp(logit)
        a = jax.nn.sigmoid(abase + lin[:, LANES:])
        kk = k * kns
        kk = kk / jnp.maximum(jnp.sqrt(head_sum(kk * kk)), 1e-12)
        kmod = k * (1.0 + (a - 1.0) * kim)
        b = kk * a

        g = _dot(ltri, logw, precision=lax.Precision.HIGHEST)
        gl = g[c - 1:c, :]
        e_inv = jnp.exp(-g)
        kkd = kk * jnp.exp(g - logw)
        rd = r * jnp.exp(g)
        bi = b * e_inv
        ki = kmod * e_inv
        e_end = jnp.exp(gl - g)
        kdl = kmod * e_end
        bdl = b * e_end

        lhs = jnp.concatenate([kkd, rd], axis=0).astype(BF16)
        rhs = jnp.concatenate([block_diag(bi), block_diag(ki)], axis=0).astype(BF16)
        pair = _dot_nt(lhs, rhs)
        aab = jnp.where(strict, pair[:c, :LANES], 0.0)
        aak = jnp.where(strict, pair[:c, LANES:], 0.0)
        arb = jnp.where(incl, pair[c:, :LANES], 0.0)
        ark = jnp.where(incl, pair[c:, LANES:], 0.0)

        tinv = _tri_inverse(aab, is_h0_row)
        vbd = block_diag(v).astype(BF16)
        aakv = _dot(aak.astype(BF16), vbd)
        rhs2 = jnp.concatenate([block_diag(kkd), block_diag(aakv)], axis=1).astype(BF16)
        wu = _dot(tinv.astype(BF16), rhs2)
        sb = state.astype(BF16)
        ws = _dot_nt(jnp.concatenate([wu[:, :LANES], rd], axis=0).astype(BF16), sb)
        u = ws[:c] + wu[:, LANES:]
        ubd = block_diag(u).astype(BF16)
        y = (ws[c:] + _dot(ark.astype(BF16), vbd) - _dot(arb.astype(BF16), ubd))

        upd = _dot_tn(jnp.concatenate([v, -u], axis=0).astype(BF16),
                      jnp.concatenate([kdl, bdl], axis=0).astype(BF16))
        new_state = state * jnp.exp(gl) + jnp.where(same_head, upd, 0.0)

        mu = head_sum(y) * (1.0 / HEAD_DIM)
        d = y - mu
        var = head_sum(d * d) * (1.0 / HEAD_DIM)
        yn = d * lax.rsqrt(var + GN_EPS) * gng + gnb
        yn = yn + head_sum(r * kmod * bonus) * v
        z = z_ref[0, sl, :]
        o_ref[0, sl, :] = (yn * (z * jax.nn.sigmoid(z))).astype(o_ref.dtype)
        return new_state

    lax.fori_loop(0, t_len // c, chunk, jnp.zeros((LANES, LANES), F32))


def _attn_kernel(q_ref, k_ref, v_ref, z_ref, o_ref, kp_ref, vp_ref, acc_ref, m_ref, l_ref):
    t_len = q_ref.shape[1]
    blk = ATTN_BLOCK
    pad = kp_ref.shape[0] - t_len
    kp_ref[0:pad, :] = jnp.zeros((pad, LANES), F32)
    vp_ref[0:pad, :] = jnp.zeros((pad, LANES), F32)
    kp_ref[pad:, :] = k_ref[0]
    vp_ref[pad:, :] = v_ref[0]

    lane = lax.broadcasted_iota(jnp.int32, (blk, LANES), 1)
    is_h0 = lane < HEAD_DIM
    qi = lax.broadcasted_iota(jnp.int32, (blk, 2 * blk), 0)
    kj = lax.broadcasted_iota(jnp.int32, (blk, 2 * blk), 1)
    scale = HEAD_DIM ** -0.5

    for pi, (window, dil) in enumerate(DILATION_PATTERNS):
        span = window // dil
        nblk = t_len // (dil * blk)
        in_band = (qi + blk - kj >= 0) & (qi + blk - kj <= span)

        def body(it, _, pi=pi, dil=dil, nblk=nblk, in_band=in_band):
            res = it // nblk
            n = it % nblk
            q0 = res + dil * blk * n
            if dil == 1:
                q0 = pl.multiple_of(q0, blk)
                rows = pl.ds(q0, blk)
                krows = pl.ds(pl.multiple_of(pad + q0 - blk, blk), 2 * blk)
            else:
                rows = pl.ds(q0, blk, stride=dil)
                krows = pl.ds(pad + q0 - dil * blk, 2 * blk, stride=dil)
            q = q_ref[0, rows, :] * scale
            kw = kp_ref[krows, :].astype(BF16)
            vw = vp_ref[krows, :].astype(BF16)
            valid = in_band & (kj >= jnp.where(n == 0, blk, 0))
            accs, ms, ls = [], [], []
            for hsel in (is_h0, ~is_h0):
                s = _dot_nt(jnp.where(hsel, q, 0.0).astype(BF16), kw)
                s = jnp.where(valid, s, NEG_BIG)
                m = jnp.max(s, axis=1, keepdims=True)
                p = jnp.exp(s - m)
                ls.append(jnp.sum(p, axis=1, keepdims=True))
                ms.append(m)
                accs.append(_dot(p.astype(BF16), vw))
            acc = jnp.where(is_h0, accs[0], accs[1])
            m = jnp.where(is_h0, ms[0], ms[1])
            l = jnp.where(is_h0, ls[0], ls[1])
            if pi > 0:
                m_old = m_ref[rows, :]
                m_new = jnp.maximum(m_old, m)
                w_old = jnp.exp(m_old - m_new)
                w_cur = jnp.exp(m - m_new)
                acc = w_old * acc_ref[rows, :] + w_cur * acc
                l = w_old * l_ref[rows, :] + w_cur * l
                m = m_new
            acc_ref[rows, :] = acc
            m_ref[rows, :] = m
            l_ref[rows, :] = l
            return 0

        lax.fori_loop(0, t_len // blk, body, 0)

    def finish(it, _):
        rows = pl.ds(pl.multiple_of(it * blk, blk), blk)
        z = z_ref[0, rows, :]
        o_ref[0, rows, :] = (acc_ref[rows, :] / l_ref[rows, :]
                             * (z * jax.nn.sigmoid(z))).astype(o_ref.dtype)
        return 0

    lax.fori_loop(0, t_len // blk, finish, 0)


def _out_proj_kernel(ya_ref, yb_ref, wa_ref, wb_ref, x_ref, g_ref, o_ref):
    y = x_ref[...] + _dot(ya_ref[...], wa_ref[...]) + _dot(yb_ref[...], wb_ref[...])
    o_ref[...] = y * lax.rsqrt(jnp.mean(y * y, axis=-1, keepdims=True) + NORM_EPS) * g_ref[...]


def _rope_tables(t_len):
    half = ROPE_DIM // 2
    inv = ROPE_THETA ** (-jnp.arange(half, dtype=F32) * 2.0 / ROPE_DIM)
    ang = jnp.arange(t_len, dtype=jnp.int32).astype(F32)[:, None] * inv[None, :]
    cos, sin = jnp.cos(ang), jnp.sin(ang)
    ones = jnp.ones((t_len, HEAD_DIM - ROPE_DIM), F32)
    zeros = jnp.zeros((t_len, HEAD_DIM - ROPE_DIM), F32)
    zh = jnp.zeros((t_len, half), F32)
    tile = lambda a: jnp.tile(a, (1, LANES // HEAD_DIM))
    cos_t = tile(jnp.concatenate([cos, cos, ones], axis=1))
    sa_t = tile(jnp.concatenate([-sin, zh, zeros], axis=1))
    sb_t = tile(jnp.concatenate([zh, sin, zeros], axis=1))
    return cos_t, sa_t, sb_t


def kernel(x, norm_gain, w_in, shift_mix, decay_base, decay_up, iclr_base, iclr_up, key_norm_scale,
           key_iclr_mix, bonus, gn_gain, gn_bias, w_out, final_gain):
    bsz, t_len, d_model = x.shape
    depth = w_in.shape[0]
    c_a = decay_base.shape[1]
    n_hp = c_a // LANES
    shift_w = 3 * c_a + DECAY_RANK + ICLR_RANK
    in_w = w_in.shape[2]
    c_b = (in_w - shift_w - c_a) // 4
    assert c_a % LANES == 0 and c_b == c_a and t_len % (16 * ATTN_BLOCK) == 0
    o_za = shift_w
    o_q = o_za + c_a
    o_k, o_v, o_zb = o_q + c_b, o_q + 2 * c_b, o_q + 3 * c_b
    segs = ((0, c_a), (c_a, 2 * c_a), (2 * c_a, 3 * c_a), (3 * c_a, shift_w), (o_za, o_q),
            (o_q, o_k), (o_k, o_v), (o_v, o_zb), (o_zb, in_w))
    cos_t, sa_t, sb_t = _rope_tables(t_len)
    rows = bsz * t_len
    tm = 256
    tiles_per_seq = t_len // tm
    row2 = lambda a: a.reshape(1, -1).astype(F32)

    for layer in range(depth):
        p = pl.pallas_call(
            functools.partial(_in_proj_kernel, segs=segs, shift_w=shift_w, rope_lo=o_q,
                              rope_hi=o_v, tiles_per_seq=tiles_per_seq),
            grid=(rows // tm,),
            in_specs=[
                pl.BlockSpec((tm, d_model), lambda i: (i, 0)),
                pl.BlockSpec((1, d_model), lambda i: (0, 0)),
                pl.BlockSpec((d_model, in_w), lambda i: (0, 0)),
                pl.BlockSpec((1, shift_w), lambda i: (0, 0)),
                pl.BlockSpec((tm, LANES), lambda i: (i % tiles_per_seq, 0)),
                pl.BlockSpec((tm, LANES), lambda i: (i % tiles_per_seq, 0)),
                pl.BlockSpec((tm, LANES), lambda i: (i % tiles_per_seq, 0)),
            ],
            out_specs=pl.BlockSpec((tm, in_w), lambda i: (i, 0)),
            out_shape=jax.ShapeDtypeStruct((rows, in_w), F32),
            scratch_shapes=[pltpu.VMEM((SUBLANES, shift_w), F32)],
            compiler_params=pltpu.CompilerParams(dimension_semantics=("arbitrary",),
                                                 vmem_limit_bytes=VMEM_LIMIT),
            name="in_proj",
        )(x.reshape(rows, d_model), row2(norm_gain[layer]), w_in[layer].astype(BF16),
          row2(shift_mix[layer]), cos_t, sa_t, sb_t)
        p = p.reshape(bsz, t_len, in_w)

        zeros_up = jnp.zeros((DECAY_RANK, c_a), F32)
        up_full = jnp.concatenate(
            [jnp.concatenate([decay_up[layer].astype(F32), zeros_up], axis=0).reshape(
                DECAY_RANK + ICLR_RANK, n_hp, 1, LANES),
             jnp.concatenate([zeros_up, iclr_up[layer].astype(F32)], axis=0).reshape(
                 DECAY_RANK + ICLR_RANK, n_hp, 1, LANES)], axis=2).reshape(
                     DECAY_RANK + ICLR_RANK, n_hp * 2 * LANES)
        col = lambda off: (lambda b, h: (b, 0, off // LANES + h))
        par = lambda b, h: (0, h)
        seq_spec = lambda off: pl.BlockSpec((1, t_len, LANES), col(off))
        par_spec = pl.BlockSpec((1, LANES), par)
        ya = pl.pallas_call(
            _rwkv_kernel,
            grid=(bsz, n_hp),
            in_specs=[seq_spec(0), seq_spec(c_a), seq_spec(2 * c_a),
                      pl.BlockSpec((1, t_len, LANES), lambda b, h: (b, 0, 3 * c_a // LANES)),
                      seq_spec(o_za),
                      pl.BlockSpec((DECAY_RANK + ICLR_RANK, 2 * LANES), par),
                      par_spec, par_spec, par_spec, par_spec, par_spec, par_spec, par_spec],
            out_specs=pl.BlockSpec((1, t_len, LANES), lambda b, h: (b, 0, h)),
            out_shape=jax.ShapeDtypeStruct((bsz, t_len, c_a), BF16),
            compiler_params=pltpu.CompilerParams(dimension_semantics=("arbitrary", "arbitrary"),
                                                 vmem_limit_bytes=VMEM_LIMIT),
            name="rwkv",
        )(p, p, p, p, p, up_full, row2(decay_base[layer]), row2(iclr_base[layer]),
          row2(key_norm_scale[layer]), row2(key_iclr_mix[layer]), row2(bonus[layer]),
          row2(gn_gain[layer]), row2(gn_bias[layer]))

        pad = DILATION_PATTERNS[-1][1] * ATTN_BLOCK
        yb = pl.pallas_call(
            _attn_kernel,
            grid=(bsz, c_b // LANES),
            in_specs=[seq_spec(o_q), seq_spec(o_k), seq_spec(o_v), seq_spec(o_zb)],
            out_specs=pl.BlockSpec((1, t_len, LANES), lambda b, h: (b, 0, h)),
            out_shape=jax.ShapeDtypeStruct((bsz, t_len, c_b), BF16),
            scratch_shapes=[pltpu.VMEM((t_len + pad, LANES), F32),
                            pltpu.VMEM((t_len + pad, LANES), F32),
                            pltpu.VMEM((t_len, LANES), F32),
                            pltpu.VMEM((t_len, LANES), F32),
                            pltpu.VMEM((t_len, LANES), F32)],
            compiler_params=pltpu.CompilerParams(dimension_semantics=("arbitrary", "arbitrary"),
                                                 vmem_limit_bytes=VMEM_LIMIT),
            name="attention",
        )(p, p, p, p)

        assert depth == 1
        tmo = 512
        wo = w_out[layer].astype(BF16)
        x = pl.pallas_call(
            _out_proj_kernel,
            grid=(rows // tmo,),
            in_specs=[pl.BlockSpec((tmo, c_a), lambda i: (i, 0)),
                      pl.BlockSpec((tmo, c_b), lambda i: (i, 0)),
                      pl.BlockSpec((c_a, d_model), lambda i: (0, 0)),
                      pl.BlockSpec((c_b, d_model), lambda i: (0, 0)),
                      pl.BlockSpec((tmo, d_model), lambda i: (i, 0)),
                      pl.BlockSpec((1, d_model), lambda i: (0, 0))],
            out_specs=pl.BlockSpec((tmo, d_model), lambda i: (i, 0)),
            out_shape=jax.ShapeDtypeStruct((rows, d_model), F32),
            compiler_params=pltpu.CompilerParams(dimension_semantics=("arbitrary",),
                                                 vmem_limit_bytes=VMEM_LIMIT),
            name="out_proj",
        )(ya.reshape(rows, c_a), yb.reshape(rows, c_b), wo[:c_a], wo[c_a:],
          x.reshape(rows, d_model), row2(final_gain)).reshape(bsz, t_len, d_model)
    return x
```

```python
import functools

import jax
import jax.numpy as jnp
from jax import lax
from jax.experimental import pallas as pl
from jax.experimental.pallas import tpu as pltpu

HEAD_DIM = 64
LANES = 128
SUBLANES = 8
DECAY_RANK = 64
ICLR_RANK = 64
DILATION_PATTERNS = ((128, 1), (512, 4), (2048, 16))
ATTN_BLOCK = 128
ROPE_THETA = 500000.0
ROPE_DIM = HEAD_DIM // 4
NORM_EPS = 1e-6
GN_EPS = 64e-5
CHUNK = 64
SUB = 16
RWKV_GROUP = 4
NEG_BIG = -1e30
VMEM_LIMIT = 56 * 1024 * 1024

F32 = jnp.float32
BF16 = jnp.bfloat16


def _dot(a, b, **kw):
    return jnp.dot(a, b, preferred_element_type=F32, **kw)


def _dot_nt(a, b):
    return lax.dot_general(a, b, (((1,), (1,)), ((), ())), preferred_element_type=F32)


def _split2(x):
    hi = x.astype(BF16)
    return hi, (x - hi.astype(F32)).astype(BF16)


def _dot_tn(a, b):
    return lax.dot_general(a, b, (((0,), (0,)), ((), ())), preferred_element_type=F32)


def _in_proj_kernel(x_ref, g_ref, w_ref, mix_ref, cos_ref, sa_ref, sb_ref, o_ref, carry_ref,
                    *, segs, shift_w, rope_lo, rope_hi, tiles_per_seq):
    i = pl.program_id(0)

    @pl.when(i == 0)
    def _():
        carry_ref[...] = jnp.zeros(carry_ref.shape, F32)

    x = x_ref[...]
    h = x * lax.rsqrt(jnp.mean(x * x, axis=-1, keepdims=True) + NORM_EPS) * g_ref[...]
    hb = h.astype(BF16)
    tm = x.shape[0]
    first = (i % tiles_per_seq) == 0
    row0 = lax.broadcasted_iota(jnp.int32, (tm, 1), 0) == 0
    for lo, hi in segs:
        p = _dot(hb, w_ref[:, lo:hi])
        if hi <= shift_w:
            old = jnp.where(first, 0.0, carry_ref[0:1, lo:hi])
            carry_ref[0:1, lo:hi] = p[tm - 1:tm, :]
            prev = jnp.where(row0, old, pltpu.roll(p, 1, axis=0))
            p = p + (prev - p) * mix_ref[:, lo:hi]
            o_ref[:, lo:hi] = p
        elif lo >= rope_lo and hi <= rope_hi:
            for j in range(lo, hi, LANES):
                t = p[:, j - lo:j - lo + LANES]
                t = (t * cos_ref[...] + pltpu.roll(t, LANES - ROPE_DIM // 2, axis=1) * sa_ref[...]
                     + pltpu.roll(t, ROPE_DIM // 2, axis=1) * sb_ref[...])
                o_ref[:, j:j + LANES] = t
        else:
            o_ref[:, lo:hi] = p


def _spread_matrix():
    j = lax.broadcasted_iota(jnp.int32, (LANES, (SUB - 1) * LANES), 0)
    col = lax.broadcasted_iota(jnp.int32, (LANES, (SUB - 1) * LANES), 1)
    sel = j == SUB * ((col % LANES) // SUB) + col // LANES
    one = jnp.where(sel, 1.0, 0.0).astype(BF16)
    return jnp.concatenate([one, one], axis=0)


def _diag_block_inverse(n_diag, spread):
    rows = n_diag.shape[0]
    ng = rows // SUBLANES
    gpb = SUB // SUBLANES
    n_hi = n_diag.astype(BF16)
    n_lo = (n_diag - n_hi.astype(F32)).astype(BF16)
    coef = _dot(jnp.concatenate([n_hi, n_lo], axis=1), spread)
    row = lax.broadcasted_iota(jnp.int32, (SUBLANES, LANES), 0)
    col = lax.broadcasted_iota(jnp.int32, (SUBLANES, LANES), 1) % HEAD_DIM
    xs = [jnp.where(col == row + (g * SUBLANES) % CHUNK, 1.0, 0.0).astype(F32)
          for g in range(ng)]
    for ss in range(SUB - 1):
        gs, rs = divmod(ss, SUBLANES)
        for blk in range(ng // gpb):
            xrow = xs[blk * gpb + gs][rs:rs + 1, :]
            for g in range(blk * gpb + (ss + 1) // SUBLANES, (blk + 1) * gpb):
                cf = coef[g * SUBLANES:(g + 1) * SUBLANES, ss * LANES:(ss + 1) * LANES]
                xs[g] = xs[g] - cf * xrow
    return jnp.concatenate(xs, axis=0)


def _rwkv_scratch(t_len):
    n_chunks = t_len // CHUNK
    return [pltpu.VMEM((2 * LANES, (SUB - 1) * LANES), BF16),
            pltpu.VMEM((t_len, LANES), BF16),
            pltpu.VMEM((t_len, LANES), F32),
            pltpu.VMEM((t_len, LANES), F32),
            pltpu.VMEM((n_chunks, 3 * LANES, LANES), BF16),
            pltpu.VMEM((n_chunks, LANES, LANES), F32),
            pltpu.VMEM((n_chunks, LANES, LANES), BF16)]


def _rwkv_kernel(r_ref, k_ref, v_ref, wa_ref, z_ref, up_ref, dbase_ref, abase_ref, kns_ref,
                 kim_ref, bonus_ref, gng_ref, gnb_ref, o_ref,
                 spread_ref, qt_ref, yloc_ref, bon_ref, mt_ref, nn_ref, s_ref):
    t_len = r_ref.shape[1]
    c = CHUNK
    grp = RWKV_GROUP
    rows = grp * c
    n_chunks = t_len // c
    lane = lax.broadcasted_iota(jnp.int32, (c, LANES), 1)
    is_h0 = lane < HEAD_DIM
    is_h0_g = lax.broadcasted_iota(jnp.int32, (rows, LANES), 1) < HEAD_DIM
    spread_ref[...] = _spread_matrix()
    trow = lax.broadcasted_iota(jnp.int32, (c, LANES), 0)
    scol = lane % HEAD_DIM
    tb, sb = trow // SUB, scol // SUB
    strict = trow > scol
    incl = trow >= scol
    in_block = tb == sb
    ltri = (lax.broadcasted_iota(jnp.int32, (c, c), 0)
            >= lax.broadcasted_iota(jnp.int32, (c, c), 1)).astype(BF16)
    sq_r = lax.broadcasted_iota(jnp.int32, (LANES, LANES), 0)
    sq_c = lax.broadcasted_iota(jnp.int32, (LANES, LANES), 1)
    same_head = sq_r // HEAD_DIM == sq_c // HEAD_DIM
    on_diag = sq_r == sq_c

    dbase, abase = dbase_ref[...], abase_ref[...]
    kns, kim, bonus = kns_ref[...], kim_ref[...], bonus_ref[...]
    gng, gnb = gng_ref[...], gnb_ref[...]
    up_hi, up_lo = _split2(up_ref[...])
    up_hh = jnp.concatenate([up_hi, up_hi], axis=0)

    def head_sum(x):
        s0 = jnp.sum(jnp.where(is_h0_g, x, 0.0), axis=1, keepdims=True)
        s1 = jnp.sum(jnp.where(is_h0_g, 0.0, x), axis=1, keepdims=True)
        return jnp.where(is_h0_g, s0, s1)

    def block_diag(x):
        return jnp.concatenate([jnp.where(is_h0, x, 0.0), jnp.where(is_h0, 0.0, x)], axis=0)

    def pass_a(gi, _):
        sl = pl.ds(pl.multiple_of(gi * rows, rows), rows)
        r = r_ref[0, sl, :]
        k = k_ref[0, sl, :]
        v = v_ref[0, sl, :]
        wa = wa_ref[0, sl, :]
        x_hi, x_lo = _split2(jnp.where(is_h0_g, jnp.tanh(wa), wa))
        lin = _dot(jnp.concatenate([x_hi, x_lo], axis=1), up_hh) + _dot(x_hi, up_lo)
        logit = -jax.nn.softplus(-(dbase + lin[:, :LANES])) - 0.5
        logw = -jnp.exp(logit)
        a = jax.nn.sigmoid(abase + lin[:, LANES:])
        kk = k * kns
        kk = kk / jnp.maximum(jnp.sqrt(head_sum(kk * kk)), 1e-12)
        kmod = k * (1.0 + (a - 1.0) * kim)
        b = kk * a
        bon_ref[sl, :] = head_sum(r * kmod * bonus) * v

        lw_hi = logw.astype(BF16)
        lw_mid, lw_lo = _split2(logw - lw_hi.astype(F32))
        parts = jnp.concatenate([lw_hi, lw_mid, lw_lo], axis=1)
        gs = [_dot(ltri, parts[j * c:(j + 1) * c]) for j in range(grp)]
        g = jnp.concatenate([(x[:, 2 * LANES:] + x[:, LANES:2 * LANES]) + x[:, :LANES] for x in gs],
                            axis=0)
        gl =jnp.concatenate([jnp.broadcast_to(g[(j + 1) * c - 1:(j + 1) * c, :], (c, LANES))
                              for j in range(grp)], axis=0)
        e_inv = jnp.exp(-g)
        kkd = kk * jnp.exp(g - logw)
        rd = r * jnp.exp(g)
        bi = b * e_inv
        ki = kmod * e_inv
        e_end = jnp.exp(gl - g)
        kdl = kmod * e_end
        bdl = b * e_end
        p_end = jnp.exp(gl)

        cs = [slice(j * c, (j + 1) * c) for j in range(grp)]
        aab, aak, arb, ark = [], [], [], []
        for j in range(grp):
            lhs = jnp.concatenate([kkd[cs[j]], rd[cs[j]]], axis=0).astype(BF16)
            rhs = jnp.concatenate([block_diag(bi[cs[j]]), block_diag(ki[cs[j]])],
                                  axis=0).astype(BF16)
            pair = _dot_nt(lhs, rhs)
            aab.append(jnp.where(strict, pair[:c, :LANES], 0.0))
            aak.append(jnp.where(strict, pair[:c, LANES:], 0.0))
            arb.append(jnp.where(incl, pair[c:, :LANES], 0.0))
            ark.append(jnp.where(incl, pair[c:, LANES:], 0.0))
        js = range(grp)
        vbd = [block_diag(v[cs[j]]).astype(BF16) for j in js]
        aakv = [_dot(aak[j].astype(BF16), vbd[j]) for j in js]
        t_diag = _diag_block_inverse(
            jnp.concatenate([jnp.where(in_block, x, 0.0) for x in aab], axis=0), spread_ref[...])
        tinv = [t_diag[cs[j]] for j in js]
        span = 1
        while span * SUB < c:
            low_mask = (tb // span == sb // span + 1) & ((tb // span) % 2 == 1)
            inner = [_dot(jnp.where(low_mask, aab[j], 0.0).astype(BF16),
                          block_diag(tinv[j]).astype(BF16)) for j in js]
            tinv = [tinv[j] - _dot(tinv[j].astype(BF16), block_diag(inner[j]).astype(BF16))
                    for j in js]
            span *= 2
        wu = [_dot(tinv[j].astype(BF16),
                   jnp.concatenate([block_diag(kkd[cs[j]]), block_diag(aakv[j])],
                                   axis=1).astype(BF16)) for j in js]
        w = [x[:, :LANES] for x in wu]
        uloc = [x[:, LANES:] for x in wu]
        arb_b = [arb[j].astype(BF16) for j in js]
        bdl_b = [bdl[cs[j]].astype(BF16) for j in js]
        qt = [rd[cs[j]] - _dot(arb_b[j], block_diag(w[j]).astype(BF16)) for j in js]
        yloc = [_dot(jnp.concatenate([ark[j].astype(BF16), -arb_b[j]], axis=1),
                     jnp.concatenate([vbd[j], block_diag(uloc[j]).astype(BF16)], axis=0))
                for j in js]
        wtb = [_dot_tn(w[j].astype(BF16), bdl_b[j]) for j in js]
        nn = [_dot_tn(jnp.concatenate([v[cs[j]], -uloc[j]], axis=0).astype(BF16),
                      jnp.concatenate([kdl[cs[j]].astype(BF16), bdl_b[j]], axis=0))
              for j in js]
        for j in js:
            ci = gi * grp + j
            csl = pl.ds(pl.multiple_of(ci * c, c), c)
            qt_ref[csl, :] = qt[j].astype(BF16)
            yloc_ref[csl, :] = yloc[j]
            mt = (jnp.where(on_diag, p_end[j * c:j * c + 1, :], 0.0)
                  - jnp.where(same_head, wtb[j], 0.0))
            mt_hi, mt_lo = _split2(mt)
            mt_ref[ci] = jnp.concatenate([mt_hi, mt_hi, mt_lo], axis=0)
            nn_ref[ci] = jnp.where(same_head, nn[j], 0.0)
        return 0

    lax.fori_loop(0, n_chunks // grp, pass_a, 0)

    def pass_b(ci, state):
        s_hi = state.astype(BF16)
        s_ref[ci] = s_hi
        s_lo = (state - s_hi.astype(F32)).astype(BF16)
        return (_dot(jnp.concatenate([s_hi, s_lo], axis=1), mt_ref[ci, 0:2 * LANES, :])
                + _dot(s_hi, mt_ref[ci, 2 * LANES:3 * LANES, :]) + nn_ref[ci])

    lax.fori_loop(0, n_chunks, pass_b, jnp.zeros((LANES, LANES), F32))

    def pass_c(gi, _):
        sl = pl.ds(pl.multiple_of(gi * rows, rows), rows)
        y = jnp.concatenate(
            [_dot_nt(qt_ref[pl.ds(pl.multiple_of((gi * grp + j) * c, c), c), :], s_ref[gi * grp + j])
             for j in range(grp)], axis=0) + yloc_ref[sl, :]
        mu = head_sum(y) * (1.0 / HEAD_DIM)
        d = y - mu
        var = head_sum(d * d) * (1.0 / HEAD_DIM)
        yn = d * lax.rsqrt(var + GN_EPS) * gng + gnb + bon_ref[sl, :]
        z = z_ref[0, sl, :]
        o_ref[0, sl, :] = (yn * (z * jax.nn.sigmoid(z))).astype(o_ref.dtype)
        return 0

    lax.fori_loop(0, n_chunks // grp, pass_c, 0)


def _attn_kernel(q_ref, k_ref, v_ref, z_ref, o_ref, kp_ref, vp_ref, acc_ref, m_ref, l_ref):
    t_len = q_ref.shape[1]
    blk = ATTN_BLOCK
    pad = kp_ref.shape[0] - t_len
    kp_ref[0:pad, :] = jnp.zeros((pad, LANES), F32)
    vp_ref[0:pad, :] = jnp.zeros((pad, LANES), F32)
    kp_ref[pad:, :] = k_ref[0]
    vp_ref[pad:, :] = v_ref[0]

    lane = lax.broadcasted_iota(jnp.int32, (blk, LANES), 1)
    is_h0 = lane < HEAD_DIM
    qi = lax.broadcasted_iota(jnp.int32, (blk, 2 * blk), 0)
    kj = lax.broadcasted_iota(jnp.int32, (blk, 2 * blk), 1)
    scale = HEAD_DIM ** -0.5

    for pi, (window, dil) in enumerate(DILATION_PATTERNS):
        span = window // dil
        nblk = t_len // (dil * blk)
        in_band = (qi + blk - kj >= 0) & (qi + blk - kj <= span)

        def body(it, _, pi=pi, dil=dil, nblk=nblk, in_band=in_band):
            res = it // nblk
            n = it % nblk
            q0 = res + dil * blk * n
            if dil == 1:
                q0 = pl.multiple_of(q0, blk)
                rows = pl.ds(q0, blk)
                krows = pl.ds(pl.multiple_of(pad + q0 - blk, blk), 2 * blk)
            else:
                rows = pl.ds(q0, blk, stride=dil)
                krows = pl.ds(pad + q0 - dil * blk, 2 * blk, stride=dil)
            q = q_ref[0, rows, :] * scale
            kw = kp_ref[krows, :].astype(BF16)
            vw = vp_ref[krows, :].astype(BF16)
            valid = in_band & (kj >= jnp.where(n == 0, blk, 0))
            accs, ms, ls = [], [], []
            for hsel in (is_h0, ~is_h0):
                s = _dot_nt(jnp.where(hsel, q, 0.0).astype(BF16), kw)
                s = jnp.where(valid, s, NEG_BIG)
                m = jnp.max(s, axis=1, keepdims=True)
                p = jnp.exp(s - m)
                ls.append(jnp.sum(p, axis=1, keepdims=True))
                ms.append(m)
                accs.append(_dot(p.astype(BF16), vw))
            acc = jnp.where(is_h0, accs[0], accs[1])
            m = jnp.where(is_h0, ms[0], ms[1])
            l = jnp.where(is_h0, ls[0], ls[1])
            if pi > 0:
                m_old = m_ref[rows, :]
                m_new = jnp.maximum(m_old, m)
                w_old = jnp.exp(m_old - m_new)
                w_cur = jnp.exp(m - m_new)
                acc = w_old * acc_ref[rows, :] + w_cur * acc
                l = w_old * l_ref[rows, :] + w_cur * l
                m = m_new
            acc_ref[rows, :] = acc
            m_ref[rows, :] = m
            l_ref[rows, :] = l
            return 0

        lax.fori_loop(0, t_len // blk, body, 0)

    def finish(it, _):
        rows = pl.ds(pl.multiple_of(it * blk, blk), blk)
        z = z_ref[0, rows, :]
        o_ref[0, rows, :] = (acc_ref[rows, :] / l_ref[rows, :]
                             * (z * jax.nn.sigmoid(z))).astype(o_ref.dtype)
        return 0

    lax.fori_loop(0, t_len // blk, finish, 0)


def _out_proj_kernel(ya_ref, yb_ref, wa_ref, wb_ref, x_ref, g_ref, o_ref):
    y = x_ref[...] + _dot(ya_ref[...], wa_ref[...]) + _dot(yb_ref[...], wb_ref[...])
    o_ref[...] = y * lax.rsqrt(jnp.mean(y * y, axis=-1, keepdims=True) + NORM_EPS) * g_ref[...]


def _rope_tables(t_len):
    half = ROPE_DIM // 2
    inv = ROPE_THETA ** (-jnp.arange(half, dtype=F32) * 2.0 / ROPE_DIM)
    ang = jnp.arange(t_len, dtype=jnp.int32).astype(F32)[:, None] * inv[None, :]
    cos, sin = jnp.cos(ang), jnp.sin(ang)
    ones = jnp.ones((t_len, HEAD_DIM - ROPE_DIM), F32)
    zeros = jnp.zeros((t_len, HEAD_DIM - ROPE_DIM), F32)
    zh = jnp.zeros((t_len, half), F32)
    tile = lambda a: jnp.tile(a, (1, LANES // HEAD_DIM))
    cos_t = tile(jnp.concatenate([cos, cos, ones], axis=1))
    sa_t = tile(jnp.concatenate([-sin, zh, zeros], axis=1))
    sb_t = tile(jnp.concatenate([zh, sin, zeros], axis=1))
    return cos_t, sa_t, sb_t


def kernel(x, norm_gain, w_in, shift_mix, decay_base, decay_up, iclr_base, iclr_up, key_norm_scale,
           key_iclr_mix, bonus, gn_gain, gn_bias, w_out, final_gain):
    bsz, t_len, d_model = x.shape
    depth = w_in.shape[0]
    c_a = decay_base.shape[1]
    n_hp = c_a // LANES
    shift_w = 3 * c_a + DECAY_RANK + ICLR_RANK
    in_w = w_in.shape[2]
    c_b = (in_w - shift_w - c_a) // 4
    assert c_a % LANES == 0 and c_b == c_a and t_len % (16 * ATTN_BLOCK) == 0
    o_za = shift_w
    o_q = o_za + c_a
    o_k, o_v, o_zb = o_q + c_b, o_q + 2 * c_b, o_q + 3 * c_b
    segs = ((0, c_a), (c_a, 2 * c_a), (2 * c_a, 3 * c_a), (3 * c_a, shift_w), (o_za, o_q),
            (o_q, o_k), (o_k, o_v), (o_v, o_zb), (o_zb, in_w))
    cos_t, sa_t, sb_t = _rope_tables(t_len)
    rows = bsz * t_len
    tm = 256
    tiles_per_seq = t_len // tm
    row2 = lambda a: a.reshape(1, -1).astype(F32)

    for layer in range(depth):
        p = pl.pallas_call(
            functools.partial(_in_proj_kernel, segs=segs, shift_w=shift_w, rope_lo=o_q,
                              rope_hi=o_v, tiles_per_seq=tiles_per_seq),
            grid=(rows // tm,),
            in_specs=[
                pl.BlockSpec((tm, d_model), lambda i: (i, 0)),
                pl.BlockSpec((1, d_model), lambda i: (0, 0)),
                pl.BlockSpec((d_model, in_w), lambda i: (0, 0)),
                pl.BlockSpec((1, shift_w), lambda i: (0, 0)),
                pl.BlockSpec((tm, LANES), lambda i: (i % tiles_per_seq, 0)),
                pl.BlockSpec((tm, LANES), lambda i: (i % tiles_per_seq, 0)),
                pl.BlockSpec((tm, LANES), lambda i: (i % tiles_per_seq, 0)),
            ],
            out_specs=pl.BlockSpec((tm, in_w), lambda i: (i, 0)),
            out_shape=jax.ShapeDtypeStruct((rows, in_w), F32),
            scratch_shapes=[pltpu.VMEM((SUBLANES, shift_w), F32)],
            compiler_params=pltpu.CompilerParams(dimension_semantics=("arbitrary",),
                                                 vmem_limit_bytes=VMEM_LIMIT),
            name="in_proj",
        )(x.reshape(rows, d_model), row2(norm_gain[layer]), w_in[layer].astype(BF16),
          row2(shift_mix[layer]), cos_t, sa_t, sb_t)
        p = p.reshape(bsz, t_len, in_w)

        zeros_up = jnp.zeros((DECAY_RANK, c_a), F32)
        up_full = jnp.concatenate(
            [jnp.concatenate([decay_up[layer].astype(F32), zeros_up], axis=0).reshape(
                DECAY_RANK + ICLR_RANK, n_hp, 1, LANES),
             jnp.concatenate([zeros_up, iclr_up[layer].astype(F32)], axis=0).reshape(
                 DECAY_RANK + ICLR_RANK, n_hp, 1, LANES)], axis=2).reshape(
                     DECAY_RANK + ICLR_RANK, n_hp * 2 * LANES)
        col = lambda off: (lambda b, h: (b, 0, off // LANES + h))
        par = lambda b, h: (0, h)
        seq_spec = lambda off: pl.BlockSpec((1, t_len, LANES), col(off))
        par_spec = pl.BlockSpec((1, LANES), par)
        ya = pl.pallas_call(
            _rwkv_kernel,
            grid=(bsz, n_hp),
            in_specs=[seq_spec(0), seq_spec(c_a), seq_spec(2 * c_a),
                      pl.BlockSpec((1, t_len, LANES), lambda b, h: (b, 0, 3 * c_a // LANES)),
                      seq_spec(o_za),
                      pl.BlockSpec((DECAY_RANK + ICLR_RANK, 2 * LANES), par),
                      par_spec, par_spec, par_spec, par_spec, par_spec, par_spec, par_spec],
            out_specs=pl.BlockSpec((1, t_len, LANES), lambda b, h: (b, 0, h)),
            out_shape=jax.ShapeDtypeStruct((bsz, t_len, c_a), BF16),
            scratch_shapes=_rwkv_scratch(t_len),
            compiler_params=pltpu.CompilerParams(dimension_semantics=("arbitrary", "arbitrary"),
                                                 vmem_limit_bytes=VMEM_LIMIT),
            name="rwkv",
        )(p, p, p, p, p, up_full, row2(decay_base[layer]), row2(iclr_base[layer]),
          row2(key_norm_scale[layer]), row2(key_iclr_mix[layer]), row2(bonus[layer]),
          row2(gn_gain[layer]), row2(gn_bias[layer]))

        pad = DILATION_PATTERNS[-1][1] * ATTN_BLOCK
        yb = pl.pallas_call(
            _attn_kernel,
            grid=(bsz, c_b // LANES),
            in_specs=[seq_spec(o_q), seq_spec(o_k), seq_spec(o_v), seq_spec(o_zb)],
            out_specs=pl.BlockSpec((1, t_len, LANES), lambda b, h: (b, 0, h)),
            out_shape=jax.ShapeDtypeStruct((bsz, t_len, c_b), BF16),
            scratch_shapes=[pltpu.VMEM((t_len + pad, LANES), F32),
                            pltpu.VMEM((t_len + pad, LANES), F32),
                            pltpu.VMEM((t_len, LANES), F32),
                            pltpu.VMEM((t_len, LANES), F32),
                            pltpu.VMEM((t_len, LANES), F32)],
            compiler_params=pltpu.CompilerParams(dimension_semantics=("arbitrary", "arbitrary"),
                                                 vmem_limit_bytes=VMEM_LIMIT),
            name="attention",
        )(p, p, p, p)

        assert depth == 1
        tmo = 512
        wo = w_out[layer].astype(BF16)
        x = pl.pallas_call(
            _out_proj_kernel,
            grid=(rows // tmo,),
            in_specs=[pl.BlockSpec((tmo, c_a), lambda i: (i, 0)),
                      pl.BlockSpec((tmo, c_b), lambda i: (i, 0)),
                      pl.BlockSpec((c_a, d_model), lambda i: (0, 0)),
                      pl.BlockSpec((c_b, d_model), lambda i: (0, 0)),
                      pl.BlockSpec((tmo, d_model), lambda i: (i, 0)),
                      pl.BlockSpec((1, d_model), lambda i: (0, 0))],
            out_specs=pl.BlockSpec((tmo, d_model), lambda i: (i, 0)),
            out_shape=jax.ShapeDtypeStruct((rows, d_model), F32),
            compiler_params=pltpu.CompilerParams(dimension_semantics=("arbitrary",),
                                                 vmem_limit_bytes=VMEM_LIMIT),
            name="out_proj",
        )(ya.reshape(rows, c_a), yb.reshape(rows, c_b), wo[:c_a], wo[c_a:],
          x.reshape(rows, d_model), row2(final_gain)).reshape(bsz, t_len, d_model)
    return x
```

```python
import functools

import jax
import jax.numpy as jnp
from jax import lax
from jax.experimental import pallas as pl
from jax.experimental.pallas import tpu as pltpu

HEAD_DIM = 64
LANES = 128
SUBLANES = 8
DECAY_RANK = 64
ICLR_RANK = 64
DILATION_PATTERNS = ((128, 1), (512, 4), (2048, 16))
ATTN_BLOCK = 128
ATTN_GROUP = 4
ROPE_THETA = 500000.0
ROPE_DIM = HEAD_DIM // 4
NORM_EPS = 1e-6
GN_EPS = 64e-5
CHUNK = 64
SUB = 16
RWKV_GROUP = 8
NEG_BIG = -1e30
VMEM_LIMIT = 56 * 1024 * 1024

F32 = jnp.float32
BF16 = jnp.bfloat16


def _dot(a, b, **kw):
    return jnp.dot(a, b, preferred_element_type=F32, **kw)


def _dot_nt(a, b):
    return lax.dot_general(a, b, (((1,), (1,)), ((), ())), preferred_element_type=F32)


def _split2(x):
    hi = x.astype(BF16)
    return hi, (x - hi.astype(F32)).astype(BF16)


def _dot_tn(a, b):
    return lax.dot_general(a, b, (((0,), (0,)), ((), ())), preferred_element_type=F32)


def _in_proj_kernel(x_ref, g_ref, w_ref, mix_ref, cos_ref, sa_ref, sb_ref, o_ref, carry_ref,
                    *, segs, shift_w, rope_lo, rope_hi, tiles_per_seq):
    i = pl.program_id(0)

    @pl.when(i == 0)
    def _():
        carry_ref[...] = jnp.zeros(carry_ref.shape, F32)

    x = x_ref[...]
    h = x * lax.rsqrt(jnp.mean(x * x, axis=-1, keepdims=True) + NORM_EPS) * g_ref[...]
    hb = h.astype(BF16)
    tm = x.shape[0]
    first = (i % tiles_per_seq) == 0
    row0 = lax.broadcasted_iota(jnp.int32, (tm, 1), 0) == 0
    for lo, hi in segs:
        p = _dot(hb, w_ref[:, lo:hi])
        if hi <= shift_w:
            old = jnp.where(first, 0.0, carry_ref[0:1, lo:hi])
            carry_ref[0:1, lo:hi] = p[tm - 1:tm, :]
            prev = jnp.where(row0, old, pltpu.roll(p, 1, axis=0))
            p = p + (prev - p) * mix_ref[:, lo:hi]
            o_ref[:, lo:hi] = p
        elif lo >= rope_lo and hi <= rope_hi:
            for j in range(lo, hi, LANES):
                t = p[:, j - lo:j - lo + LANES]
                t = (t * cos_ref[...] + pltpu.roll(t, LANES - ROPE_DIM // 2, axis=1) * sa_ref[...]
                     + pltpu.roll(t, ROPE_DIM // 2, axis=1) * sb_ref[...])
                o_ref[:, j:j + LANES] = t
        else:
            o_ref[:, lo:hi] = p


def _spread_matrix():
    j = lax.broadcasted_iota(jnp.int32, (LANES, (SUB - 1) * LANES), 0)
    col = lax.broadcasted_iota(jnp.int32, (LANES, (SUB - 1) * LANES), 1)
    sel = j == SUB * ((col % LANES) // SUB) + col // LANES
    one = jnp.where(sel, 1.0, 0.0).astype(BF16)
    return jnp.concatenate([one, one], axis=0)


def _diag_block_inverse(n_diag, spread):
    rows = n_diag.shape[0]
    ng = rows // SUBLANES
    gpb = SUB // SUBLANES
    n_hi = n_diag.astype(BF16)
    n_lo = (n_diag - n_hi.astype(F32)).astype(BF16)
    coef = _dot(jnp.concatenate([n_hi, n_lo], axis=1), spread)
    row = lax.broadcasted_iota(jnp.int32, (SUBLANES, LANES), 0)
    col = lax.broadcasted_iota(jnp.int32, (SUBLANES, LANES), 1) % HEAD_DIM
    xs = [jnp.where(col == row + (g * SUBLANES) % CHUNK, 1.0, 0.0).astype(F32)
          for g in range(ng)]
    for ss in range(SUB - 1):
        gs, rs = divmod(ss, SUBLANES)
        for blk in range(ng // gpb):
            xrow = xs[blk * gpb + gs][rs:rs + 1, :]
            for g in range(blk * gpb + (ss + 1) // SUBLANES, (blk + 1) * gpb):
                cf = coef[g * SUBLANES:(g + 1) * SUBLANES, ss * LANES:(ss + 1) * LANES]
                xs[g] = xs[g] - cf * xrow
    return jnp.concatenate(xs, axis=0)


def _rwkv_scratch(t_len):
    n_chunks = t_len // CHUNK
    return [pltpu.VMEM((2 * LANES, (SUB - 1) * LANES), BF16),
            pltpu.VMEM((t_len, LANES), BF16),
            pltpu.VMEM((t_len, LANES), F32),
            pltpu.VMEM((t_len, LANES), F32),
            pltpu.VMEM((n_chunks, 3 * LANES, LANES), BF16),
            pltpu.VMEM((n_chunks, LANES, LANES), F32),
            pltpu.VMEM((n_chunks, LANES, LANES), BF16)]


def _rwkv_kernel(r_ref, k_ref, v_ref, wa_ref, z_ref, up_ref, dbase_ref, abase_ref, kns_ref,
                 kim_ref, bonus_ref, gng_ref, gnb_ref, o_ref,
                 spread_ref, qt_ref, yloc_ref, bon_ref, mt_ref, nn_ref, s_ref):
    t_len = r_ref.shape[1]
    c = CHUNK
    grp = RWKV_GROUP
    rows = grp * c
    n_chunks = t_len // c
    lane = lax.broadcasted_iota(jnp.int32, (c, LANES), 1)
    is_h0 = lane < HEAD_DIM
    is_h0_g = lax.broadcasted_iota(jnp.int32, (rows, LANES), 1) < HEAD_DIM
    spread_ref[...] = _spread_matrix()
    trow = lax.broadcasted_iota(jnp.int32, (c, LANES), 0)
    scol = lane % HEAD_DIM
    tb, sb = trow // SUB, scol // SUB
    strict = trow > scol
    incl = trow >= scol
    in_block = tb == sb
    ltri = (lax.broadcasted_iota(jnp.int32, (c, c), 0)
            >= lax.broadcasted_iota(jnp.int32, (c, c), 1)).astype(BF16)
    sq_r = lax.broadcasted_iota(jnp.int32, (LANES, LANES), 0)
    sq_c = lax.broadcasted_iota(jnp.int32, (LANES, LANES), 1)
    same_head = sq_r // HEAD_DIM == sq_c // HEAD_DIM
    on_diag = sq_r == sq_c

    dbase, abase = dbase_ref[...], abase_ref[...]
    kns, kim, bonus = kns_ref[...], kim_ref[...], bonus_ref[...]
    gng, gnb = gng_ref[...], gnb_ref[...]
    up_hi, up_lo = _split2(up_ref[...])
    up_hh = jnp.concatenate([up_hi, up_hi], axis=0)

    def head_sum(x):
        s0 = jnp.sum(jnp.where(is_h0_g, x, 0.0), axis=1, keepdims=True)
        s1 = jnp.sum(jnp.where(is_h0_g, 0.0, x), axis=1, keepdims=True)
        return jnp.where(is_h0_g, s0, s1)

    def block_diag(x):
        return jnp.concatenate([jnp.where(is_h0, x, 0.0), jnp.where(is_h0, 0.0, x)], axis=0)

    def pass_a(gi, _):
        sl = pl.ds(pl.multiple_of(gi * rows, rows), rows)
        r = r_ref[0, sl, :]
        k = k_ref[0, sl, :]
        v = v_ref[0, sl, :]
        wa = wa_ref[0, sl, :]
        x_hi, x_lo = _split2(jnp.where(is_h0_g, jnp.tanh(wa), wa))
        lin = _dot(jnp.concatenate([x_hi, x_lo], axis=1), up_hh) + _dot(x_hi, up_lo)
        logit = -jax.nn.softplus(-(dbase + lin[:, :LANES])) - 0.5
        logw = -jnp.exp(logit)
        a = jax.nn.sigmoid(abase + lin[:, LANES:])
        kk = k * kns
        kk = kk / jnp.maximum(jnp.sqrt(head_sum(kk * kk)), 1e-12)
        kmod = k * (1.0 + (a - 1.0) * kim)
        b = kk * a
        bon_ref[sl, :] = head_sum(r * kmod * bonus) * v

        lw_hi = logw.astype(BF16)
        lw_mid, lw_lo = _split2(logw - lw_hi.astype(F32))
        parts = jnp.concatenate([lw_hi, lw_mid, lw_lo], axis=1)
        gs = [_dot(ltri, parts[j * c:(j + 1) * c]) for j in range(grp)]
        g = jnp.concatenate([(x[:, 2 * LANES:] + x[:, LANES:2 * LANES]) + x[:, :LANES] for x in gs],
                            axis=0)
        gl =jnp.concatenate([jnp.broadcast_to(g[(j + 1) * c - 1:(j + 1) * c, :], (c, LANES))
                              for j in range(grp)], axis=0)
        e_inv = jnp.exp(-g)
        kkd = kk * jnp.exp(g - logw)
        rd = r * jnp.exp(g)
        bi = b * e_inv
        ki = kmod * e_inv
        e_end = jnp.exp(gl - g)
        kdl = kmod * e_end
        bdl = b * e_end
        p_end = jnp.exp(gl)

        cs = [slice(j * c, (j + 1) * c) for j in range(grp)]
        aab, aak, arb, ark = [], [], [], []
        for j in range(grp):
            lhs = jnp.concatenate([kkd[cs[j]], rd[cs[j]]], axis=0).astype(BF16)
            rhs = jnp.concatenate([block_diag(bi[cs[j]]), block_diag(ki[cs[j]])],
                                  axis=0).astype(BF16)
            pair = _dot_nt(lhs, rhs)
            aab.append(jnp.where(strict, pair[:c, :LANES], 0.0))
            aak.append(jnp.where(strict, pair[:c, LANES:], 0.0))
            arb.append(jnp.where(incl, pair[c:, :LANES], 0.0))
            ark.append(jnp.where(incl, pair[c:, LANES:], 0.0))
        js = range(grp)
        vbd = [block_diag(v[cs[j]]).astype(BF16) for j in js]
        aakv = [_dot(aak[j].astype(BF16), vbd[j]) for j in js]
        t_diag = _diag_block_inverse(
            jnp.concatenate([jnp.where(in_block, x, 0.0) for x in aab], axis=0), spread_ref[...])
        tinv = [t_diag[cs[j]] for j in js]
        span = 1
        while span * SUB < c:
            low_mask = (tb // span == sb // span + 1) & ((tb // span) % 2 == 1)
            inner = [_dot(jnp.where(low_mask, aab[j], 0.0).astype(BF16),
                          block_diag(tinv[j]).astype(BF16)) for j in js]
            tinv = [tinv[j] - _dot(tinv[j].astype(BF16), block_diag(inner[j]).astype(BF16))
                    for j in js]
            span *= 2
        wu = [_dot(tinv[j].astype(BF16),
                   jnp.concatenate([block_diag(kkd[cs[j]]), block_diag(aakv[j])],
                                   axis=1).astype(BF16)) for j in js]
        w = [x[:, :LANES] for x in wu]
        uloc = [x[:, LANES:] for x in wu]
        arb_b = [arb[j].astype(BF16) for j in js]
        bdl_b = [bdl[cs[j]].astype(BF16) for j in js]
        qt = [rd[cs[j]] - _dot(arb_b[j], block_diag(w[j]).astype(BF16)) for j in js]
        yloc = [_dot(jnp.concatenate([ark[j].astype(BF16), -arb_b[j]], axis=1),
                     jnp.concatenate([vbd[j], block_diag(uloc[j]).astype(BF16)], axis=0))
                for j in js]
        wtb = [_dot_tn(w[j].astype(BF16), bdl_b[j]) for j in js]
        nn = [_dot_tn(jnp.concatenate([v[cs[j]], -uloc[j]], axis=0).astype(BF16),
                      jnp.concatenate([kdl[cs[j]].astype(BF16), bdl_b[j]], axis=0))
              for j in js]
        for j in js:
            ci = gi * grp + j
            csl = pl.ds(pl.multiple_of(ci * c, c), c)
            qt_ref[csl, :] = qt[j].astype(BF16)
            yloc_ref[csl, :] = yloc[j]
            mt = (jnp.where(on_diag, p_end[j * c:j * c + 1, :], 0.0)
                  - jnp.where(same_head, wtb[j], 0.0))
            mt_hi, mt_lo = _split2(mt)
            mt_ref[ci] = jnp.concatenate([mt_hi, mt_hi, mt_lo], axis=0)
            nn_ref[ci] = jnp.where(same_head, nn[j], 0.0)
        return 0

    lax.fori_loop(0, n_chunks // grp, pass_a, 0)

    def state_step(ci, state):
        s_hi, s_lo = _split2(state)
        s_ref[ci] = s_hi
        return (_dot(jnp.concatenate([s_hi, s_lo], axis=1), mt_ref[ci, 0:2 * LANES, :])
                + _dot(s_hi, mt_ref[ci, 2 * LANES:3 * LANES, :]) + nn_ref[ci])

    def y_state_part(ci):
        return _dot_nt(qt_ref[pl.ds(pl.multiple_of(ci * c, c), c), :], s_ref[ci])

    def finish(gi, ys):
        sl = pl.ds(pl.multiple_of(gi * rows, rows), rows)
        y = jnp.concatenate(ys, axis=0) + yloc_ref[sl, :]
        mu = head_sum(y) * (1.0 / HEAD_DIM)
        d = y - mu
        var = head_sum(d * d) * (1.0 / HEAD_DIM)
        yn = d * lax.rsqrt(var + GN_EPS) * gng + gnb + bon_ref[sl, :]
        z = z_ref[0, sl, :]
        o_ref[0, sl, :] = (yn * (z * jax.nn.sigmoid(z))).astype(o_ref.dtype)

    def pass_b(gi, state):
        ys = []
        for j in range(grp):
            state = state_step(gi * grp + j, state)
            ys.append(y_state_part((gi - 1) * grp + j))
        finish(gi - 1, ys)
        return state

    n_groups = n_chunks // grp
    state = jnp.zeros((LANES, LANES), F32)
    for j in range(grp):
        state = state_step(j, state)
    lax.fori_loop(1, n_groups, pass_b, state)
    finish(n_groups - 1, [y_state_part((n_groups - 1) * grp + j) for j in range(grp)])


def _attn_kernel(q_ref, k_ref, v_ref, z_ref, o_ref, kp_ref, vp_ref, acc_ref, m_ref, l_ref):
    t_len = q_ref.shape[1]
    blk = ATTN_BLOCK
    pad = kp_ref.shape[0] - t_len
    kp_ref[0:pad, :] = jnp.zeros((pad, LANES), F32)
    vp_ref[0:pad, :] = jnp.zeros((pad, LANES), F32)
    kp_ref[pad:, :] = k_ref[0]
    vp_ref[pad:, :] = v_ref[0]

    lane = lax.broadcasted_iota(jnp.int32, (blk, LANES), 1)
    is_h0 = lane < HEAD_DIM
    qi = lax.broadcasted_iota(jnp.int32, (blk, 2 * blk), 0)
    kj = lax.broadcasted_iota(jnp.int32, (blk, 2 * blk), 1)
    scale = HEAD_DIM ** -0.5
    nb = ATTN_GROUP
    heads = (is_h0, ~is_h0)

    for pi, (window, dil) in enumerate(DILATION_PATTERNS):
        span = window // dil
        nblk = t_len // (dil * blk)
        in_band = (qi + blk - kj >= 0) & (qi + blk - kj <= span)
        bias_mid = jnp.where(in_band, 0.0, NEG_BIG).astype(F32)
        bias_first = jnp.where(in_band & (kj >= blk), 0.0, NEG_BIG).astype(F32)

        def body(it, _, pi=pi, dil=dil, nblk=nblk, bias_mid=bias_mid, bias_first=bias_first):
            rows, krows, biases = [], [], []
            for j in range(nb):
                res = (it * nb + j) // nblk
                n = (it * nb + j) % nblk
                q0 = res + dil * blk * n
                if dil == 1:
                    q0 = pl.multiple_of(q0, blk)
                    rows.append(pl.ds(q0, blk))
                    krows.append(pl.ds(pl.multiple_of(pad + q0 - blk, blk), 2 * blk))
                else:
                    rows.append(pl.ds(q0, blk, stride=dil))
                    krows.append(pl.ds(pad + q0 - dil * blk, 2 * blk, stride=dil))
                biases.append(jnp.where(n == 0, bias_first, bias_mid))
            q = [q_ref[0, rows[j], :] * scale for j in range(nb)]
            kw = [kp_ref[krows[j], :].astype(BF16) for j in range(nb)]
            vw = [vp_ref[krows[j], :].astype(BF16) for j in range(nb)]
            s = [[_dot_nt(jnp.where(h, q[j], 0.0).astype(BF16), kw[j]) + biases[j] for h in heads]
                 for j in range(nb)]
            m = [[jnp.max(x, axis=1, keepdims=True) for x in sj] for sj in s]
            p = [[jnp.exp(s[j][h] - m[j][h]) for h in range(2)] for j in range(nb)]
            l = [[jnp.sum(x, axis=1, keepdims=True) for x in pj] for pj in p]
            pv = [[_dot(p[j][h].astype(BF16), vw[j]) for h in range(2)] for j in range(nb)]
            for j in range(nb):
                acc = jnp.where(is_h0, pv[j][0], pv[j][1])
                mj = jnp.where(is_h0, m[j][0], m[j][1])
                lj = jnp.where(is_h0, l[j][0], l[j][1])
                if pi > 0:
                    m_old = m_ref[rows[j], :]
                    m_new = jnp.maximum(m_old, mj)
                    w_old = jnp.exp(m_old - m_new)
                    w_cur = jnp.exp(mj - m_new)
                    acc = w_old * acc_ref[rows[j], :] + w_cur * acc
                    lj = w_old * l_ref[rows[j], :] + w_cur * lj
                    mj = m_new
                acc_ref[rows[j], :] = acc
                m_ref[rows[j], :] = mj
                l_ref[rows[j], :] = lj
            return 0

        lax.fori_loop(0, t_len // (blk * nb), body, 0)

    def finish(it, _):
        rows = pl.ds(pl.multiple_of(it * blk, blk), blk)
        z = z_ref[0, rows, :]
        o_ref[0, rows, :] = (acc_ref[rows, :] / l_ref[rows, :]
                             * (z * jax.nn.sigmoid(z))).astype(o_ref.dtype)
        return 0

    lax.fori_loop(0, t_len // blk, finish, 0)


def _out_proj_kernel(ya_ref, yb_ref, wa_ref, wb_ref, x_ref, g_ref, o_ref):
    y = x_ref[...] + _dot(ya_ref[...], wa_ref[...]) + _dot(yb_ref[...], wb_ref[...])
    o_ref[...] = y * lax.rsqrt(jnp.mean(y * y, axis=-1, keepdims=True) + NORM_EPS) * g_ref[...]


def _rope_tables(t_len):
    half = ROPE_DIM // 2
    inv = ROPE_THETA ** (-jnp.arange(half, dtype=F32) * 2.0 / ROPE_DIM)
    ang = jnp.arange(t_len, dtype=jnp.int32).astype(F32)[:, None] * inv[None, :]
    cos, sin = jnp.cos(ang), jnp.sin(ang)
    ones = jnp.ones((t_len, HEAD_DIM - ROPE_DIM), F32)
    zeros = jnp.zeros((t_len, HEAD_DIM - ROPE_DIM), F32)
    zh = jnp.zeros((t_len, half), F32)
    tile = lambda a: jnp.tile(a, (1, LANES // HEAD_DIM))
    cos_t = tile(jnp.concatenate([cos, cos, ones], axis=1))
    sa_t = tile(jnp.concatenate([-sin, zh, zeros], axis=1))
    sb_t = tile(jnp.concatenate([zh, sin, zeros], axis=1))
    return cos_t, sa_t, sb_t


def kernel(x, norm_gain, w_in, shift_mix, decay_base, decay_up, iclr_base, iclr_up, key_norm_scale,
           key_iclr_mix, bonus, gn_gain, gn_bias, w_out, final_gain):
    bsz, t_len, d_model = x.shape
    depth = w_in.shape[0]
    c_a = decay_base.shape[1]
    n_hp = c_a // LANES
    shift_w = 3 * c_a + DECAY_RANK + ICLR_RANK
    in_w = w_in.shape[2]
    c_b = (in_w - shift_w - c_a) // 4
    assert c_a % LANES == 0 and c_b == c_a and t_len % (16 * ATTN_BLOCK) == 0
    o_za = shift_w
    o_q = o_za + c_a
    o_k, o_v, o_zb = o_q + c_b, o_q + 2 * c_b, o_q + 3 * c_b
    segs = ((0, c_a), (c_a, 2 * c_a), (2 * c_a, 3 * c_a), (3 * c_a, shift_w), (o_za, o_q),
            (o_q, o_k), (o_k, o_v), (o_v, o_zb), (o_zb, in_w))
    cos_t, sa_t, sb_t = _rope_tables(t_len)
    rows = bsz * t_len
    tm = 256
    tiles_per_seq = t_len // tm
    row2 = lambda a: a.reshape(1, -1).astype(F32)

    for layer in range(depth):
        p = pl.pallas_call(
            functools.partial(_in_proj_kernel, segs=segs, shift_w=shift_w, rope_lo=o_q,
                              rope_hi=o_v, tiles_per_seq=tiles_per_seq),
            grid=(rows // tm,),
            in_specs=[
                pl.BlockSpec((tm, d_model), lambda i: (i, 0)),
                pl.BlockSpec((1, d_model), lambda i: (0, 0)),
                pl.BlockSpec((d_model, in_w), lambda i: (0, 0)),
                pl.BlockSpec((1, shift_w), lambda i: (0, 0)),
                pl.BlockSpec((tm, LANES), lambda i: (i % tiles_per_seq, 0)),
                pl.BlockSpec((tm, LANES), lambda i: (i % tiles_per_seq, 0)),
                pl.BlockSpec((tm, LANES), lambda i: (i % tiles_per_seq, 0)),
            ],
            out_specs=pl.BlockSpec((tm, in_w), lambda i: (i, 0)),
            out_shape=jax.ShapeDtypeStruct((rows, in_w), F32),
            scratch_shapes=[pltpu.VMEM((SUBLANES, shift_w), F32)],
            compiler_params=pltpu.CompilerParams(dimension_semantics=("arbitrary",),
                                                 vmem_limit_bytes=VMEM_LIMIT),
            name="in_proj",
        )(x.reshape(rows, d_model), row2(norm_gain[layer]), w_in[layer].astype(BF16),
          row2(shift_mix[layer]), cos_t, sa_t, sb_t)
        p = p.reshape(bsz, t_len, in_w)

        zeros_up = jnp.zeros((DECAY_RANK, c_a), F32)
        up_full = jnp.concatenate(
            [jnp.concatenate([decay_up[layer].astype(F32), zeros_up], axis=0).reshape(
                DECAY_RANK + ICLR_RANK, n_hp, 1, LANES),
             jnp.concatenate([zeros_up, iclr_up[layer].astype(F32)], axis=0).reshape(
                 DECAY_RANK + ICLR_RANK, n_hp, 1, LANES)], axis=2).reshape(
                     DECAY_RANK + ICLR_RANK, n_hp * 2 * LANES)
        col = lambda off: (lambda b, h: (b, 0, off // LANES + h))
        par = lambda b, h: (0, h)
        seq_spec = lambda off: pl.BlockSpec((1, t_len, LANES), col(off))
        par_spec = pl.BlockSpec((1, LANES), par)
        ya = pl.pallas_call(
            _rwkv_kernel,
            grid=(bsz, n_hp),
            in_specs=[seq_spec(0), seq_spec(c_a), seq_spec(2 * c_a),
                      pl.BlockSpec((1, t_len, LANES), lambda b, h: (b, 0, 3 * c_a // LANES)),
                      seq_spec(o_za),
                      pl.BlockSpec((DECAY_RANK + ICLR_RANK, 2 * LANES), par),
                      par_spec, par_spec, par_spec, par_spec, par_spec, par_spec, par_spec],
            out_specs=pl.BlockSpec((1, t_len, LANES), lambda b, h: (b, 0, h)),
            out_shape=jax.ShapeDtypeStruct((bsz, t_len, c_a), BF16),
            scratch_shapes=_rwkv_scratch(t_len),
            compiler_params=pltpu.CompilerParams(dimension_semantics=("arbitrary", "arbitrary"),
                                                 vmem_limit_bytes=VMEM_LIMIT),
            name="rwkv",
        )(p, p, p, p, p, up_full, row2(decay_base[layer]), row2(iclr_base[layer]),
          row2(key_norm_scale[layer]), row2(key_iclr_mix[layer]), row2(bonus[layer]),
          row2(gn_gain[layer]), row2(gn_bias[layer]))

        pad = DILATION_PATTERNS[-1][1] * ATTN_BLOCK
        yb = pl.pallas_call(
            _attn_kernel,
            grid=(bsz, c_b // LANES),
            in_specs=[seq_spec(o_q), seq_spec(o_k), seq_spec(o_v), seq_spec(o_zb)],
            out_specs=pl.BlockSpec((1, t_len, LANES), lambda b, h: (b, 0, h)),
            out_shape=jax.ShapeDtypeStruct((bsz, t_len, c_b), BF16),
            scratch_shapes=[pltpu.VMEM((t_len + pad, LANES), F32),
                            pltpu.VMEM((t_len + pad, LANES), F32),
                            pltpu.VMEM((t_len, LANES), F32),
                            pltpu.VMEM((t_len, LANES), F32),
                            pltpu.VMEM((t_len, LANES), F32)],
            compiler_params=pltpu.CompilerParams(dimension_semantics=("arbitrary", "arbitrary"),
                                                 vmem_limit_bytes=VMEM_LIMIT),
            name="attention",
        )(p, p, p, p)

        assert depth == 1
        tmo = 512
        wo = w_out[layer].astype(BF16)
        x = pl.pallas_call(
            _out_proj_kernel,
            grid=(rows // tmo,),
            in_specs=[pl.BlockSpec((tmo, c_a), lambda i: (i, 0)),
                      pl.BlockSpec((tmo, c_b), lambda i: (i, 0)),
                      pl.BlockSpec((c_a, d_model), lambda i: (0, 0)),
                      pl.BlockSpec((c_b, d_model), lambda i: (0, 0)),
                      pl.BlockSpec((tmo, d_model), lambda i: (i, 0)),
                      pl.BlockSpec((1, d_model), lambda i: (0, 0))],
            out_specs=pl.BlockSpec((tmo, d_model), lambda i: (i, 0)),
            out_shape=jax.ShapeDtypeStruct((rows, d_model), F32),
            compiler_params=pltpu.CompilerParams(dimension_semantics=("arbitrary",),
                                                 vmem_limit_bytes=VMEM_LIMIT),
            name="out_proj",
        )(ya.reshape(rows, c_a), yb.reshape(rows, c_b), wo[:c_a], wo[c_a:],
          x.reshape(rows, d_model), row2(final_gain)).reshape(bsz, t_len, d_model)
    return x
```

```python
import functools

import jax
import jax.numpy as jnp
from jax import lax
from jax.experimental import pallas as pl
from jax.experimental.pallas import tpu as pltpu

HEAD_DIM = 64
LANES = 128
SUBLANES = 8
DECAY_RANK = 64
ICLR_RANK = 64
DILATION_PATTERNS = ((128, 1), (512, 4), (2048, 16))
ATTN_BLOCK = 128
ATTN_GROUP = 4
ROPE_THETA = 500000.0
ROPE_DIM = HEAD_DIM // 4
NORM_EPS = 1e-6
GN_EPS = 64e-5
CHUNK = 64
SUB = 16
RWKV_GROUP = 8
LOG2_E = 1.4426950408889634
NEG_BIG = -1e30
VMEM_LIMIT = 56 * 1024 * 1024

F32 = jnp.float32
BF16 = jnp.bfloat16


def _dot(a, b, **kw):
    return jnp.dot(a, b, preferred_element_type=F32, **kw)


def _dot_nt(a, b):
    return lax.dot_general(a, b, (((1,), (1,)), ((), ())), preferred_element_type=F32)


def _split2(x):
    hi = x.astype(BF16)
    return hi, (x - hi.astype(F32)).astype(BF16)


def _dot_tn(a, b):
    return lax.dot_general(a, b, (((0,), (0,)), ((), ())), preferred_element_type=F32)


def _in_proj_kernel(x_ref, g_ref, w_ref, mix_ref, cos_ref, sa_ref, sb_ref, o_ref, carry_ref,
                    *, segs, shift_w, rope_lo, rope_hi, tiles_per_seq):
    i = pl.program_id(0)

    @pl.when(i == 0)
    def _():
        carry_ref[...] = jnp.zeros(carry_ref.shape, F32)

    x = x_ref[...]
    h = x * lax.rsqrt(jnp.mean(x * x, axis=-1, keepdims=True) + NORM_EPS) * g_ref[...]
    hb = h.astype(BF16)
    tm = x.shape[0]
    first = (i % tiles_per_seq) == 0
    row0 = lax.broadcasted_iota(jnp.int32, (tm, 1), 0) == 0
    for lo, hi in segs:
        p = _dot(hb, w_ref[:, lo:hi])
        if hi <= shift_w:
            old = jnp.where(first, 0.0, carry_ref[0:1, lo:hi])
            carry_ref[0:1, lo:hi] = p[tm - 1:tm, :]
            prev = jnp.where(row0, old, pltpu.roll(p, 1, axis=0))
            p = p + (prev - p) * mix_ref[:, lo:hi]
            o_ref[:, lo:hi] = p
        elif lo >= rope_lo and hi <= rope_hi:
            for j in range(lo, hi, LANES):
                t = p[:, j - lo:j - lo + LANES]
                t = (t * cos_ref[...] + pltpu.roll(t, LANES - ROPE_DIM // 2, axis=1) * sa_ref[...]
                     + pltpu.roll(t, ROPE_DIM // 2, axis=1) * sb_ref[...])
                o_ref[:, j:j + LANES] = t
        else:
            o_ref[:, lo:hi] = p


def _spread_matrix():
    j = lax.broadcasted_iota(jnp.int32, (LANES, (SUB - 1) * LANES), 0)
    col = lax.broadcasted_iota(jnp.int32, (LANES, (SUB - 1) * LANES), 1)
    sel = j == SUB * ((col % LANES) // SUB) + col // LANES
    one = jnp.where(sel, 1.0, 0.0).astype(BF16)
    return jnp.concatenate([one, one], axis=0)


def _diag_block_inverse(n_diag, spread):
    rows = n_diag.shape[0]
    ng = rows // SUBLANES
    gpb = SUB // SUBLANES
    n_hi = n_diag.astype(BF16)
    n_lo = (n_diag - n_hi.astype(F32)).astype(BF16)
    coef = _dot(jnp.concatenate([n_hi, n_lo], axis=1), spread)
    row = lax.broadcasted_iota(jnp.int32, (SUBLANES, LANES), 0)
    col = lax.broadcasted_iota(jnp.int32, (SUBLANES, LANES), 1) % HEAD_DIM
    xs = [jnp.where(col == row + (g * SUBLANES) % CHUNK, 1.0, 0.0).astype(F32)
          for g in range(ng)]
    for ss in range(SUB - 1):
        gs, rs = divmod(ss, SUBLANES)
        for blk in range(ng // gpb):
            xrow = xs[blk * gpb + gs][rs:rs + 1, :]
            for g in range(blk * gpb + (ss + 1) // SUBLANES, (blk + 1) * gpb):
                cf = coef[g * SUBLANES:(g + 1) * SUBLANES, ss * LANES:(ss + 1) * LANES]
                xs[g] = xs[g] - cf * xrow
    return jnp.concatenate(xs, axis=0)


def _rwkv_scratch(t_len):
    n_chunks = t_len // CHUNK
    return [pltpu.VMEM((2 * LANES, (SUB - 1) * LANES), BF16),
            pltpu.VMEM((t_len, LANES), BF16),
            pltpu.VMEM((t_len, LANES), F32),
            pltpu.VMEM((t_len, LANES), F32),
            pltpu.VMEM((n_chunks, 3 * LANES, LANES), BF16),
            pltpu.VMEM((n_chunks, LANES, LANES), F32)]


def _rwkv_kernel(r_ref, k_ref, v_ref, wa_ref, z_ref, up_ref, dbase_ref, abase_ref, kns_ref,
                 kim_ref, bonus_ref, gng_ref, gnb_ref, o_ref,
                 spread_ref, qt_ref, yloc_ref, bon_ref, mt_ref, nn_ref):
    t_len = r_ref.shape[1]
    c = CHUNK
    grp = RWKV_GROUP
    rows = grp * c
    n_chunks = t_len // c
    lane = lax.broadcasted_iota(jnp.int32, (c, LANES), 1)
    is_h0 = lane < HEAD_DIM
    is_h0_g = lax.broadcasted_iota(jnp.int32, (rows, LANES), 1) < HEAD_DIM
    spread_ref[...] = _spread_matrix()
    trow = lax.broadcasted_iota(jnp.int32, (c, LANES), 0)
    scol = lane % HEAD_DIM
    tb, sb = trow // SUB, scol // SUB
    strict = trow > scol
    incl = trow >= scol
    in_block = tb == sb
    ltri = (lax.broadcasted_iota(jnp.int32, (c, c), 0)
            >= lax.broadcasted_iota(jnp.int32, (c, c), 1)).astype(BF16)
    sq_r = lax.broadcasted_iota(jnp.int32, (LANES, LANES), 0)
    sq_c = lax.broadcasted_iota(jnp.int32, (LANES, LANES), 1)
    same_head = sq_r // HEAD_DIM == sq_c // HEAD_DIM
    on_diag = sq_r == sq_c

    dbase, abase = dbase_ref[...], abase_ref[...]
    kns, kim, bonus = kns_ref[...], kim_ref[...], bonus_ref[...]
    gng, gnb = gng_ref[...], gnb_ref[...]
    up_hi, up_lo = _split2(up_ref[...])
    up_hh = jnp.concatenate([up_hi, up_hi], axis=0)

    def head_sum(x):
        s0 = jnp.sum(jnp.where(is_h0_g, x, 0.0), axis=1, keepdims=True)
        s1 = jnp.sum(jnp.where(is_h0_g, 0.0, x), axis=1, keepdims=True)
        return jnp.where(is_h0_g, s0, s1)

    def block_diag(x):
        return jnp.concatenate([jnp.where(is_h0, x, 0.0), jnp.where(is_h0, 0.0, x)], axis=0)

    def pass_a(gi, tick):
        sl = pl.ds(pl.multiple_of(gi * rows, rows), rows)
        r = r_ref[0, sl, :]
        k = k_ref[0, sl, :]
        v = v_ref[0, sl, :]
        wa = wa_ref[0, sl, :]
        x_hi, x_lo = _split2(jnp.where(is_h0_g, jnp.tanh(wa), wa))
        lin = _dot(jnp.concatenate([x_hi, x_lo], axis=1), up_hh) + _dot(x_hi, up_lo)
        logit = -jax.nn.softplus(-(dbase + lin[:, :LANES])) - 0.5
        logw = -jnp.exp(logit)
        a = jax.nn.sigmoid(abase + lin[:, LANES:])
        kk = k * kns
        kk = kk / jnp.maximum(jnp.sqrt(head_sum(kk * kk)), 1e-12)
        kmod = k * (1.0 + (a - 1.0) * kim)
        b = kk * a
        bon_ref[sl, :] = head_sum(r * kmod * bonus) * v

        lw_hi = logw.astype(BF16)
        lw_mid, lw_lo = _split2(logw - lw_hi.astype(F32))
        parts = jnp.concatenate([lw_hi, lw_mid, lw_lo], axis=1)
        gs = [_dot(ltri, parts[j * c:(j + 1) * c]) for j in range(grp)]
        g = jnp.concatenate([(x[:, 2 * LANES:] + x[:, LANES:2 * LANES]) + x[:, :LANES] for x in gs],
                            axis=0)
        gl =jnp.concatenate([jnp.broadcast_to(g[(j + 1) * c - 1:(j + 1) * c, :], (c, LANES))
                              for j in range(grp)], axis=0)
        e_inv = jnp.exp(-g)
        kkd = kk * jnp.exp(g - logw)
        rd = r * jnp.exp(g)
        bi = b * e_inv
        ki = kmod * e_inv
        e_end = jnp.exp(gl - g)
        kdl = kmod * e_end
        bdl = b * e_end
        p_end = jnp.exp(gl)

        cs = [slice(j * c, (j + 1) * c) for j in range(grp)]
        aab, aak, arb, ark = [], [], [], []
        for j in range(grp):
            lhs = jnp.concatenate([kkd[cs[j]], rd[cs[j]]], axis=0).astype(BF16)
            rhs = jnp.concatenate([block_diag(bi[cs[j]]), block_diag(ki[cs[j]])],
                                  axis=0).astype(BF16)
            pair = _dot_nt(lhs, rhs)
            aab.append(jnp.where(strict, pair[:c, :LANES], 0.0))
            aak.append(jnp.where(strict, pair[:c, LANES:], 0.0))
            arb.append(jnp.where(incl, pair[c:, :LANES], 0.0))
            ark.append(jnp.where(incl, pair[c:, LANES:], 0.0))
        tick()
        js = range(grp)
        vbd = [block_diag(v[cs[j]]).astype(BF16) for j in js]
        aakv = [_dot(aak[j].astype(BF16), vbd[j]) for j in js]
        t_diag = _diag_block_inverse(
            jnp.concatenate([jnp.where(in_block, x, 0.0) for x in aab], axis=0), spread_ref[...])
        tinv = [t_diag[cs[j]] for j in js]
        tick()
        span = 1
        while span * SUB < c:
            low_mask = (tb // span == sb // span + 1) & ((tb // span) % 2 == 1)
            inner = [_dot(jnp.where(low_mask, aab[j], 0.0).astype(BF16),
                          block_diag(tinv[j]).astype(BF16)) for j in js]
            tick()
            tinv = [tinv[j] - _dot(tinv[j].astype(BF16), block_diag(inner[j]).astype(BF16))
                    for j in js]
            tick()
            span *= 2
        wu = [_dot(tinv[j].astype(BF16),
                   jnp.concatenate([block_diag(kkd[cs[j]]), block_diag(aakv[j])],
                                   axis=1).astype(BF16)) for j in js]
        tick()
        w = [x[:, :LANES] for x in wu]
        uloc = [x[:, LANES:] for x in wu]
        arb_b = [arb[j].astype(BF16) for j in js]
        bdl_b = [bdl[cs[j]].astype(BF16) for j in js]
        qt = [rd[cs[j]] - _dot(arb_b[j], block_diag(w[j]).astype(BF16)) for j in js]
        yloc = [_dot(jnp.concatenate([ark[j].astype(BF16), -arb_b[j]], axis=1),
                     jnp.concatenate([vbd[j], block_diag(uloc[j]).astype(BF16)], axis=0))
                for j in js]
        wtb = [_dot_tn(w[j].astype(BF16), bdl_b[j]) for j in js]
        nn = [_dot_tn(jnp.concatenate([v[cs[j]], -uloc[j]], axis=0).astype(BF16),
                      jnp.concatenate([kdl[cs[j]].astype(BF16), bdl_b[j]], axis=0))
              for j in js]
        tick()
        for j in js:
            ci = gi * grp + j
            csl = pl.ds(pl.multiple_of(ci * c, c), c)
            qt_ref[csl, :] = qt[j].astype(BF16)
            yloc_ref[csl, :] = yloc[j]
            mt = (jnp.where(on_diag, p_end[j * c:j * c + 1, :], 0.0)
                  - jnp.where(same_head, wtb[j], 0.0))
            mt_hi, mt_lo = _split2(mt)
            mt_ref[ci] = jnp.concatenate([mt_hi, mt_hi, mt_lo], axis=0)
            nn_ref[ci] = jnp.where(same_head, nn[j], 0.0)

    def chunk_step(ci, state):
        s_hi, s_lo = _split2(state)
        y_part = _dot_nt(qt_ref[pl.ds(pl.multiple_of(ci * c, c), c), :], s_hi)
        new_state = (_dot(jnp.concatenate([s_hi, s_lo], axis=1), mt_ref[ci, 0:2 * LANES, :])
                     + _dot(s_hi, mt_ref[ci, 2 * LANES:3 * LANES, :]) + nn_ref[ci])
        return y_part, new_state

    def finish(gi, ys):
        sl = pl.ds(pl.multiple_of(gi * rows, rows), rows)
        y = jnp.concatenate(ys, axis=0) + yloc_ref[sl, :]
        mu = head_sum(y) * (1.0 / HEAD_DIM)
        d = y - mu
        var = head_sum(d * d) * (1.0 / HEAD_DIM)
        yn = d * lax.rsqrt(var + GN_EPS) * gng + gnb + bon_ref[sl, :]
        z = z_ref[0, sl, :]
        o_ref[0, sl, :] = (yn * (z * jax.nn.sigmoid(z))).astype(o_ref.dtype)

    class Chain:
        def __init__(self, gi, state):
            self.gi, self.state, self.ys = gi, state, []

        def tick(self):
            if len(self.ys) < grp:
                y_part, self.state = chunk_step(self.gi * grp + len(self.ys), self.state)
                self.ys.append(y_part)

        def drain(self):
            while len(self.ys) < grp:
                self.tick()
            finish(self.gi, self.ys)
            return self.state

    def body(gi, state):
        chain = Chain(gi - 1, state)
        pass_a(gi, chain.tick)
        return chain.drain()

    n_groups = n_chunks // grp
    pass_a(0, lambda: None)
    state = lax.fori_loop(1, n_groups, body, jnp.zeros((LANES, LANES), F32))
    Chain(n_groups - 1, state).drain()


def _attn_kernel(q_ref, k_ref, v_ref, z_ref, o_ref, kp_ref, vp_ref, acc_ref, m_ref, l_ref):
    t_len = q_ref.shape[1]
    blk = ATTN_BLOCK
    pad = kp_ref.shape[0] - t_len
    kp_ref[0:pad, :] = jnp.zeros((pad, LANES), F32)
    vp_ref[0:pad, :] = jnp.zeros((pad, LANES), F32)
    kp_ref[pad:, :] = k_ref[0]
    vp_ref[pad:, :] = v_ref[0]

    lane = lax.broadcasted_iota(jnp.int32, (blk, LANES), 1)
    is_h0 = lane < HEAD_DIM
    qi = lax.broadcasted_iota(jnp.int32, (blk, 2 * blk), 0)
    kj = lax.broadcasted_iota(jnp.int32, (blk, 2 * blk), 1)
    scale = HEAD_DIM ** -0.5 * LOG2_E
    nb = ATTN_GROUP
    heads = (is_h0, ~is_h0)

    for pi, (window, dil) in enumerate(reversed(DILATION_PATTERNS)):
        span = window // dil
        nblk = t_len // (dil * blk)
        in_band = (qi + blk - kj >= 0) & (qi + blk - kj <= span)
        bias_mid = jnp.where(in_band, 0.0, NEG_BIG).astype(F32)
        bias_first = jnp.where(in_band & (kj >= blk), 0.0, NEG_BIG).astype(F32)

        def body(it, _, pi=pi, dil=dil, nblk=nblk, bias_mid=bias_mid, bias_first=bias_first):
            rows, krows, biases = [], [], []
            for j in range(nb):
                res = (it * nb + j) // nblk
                n = (it * nb + j) % nblk
                q0 = res + dil * blk * n
                if dil == 1:
                    q0 = pl.multiple_of(q0, blk)
                    rows.append(pl.ds(q0, blk))
                    krows.append(pl.ds(pl.multiple_of(pad + q0 - blk, blk), 2 * blk))
                else:
                    rows.append(pl.ds(q0, blk, stride=dil))
                    krows.append(pl.ds(pad + q0 - dil * blk, 2 * blk, stride=dil))
                biases.append(jnp.where(n == 0, bias_first, bias_mid))
            q = [q_ref[0, rows[j], :] * scale for j in range(nb)]
            kw = [kp_ref[krows[j], :].astype(BF16) for j in range(nb)]
            vw = [vp_ref[krows[j], :].astype(BF16) for j in range(nb)]
            s = [[_dot_nt(jnp.where(h, q[j], 0.0).astype(BF16), kw[j]) + biases[j] for h in heads]
                 for j in range(nb)]
            m = [[jnp.max(x, axis=1, keepdims=True) for x in sj] for sj in s]
            p = [[jnp.exp2(s[j][h] - m[j][h]) for h in range(2)] for j in range(nb)]
            l = [[jnp.sum(x, axis=1, keepdims=True) for x in pj] for pj in p]
            pv = [[_dot(p[j][h].astype(BF16), vw[j]) for h in range(2)] for j in range(nb)]
            for j in range(nb):
                acc = jnp.where(is_h0, pv[j][0], pv[j][1])
                mj = jnp.where(is_h0, m[j][0], m[j][1])
                lj = jnp.where(is_h0, l[j][0], l[j][1])
                if pi > 0:
                    m_old = m_ref[rows[j], :]
                    m_new = jnp.maximum(m_old, mj)
                    w_old = jnp.exp2(m_old - m_new)
                    w_cur = jnp.exp2(mj - m_new)
                    acc = w_old * acc_ref[rows[j], :] + w_cur * acc
                    lj = w_old * l_ref[rows[j], :] + w_cur * lj
                    mj = m_new
                acc_ref[rows[j], :] = acc
                m_ref[rows[j], :] = mj
                l_ref[rows[j], :] = lj
            return 0

        lax.fori_loop(0, t_len // (blk * nb), body, 0)

    def finish(it, _):
        rows = pl.ds(pl.multiple_of(it * blk, blk), blk)
        z = z_ref[0, rows, :]
        o_ref[0, rows, :] = (acc_ref[rows, :] / l_ref[rows, :]
                             * (z * jax.nn.sigmoid(z))).astype(o_ref.dtype)
        return 0

    lax.fori_loop(0, t_len // blk, finish, 0)


def _out_proj_kernel(ya_ref, yb_ref, wa_ref, wb_ref, x_ref, g_ref, o_ref):
    y = x_ref[...] + _dot(ya_ref[...], wa_ref[...]) + _dot(yb_ref[...], wb_ref[...])
    o_ref[...] = y * lax.rsqrt(jnp.mean(y * y, axis=-1, keepdims=True) + NORM_EPS) * g_ref[...]


def _rope_tables(t_len):
    half = ROPE_DIM // 2
    inv = ROPE_THETA ** (-jnp.arange(half, dtype=F32) * 2.0 / ROPE_DIM)
    ang = jnp.arange(t_len, dtype=jnp.int32).astype(F32)[:, None] * inv[None, :]
    cos, sin = jnp.cos(ang), jnp.sin(ang)
    ones = jnp.ones((t_len, HEAD_DIM - ROPE_DIM), F32)
    zeros = jnp.zeros((t_len, HEAD_DIM - ROPE_DIM), F32)
    zh = jnp.zeros((t_len, half), F32)
    tile = lambda a: jnp.tile(a, (1, LANES // HEAD_DIM))
    cos_t = tile(jnp.concatenate([cos, cos, ones], axis=1))
    sa_t = tile(jnp.concatenate([-sin, zh, zeros], axis=1))
    sb_t = tile(jnp.concatenate([zh, sin, zeros], axis=1))
    return cos_t, sa_t, sb_t


def kernel(x, norm_gain, w_in, shift_mix, decay_base, decay_up, iclr_base, iclr_up, key_norm_scale,
           key_iclr_mix, bonus, gn_gain, gn_bias, w_out, final_gain):
    bsz, t_len, d_model = x.shape
    depth = w_in.shape[0]
    c_a = decay_base.shape[1]
    n_hp = c_a // LANES
    shift_w = 3 * c_a + DECAY_RANK + ICLR_RANK
    in_w = w_in.shape[2]
    c_b = (in_w - shift_w - c_a) // 4
    assert c_a % LANES == 0 and c_b == c_a and t_len % (16 * ATTN_BLOCK) == 0
    o_za = shift_w
    o_q = o_za + c_a
    o_k, o_v, o_zb = o_q + c_b, o_q + 2 * c_b, o_q + 3 * c_b
    segs = ((0, c_a), (c_a, 2 * c_a), (2 * c_a, 3 * c_a), (3 * c_a, shift_w), (o_za, o_q),
            (o_q, o_k), (o_k, o_v), (o_v, o_zb), (o_zb, in_w))
    cos_t, sa_t, sb_t = _rope_tables(t_len)
    rows = bsz * t_len
    tm = 256
    tiles_per_seq = t_len // tm
    row2 = lambda a: a.reshape(1, -1).astype(F32)

    for layer in range(depth):
        p = pl.pallas_call(
            functools.partial(_in_proj_kernel, segs=segs, shift_w=shift_w, rope_lo=o_q,
                              rope_hi=o_v, tiles_per_seq=tiles_per_seq),
            grid=(rows // tm,),
            in_specs=[
                pl.BlockSpec((tm, d_model), lambda i: (i, 0)),
                pl.BlockSpec((1, d_model), lambda i: (0, 0)),
                pl.BlockSpec((d_model, in_w), lambda i: (0, 0)),
                pl.BlockSpec((1, shift_w), lambda i: (0, 0)),
                pl.BlockSpec((tm, LANES), lambda i: (i % tiles_per_seq, 0)),
                pl.BlockSpec((tm, LANES), lambda i: (i % tiles_per_seq, 0)),
                pl.BlockSpec((tm, LANES), lambda i: (i % tiles_per_seq, 0)),
            ],
            out_specs=pl.BlockSpec((tm, in_w), lambda i: (i, 0)),
            out_shape=jax.ShapeDtypeStruct((rows, in_w), F32),
            scratch_shapes=[pltpu.VMEM((SUBLANES, shift_w), F32)],
            compiler_params=pltpu.CompilerParams(dimension_semantics=("arbitrary",),
                                                 vmem_limit_bytes=VMEM_LIMIT),
            name="in_proj",
        )(x.reshape(rows, d_model), row2(norm_gain[layer]), w_in[layer].astype(BF16),
          row2(shift_mix[layer]), cos_t, sa_t, sb_t)
        p = p.reshape(bsz, t_len, in_w)

        zeros_up = jnp.zeros((DECAY_RANK, c_a), F32)
        up_full = jnp.concatenate(
            [jnp.concatenate([decay_up[layer].astype(F32), zeros_up], axis=0).reshape(
                DECAY_RANK + ICLR_RANK, n_hp, 1, LANES),
             jnp.concatenate([zeros_up, iclr_up[layer].astype(F32)], axis=0).reshape(
                 DECAY_RANK + ICLR_RANK, n_hp, 1, LANES)], axis=2).reshape(
                     DECAY_RANK + ICLR_RANK, n_hp * 2 * LANES)
        col = lambda off: (lambda b, h: (b, 0, off // LANES + h))
        par = lambda b, h: (0, h)
        seq_spec = lambda off: pl.BlockSpec((1, t_len, LANES), col(off))
        par_spec = pl.BlockSpec((1, LANES), par)
        ya = pl.pallas_call(
            _rwkv_kernel,
            grid=(bsz, n_hp),
            in_specs=[seq_spec(0), seq_spec(c_a), seq_spec(2 * c_a),
                      pl.BlockSpec((1, t_len, LANES), lambda b, h: (b, 0, 3 * c_a // LANES)),
                      seq_spec(o_za),
                      pl.BlockSpec((DECAY_RANK + ICLR_RANK, 2 * LANES), par),
                      par_spec, par_spec, par_spec, par_spec, par_spec, par_spec, par_spec],
            out_specs=pl.BlockSpec((1, t_len, LANES), lambda b, h: (b, 0, h)),
            out_shape=jax.ShapeDtypeStruct((bsz, t_len, c_a), BF16),
            scratch_shapes=_rwkv_scratch(t_len),
            compiler_params=pltpu.CompilerParams(dimension_semantics=("arbitrary", "arbitrary"),
                                                 vmem_limit_bytes=VMEM_LIMIT),
            name="rwkv",
        )(p, p, p, p, p, up_full, row2(decay_base[layer]), row2(iclr_base[layer]),
          row2(key_norm_scale[layer]), row2(key_iclr_mix[layer]), row2(bonus[layer]),
          row2(gn_gain[layer]), row2(gn_bias[layer]))

        pad = DILATION_PATTERNS[-1][1] * ATTN_BLOCK
        yb = pl.pallas_call(
            _attn_kernel,
            grid=(bsz, c_b // LANES),
            in_specs=[seq_spec(o_q), seq_spec(o_k), seq_spec(o_v), seq_spec(o_zb)],
            out_specs=pl.BlockSpec((1, t_len, LANES), lambda b, h: (b, 0, h)),
            out_shape=jax.ShapeDtypeStruct((bsz, t_len, c_b), BF16),
            scratch_shapes=[pltpu.VMEM((t_len + pad, LANES), F32),
                            pltpu.VMEM((t_len + pad, LANES), F32),
                            pltpu.VMEM((t_len, LANES), F32),
                            pltpu.VMEM((t_len, LANES), F32),
                            pltpu.VMEM((t_len, LANES), F32)],
            compiler_params=pltpu.CompilerParams(dimension_semantics=("arbitrary", "arbitrary"),
                                                 vmem_limit_bytes=VMEM_LIMIT),
            name="attention",
        )(p, p, p, p)

        assert depth == 1
        tmo = 512
        wo = w_out[layer].astype(BF16)
        x = pl.pallas_call(
            _out_proj_kernel,
            grid=(rows // tmo,),
            in_specs=[pl.BlockSpec((tmo, c_a), lambda i: (i, 0)),
                      pl.BlockSpec((tmo, c_b), lambda i: (i, 0)),
                      pl.BlockSpec((c_a, d_model), lambda i: (0, 0)),
                      pl.BlockSpec((c_b, d_model), lambda i: (0, 0)),
                      pl.BlockSpec((tmo, d_model), lambda i: (i, 0)),
                      pl.BlockSpec((1, d_model), lambda i: (0, 0))],
            out_specs=pl.BlockSpec((tmo, d_model), lambda i: (i, 0)),
            out_shape=jax.ShapeDtypeStruct((rows, d_model), F32),
            compiler_params=pltpu.CompilerParams(dimension_semantics=("arbitrary",),
                                                 vmem_limit_bytes=VMEM_LIMIT),
            name="out_proj",
        )(ya.reshape(rows, c_a), yb.reshape(rows, c_b), wo[:c_a], wo[c_a:],
          x.reshape(rows, d_model), row2(final_gain)).reshape(bsz, t_len, d_model)
    return x
```

```python
import functools

import jax
import jax.numpy as jnp
from jax import lax
from jax.experimental import pallas as pl
from jax.experimental.pallas import tpu as pltpu

HEAD_DIM = 64
LANES = 128
SUBLANES = 8
DECAY_RANK = 64
ICLR_RANK = 64
DILATION_PATTERNS = ((128, 1), (512, 4), (2048, 16))
ATTN_BLOCK = 128
ATTN_GROUP = 4
ROPE_THETA = 500000.0
ROPE_DIM = HEAD_DIM // 4
NORM_EPS = 1e-6
GN_EPS = 64e-5
CHUNK = 64
SUB = 16
RWKV_GROUP = 8
IN_PROJ_ROWS = 512
OUT_PROJ_ROWS = 1024
LOG2_E = 1.4426950408889634
NEG_BIG = -1e30
VMEM_LIMIT = 56 * 1024 * 1024

F32 = jnp.float32
BF16 = jnp.bfloat16


def _dot(a, b, **kw):
    return jnp.dot(a, b, preferred_element_type=F32, **kw)


def _dot_nt(a, b):
    return lax.dot_general(a, b, (((1,), (1,)), ((), ())), preferred_element_type=F32)


def _split2(x):
    hi = x.astype(BF16)
    return hi, (x - hi.astype(F32)).astype(BF16)


def _dot_tn(a, b):
    return lax.dot_general(a, b, (((0,), (0,)), ((), ())), preferred_element_type=F32)


def _in_proj_kernel(x_ref, g_ref, w_ref, mix_ref, cos_ref, sa_ref, sb_ref, o_ref, carry_ref,
                    *, segs, shift_w, rope_lo, rope_hi, tiles_per_seq):
    i = pl.program_id(0)

    @pl.when(i == 0)
    def _():
        carry_ref[...] = jnp.zeros(carry_ref.shape, F32)

    x = x_ref[...]
    h = x * lax.rsqrt(jnp.mean(x * x, axis=-1, keepdims=True) + NORM_EPS) * g_ref[...]
    hb = h.astype(BF16)
    tm = x.shape[0]
    first = (i % tiles_per_seq) == 0
    row0 = lax.broadcasted_iota(jnp.int32, (tm, 1), 0) == 0
    for lo, hi in segs:
        p = _dot(hb, w_ref[:, lo:hi])
        if hi <= shift_w:
            old = jnp.where(first, 0.0, carry_ref[0:1, lo:hi])
            carry_ref[0:1, lo:hi] = p[tm - 1:tm, :]
            prev = jnp.where(row0, old, pltpu.roll(p, 1, axis=0))
            p = p + (prev - p) * mix_ref[:, lo:hi]
            o_ref[:, lo:hi] = p
        elif lo >= rope_lo and hi <= rope_hi:
            for j in range(lo, hi, LANES):
                t = p[:, j - lo:j - lo + LANES]
                t = (t * cos_ref[...] + pltpu.roll(t, LANES - ROPE_DIM // 2, axis=1) * sa_ref[...]
                     + pltpu.roll(t, ROPE_DIM // 2, axis=1) * sb_ref[...])
                o_ref[:, j:j + LANES] = t
        else:
            o_ref[:, lo:hi] = p


def _spread_matrix():
    j = lax.broadcasted_iota(jnp.int32, (LANES, (SUB - 1) * LANES), 0)
    col = lax.broadcasted_iota(jnp.int32, (LANES, (SUB - 1) * LANES), 1)
    sel = j == SUB * ((col % LANES) // SUB) + col // LANES
    one = jnp.where(sel, 1.0, 0.0).astype(BF16)
    return jnp.concatenate([one, one], axis=0)


def _diag_block_inverse(n_diag, spread):
    rows = n_diag.shape[0]
    ng = rows // SUBLANES
    gpb = SUB // SUBLANES
    n_hi = n_diag.astype(BF16)
    n_lo = (n_diag - n_hi.astype(F32)).astype(BF16)
    coef = _dot(jnp.concatenate([n_hi, n_lo], axis=1), spread)
    row = lax.broadcasted_iota(jnp.int32, (SUBLANES, LANES), 0)
    col = lax.broadcasted_iota(jnp.int32, (SUBLANES, LANES), 1) % HEAD_DIM
    xs = [jnp.where(col == row + (g * SUBLANES) % CHUNK, 1.0, 0.0).astype(F32)
          for g in range(ng)]
    for ss in range(SUB - 1):
        gs, rs = divmod(ss, SUBLANES)
        for blk in range(ng // gpb):
            xrow = xs[blk * gpb + gs][rs:rs + 1, :]
            for g in range(blk * gpb + (ss + 1) // SUBLANES, (blk + 1) * gpb):
                cf = coef[g * SUBLANES:(g + 1) * SUBLANES, ss * LANES:(ss + 1) * LANES]
                xs[g] = xs[g] - cf * xrow
    return jnp.concatenate(xs, axis=0)


def _rwkv_scratch(t_len):
    n_chunks = t_len // CHUNK
    return [pltpu.VMEM((2 * LANES, (SUB - 1) * LANES), BF16),
            pltpu.VMEM((t_len, LANES), BF16),
            pltpu.VMEM((t_len, LANES), F32),
            pltpu.VMEM((t_len, LANES), F32),
            pltpu.VMEM((n_chunks, 3 * LANES, LANES), BF16),
            pltpu.VMEM((n_chunks, HEAD_DIM, LANES), F32)]


def _rwkv_kernel(r_ref, k_ref, v_ref, wa_ref, z_ref, up_ref, dbase_ref, abase_ref, kns_ref,
                 kim_ref, bonus_ref, gng_ref, gnb_ref, o_ref,
                 spread_ref, qt_ref, yloc_ref, bon_ref, mt_ref, nn_ref):
    t_len = r_ref.shape[1]
    c = CHUNK
    grp = RWKV_GROUP
    rows = grp * c
    n_chunks = t_len // c
    lane = lax.broadcasted_iota(jnp.int32, (c, LANES), 1)
    is_h0 = lane < HEAD_DIM
    is_h0_g = lax.broadcasted_iota(jnp.int32, (rows, LANES), 1) < HEAD_DIM
    spread_ref[...] = _spread_matrix()
    trow = lax.broadcasted_iota(jnp.int32, (c, LANES), 0)
    scol = lane % HEAD_DIM
    tb, sb = trow // SUB, scol // SUB
    strict = trow > scol
    incl = trow >= scol
    in_block = tb == sb
    ltri = (lax.broadcasted_iota(jnp.int32, (c, c), 0)
            >= lax.broadcasted_iota(jnp.int32, (c, c), 1)).astype(BF16)
    sq_r = lax.broadcasted_iota(jnp.int32, (LANES, LANES), 0)
    sq_c = lax.broadcasted_iota(jnp.int32, (LANES, LANES), 1)
    same_head = sq_r // HEAD_DIM == sq_c // HEAD_DIM
    on_diag = sq_r == sq_c

    dbase, abase = dbase_ref[...], abase_ref[...]
    kns, kim, bonus = kns_ref[...], kim_ref[...], bonus_ref[...]
    gng, gnb = gng_ref[...], gnb_ref[...]
    up_hi, up_lo = _split2(up_ref[...])
    up_hh = jnp.concatenate([up_hi, up_hi], axis=0)

    def head_sum(x):
        s0 = jnp.sum(jnp.where(is_h0_g, x, 0.0), axis=1, keepdims=True)
        s1 = jnp.sum(jnp.where(is_h0_g, 0.0, x), axis=1, keepdims=True)
        return jnp.where(is_h0_g, s0, s1)

    def block_diag(x):
        return jnp.concatenate([jnp.where(is_h0, x, 0.0), jnp.where(is_h0, 0.0, x)], axis=0)

    def pass_a(gi, tick):
        sl = pl.ds(pl.multiple_of(gi * rows, rows), rows)
        r = r_ref[0, sl, :]
        k = k_ref[0, sl, :]
        v = v_ref[0, sl, :]
        wa = wa_ref[0, sl, :]
        x_hi, x_lo = _split2(jnp.where(is_h0_g, jnp.tanh(wa), wa))
        lin = _dot(jnp.concatenate([x_hi, x_lo], axis=1), up_hh) + _dot(x_hi, up_lo)
        logit = -jax.nn.softplus(-(dbase + lin[:, :LANES])) - 0.5
        logw = -jnp.exp(logit)
        a = jax.nn.sigmoid(abase + lin[:, LANES:])
        kk = k * kns
        kk = kk / jnp.maximum(jnp.sqrt(head_sum(kk * kk)), 1e-12)
        kmod = k * (1.0 + (a - 1.0) * kim)
        b = kk * a
        bon_ref[sl, :] = head_sum(r * kmod * bonus) * v

        lw_hi = logw.astype(BF16)
        lw_mid, lw_lo = _split2(logw - lw_hi.astype(F32))
        parts = jnp.concatenate([lw_hi, lw_mid, lw_lo], axis=1)
        gs = [_dot(ltri, parts[j * c:(j + 1) * c]) for j in range(grp)]
        g = jnp.concatenate([(x[:, 2 * LANES:] + x[:, LANES:2 * LANES]) + x[:, :LANES] for x in gs],
                            axis=0)
        gl =jnp.concatenate([jnp.broadcast_to(g[(j + 1) * c - 1:(j + 1) * c, :], (c, LANES))
                              for j in range(grp)], axis=0)
        e_inv = jnp.exp(-g)
        kkd = kk * jnp.exp(g - logw)
        rd = r * jnp.exp(g)
        bi = b * e_inv
        ki = kmod * e_inv
        e_end = jnp.exp(gl - g)
        kdl = kmod * e_end
        bdl = b * e_end
        p_end = jnp.exp(gl)

        cs = [slice(j * c, (j + 1) * c) for j in range(grp)]
        aab, aak, arb, ark = [], [], [], []
        for j in range(grp):
            lhs = jnp.concatenate([kkd[cs[j]], rd[cs[j]]], axis=0).astype(BF16)
            rhs = jnp.concatenate([block_diag(bi[cs[j]]), block_diag(ki[cs[j]])],
                                  axis=0).astype(BF16)
            pair = _dot_nt(lhs, rhs)
            aab.append(jnp.where(strict, pair[:c, :LANES], 0.0))
            aak.append(jnp.where(strict, pair[:c, LANES:], 0.0))
            arb.append(jnp.where(incl, pair[c:, :LANES], 0.0))
            ark.append(jnp.where(incl, pair[c:, LANES:], 0.0))
        tick()
        js = range(grp)
        vbd = [block_diag(v[cs[j]]).astype(BF16) for j in js]
        aakv = [_dot(aak[j].astype(BF16), vbd[j]) for j in js]
        t_diag = _diag_block_inverse(
            jnp.concatenate([jnp.where(in_block, x, 0.0) for x in aab], axis=0), spread_ref[...])
        tinv = [t_diag[cs[j]] for j in js]
        tick()
        span = 1
        while span * SUB < c:
            low_mask = (tb // span == sb // span + 1) & ((tb // span) % 2 == 1)
            inner = [_dot(jnp.where(low_mask, aab[j], 0.0).astype(BF16),
                          block_diag(tinv[j]).astype(BF16)) for j in js]
            tick()
            tinv = [tinv[j] - _dot(tinv[j].astype(BF16), block_diag(inner[j]).astype(BF16))
                    for j in js]
            tick()
            span *= 2
        wu = [_dot(tinv[j].astype(BF16),
                   jnp.concatenate([block_diag(kkd[cs[j]]), block_diag(aakv[j])],
                                   axis=1).astype(BF16)) for j in js]
        tick()
        w = [x[:, :LANES] for x in wu]
        uloc = [x[:, LANES:] for x in wu]
        arb_b = [arb[j].astype(BF16) for j in js]
        bdl_b = [bdl[cs[j]].astype(BF16) for j in js]
        qt = [rd[cs[j]] - _dot(arb_b[j], block_diag(w[j]).astype(BF16)) for j in js]
        yloc = [_dot(jnp.concatenate([ark[j].astype(BF16), -arb_b[j]], axis=1),
                     jnp.concatenate([vbd[j], block_diag(uloc[j]).astype(BF16)], axis=0))
                for j in js]
        wtb = [_dot_tn(w[j].astype(BF16), bdl_b[j]) for j in js]
        nn = [_dot_tn(jnp.concatenate([v[cs[j]], -uloc[j]], axis=0).astype(BF16),
                      jnp.concatenate([kdl[cs[j]].astype(BF16), bdl_b[j]], axis=0))
              for j in js]
        tick()
        for j in js:
            ci = gi * grp + j
            csl = pl.ds(pl.multiple_of(ci * c, c), c)
            qt_ref[csl, :] = qt[j].astype(BF16)
            yloc_ref[csl, :] = yloc[j]
            mt = (jnp.where(on_diag, p_end[j * c:j * c + 1, :], 0.0)
                  - jnp.where(same_head, wtb[j], 0.0))
            mt_hi, mt_lo = _split2(mt)
            mt_ref[ci] = jnp.concatenate([mt_hi, mt_hi, mt_lo], axis=0)
            nn_ref[ci] = jnp.where(is_h0, nn[j][:HEAD_DIM], nn[j][HEAD_DIM:])

    def chunk_step(ci, state):
        s_hi, s_lo = _split2(state)
        y_part = _dot_nt(qt_ref[pl.ds(pl.multiple_of(ci * c, c), c), :], block_diag(s_hi))
        new_state = (_dot(jnp.concatenate([s_hi, s_lo], axis=1), mt_ref[ci, 0:2 * LANES, :])
                     + _dot(s_hi, mt_ref[ci, 2 * LANES:3 * LANES, :]) + nn_ref[ci])
        return y_part, new_state

    def finish(gi, ys):
        sl = pl.ds(pl.multiple_of(gi * rows, rows), rows)
        y = jnp.concatenate(ys, axis=0) + yloc_ref[sl, :]
        mu = head_sum(y) * (1.0 / HEAD_DIM)
        d = y - mu
        var = head_sum(d * d) * (1.0 / HEAD_DIM)
        yn = d * lax.rsqrt(var + GN_EPS) * gng + gnb + bon_ref[sl, :]
        z = z_ref[0, sl, :]
        o_ref[0, sl, :] = (yn * (z * jax.nn.sigmoid(z))).astype(o_ref.dtype)

    class Chain:
        def __init__(self, gi, state):
            self.gi, self.state, self.ys = gi, state, []

        def tick(self):
            if len(self.ys) < grp:
                y_part, self.state = chunk_step(self.gi * grp + len(self.ys), self.state)
                self.ys.append(y_part)

        def drain(self):
            while len(self.ys) < grp:
                self.tick()
            finish(self.gi, self.ys)
            return self.state

    def body(gi, state):
        chain = Chain(gi - 1, state)
        pass_a(gi, chain.tick)
        return chain.drain()

    n_groups = n_chunks // grp
    pass_a(0, lambda: None)
    state = lax.fori_loop(1, n_groups, body, jnp.zeros((HEAD_DIM, LANES), F32))
    Chain(n_groups - 1, state).drain()


def _attn_kernel(q_ref, k_ref, v_ref, z_ref, o_ref, kp_ref, vp_ref, kb_ref, vb_ref, acc_ref, m_ref,
                 l_ref):
    t_len = q_ref.shape[1]
    blk = ATTN_BLOCK
    pad = kp_ref.shape[0] - t_len
    kp_ref[0:pad, :] = jnp.zeros((pad, LANES), F32)
    vp_ref[0:pad, :] = jnp.zeros((pad, LANES), F32)
    kp_ref[pad:, :] = k_ref[0]
    vp_ref[pad:, :] = v_ref[0]
    kb_ref[0:blk, :] = jnp.zeros((blk, LANES), BF16)
    vb_ref[0:blk, :] = jnp.zeros((blk, LANES), BF16)
    kb_ref[blk:, :] = k_ref[0].astype(BF16)
    vb_ref[blk:, :] = v_ref[0].astype(BF16)

    lane = lax.broadcasted_iota(jnp.int32, (blk, LANES), 1)
    is_h0 = lane < HEAD_DIM
    qi = lax.broadcasted_iota(jnp.int32, (blk, 2 * blk), 0)
    kj = lax.broadcasted_iota(jnp.int32, (blk, 2 * blk), 1)
    scale = HEAD_DIM ** -0.5 * LOG2_E
    nb = ATTN_GROUP
    heads = (is_h0, ~is_h0)

    for pi, (window, dil) in enumerate(reversed(DILATION_PATTERNS)):
        span = window // dil
        nblk = t_len // (dil * blk)
        in_band = (qi + blk - kj >= 0) & (qi + blk - kj <= span)
        bias_mid = jnp.where(in_band, 0.0, NEG_BIG).astype(F32)
        bias_first = jnp.where(in_band & (kj >= blk), 0.0, NEG_BIG).astype(F32)
        bias_causal = jnp.where(kj <= qi, 0.0, NEG_BIG).astype(F32)

        def body(it, _, pi=pi, dil=dil, nblk=nblk, bias_mid=bias_mid, bias_first=bias_first,
                 bias_causal=bias_causal):
            rows, q, kw, vw, biases = [], [], [], [], []
            if nblk == 2 and nb % 2 == 0:
                for jr in range(nb // 2):
                    res = it * (nb // 2) + jr
                    seq = pl.ds(res, 2 * blk, stride=dil)
                    q_seq = q_ref[0, seq, :] * scale
                    k_seq = kp_ref[pl.ds(pad + res, 2 * blk, stride=dil), :].astype(BF16)
                    v_seq = vp_ref[pl.ds(pad + res, 2 * blk, stride=dil), :].astype(BF16)
                    for n in range(2):
                        rows.append(pl.ds(res + dil * blk * n, blk, stride=dil))
                        q.append(q_seq[n * blk:(n + 1) * blk])
                        kw.append(k_seq)
                        vw.append(v_seq)
                        biases.append(bias_mid if n else bias_causal)
            else:
                for j in range(nb):
                    res = (it * nb + j) // nblk
                    n = (it * nb + j) % nblk
                    q0 = res + dil * blk * n
                    if dil == 1:
                        q0 = pl.multiple_of(q0, blk)
                        rows.append(pl.ds(q0, blk))
                        kw.append(kb_ref[pl.ds(q0, 2 * blk), :])
                        vw.append(vb_ref[pl.ds(q0, 2 * blk), :])
                    else:
                        rows.append(pl.ds(q0, blk, stride=dil))
                        krows = pl.ds(pad + q0 - dil * blk, 2 * blk, stride=dil)
                        kw.append(kp_ref[krows, :].astype(BF16))
                        vw.append(vp_ref[krows, :].astype(BF16))
                    q.append(q_ref[0, rows[j], :] * scale)
                    biases.append(jnp.where(n == 0, bias_first, bias_mid))
            s = [[_dot_nt(jnp.where(h, q[j], 0.0).astype(BF16), kw[j]) + biases[j] for h in heads]
                 for j in range(nb)]
            m = [[jnp.max(x, axis=1, keepdims=True) for x in sj] for sj in s]
            p = [[jnp.exp2(s[j][h] - m[j][h]) for h in range(2)] for j in range(nb)]
            l = [[jnp.sum(x, axis=1, keepdims=True) for x in pj] for pj in p]
            pv = [[_dot(p[j][h].astype(BF16), vw[j]) for h in range(2)] for j in range(nb)]
            for j in range(nb):
                acc = jnp.where(is_h0, pv[j][0], pv[j][1])
                mj = jnp.where(is_h0, m[j][0], m[j][1])
                lj = jnp.where(is_h0, l[j][0], l[j][1])
                if pi > 0:
                    m_old = m_ref[rows[j], :]
                    m_new = jnp.maximum(m_old, mj)
                    w_old = jnp.exp2(m_old - m_new)
                    w_cur = jnp.exp2(mj - m_new)
                    acc = w_old * acc_ref[rows[j], :] + w_cur * acc
                    lj = w_old * l_ref[rows[j], :] + w_cur * lj
                    mj = m_new
                acc_ref[rows[j], :] = acc
                m_ref[rows[j], :] = mj
                l_ref[rows[j], :] = lj
            return 0

        lax.fori_loop(0, t_len // (blk * nb), body, 0)

    def finish(it, _):
        rows = pl.ds(pl.multiple_of(it * blk, blk), blk)
        z = z_ref[0, rows, :]
        o_ref[0, rows, :] = (acc_ref[rows, :] / l_ref[rows, :]
                             * (z * jax.nn.sigmoid(z))).astype(o_ref.dtype)
        return 0

    lax.fori_loop(0, t_len // blk, finish, 0)


def _out_proj_kernel(ya_ref, yb_ref, wa_ref, wb_ref, x_ref, g_ref, o_ref):
    y = x_ref[...] + _dot(ya_ref[...], wa_ref[...]) + _dot(yb_ref[...], wb_ref[...])
    o_ref[...] = y * lax.rsqrt(jnp.mean(y * y, axis=-1, keepdims=True) + NORM_EPS) * g_ref[...]


def _rope_tables(t_len):
    half = ROPE_DIM // 2
    inv = ROPE_THETA ** (-jnp.arange(half, dtype=F32) * 2.0 / ROPE_DIM)
    ang = jnp.arange(t_len, dtype=jnp.int32).astype(F32)[:, None] * inv[None, :]
    cos, sin = jnp.cos(ang), jnp.sin(ang)
    ones = jnp.ones((t_len, HEAD_DIM - ROPE_DIM), F32)
    zeros = jnp.zeros((t_len, HEAD_DIM - ROPE_DIM), F32)
    zh = jnp.zeros((t_len, half), F32)
    tile = lambda a: jnp.tile(a, (1, LANES // HEAD_DIM))
    cos_t = tile(jnp.concatenate([cos, cos, ones], axis=1))
    sa_t = tile(jnp.concatenate([-sin, zh, zeros], axis=1))
    sb_t = tile(jnp.concatenate([zh, sin, zeros], axis=1))
    return cos_t, sa_t, sb_t


def kernel(x, norm_gain, w_in, shift_mix, decay_base, decay_up, iclr_base, iclr_up, key_norm_scale,
           key_iclr_mix, bonus, gn_gain, gn_bias, w_out, final_gain):
    bsz, t_len, d_model = x.shape
    depth = w_in.shape[0]
    c_a = decay_base.shape[1]
    n_hp = c_a // LANES
    shift_w = 3 * c_a + DECAY_RANK + ICLR_RANK
    in_w = w_in.shape[2]
    c_b = (in_w - shift_w - c_a) // 4
    assert c_a % LANES == 0 and c_b == c_a and t_len % (16 * ATTN_BLOCK) == 0
    o_za = shift_w
    o_q = o_za + c_a
    o_k, o_v, o_zb = o_q + c_b, o_q + 2 * c_b, o_q + 3 * c_b
    segs = ((0, c_a), (c_a, 2 * c_a), (2 * c_a, 3 * c_a), (3 * c_a, shift_w), (o_za, o_q),
            (o_q, o_k), (o_k, o_v), (o_v, o_zb), (o_zb, in_w))
    cos_t, sa_t, sb_t = _rope_tables(t_len)
    rows = bsz * t_len
    tm = IN_PROJ_ROWS
    tiles_per_seq = t_len // tm
    row2 = lambda a: a.reshape(1, -1).astype(F32)

    for layer in range(depth):
        p = pl.pallas_call(
            functools.partial(_in_proj_kernel, segs=segs, shift_w=shift_w, rope_lo=o_q,
                              rope_hi=o_v, tiles_per_seq=tiles_per_seq),
            grid=(rows // tm,),
            in_specs=[
                pl.BlockSpec((tm, d_model), lambda i: (i, 0)),
                pl.BlockSpec((1, d_model), lambda i: (0, 0)),
                pl.BlockSpec((d_model, in_w), lambda i: (0, 0)),
                pl.BlockSpec((1, shift_w), lambda i: (0, 0)),
                pl.BlockSpec((tm, LANES), lambda i: (i % tiles_per_seq, 0)),
                pl.BlockSpec((tm, LANES), lambda i: (i % tiles_per_seq, 0)),
                pl.BlockSpec((tm, LANES), lambda i: (i % tiles_per_seq, 0)),
            ],
            out_specs=pl.BlockSpec((tm, in_w), lambda i: (i, 0)),
            out_shape=jax.ShapeDtypeStruct((rows, in_w), F32),
            scratch_shapes=[pltpu.VMEM((SUBLANES, shift_w), F32)],
            compiler_params=pltpu.CompilerParams(dimension_semantics=("arbitrary",),
                                                 vmem_limit_bytes=VMEM_LIMIT),
            name="in_proj",
        )(x.reshape(rows, d_model), row2(norm_gain[layer]), w_in[layer].astype(BF16),
          row2(shift_mix[layer]), cos_t, sa_t, sb_t)
        p = p.reshape(bsz, t_len, in_w)

        zeros_up = jnp.zeros((DECAY_RANK, c_a), F32)
        up_full = jnp.concatenate(
            [jnp.concatenate([decay_up[layer].astype(F32), zeros_up], axis=0).reshape(
                DECAY_RANK + ICLR_RANK, n_hp, 1, LANES),
             jnp.concatenate([zeros_up, iclr_up[layer].astype(F32)], axis=0).reshape(
                 DECAY_RANK + ICLR_RANK, n_hp, 1, LANES)], axis=2).reshape(
                     DECAY_RANK + ICLR_RANK, n_hp * 2 * LANES)
        col = lambda off: (lambda b, h: (b, 0, off // LANES + h))
        par = lambda b, h: (0, h)
        seq_spec = lambda off: pl.BlockSpec((1, t_len, LANES), col(off))
        par_spec = pl.BlockSpec((1, LANES), par)
        ya = pl.pallas_call(
            _rwkv_kernel,
            grid=(bsz, n_hp),
            in_specs=[seq_spec(0), seq_spec(c_a), seq_spec(2 * c_a),
                      pl.BlockSpec((1, t_len, LANES), lambda b, h: (b, 0, 3 * c_a // LANES)),
                      seq_spec(o_za),
                      pl.BlockSpec((DECAY_RANK + ICLR_RANK, 2 * LANES), par),
                      par_spec, par_spec, par_spec, par_spec, par_spec, par_spec, par_spec],
            out_specs=pl.BlockSpec((1, t_len, LANES), lambda b, h: (b, 0, h)),
            out_shape=jax.ShapeDtypeStruct((bsz, t_len, c_a), BF16),
            scratch_shapes=_rwkv_scratch(t_len),
            compiler_params=pltpu.CompilerParams(dimension_semantics=("arbitrary", "arbitrary"),
                                                 vmem_limit_bytes=VMEM_LIMIT),
            name="rwkv",
        )(p, p, p, p, p, up_full, row2(decay_base[layer]), row2(iclr_base[layer]),
          row2(key_norm_scale[layer]), row2(key_iclr_mix[layer]), row2(bonus[layer]),
          row2(gn_gain[layer]), row2(gn_bias[layer]))

        pad = DILATION_PATTERNS[-1][1] * ATTN_BLOCK
        yb = pl.pallas_call(
            _attn_kernel,
            grid=(bsz, c_b // LANES),
            in_specs=[seq_spec(o_q), seq_spec(o_k), seq_spec(o_v), seq_spec(o_zb)],
            out_specs=pl.BlockSpec((1, t_len, LANES), lambda b, h: (b, 0, h)),
            out_shape=jax.ShapeDtypeStruct((bsz, t_len, c_b), BF16),
            scratch_shapes=[pltpu.VMEM((t_len + pad, LANES), F32),
                            pltpu.VMEM((t_len + pad, LANES), F32),
                            pltpu.VMEM((t_len + ATTN_BLOCK, LANES), BF16),
                            pltpu.VMEM((t_len + ATTN_BLOCK, LANES), BF16),
                            pltpu.VMEM((t_len, LANES), F32),
                            pltpu.VMEM((t_len, LANES), F32),
                            pltpu.VMEM((t_len, LANES), F32)],
            compiler_params=pltpu.CompilerParams(dimension_semantics=("arbitrary", "arbitrary"),
                                                 vmem_limit_bytes=VMEM_LIMIT),
            name="attention",
        )(p, p, p, p)

        assert depth == 1
        tmo = OUT_PROJ_ROWS
        wo = w_out[layer].astype(BF16)
        x = pl.pallas_call(
            _out_proj_kernel,
            grid=(rows // tmo,),
            in_specs=[pl.BlockSpec((tmo, c_a), lambda i: (i, 0)),
                      pl.BlockSpec((tmo, c_b), lambda i: (i, 0)),
                      pl.BlockSpec((c_a, d_model), lambda i: (0, 0)),
                      pl.BlockSpec((c_b, d_model), lambda i: (0, 0)),
                      pl.BlockSpec((tmo, d_model), lambda i: (i, 0)),
                      pl.BlockSpec((1, d_model), lambda i: (0, 0))],
            out_specs=pl.BlockSpec((tmo, d_model), lambda i: (i, 0)),
            out_shape=jax.ShapeDtypeStruct((rows, d_model), F32),
            compiler_params=pltpu.CompilerParams(dimension_semantics=("arbitrary",),
                                                 vmem_limit_bytes=VMEM_LIMIT),
            name="out_proj",
        )(ya.reshape(rows, c_a), yb.reshape(rows, c_b), wo[:c_a], wo[c_a:],
          x.reshape(rows, d_model), row2(final_gain)).reshape(bsz, t_len, d_model)
    return x
```

```python
import functools

import jax
import jax.numpy as jnp
from jax import lax
from jax.experimental import pallas as pl
from jax.experimental.pallas import tpu as pltpu

HEAD_DIM = 64
LANES = 128
SUBLANES = 8
DECAY_RANK = 64
ICLR_RANK = 64
DILATION_PATTERNS = ((128, 1), (512, 4), (2048, 16))
ATTN_BLOCK = 128
ATTN_GROUP = 4
ROPE_THETA = 500000.0
ROPE_DIM = HEAD_DIM // 4
NORM_EPS = 1e-6
GN_EPS = 64e-5
CHUNK = 64
SUB = 16
RWKV_GROUP = 8
IN_PROJ_ROWS = 512
OUT_PROJ_ROWS = 1024
LOG2_E = 1.4426950408889634
NEG_BIG = -1e30
VMEM_LIMIT = 56 * 1024 * 1024

F32 = jnp.float32
BF16 = jnp.bfloat16


def _dot(a, b, **kw):
    return jnp.dot(a, b, preferred_element_type=F32, **kw)


def _dot_nt(a, b):
    return lax.dot_general(a, b, (((1,), (1,)), ((), ())), preferred_element_type=F32)


def _split2(x):
    hi = x.astype(BF16)
    return hi, (x - hi.astype(F32)).astype(BF16)


def _dot_tn(a, b):
    return lax.dot_general(a, b, (((0,), (0,)), ((), ())), preferred_element_type=F32)


def _in_proj_kernel(x_ref, g_ref, w_ref, mix_ref, cos_ref, sa_ref, sb_ref, o_ref, carry_ref,
                    *, segs, shift_w, rope_lo, rope_hi, tiles_per_seq):
    i = pl.program_id(0)

    @pl.when(i == 0)
    def _():
        carry_ref[...] = jnp.zeros(carry_ref.shape, F32)

    x = x_ref[...]
    h = x * lax.rsqrt(jnp.mean(x * x, axis=-1, keepdims=True) + NORM_EPS) * g_ref[...]
    hb = h.astype(BF16)
    tm = x.shape[0]
    first = (i % tiles_per_seq) == 0
    row0 = lax.broadcasted_iota(jnp.int32, (tm, 1), 0) == 0
    for lo, hi in segs:
        p = _dot(hb, w_ref[:, lo:hi])
        if hi <= shift_w:
            old = jnp.where(first, 0.0, carry_ref[0:1, lo:hi])
            carry_ref[0:1, lo:hi] = p[tm - 1:tm, :]
            prev = jnp.where(row0, old, pltpu.roll(p, 1, axis=0))
            p = p + (prev - p) * mix_ref[:, lo:hi]
            o_ref[:, lo:hi] = p
        elif lo >= rope_lo and hi <= rope_hi:
            for j in range(lo, hi, LANES):
                t = p[:, j - lo:j - lo + LANES]
                t = (t * cos_ref[...] + pltpu.roll(t, LANES - ROPE_DIM // 2, axis=1) * sa_ref[...]
                     + pltpu.roll(t, ROPE_DIM // 2, axis=1) * sb_ref[...])
                o_ref[:, j:j + LANES] = t
        else:
            o_ref[:, lo:hi] = p


def _spread_matrix():
    j = lax.broadcasted_iota(jnp.int32, (LANES, (SUB - 1) * LANES), 0)
    col = lax.broadcasted_iota(jnp.int32, (LANES, (SUB - 1) * LANES), 1)
    sel = j == SUB * ((col % LANES) // SUB) + col // LANES
    one = jnp.where(sel, 1.0, 0.0).astype(BF16)
    return jnp.concatenate([one, one], axis=0)


def _diag_block_inverse(n_diag, spread):
    rows = n_diag.shape[0]
    ng = rows // SUBLANES
    gpb = SUB // SUBLANES
    n_hi = n_diag.astype(BF16)
    n_lo = (n_diag - n_hi.astype(F32)).astype(BF16)
    coef = _dot(jnp.concatenate([n_hi, n_lo], axis=1), spread)
    row = lax.broadcasted_iota(jnp.int32, (SUBLANES, LANES), 0)
    col = lax.broadcasted_iota(jnp.int32, (SUBLANES, LANES), 1) % HEAD_DIM
    xs = [jnp.where(col == row + (g * SUBLANES) % CHUNK, 1.0, 0.0).astype(F32)
          for g in range(ng)]
    for ss in range(SUB - 1):
        gs, rs = divmod(ss, SUBLANES)
        for blk in range(ng // gpb):
            xrow = xs[blk * gpb + gs][rs:rs + 1, :]
            for g in range(blk * gpb + (ss + 1) // SUBLANES, (blk + 1) * gpb):
                cf = coef[g * SUBLANES:(g + 1) * SUBLANES, ss * LANES:(ss + 1) * LANES]
                xs[g] = xs[g] - cf * xrow
    return jnp.concatenate(xs, axis=0)


def _rwkv_scratch(t_len):
    n_chunks = t_len // CHUNK
    return [pltpu.VMEM((2 * LANES, (SUB - 1) * LANES), BF16),
            pltpu.VMEM((t_len, LANES), BF16),
            pltpu.VMEM((t_len, LANES), F32),
            pltpu.VMEM((t_len, LANES), F32),
            pltpu.VMEM((n_chunks, 3 * LANES, LANES), BF16),
            pltpu.VMEM((n_chunks, HEAD_DIM, LANES), F32),
            pltpu.VMEM((n_chunks, SUBLANES, LANES), F32)]


def _rwkv_kernel(r_ref, k_ref, v_ref, wa_ref, z_ref, up_ref, dbase_ref, abase_ref, kns_ref,
                 kim_ref, bonus_ref, gng_ref, gnb_ref, o_ref,
                 spread_ref, qt_ref, yloc_ref, bon_ref, mt_ref, nn_ref, pe_ref):
    t_len = r_ref.shape[1]
    c = CHUNK
    grp = RWKV_GROUP
    rows = grp * c
    n_chunks = t_len // c
    lane = lax.broadcasted_iota(jnp.int32, (c, LANES), 1)
    is_h0 = lane < HEAD_DIM
    is_h0_g = lax.broadcasted_iota(jnp.int32, (rows, LANES), 1) < HEAD_DIM
    spread_ref[...] = _spread_matrix()
    trow = lax.broadcasted_iota(jnp.int32, (c, LANES), 0)
    scol = lane % HEAD_DIM
    tb, sb = trow // SUB, scol // SUB
    strict = trow > scol
    incl = trow >= scol
    in_block = tb == sb
    ltri = (lax.broadcasted_iota(jnp.int32, (c, c), 0)
            >= lax.broadcasted_iota(jnp.int32, (c, c), 1)).astype(BF16)
    sq_r = lax.broadcasted_iota(jnp.int32, (LANES, LANES), 0)
    sq_c = lax.broadcasted_iota(jnp.int32, (LANES, LANES), 1)
    same_head = sq_r // HEAD_DIM == sq_c // HEAD_DIM

    dbase, abase = dbase_ref[...], abase_ref[...]
    kns, kim, bonus = kns_ref[...], kim_ref[...], bonus_ref[...]
    gng, gnb = gng_ref[...], gnb_ref[...]
    up_hi, up_lo = _split2(up_ref[...])
    up_hh = jnp.concatenate([up_hi, up_hi], axis=0)

    def head_sum(x):
        s0 = jnp.sum(jnp.where(is_h0_g, x, 0.0), axis=1, keepdims=True)
        s1 = jnp.sum(jnp.where(is_h0_g, 0.0, x), axis=1, keepdims=True)
        return jnp.where(is_h0_g, s0, s1)

    def block_diag(x):
        zero = jnp.zeros_like(x)
        return jnp.concatenate([jnp.where(is_h0, x, zero), jnp.where(is_h0, zero, x)],
                               axis=0).astype(BF16)

    def pass_a(gi, tick):
        sl = pl.ds(pl.multiple_of(gi * rows, rows), rows)
        r = r_ref[0, sl, :]
        k = k_ref[0, sl, :]
        v = v_ref[0, sl, :]
        wa = wa_ref[0, sl, :]
        x_hi, x_lo = _split2(jnp.where(is_h0_g, jnp.tanh(wa), wa))
        lin = _dot(jnp.concatenate([x_hi, x_lo], axis=1), up_hh) + _dot(x_hi, up_lo)
        logit = -jax.nn.softplus(-(dbase + lin[:, :LANES])) - 0.5
        logw = -jnp.exp(logit)
        a = jax.nn.sigmoid(abase + lin[:, LANES:])
        kk = k * kns
        kk = kk / jnp.maximum(jnp.sqrt(head_sum(kk * kk)), 1e-12)
        kmod = k * (1.0 + (a - 1.0) * kim)
        b = kk * a
        bon_ref[sl, :] = head_sum(r * kmod * bonus) * v

        lw_hi = logw.astype(BF16)
        lw_mid, lw_lo = _split2(logw - lw_hi.astype(F32))
        parts = jnp.concatenate([lw_hi, lw_mid, lw_lo], axis=1)
        gs = [_dot(ltri, parts[j * c:(j + 1) * c]) for j in range(grp)]
        g = jnp.concatenate([(x[:, 2 * LANES:] + x[:, LANES:2 * LANES]) + x[:, :LANES] for x in gs],
                            axis=0)
        gl =jnp.concatenate([jnp.broadcast_to(g[(j + 1) * c - 1:(j + 1) * c, :], (c, LANES))
                              for j in range(grp)], axis=0)
        e_inv = jnp.exp(-g)
        kkd = kk * jnp.exp(g - logw)
        rd = r * jnp.exp(g)
        bi = b * e_inv
        ki = kmod * e_inv
        e_end = jnp.exp(gl - g)
        kdl = kmod * e_end
        bdl = b * e_end
        p_end = jnp.exp(gl)

        cs = [slice(j * c, (j + 1) * c) for j in range(grp)]
        aab, aak, arb, ark = [], [], [], []
        for j in range(grp):
            lhs = jnp.concatenate([kkd[cs[j]], rd[cs[j]]], axis=0).astype(BF16)
            rhs = jnp.concatenate([block_diag(bi[cs[j]]), block_diag(ki[cs[j]])], axis=0)
            pair = _dot_nt(lhs, rhs)
            aab.append(jnp.where(strict, pair[:c, :LANES], 0.0))
            aak.append(jnp.where(strict, pair[:c, LANES:], 0.0))
            arb.append(jnp.where(incl, pair[c:, :LANES], 0.0))
            ark.append(jnp.where(incl, pair[c:, LANES:], 0.0))
        tick()
        js = range(grp)
        vbd = [block_diag(v[cs[j]]) for j in js]
        aakv = [_dot(aak[j].astype(BF16), vbd[j]) for j in js]
        t_diag = _diag_block_inverse(
            jnp.concatenate([jnp.where(in_block, x, 0.0) for x in aab], axis=0), spread_ref[...])
        tinv = [t_diag[cs[j]] for j in js]
        tick()
        span = 1
        while span * SUB < c:
            low_mask = (tb // span == sb // span + 1) & ((tb // span) % 2 == 1)
            inner = [_dot(jnp.where(low_mask, aab[j], 0.0).astype(BF16),
                          block_diag(tinv[j])) for j in js]
            tick()
            tinv = [tinv[j] - _dot(tinv[j].astype(BF16), block_diag(inner[j])) for j in js]
            tick()
            span *= 2
        wu = [_dot(tinv[j].astype(BF16),
                   jnp.concatenate([block_diag(kkd[cs[j]]), block_diag(aakv[j])], axis=1))
              for j in js]
        tick()
        w = [x[:, :LANES] for x in wu]
        uloc = [x[:, LANES:] for x in wu]
        arb_b = [arb[j].astype(BF16) for j in js]
        bdl_b = [bdl[cs[j]].astype(BF16) for j in js]
        qt = [rd[cs[j]] - _dot(arb_b[j], block_diag(w[j])) for j in js]
        yloc = [_dot(jnp.concatenate([ark[j].astype(BF16), -arb_b[j]], axis=1),
                     jnp.concatenate([vbd[j], block_diag(uloc[j])], axis=0))
                for j in js]
        wtb = [_dot_tn(w[j].astype(BF16), bdl_b[j]) for j in js]
        nn = [_dot_tn(jnp.concatenate([v[cs[j]], -uloc[j]], axis=0).astype(BF16),
                      jnp.concatenate([kdl[cs[j]].astype(BF16), bdl_b[j]], axis=0))
              for j in js]
        tick()
        for j in js:
            ci = gi * grp + j
            csl = pl.ds(pl.multiple_of(ci * c, c), c)
            qt_ref[csl, :] = qt[j].astype(BF16)
            yloc_ref[csl, :] = yloc[j]
            mt_hi, mt_lo = _split2(jnp.where(same_head, wtb[j], 0.0))
            mt_ref[ci] = jnp.concatenate([mt_hi, mt_hi, mt_lo], axis=0)
            pe_ref[ci] = p_end[j * c:j * c + SUBLANES, :]
            nn_ref[ci] = jnp.where(is_h0, nn[j][:HEAD_DIM], nn[j][HEAD_DIM:])

    def chunk_step(ci, state):
        s_hi, s_lo = _split2(state)
        y_part = _dot_nt(qt_ref[pl.ds(pl.multiple_of(ci * c, c), c), :], block_diag(s_hi))
        new_state = (state * pe_ref[ci, 0:1, :] + nn_ref[ci]
                     - (_dot(jnp.concatenate([s_hi, s_lo], axis=1), mt_ref[ci, 0:2 * LANES, :])
                        + _dot(s_hi, mt_ref[ci, 2 * LANES:3 * LANES, :])))
        return y_part, new_state

    def finish(gi, ys):
        sl = pl.ds(pl.multiple_of(gi * rows, rows), rows)
        y = jnp.concatenate(ys, axis=0) + yloc_ref[sl, :]
        mu = head_sum(y) * (1.0 / HEAD_DIM)
        d = y - mu
        var = head_sum(d * d) * (1.0 / HEAD_DIM)
        yn = d * lax.rsqrt(var + GN_EPS) * gng + gnb + bon_ref[sl, :]
        z = z_ref[0, sl, :]
        o_ref[0, sl, :] = (yn * (z * jax.nn.sigmoid(z))).astype(o_ref.dtype)

    class Chain:
        def __init__(self, gi, state):
            self.gi, self.state, self.ys = gi, state, []

        def tick(self):
            if len(self.ys) < grp:
                y_part, self.state = chunk_step(self.gi * grp + len(self.ys), self.state)
                self.ys.append(y_part)

        def drain(self):
            while len(self.ys) < grp:
                self.tick()
            finish(self.gi, self.ys)
            return self.state

    def body(gi, state):
        chain = Chain(gi - 1, state)
        pass_a(gi, chain.tick)
        return chain.drain()

    n_groups = n_chunks // grp
    pass_a(0, lambda: None)
    state = lax.fori_loop(1, n_groups, body, jnp.zeros((HEAD_DIM, LANES), F32))
    Chain(n_groups - 1, state).drain()


def _attn_kernel(q_ref, k_ref, v_ref, z_ref, o_ref, kb_ref, vb_ref, acc_ref, m_ref, l_ref):
    t_len = q_ref.shape[1]
    blk = ATTN_BLOCK
    kb_ref[...] = k_ref[0].astype(BF16)
    vb_ref[...] = v_ref[0].astype(BF16)

    lane = lax.broadcasted_iota(jnp.int32, (blk, LANES), 1)
    is_h0 = lane < HEAD_DIM
    qi = lax.broadcasted_iota(jnp.int32, (blk, 2 * blk), 0)
    kj = lax.broadcasted_iota(jnp.int32, (blk, 2 * blk), 1)
    scale = HEAD_DIM ** -0.5 * LOG2_E
    nb = ATTN_GROUP
    heads = (is_h0, ~is_h0)
    patterns = tuple(reversed(DILATION_PATTERNS))
    assert patterns[-1][1] == 1
    bias_causal = jnp.where(kj <= qi, 0.0, NEG_BIG).astype(F32)

    for pi, (window, dil) in enumerate(patterns):
        span = window // dil
        nblk = t_len // (dil * blk)
        bias_band = jnp.where((qi + blk - kj >= 0) & (qi + blk - kj <= span), 0.0,
                              NEG_BIG).astype(F32)
        last = pi == len(patterns) - 1

        def body(it, _, pi=pi, dil=dil, nblk=nblk, bias_band=bias_band, last=last):
            rows, q, kw, vw, biases = [], [], [], [], []
            if nblk == 2 and nb % 2 == 0:
                for jr in range(nb // 2):
                    res = it * (nb // 2) + jr
                    seq = pl.ds(res, 2 * blk, stride=dil)
                    q_seq = q_ref[0, seq, :] * scale
                    k_seq = k_ref[0, seq, :].astype(BF16)
                    v_seq = v_ref[0, seq, :].astype(BF16)
                    for n in range(2):
                        rows.append(pl.ds(res + dil * blk * n, blk, stride=dil))
                        q.append(q_seq[n * blk:(n + 1) * blk])
                        kw.append(k_seq)
                        vw.append(v_seq)
                        biases.append(bias_band if n else bias_causal)
            else:
                for j in range(nb):
                    res = (it * nb + j) // nblk
                    n = (it * nb + j) % nblk
                    maybe_first = j == 0 or nblk % nb != 0
                    n_prev = jnp.maximum(n - 1, 0) if maybe_first else n - 1
                    q0 = res + dil * blk * n
                    k0 = res + dil * blk * n_prev
                    if dil == 1:
                        rows.append(pl.ds(pl.multiple_of(q0, blk), blk))
                        krows = pl.ds(pl.multiple_of(k0, blk), 2 * blk)
                        kw.append(kb_ref[krows, :])
                        vw.append(vb_ref[krows, :])
                    else:
                        rows.append(pl.ds(q0, blk, stride=dil))
                        krows = pl.ds(k0, 2 * blk, stride=dil)
                        kw.append(k_ref[0, krows, :].astype(BF16))
                        vw.append(v_ref[0, krows, :].astype(BF16))
                    q.append(q_ref[0, rows[j], :] * scale)
                    biases.append(jnp.where(n == 0, bias_causal, bias_band) if maybe_first
                                  else bias_band)
            s = [[_dot_nt(jnp.where(h, q[j], 0.0).astype(BF16), kw[j]) + biases[j] for h in heads]
                 for j in range(nb)]
            m = [[jnp.max(x, axis=1, keepdims=True) for x in sj] for sj in s]
            p = [[jnp.exp2(s[j][h] - m[j][h]) for h in range(2)] for j in range(nb)]
            l = [[jnp.sum(x, axis=1, keepdims=True) for x in pj] for pj in p]
            pv = [[_dot(p[j][h].astype(BF16), vw[j]) for h in range(2)] for j in range(nb)]
            for j in range(nb):
                acc = jnp.where(is_h0, pv[j][0], pv[j][1])
                mj = jnp.where(is_h0, m[j][0], m[j][1])
                lj = jnp.where(is_h0, l[j][0], l[j][1])
                if pi > 0:
                    m_old = m_ref[rows[j], :]
                    m_new = jnp.maximum(m_old, mj)
                    w_old = jnp.exp2(m_old - m_new)
                    w_cur = jnp.exp2(mj - m_new)
                    acc = w_old * acc_ref[rows[j], :] + w_cur * acc
                    lj = w_old * l_ref[rows[j], :] + w_cur * lj
                    mj = m_new
                if last:
                    z = z_ref[0, rows[j], :]
                    o_ref[0, rows[j], :] = (acc / lj * (z * jax.nn.sigmoid(z))).astype(o_ref.dtype)
                else:
                    acc_ref[rows[j], :] = acc
                    m_ref[rows[j], :] = mj
                    l_ref[rows[j], :] = lj
            return 0

        lax.fori_loop(0, t_len // (blk * nb), body, 0)


def _out_proj_kernel(ya_ref, yb_ref, wa_ref, wb_ref, x_ref, g_ref, o_ref):
    y = x_ref[...] + _dot(ya_ref[...], wa_ref[...]) + _dot(yb_ref[...], wb_ref[...])
    o_ref[...] = y * lax.rsqrt(jnp.mean(y * y, axis=-1, keepdims=True) + NORM_EPS) * g_ref[...]


def _rope_tables(t_len):
    half = ROPE_DIM // 2
    inv = ROPE_THETA ** (-jnp.arange(half, dtype=F32) * 2.0 / ROPE_DIM)
    ang = jnp.arange(t_len, dtype=jnp.int32).astype(F32)[:, None] * inv[None, :]
    cos, sin = jnp.cos(ang), jnp.sin(ang)
    ones = jnp.ones((t_len, HEAD_DIM - ROPE_DIM), F32)
    zeros = jnp.zeros((t_len, HEAD_DIM - ROPE_DIM), F32)
    zh = jnp.zeros((t_len, half), F32)
    tile = lambda a: jnp.tile(a, (1, LANES // HEAD_DIM))
    cos_t = tile(jnp.concatenate([cos, cos, ones], axis=1))
    sa_t = tile(jnp.concatenate([-sin, zh, zeros], axis=1))
    sb_t = tile(jnp.concatenate([zh, sin, zeros], axis=1))
    return cos_t, sa_t, sb_t


def kernel(x, norm_gain, w_in, shift_mix, decay_base, decay_up, iclr_base, iclr_up, key_norm_scale,
           key_iclr_mix, bonus, gn_gain, gn_bias, w_out, final_gain):
    bsz, t_len, d_model = x.shape
    depth = w_in.shape[0]
    c_a = decay_base.shape[1]
    n_hp = c_a // LANES
    shift_w = 3 * c_a + DECAY_RANK + ICLR_RANK
    in_w = w_in.shape[2]
    c_b = (in_w - shift_w - c_a) // 4
    assert c_a % LANES == 0 and c_b == c_a
    assert all(t_len % (2 * dil * ATTN_BLOCK) == 0 for _, dil in DILATION_PATTERNS)
    o_za = shift_w
    o_q = o_za + c_a
    o_k, o_v, o_zb = o_q + c_b, o_q + 2 * c_b, o_q + 3 * c_b
    segs = ((0, c_a), (c_a, 2 * c_a), (2 * c_a, 3 * c_a), (3 * c_a, shift_w), (o_za, o_q),
            (o_q, o_k), (o_k, o_v), (o_v, o_zb), (o_zb, in_w))
    cos_t, sa_t, sb_t = _rope_tables(t_len)
    rows = bsz * t_len
    tm = IN_PROJ_ROWS
    tiles_per_seq = t_len // tm
    row2 = lambda a: a.reshape(1, -1).astype(F32)

    for layer in range(depth):
        p = pl.pallas_call(
            functools.partial(_in_proj_kernel, segs=segs, shift_w=shift_w, rope_lo=o_q,
                              rope_hi=o_v, tiles_per_seq=tiles_per_seq),
            grid=(rows // tm,),
            in_specs=[
                pl.BlockSpec((tm, d_model), lambda i: (i, 0)),
                pl.BlockSpec((1, d_model), lambda i: (0, 0)),
                pl.BlockSpec((d_model, in_w), lambda i: (0, 0)),
                pl.BlockSpec((1, shift_w), lambda i: (0, 0)),
                pl.BlockSpec((tm, LANES), lambda i: (i % tiles_per_seq, 0)),
                pl.BlockSpec((tm, LANES), lambda i: (i % tiles_per_seq, 0)),
                pl.BlockSpec((tm, LANES), lambda i: (i % tiles_per_seq, 0)),
            ],
            out_specs=pl.BlockSpec((tm, in_w), lambda i: (i, 0)),
            out_shape=jax.ShapeDtypeStruct((rows, in_w), F32),
            scratch_shapes=[pltpu.VMEM((SUBLANES, shift_w), F32)],
            compiler_params=pltpu.CompilerParams(dimension_semantics=("arbitrary",),
                                                 vmem_limit_bytes=VMEM_LIMIT),
            name="in_proj",
        )(x.reshape(rows, d_model), row2(norm_gain[layer]), w_in[layer].astype(BF16),
          row2(shift_mix[layer]), cos_t, sa_t, sb_t)
        p = p.reshape(bsz, t_len, in_w)

        zeros_up = jnp.zeros((DECAY_RANK, c_a), F32)
        up_full = jnp.concatenate(
            [jnp.concatenate([decay_up[layer].astype(F32), zeros_up], axis=0).reshape(
                DECAY_RANK + ICLR_RANK, n_hp, 1, LANES),
             jnp.concatenate([zeros_up, iclr_up[layer].astype(F32)], axis=0).reshape(
                 DECAY_RANK + ICLR_RANK, n_hp, 1, LANES)], axis=2).reshape(
                     DECAY_RANK + ICLR_RANK, n_hp * 2 * LANES)
        col = lambda off: (lambda b, h: (b, 0, off // LANES + h))
        par = lambda b, h: (0, h)
        seq_spec = lambda off: pl.BlockSpec((1, t_len, LANES), col(off))
        par_spec = pl.BlockSpec((1, LANES), par)
        ya = pl.pallas_call(
            _rwkv_kernel,
            grid=(bsz, n_hp),
            in_specs=[seq_spec(0), seq_spec(c_a), seq_spec(2 * c_a),
                      pl.BlockSpec((1, t_len, LANES), lambda b, h: (b, 0, 3 * c_a // LANES)),
                      seq_spec(o_za),
                      pl.BlockSpec((DECAY_RANK + ICLR_RANK, 2 * LANES), par),
                      par_spec, par_spec, par_spec, par_spec, par_spec, par_spec, par_spec],
            out_specs=pl.BlockSpec((1, t_len, LANES), lambda b, h: (b, 0, h)),
            out_shape=jax.ShapeDtypeStruct((bsz, t_len, c_a), BF16),
            scratch_shapes=_rwkv_scratch(t_len),
            compiler_params=pltpu.CompilerParams(dimension_semantics=("arbitrary", "arbitrary"),
                                                 vmem_limit_bytes=VMEM_LIMIT),
            name="rwkv",
        )(p, p, p, p, p, up_full, row2(decay_base[layer]), row2(iclr_base[layer]),
          row2(key_norm_scale[layer]), row2(key_iclr_mix[layer]), row2(bonus[layer]),
          row2(gn_gain[layer]), row2(gn_bias[layer]))

        yb = pl.pallas_call(
            _attn_kernel,
            grid=(bsz, c_b // LANES),
            in_specs=[seq_spec(o_q), seq_spec(o_k), seq_spec(o_v), seq_spec(o_zb)],
            out_specs=pl.BlockSpec((1, t_len, LANES), lambda b, h: (b, 0, h)),
            out_shape=jax.ShapeDtypeStruct((bsz, t_len, c_b), BF16),
            scratch_shapes=[pltpu.VMEM((t_len, LANES), BF16),
                            pltpu.VMEM((t_len, LANES), BF16),
                            pltpu.VMEM((t_len, LANES), F32),
                            pltpu.VMEM((t_len, LANES), F32),
                            pltpu.VMEM((t_len, LANES), F32)],
            compiler_params=pltpu.CompilerParams(dimension_semantics=("arbitrary", "arbitrary"),
                                                 vmem_limit_bytes=VMEM_LIMIT),
            name="attention",
        )(p, p, p, p)

        assert depth == 1
        tmo = OUT_PROJ_ROWS
        wo = w_out[layer].astype(BF16)
        x = pl.pallas_call(
            _out_proj_kernel,
            grid=(rows // tmo,),
            in_specs=[pl.BlockSpec((tmo, c_a), lambda i: (i, 0)),
                      pl.BlockSpec((tmo, c_b), lambda i: (i, 0)),
                      pl.BlockSpec((c_a, d_model), lambda i: (0, 0)),
                      pl.BlockSpec((c_b, d_model), lambda i: (0, 0)),
                      pl.BlockSpec((tmo, d_model), lambda i: (i, 0)),
                      pl.BlockSpec((1, d_model), lambda i: (0, 0))],
            out_specs=pl.BlockSpec((tmo, d_model), lambda i: (i, 0)),
            out_shape=jax.ShapeDtypeStruct((rows, d_model), F32),
            compiler_params=pltpu.CompilerParams(dimension_semantics=("arbitrary",),
                                                 vmem_limit_bytes=VMEM_LIMIT),
            name="out_proj",
        )(ya.reshape(rows, c_a), yb.reshape(rows, c_b), wo[:c_a], wo[c_a:],
          x.reshape(rows, d_model), row2(final_gain)).reshape(bsz, t_len, d_model)
    return x
```

```python
import functools

import jax
import jax.numpy as jnp
from jax import lax
from jax.experimental import pallas as pl
from jax.experimental.pallas import tpu as pltpu

HEAD_DIM = 64
LANES = 128
SUBLANES = 8
DECAY_RANK = 64
ICLR_RANK = 64
DILATION_PATTERNS = ((128, 1), (512, 4), (2048, 16))
ATTN_BLOCK = 128
ATTN_GROUP = 4
ROPE_THETA = 500000.0
ROPE_DIM = HEAD_DIM // 4
NORM_EPS = 1e-6
GN_EPS = 64e-5
CHUNK = 64
SUB = 16
PACK = LANES // (2 * SUB)
RWKV_GROUP = 8
IN_PROJ_ROWS = 512
OUT_PROJ_ROWS = 1024
LOG2_E = 1.4426950408889634
ATTN_Q_SCALE = HEAD_DIM ** -0.5 * LOG2_E
NEG_BIG = -1e30
VMEM_LIMIT = 56 * 1024 * 1024

F32 = jnp.float32
BF16 = jnp.bfloat16


def _dot(a, b, **kw):
    return jnp.dot(a, b, preferred_element_type=F32, **kw)


def _dot_nt(a, b):
    return lax.dot_general(a, b, (((1,), (1,)), ((), ())), preferred_element_type=F32)


def _split2(x):
    hi = x.astype(BF16)
    return hi, (x - hi.astype(F32)).astype(BF16)


def _dot_tn(a, b):
    return lax.dot_general(a, b, (((0,), (0,)), ((), ())), preferred_element_type=F32)


def _in_proj_kernel(x_ref, g_ref, w_ref, mix_ref, cos_ref, sa_ref, sb_ref, o_ref, carry_ref,
                    *, segs, shift_w, rope_lo, q_hi, rope_hi, tiles_per_seq):
    i = pl.program_id(0)

    @pl.when(i == 0)
    def _():
        carry_ref[...] = jnp.zeros(carry_ref.shape, F32)

    x = x_ref[...]
    h = x * lax.rsqrt(jnp.mean(x * x, axis=-1, keepdims=True) + NORM_EPS) * g_ref[...]
    hb = h.astype(BF16)
    tm = x.shape[0]
    first = (i % tiles_per_seq) == 0
    row0 = lax.broadcasted_iota(jnp.int32, (tm, 1), 0) == 0
    for lo, hi in segs:
        p = _dot(hb, w_ref[:, lo:hi])
        if hi <= shift_w:
            old = jnp.where(first, 0.0, carry_ref[0:1, lo:hi])
            carry_ref[0:1, lo:hi] = p[tm - 1:tm, :]
            prev = jnp.where(row0, old, pltpu.roll(p, 1, axis=0))
            p = p + (prev - p) * mix_ref[:, lo:hi]
            o_ref[:, lo:hi] = p
        elif lo >= rope_lo and hi <= rope_hi:
            if hi <= q_hi:
                p = p * ATTN_Q_SCALE
            for j in range(lo, hi, LANES):
                t = p[:, j - lo:j - lo + LANES]
                t = (t * cos_ref[...] + pltpu.roll(t, LANES - ROPE_DIM // 2, axis=1) * sa_ref[...]
                     + pltpu.roll(t, ROPE_DIM // 2, axis=1) * sb_ref[...])
                o_ref[:, j:j + LANES] = t
        else:
            o_ref[:, lo:hi] = p


def _spread_matrix():
    j = lax.broadcasted_iota(jnp.int32, (LANES, (SUB - 1) * LANES), 0)
    col = lax.broadcasted_iota(jnp.int32, (LANES, (SUB - 1) * LANES), 1)
    sel = j == SUB * ((col % LANES) // SUB) + col // LANES
    return jnp.where(sel, 1.0, 0.0).astype(BF16)


def _fold_matrices():
    r = lax.broadcasted_iota(jnp.int32, (LANES, LANES), 0)
    l = lax.broadcasted_iota(jnp.int32, (LANES, LANES), 1)
    per_chunk = LANES // PACK
    match = ((l % per_chunk) // SUB == r // HEAD_DIM) & (l % SUB == r % SUB)
    fold = [jnp.where(match & (l // per_chunk == c), 1.0, 0.0).astype(BF16) for c in range(PACK)]
    rr = lax.broadcasted_iota(jnp.int32, (LANES, PACK * LANES), 0)
    cc = lax.broadcasted_iota(jnp.int32, (LANES, PACK * LANES), 1)
    src = per_chunk * (cc // LANES) + SUB * ((cc % LANES) // HEAD_DIM) + cc % SUB
    unfold = jnp.where(rr == src, 1.0, 0.0).astype(BF16)
    return fold, unfold


def _substitute(coef):
    rows = coef.shape[0]
    ng = rows // SUBLANES
    gpb = SUB // SUBLANES
    row = lax.broadcasted_iota(jnp.int32, (SUBLANES, LANES), 0)
    col = lax.broadcasted_iota(jnp.int32, (SUBLANES, LANES), 1) % SUB
    xs = [jnp.where(col == (row + g * SUBLANES) % SUB, 1.0, 0.0).astype(F32) for g in range(ng)]
    for ss in range(SUB - 1):
        gs, rs = divmod(ss, SUBLANES)
        for blk in range(ng // gpb):
            xrow = xs[blk * gpb + gs][rs:rs + 1, :]
            for g in range(blk * gpb + (ss + 1) // SUBLANES, (blk + 1) * gpb):
                cf = coef[g * SUBLANES:(g + 1) * SUBLANES, ss * LANES:(ss + 1) * LANES]
                xs[g] = xs[g] - cf * xrow
    return jnp.concatenate(xs, axis=0)


def _rwkv_scratch(t_len):
    n_chunks = t_len // CHUNK
    return [pltpu.VMEM((LANES, (SUB - 1) * LANES), BF16),
            pltpu.VMEM((PACK, LANES, LANES), BF16),
            pltpu.VMEM((LANES, PACK * LANES), BF16),
            pltpu.VMEM((t_len, LANES), BF16),
            pltpu.VMEM((t_len, LANES), F32),
            pltpu.VMEM((t_len, LANES), F32),
            pltpu.VMEM((n_chunks, LANES, LANES), BF16),
            pltpu.VMEM((n_chunks, HEAD_DIM, LANES), F32),
            pltpu.VMEM((n_chunks, SUBLANES, LANES), F32)]


def _rwkv_kernel(r_ref, k_ref, v_ref, wa_ref, z_ref, up_ref, dbase_ref, abase_ref, kns_ref,
                 kim_ref, bonus_ref, gng_ref, gnb_ref, o_ref,
                 spread_ref, fold_ref, unfold_ref, qt_ref, yloc_ref, bon_ref, mt_ref, nn_ref,
                 pe_ref):
    t_len = r_ref.shape[1]
    c = CHUNK
    grp = RWKV_GROUP
    rows = grp * c
    n_chunks = t_len // c
    lane = lax.broadcasted_iota(jnp.int32, (c, LANES), 1)
    is_h0 = lane < HEAD_DIM
    spread_ref[...] = _spread_matrix()
    fold, unfold = _fold_matrices()
    for i in range(PACK):
        fold_ref[i] = fold[i]
    unfold_ref[...] = unfold
    assert grp % PACK == 0
    trow = lax.broadcasted_iota(jnp.int32, (c, LANES), 0)
    scol = lane % HEAD_DIM
    tb, sb = trow // SUB, scol // SUB
    strict = trow > scol
    incl = trow >= scol
    in_block = tb == sb
    ltri = (lax.broadcasted_iota(jnp.int32, (c, c), 0)
            >= lax.broadcasted_iota(jnp.int32, (c, c), 1)).astype(BF16)
    sq_r = lax.broadcasted_iota(jnp.int32, (LANES, LANES), 0)
    sq_c = lax.broadcasted_iota(jnp.int32, (LANES, LANES), 1)
    same_head = sq_r // HEAD_DIM == sq_c // HEAD_DIM

    dbase, abase = dbase_ref[...], abase_ref[...]
    kns, kim, bonus = kns_ref[...], kim_ref[...], bonus_ref[...]
    gng, gnb = gng_ref[...], gnb_ref[...]
    up_hi, up_lo = _split2(up_ref[...])
    up_hh = jnp.concatenate([up_hi, up_hi], axis=0)

    def head_sum(x):
        h0 = lax.broadcasted_iota(jnp.int32, x.shape, 1) < HEAD_DIM
        s0 = jnp.sum(jnp.where(h0, x, 0.0), axis=1, keepdims=True)
        s1 = jnp.sum(jnp.where(h0, 0.0, x), axis=1, keepdims=True)
        return jnp.where(h0, s0, s1)

    def block_diag(x):
        zero = jnp.zeros_like(x)
        return jnp.concatenate([jnp.where(is_h0, x, zero), jnp.where(is_h0, zero, x)],
                               axis=0).astype(BF16)

    def pass_a(first_chunk):
        n = grp
        sl = pl.ds(pl.multiple_of(first_chunk * c, n * c), n * c)
        r = r_ref[0, sl, :]
        k = k_ref[0, sl, :]
        v = v_ref[0, sl, :]
        wa = wa_ref[0, sl, :]
        is_h0_s = lax.broadcasted_iota(jnp.int32, (n * c, LANES), 1) < HEAD_DIM
        x_hi, x_lo = _split2(jnp.where(is_h0_s, jnp.tanh(wa), wa))
        lin = _dot(jnp.concatenate([x_hi, x_lo], axis=1), up_hh) + _dot(x_hi, up_lo)
        logit = -jax.nn.softplus(-(dbase + lin[:, :LANES])) - 0.5
        logw = -jnp.exp(logit)
        a = jax.nn.sigmoid(abase + lin[:, LANES:])
        kk = k * kns
        kk = kk / jnp.maximum(jnp.sqrt(head_sum(kk * kk)), 1e-12)
        kmod = k * (1.0 + (a - 1.0) * kim)
        b = kk * a
        bon_ref[sl, :] = head_sum(r * kmod * bonus) * v

        parts = jnp.concatenate(_split2(logw), axis=1)
        gs = [_dot(ltri, parts[j * c:(j + 1) * c]) for j in range(n)]
        g = jnp.concatenate([x[:, LANES:] + x[:, :LANES] for x in gs], axis=0)
        gl = jnp.concatenate([jnp.broadcast_to(g[(j + 1) * c - 1:(j + 1) * c, :], (c, LANES))
                              for j in range(n)], axis=0)
        e_inv = jnp.exp(-g)
        kkd = kk * jnp.exp(g - logw)
        rd = r * jnp.exp(g)
        bi = b * e_inv
        ki = kmod * e_inv
        e_end = jnp.exp(gl - g)
        kdl = kmod * e_end
        bdl = b * e_end
        p_end = jnp.exp(gl)
        yield

        js = range(n)
        cs = [slice(j * c, (j + 1) * c) for j in js]
        aab, aak, arb, ark = [], [], [], []
        for j in js:
            lhs = jnp.concatenate([kkd[cs[j]], rd[cs[j]]], axis=0).astype(BF16)
            rhs = jnp.concatenate([block_diag(bi[cs[j]]), block_diag(ki[cs[j]])], axis=0)
            pair = _dot_nt(lhs, rhs)
            aab.append(jnp.where(strict, pair[:c, :LANES], 0.0))
            aak.append(jnp.where(strict, pair[:c, LANES:], 0.0))
            arb.append(jnp.where(incl, pair[c:, :LANES], 0.0))
            ark.append(jnp.where(incl, pair[c:, LANES:], 0.0))
        yield
        vbd = [block_diag(v[cs[j]]) for j in js]
        aakv = [_dot(aak[j].astype(BF16), vbd[j]) for j in js]
        compact = [functools.reduce(
            jnp.add, [_dot(jnp.where(in_block, aab[q * PACK + i], 0.0).astype(BF16), fold_ref[i])
                      for i in range(PACK)]) for q in range(n // PACK)]
        yield
        coef = _dot(jnp.concatenate(compact, axis=0).astype(BF16), spread_ref[...])
        t_diag = _substitute(coef)
        yield
        spread_out = [_dot(t_diag[q * c:(q + 1) * c].astype(BF16), unfold_ref[...])
                      for q in range(n // PACK)]
        tinv = [jnp.where(in_block, spread_out[j // PACK][:, (j % PACK) * LANES:
                                                           (j % PACK + 1) * LANES], 0.0)
                for j in js]
        yield
        span = 1
        while span * SUB < c:
            low_mask = (tb // span == sb // span + 1) & ((tb // span) % 2 == 1)
            inner = [_dot(jnp.where(low_mask, aab[j], 0.0).astype(BF16),
                          block_diag(tinv[j])) for j in js]
            yield
            tinv = [tinv[j] - _dot(tinv[j].astype(BF16), block_diag(inner[j])) for j in js]
            yield
            span *= 2
        wu = [_dot(tinv[j].astype(BF16),
                   jnp.concatenate([block_diag(kkd[cs[j]]), block_diag(aakv[j])], axis=1))
              for j in js]
        yield
        w = [x[:, :LANES] for x in wu]
        uloc = [x[:, LANES:] for x in wu]
        arb_b = [arb[j].astype(BF16) for j in js]
        bdl_b = [bdl[cs[j]].astype(BF16) for j in js]
        qt = [rd[cs[j]] - _dot(arb_b[j], block_diag(w[j])) for j in js]
        yloc = [_dot(jnp.concatenate([ark[j].astype(BF16), -arb_b[j]], axis=1),
                     jnp.concatenate([vbd[j], block_diag(uloc[j])], axis=0))
                for j in js]
        wtb = [_dot_tn(w[j].astype(BF16), bdl_b[j]) for j in js]
        nn = [_dot_tn(jnp.concatenate([v[cs[j]], -uloc[j]], axis=0).astype(BF16),
                      jnp.concatenate([kdl[cs[j]].astype(BF16), bdl_b[j]], axis=0))
              for j in js]
        yield
        for j in js:
            ci = first_chunk + j
            csl = pl.ds(pl.multiple_of(ci * c, c), c)
            qt_ref[csl, :] = qt[j].astype(BF16)
            yloc_ref[csl, :] = yloc[j]
            mt_ref[ci] = jnp.where(same_head, wtb[j], 0.0).astype(BF16)
            pe_ref[ci] = p_end[j * c:j * c + SUBLANES, :]
            nn_ref[ci] = jnp.where(is_h0, nn[j][:HEAD_DIM], nn[j][HEAD_DIM:])

    def chunk_step(ci, state):
        sb = state.astype(BF16)
        y_part = _dot_nt(qt_ref[pl.ds(pl.multiple_of(ci * c, c), c), :], block_diag(sb))
        new_state = state * pe_ref[ci, 0:1, :] + nn_ref[ci] - _dot(sb, mt_ref[ci])
        return y_part, new_state

    def finish(gi, ys):
        sl = pl.ds(pl.multiple_of(gi * rows, rows), rows)
        y = jnp.concatenate(ys, axis=0) + yloc_ref[sl, :]
        mu = head_sum(y) * (1.0 / HEAD_DIM)
        d = y - mu
        var = head_sum(d * d) * (1.0 / HEAD_DIM)
        yn = d * lax.rsqrt(var + GN_EPS) * gng + gnb + bon_ref[sl, :]
        z = z_ref[0, sl, :]
        o_ref[0, sl, :] = (yn * (z * jax.nn.sigmoid(z))).astype(o_ref.dtype)

    class Chain:
        def __init__(self, gi, state):
            self.gi, self.state, self.ys = gi, state, []

        def tick(self):
            if len(self.ys) < grp:
                y_part, self.state = chunk_step(self.gi * grp + len(self.ys), self.state)
                self.ys.append(y_part)

        def drain(self):
            while len(self.ys) < grp:
                self.tick()
            finish(self.gi, self.ys)
            return self.state

    def body(gi, state):
        chain = Chain(gi - 1, state)
        for _ in pass_a(gi * grp):
            chain.tick()
        return chain.drain()

    n_groups = n_chunks // grp
    for _ in pass_a(0):
        pass
    state = lax.fori_loop(1, n_groups, body, jnp.zeros((HEAD_DIM, LANES), F32))
    Chain(n_groups - 1, state).drain()


def _attn_kernel(q_ref, k_ref, v_ref, z_ref, o_ref, kb_ref, vb_ref, acc_ref, m_ref, l_ref):
    t_len = q_ref.shape[1]
    blk = ATTN_BLOCK
    kb_ref[...] = k_ref[0].astype(BF16)
    vb_ref[...] = v_ref[0].astype(BF16)

    lane = lax.broadcasted_iota(jnp.int32, (blk, LANES), 1)
    is_h0 = lane < HEAD_DIM
    qi = lax.broadcasted_iota(jnp.int32, (blk, 2 * blk), 0)
    kj = lax.broadcasted_iota(jnp.int32, (blk, 2 * blk), 1)
    nb = ATTN_GROUP
    heads = (is_h0, ~is_h0)
    patterns = tuple(reversed(DILATION_PATTERNS))
    assert patterns[-1][1] == 1
    bias_causal = jnp.where(kj <= qi, 0.0, NEG_BIG).astype(F32)

    for pi, (window, dil) in enumerate(patterns):
        span = window // dil
        nblk = t_len // (dil * blk)
        bias_band = jnp.where((qi + blk - kj >= 0) & (qi + blk - kj <= span), 0.0,
                              NEG_BIG).astype(F32)
        last = pi == len(patterns) - 1

        def body(it, _, pi=pi, dil=dil, nblk=nblk, bias_band=bias_band, last=last):
            rows, q, kw, vw, biases = [], [], [], [], []
            if nblk == 2 and nb % 2 == 0:
                for jr in range(nb // 2):
                    res = it * (nb // 2) + jr
                    seq = pl.ds(res, 2 * blk, stride=dil)
                    q_seq = q_ref[0, seq, :]
                    k_seq = k_ref[0, seq, :].astype(BF16)
                    v_seq = v_ref[0, seq, :].astype(BF16)
                    for n in range(2):
                        rows.append(pl.ds(res + dil * blk * n, blk, stride=dil))
                        q.append(q_seq[n * blk:(n + 1) * blk])
                        kw.append(k_seq)
                        vw.append(v_seq)
                        biases.append(bias_band if n else bias_causal)
            else:
                for j in range(nb):
                    res = (it * nb + j) // nblk
                    n = (it * nb + j) % nblk
                    maybe_first = j == 0 or nblk % nb != 0
                    n_prev = jnp.maximum(n - 1, 0) if maybe_first else n - 1
                    q0 = res + dil * blk * n
                    k0 = res + dil * blk * n_prev
                    if dil == 1:
                        rows.append(pl.ds(pl.multiple_of(q0, blk), blk))
                        krows = pl.ds(pl.multiple_of(k0, blk), 2 * blk)
                        kw.append(kb_ref[krows, :])
                        vw.append(vb_ref[krows, :])
                    else:
                        rows.append(pl.ds(q0, blk, stride=dil))
                        krows = pl.ds(k0, 2 * blk, stride=dil)
                        kw.append(k_ref[0, krows, :].astype(BF16))
                        vw.append(v_ref[0, krows, :].astype(BF16))
                    q.append(q_ref[0, rows[j], :])
                    biases.append(jnp.where(n == 0, bias_causal, bias_band) if maybe_first
                                  else bias_band)
            s = [[_dot_nt(jnp.where(h, q[j], 0.0).astype(BF16), kw[j]) + biases[j] for h in heads]
                 for j in range(nb)]
            m = [[jnp.max(x, axis=1, keepdims=True) for x in sj] for sj in s]
            p = [[jnp.exp2(s[j][h] - m[j][h]) for h in range(2)] for j in range(nb)]
            l = [[jnp.sum(x, axis=1, keepdims=True) for x in pj] for pj in p]
            pv = [[_dot(p[j][h].astype(BF16), vw[j]) for h in range(2)] for j in range(nb)]
            for j in range(nb):
                acc = jnp.where(is_h0, pv[j][0], pv[j][1])
                mj = jnp.where(is_h0, m[j][0], m[j][1])
                lj = jnp.where(is_h0, l[j][0], l[j][1])
                if last:
                    ms = [mj] + [m_ref[pp, rows[j], :] for pp in range(pi)]
                    ls = [lj] + [l_ref[pp, rows[j], :] for pp in range(pi)]
                    accs = [acc] + [acc_ref[pp, rows[j], :] for pp in range(pi)]
                    m_all = functools.reduce(jnp.maximum, ms)
                    ws = [jnp.exp2(mm - m_all) for mm in ms]
                    num = functools.reduce(jnp.add, [w * a for w, a in zip(ws, accs)])
                    den = functools.reduce(jnp.add, [w * x for w, x in zip(ws, ls)])
                    z = z_ref[0, rows[j], :]
                    o_ref[0, rows[j], :] = (num / den * (z * jax.nn.sigmoid(z))).astype(o_ref.dtype)
                else:
                    acc_ref[pi, rows[j], :] = acc
                    m_ref[pi, rows[j], :] = mj
                    l_ref[pi, rows[j], :] = lj
            return 0

        lax.fori_loop(0, t_len // (blk * nb), body, 0)


def _out_proj_kernel(ya_ref, yb_ref, wa_ref, wb_ref, x_ref, g_ref, o_ref):
    y = x_ref[...] + _dot(ya_ref[...], wa_ref[...]) + _dot(yb_ref[...], wb_ref[...])
    o_ref[...] = y * lax.rsqrt(jnp.mean(y * y, axis=-1, keepdims=True) + NORM_EPS) * g_ref[...]


def _rope_tables(t_len):
    half = ROPE_DIM // 2
    inv = ROPE_THETA ** (-jnp.arange(half, dtype=F32) * 2.0 / ROPE_DIM)
    ang = jnp.arange(t_len, dtype=jnp.int32).astype(F32)[:, None] * inv[None, :]
    cos, sin = jnp.cos(ang), jnp.sin(ang)
    ones = jnp.ones((t_len, HEAD_DIM - ROPE_DIM), F32)
    zeros = jnp.zeros((t_len, HEAD_DIM - ROPE_DIM), F32)
    zh = jnp.zeros((t_len, half), F32)
    tile = lambda a: jnp.tile(a, (1, LANES // HEAD_DIM))
    cos_t = tile(jnp.concatenate([cos, cos, ones], axis=1))
    sa_t = tile(jnp.concatenate([-sin, zh, zeros], axis=1))
    sb_t = tile(jnp.concatenate([zh, sin, zeros], axis=1))
    return cos_t, sa_t, sb_t


def kernel(x, norm_gain, w_in, shift_mix, decay_base, decay_up, iclr_base, iclr_up, key_norm_scale,
           key_iclr_mix, bonus, gn_gain, gn_bias, w_out, final_gain):
    bsz, t_len, d_model = x.shape
    depth = w_in.shape[0]
    c_a = decay_base.shape[1]
    n_hp = c_a // LANES
    shift_w = 3 * c_a + DECAY_RANK + ICLR_RANK
    in_w = w_in.shape[2]
    c_b = (in_w - shift_w - c_a) // 4
    assert c_a % LANES == 0 and c_b == c_a
    assert all(t_len % (2 * dil * ATTN_BLOCK) == 0 for _, dil in DILATION_PATTERNS)
    o_za = shift_w
    o_q = o_za + c_a
    o_k, o_v, o_zb = o_q + c_b, o_q + 2 * c_b, o_q + 3 * c_b
    segs = ((0, c_a), (c_a, 2 * c_a), (2 * c_a, 3 * c_a), (3 * c_a, shift_w), (o_za, o_q),
            (o_q, o_k), (o_k, o_v), (o_v, o_zb), (o_zb, in_w))
    cos_t, sa_t, sb_t = _rope_tables(t_len)
    rows = bsz * t_len
    tm = IN_PROJ_ROWS
    tiles_per_seq = t_len // tm
    row2 = lambda a: a.reshape(1, -1).astype(F32)

    for layer in range(depth):
        p = pl.pallas_call(
            functools.partial(_in_proj_kernel, segs=segs, shift_w=shift_w, rope_lo=o_q, q_hi=o_k,
                              rope_hi=o_v, tiles_per_seq=tiles_per_seq),
            grid=(rows // tm,),
            in_specs=[
                pl.BlockSpec((tm, d_model), lambda i: (i, 0)),
                pl.BlockSpec((1, d_model), lambda i: (0, 0)),
                pl.BlockSpec((d_model, in_w), lambda i: (0, 0)),
                pl.BlockSpec((1, shift_w), lambda i: (0, 0)),
                pl.BlockSpec((tm, LANES), lambda i: (i % tiles_per_seq, 0)),
                pl.BlockSpec((tm, LANES), lambda i: (i % tiles_per_seq, 0)),
                pl.BlockSpec((tm, LANES), lambda i: (i % tiles_per_seq, 0)),
            ],
            out_specs=pl.BlockSpec((tm, in_w), lambda i: (i, 0)),
            out_shape=jax.ShapeDtypeStruct((rows, in_w), F32),
            scratch_shapes=[pltpu.VMEM((SUBLANES, shift_w), F32)],
            compiler_params=pltpu.CompilerParams(dimension_semantics=("arbitrary",),
                                                 vmem_limit_bytes=VMEM_LIMIT),
            name="in_proj",
        )(x.reshape(rows, d_model), row2(norm_gain[layer]), w_in[layer].astype(BF16),
          row2(shift_mix[layer]), cos_t, sa_t, sb_t)
        p = p.reshape(bsz, t_len, in_w)

        zeros_up = jnp.zeros((DECAY_RANK, c_a), F32)
        up_full = jnp.concatenate(
            [jnp.concatenate([decay_up[layer].astype(F32), zeros_up], axis=0).reshape(
                DECAY_RANK + ICLR_RANK, n_hp, 1, LANES),
             jnp.concatenate([zeros_up, iclr_up[layer].astype(F32)], axis=0).reshape(
                 DECAY_RANK + ICLR_RANK, n_hp, 1, LANES)], axis=2).reshape(
                     DECAY_RANK + ICLR_RANK, n_hp * 2 * LANES)
        col = lambda off: (lambda b, h: (b, 0, off // LANES + h))
        par = lambda b, h: (0, h)
        seq_spec = lambda off: pl.BlockSpec((1, t_len, LANES), col(off))
        par_spec = pl.BlockSpec((1, LANES), par)
        ya = pl.pallas_call(
            _rwkv_kernel,
            grid=(bsz, n_hp),
            in_specs=[seq_spec(0), seq_spec(c_a), seq_spec(2 * c_a),
                      pl.BlockSpec((1, t_len, LANES), lambda b, h: (b, 0, 3 * c_a // LANES)),
                      seq_spec(o_za),
                      pl.BlockSpec((DECAY_RANK + ICLR_RANK, 2 * LANES), par),
                      par_spec, par_spec, par_spec, par_spec, par_spec, par_spec, par_spec],
            out_specs=pl.BlockSpec((1, t_len, LANES), lambda b, h: (b, 0, h)),
            out_shape=jax.ShapeDtypeStruct((bsz, t_len, c_a), BF16),
            scratch_shapes=_rwkv_scratch(t_len),
            compiler_params=pltpu.CompilerParams(dimension_semantics=("arbitrary", "arbitrary"),
                                                 vmem_limit_bytes=VMEM_LIMIT),
            name="rwkv",
        )(p, p, p, p, p, up_full, row2(decay_base[layer]), row2(iclr_base[layer]),
          row2(key_norm_scale[layer]), row2(key_iclr_mix[layer]), row2(bonus[layer]),
          row2(gn_gain[layer]), row2(gn_bias[layer]))

        yb = pl.pallas_call(
            _attn_kernel,
            grid=(bsz, c_b // LANES),
            in_specs=[seq_spec(o_q), seq_spec(o_k), seq_spec(o_v), seq_spec(o_zb)],
            out_specs=pl.BlockSpec((1, t_len, LANES), lambda b, h: (b, 0, h)),
            out_shape=jax.ShapeDtypeStruct((bsz, t_len, c_b), BF16),
            scratch_shapes=[pltpu.VMEM((t_len, LANES), BF16),
                            pltpu.VMEM((t_len, LANES), BF16),
                            pltpu.VMEM((len(DILATION_PATTERNS) - 1, t_len, LANES), F32),
                            pltpu.VMEM((len(DILATION_PATTERNS) - 1, t_len, LANES), F32),
                            pltpu.VMEM((len(DILATION_PATTERNS) - 1, t_len, LANES), F32)],
            compiler_params=pltpu.CompilerParams(dimension_semantics=("arbitrary", "arbitrary"),
                                                 vmem_limit_bytes=VMEM_LIMIT),
            name="attention",
        )(p, p, p, p)

        assert depth == 1
        tmo = OUT_PROJ_ROWS
        wo = w_out[layer].astype(BF16)
        x = pl.pallas_call(
            _out_proj_kernel,
            grid=(rows // tmo,),
            in_specs=[pl.BlockSpec((tmo, c_a), lambda i: (i, 0)),
                      pl.BlockSpec((tmo, c_b), lambda i: (i, 0)),
                      pl.BlockSpec((c_a, d_model), lambda i: (0, 0)),
                      pl.BlockSpec((c_b, d_model), lambda i: (0, 0)),
                      pl.BlockSpec((tmo, d_model), lambda i: (i, 0)),
                      pl.BlockSpec((1, d_model), lambda i: (0, 0))],
            out_specs=pl.BlockSpec((tmo, d_model), lambda i: (i, 0)),
            out_shape=jax.ShapeDtypeStruct((rows, d_model), F32),
            compiler_params=pltpu.CompilerParams(dimension_semantics=("arbitrary",),
                                                 vmem_limit_bytes=VMEM_LIMIT),
            name="out_proj",
        )(ya.reshape(rows, c_a), yb.reshape(rows, c_b), wo[:c_a], wo[c_a:],
          x.reshape(rows, d_model), row2(final_gain)).reshape(bsz, t_len, d_model)
    return x
```

```python
import functools

import jax
import jax.numpy as jnp
from jax import lax
from jax.experimental import pallas as pl
from jax.experimental.pallas import tpu as pltpu

HEAD_DIM = 64
LANES = 128
SUBLANES = 8
DECAY_RANK = 64
ICLR_RANK = 64
DILATION_PATTERNS = ((128, 1), (512, 4), (2048, 16))
ATTN_BLOCK = 128
ATTN_GROUP = 4
ROPE_THETA = 500000.0
ROPE_DIM = HEAD_DIM // 4
NORM_EPS = 1e-6
GN_EPS = 64e-5
CHUNK = 64
SUB = 16
PACK = LANES // (2 * SUB)
RWKV_GROUP = 8
PREP_STEPS = (2, 5, 7)
IN_PROJ_ROWS = 512
OUT_PROJ_ROWS = 1024
LOG2_E = 1.4426950408889634
ATTN_Q_SCALE = HEAD_DIM ** -0.5 * LOG2_E
NEG_BIG = -1e30
VMEM_LIMIT = 56 * 1024 * 1024

F32 = jnp.float32
BF16 = jnp.bfloat16


def _dot(a, b, **kw):
    return jnp.dot(a, b, preferred_element_type=F32, **kw)


def _dot_nt(a, b):
    return lax.dot_general(a, b, (((1,), (1,)), ((), ())), preferred_element_type=F32)


def _split2(x):
    hi = x.astype(BF16)
    return hi, (x - hi.astype(F32)).astype(BF16)


def _dot_tn(a, b):
    return lax.dot_general(a, b, (((0,), (0,)), ((), ())), preferred_element_type=F32)


def _in_proj_kernel(x_ref, g_ref, w_ref, mix_ref, cos_ref, sa_ref, sb_ref, o_ref, carry_ref,
                    *, segs, shift_w, rope_lo, q_hi, rope_hi, tiles_per_seq):
    i = pl.program_id(0)

    @pl.when(i == 0)
    def _():
        carry_ref[...] = jnp.zeros(carry_ref.shape, F32)

    x = x_ref[...]
    h = x * lax.rsqrt(jnp.mean(x * x, axis=-1, keepdims=True) + NORM_EPS) * g_ref[...]
    hb = h.astype(BF16)
    tm = x.shape[0]
    first = (i % tiles_per_seq) == 0
    row0 = lax.broadcasted_iota(jnp.int32, (tm, 1), 0) == 0
    for lo, hi in segs:
        p = _dot(hb, w_ref[:, lo:hi])
        if hi <= shift_w:
            old = jnp.where(first, 0.0, carry_ref[0:1, lo:hi])
            carry_ref[0:1, lo:hi] = p[tm - 1:tm, :]
            prev = jnp.where(row0, old, pltpu.roll(p, 1, axis=0))
            p = p + (prev - p) * mix_ref[:, lo:hi]
            o_ref[:, lo:hi] = p
        elif lo >= rope_lo and hi <= rope_hi:
            if hi <= q_hi:
                p = p * ATTN_Q_SCALE
            for j in range(lo, hi, LANES):
                t = p[:, j - lo:j - lo + LANES]
                t = (t * cos_ref[...] + pltpu.roll(t, LANES - ROPE_DIM // 2, axis=1) * sa_ref[...]
                     + pltpu.roll(t, ROPE_DIM // 2, axis=1) * sb_ref[...])
                o_ref[:, j:j + LANES] = t
        else:
            o_ref[:, lo:hi] = p


def _spread_matrix():
    j = lax.broadcasted_iota(jnp.int32, (LANES, (SUB - 1) * LANES), 0)
    col = lax.broadcasted_iota(jnp.int32, (LANES, (SUB - 1) * LANES), 1)
    sel = j == SUB * ((col % LANES) // SUB) + col // LANES
    return jnp.where(sel, 1.0, 0.0).astype(BF16)


def _fold_matrices():
    r = lax.broadcasted_iota(jnp.int32, (LANES, LANES), 0)
    l = lax.broadcasted_iota(jnp.int32, (LANES, LANES), 1)
    per_chunk = LANES // PACK
    match = ((l % per_chunk) // SUB == r // HEAD_DIM) & (l % SUB == r % SUB)
    fold = [jnp.where(match & (l // per_chunk == c), 1.0, 0.0).astype(BF16) for c in range(PACK)]
    rr = lax.broadcasted_iota(jnp.int32, (LANES, PACK * LANES), 0)
    cc = lax.broadcasted_iota(jnp.int32, (LANES, PACK * LANES), 1)
    src = per_chunk * (cc // LANES) + SUB * ((cc % LANES) // HEAD_DIM) + cc % SUB
    unfold = jnp.where(rr == src, 1.0, 0.0).astype(BF16)
    return fold, unfold


def _substitute(coef):
    rows = coef.shape[0]
    ng = rows // SUBLANES
    gpb = SUB // SUBLANES
    row = lax.broadcasted_iota(jnp.int32, (SUBLANES, LANES), 0)
    col = lax.broadcasted_iota(jnp.int32, (SUBLANES, LANES), 1) % SUB
    xs = [jnp.where(col == (row + g * SUBLANES) % SUB, 1.0, 0.0).astype(F32) for g in range(ng)]
    for ss in range(SUB - 1):
        gs, rs = divmod(ss, SUBLANES)
        for blk in range(ng // gpb):
            xrow = xs[blk * gpb + gs][rs:rs + 1, :]
            for g in range(blk * gpb + (ss + 1) // SUBLANES, (blk + 1) * gpb):
                cf = coef[g * SUBLANES:(g + 1) * SUBLANES, ss * LANES:(ss + 1) * LANES]
                xs[g] = xs[g] - cf * xrow
    return jnp.concatenate(xs, axis=0)


def _rwkv_scratch(t_len):
    n_chunks = t_len // CHUNK
    return [pltpu.VMEM((LANES, (SUB - 1) * LANES), BF16),
            pltpu.VMEM((PACK, LANES, LANES), BF16),
            pltpu.VMEM((LANES, PACK * LANES), BF16),
            pltpu.VMEM((t_len, LANES), BF16),
            pltpu.VMEM((t_len, LANES), F32),
            pltpu.VMEM((t_len, LANES), F32),
            pltpu.VMEM((n_chunks, LANES, LANES), BF16),
            pltpu.VMEM((n_chunks, HEAD_DIM, LANES), F32),
            pltpu.VMEM((n_chunks, SUBLANES, LANES), F32),
            pltpu.VMEM((2, RWKV_GROUP, 2 * CHUNK, LANES), BF16),
            pltpu.VMEM((2, RWKV_GROUP, 4 * CHUNK, LANES), BF16),
            pltpu.VMEM((2, RWKV_GROUP, 2 * CHUNK, LANES), BF16),
            pltpu.VMEM((2, RWKV_GROUP, 2 * CHUNK, LANES), BF16),
            pltpu.VMEM((2, RWKV_GROUP * CHUNK, LANES), F32),
            pltpu.VMEM((2, RWKV_GROUP * CHUNK, LANES), BF16),
            pltpu.VMEM((2, RWKV_GROUP * CHUNK, LANES), BF16),
            pltpu.VMEM((2, RWKV_GROUP * CHUNK, LANES), BF16)]


def _rwkv_kernel(r_ref, k_ref, v_ref, wa_ref, z_ref, up_ref, dbase_ref, abase_ref, kns_ref,
                 kim_ref, bonus_ref, gng_ref, gnb_ref, o_ref,
                 spread_ref, fold_ref, unfold_ref, qt_ref, yloc_ref, bon_ref, mt_ref, nn_ref,
                 pe_ref, lhs_ref, rhs_ref, vbd_ref, kbd_ref, rd_ref, vb_ref, kdl_ref, bdl_ref):
    t_len = r_ref.shape[1]
    c = CHUNK
    grp = RWKV_GROUP
    rows = grp * c
    n_chunks = t_len // c
    lane = lax.broadcasted_iota(jnp.int32, (c, LANES), 1)
    is_h0 = lane < HEAD_DIM
    spread_ref[...] = _spread_matrix()
    fold, unfold = _fold_matrices()
    for i in range(PACK):
        fold_ref[i] = fold[i]
    unfold_ref[...] = unfold
    assert grp % PACK == 0
    trow = lax.broadcasted_iota(jnp.int32, (c, LANES), 0)
    scol = lane % HEAD_DIM
    tb, sb = trow // SUB, scol // SUB
    strict = trow > scol
    incl = trow >= scol
    in_block = tb == sb
    ltri = (lax.broadcasted_iota(jnp.int32, (c, c), 0)
            >= lax.broadcasted_iota(jnp.int32, (c, c), 1)).astype(BF16)
    sq_r = lax.broadcasted_iota(jnp.int32, (LANES, LANES), 0)
    sq_c = lax.broadcasted_iota(jnp.int32, (LANES, LANES), 1)
    same_head = sq_r // HEAD_DIM == sq_c // HEAD_DIM

    dbase, abase = dbase_ref[...], abase_ref[...]
    kns, kim, bonus = kns_ref[...], kim_ref[...], bonus_ref[...]
    gng, gnb = gng_ref[...], gnb_ref[...]
    up_hi, up_lo = _split2(up_ref[...])
    up_hh = jnp.concatenate([up_hi, up_hi], axis=0)

    def head_sum(x):
        h0 = lax.broadcasted_iota(jnp.int32, x.shape, 1) < HEAD_DIM
        s0 = jnp.sum(jnp.where(h0, x, 0.0), axis=1, keepdims=True)
        s1 = jnp.sum(jnp.where(h0, 0.0, x), axis=1, keepdims=True)
        return jnp.where(h0, s0, s1)

    def block_diag(x):
        zero = jnp.zeros_like(x)
        return jnp.concatenate([jnp.where(is_h0, x, zero), jnp.where(is_h0, zero, x)],
                               axis=0).astype(BF16)

    n = grp
    js = range(n)
    cs = [slice(j * c, (j + 1) * c) for j in js]

    def prep(gi, slot):
        first_chunk = gi * grp
        sl = pl.ds(pl.multiple_of(first_chunk * c, n * c), n * c)
        r = r_ref[0, sl, :]
        k = k_ref[0, sl, :]
        v = v_ref[0, sl, :]
        wa = wa_ref[0, sl, :]
        is_h0_s = lax.broadcasted_iota(jnp.int32, (n * c, LANES), 1) < HEAD_DIM
        x_hi, x_lo = _split2(jnp.where(is_h0_s, jnp.tanh(wa), wa))
        lin = _dot(jnp.concatenate([x_hi, x_lo], axis=1), up_hh) + _dot(x_hi, up_lo)
        yield
        logit = -jax.nn.softplus(-(dbase + lin[:, :LANES])) - 0.5
        logw = -jnp.exp(logit)
        a = jax.nn.sigmoid(abase + lin[:, LANES:])
        kk = k * kns
        kk = kk / jnp.maximum(jnp.sqrt(head_sum(kk * kk)), 1e-12)
        kmod = k * (1.0 + (a - 1.0) * kim)
        b = kk * a
        bon_ref[sl, :] = head_sum(r * kmod * bonus) * v

        parts = jnp.concatenate(_split2(logw), axis=1)
        gs = [_dot(ltri, parts[cs[j]]) for j in js]
        yield
        g = jnp.concatenate([x[:, LANES:] + x[:, :LANES] for x in gs], axis=0)
        gl = jnp.concatenate([jnp.broadcast_to(g[(j + 1) * c - 1:(j + 1) * c, :], (c, LANES))
                              for j in js], axis=0)
        e_inv = jnp.exp(-g)
        kkd = kk * jnp.exp(g - logw)
        rd = r * jnp.exp(g)
        bi = b * e_inv
        ki = kmod * e_inv
        e_end = jnp.exp(gl - g)
        p_end = jnp.exp(gl)
        rd_ref[slot] = rd
        vb_ref[slot] = v.astype(BF16)
        kdl_ref[slot] = (kmod * e_end).astype(BF16)
        bdl_ref[slot] = (b * e_end).astype(BF16)
        for j in js:
            lhs_ref[slot, j] = jnp.concatenate([kkd[cs[j]], rd[cs[j]]], axis=0).astype(BF16)
            rhs_ref[slot, j] = jnp.concatenate([block_diag(bi[cs[j]]), block_diag(ki[cs[j]])],
                                               axis=0)
            vbd_ref[slot, j] = block_diag(v[cs[j]])
            kbd_ref[slot, j] = block_diag(kkd[cs[j]])
            pe_ref[first_chunk + j] = p_end[j * c:j * c + SUBLANES, :]

    def pass_a(first_chunk, slot):
        aab, aak, arb, ark = [], [], [], []
        for j in js:
            pair = _dot_nt(lhs_ref[slot, j], rhs_ref[slot, j])
            aab.append(jnp.where(strict, pair[:c, :LANES], 0.0))
            aak.append(jnp.where(strict, pair[:c, LANES:], 0.0))
            arb.append(jnp.where(incl, pair[c:, :LANES], 0.0))
            ark.append(jnp.where(incl, pair[c:, LANES:], 0.0))
        yield
        vbd = [vbd_ref[slot, j] for j in js]
        aakv = [_dot(aak[j].astype(BF16), vbd[j]) for j in js]
        compact = [functools.reduce(
            jnp.add, [_dot(jnp.where(in_block, aab[q * PACK + i], 0.0).astype(BF16), fold_ref[i])
                      for i in range(PACK)]) for q in range(n // PACK)]
        yield
        coef = _dot(jnp.concatenate(compact, axis=0).astype(BF16), spread_ref[...])
        t_diag = _substitute(coef)
        yield
        spread_out = [_dot(t_diag[q * c:(q + 1) * c].astype(BF16), unfold_ref[...])
                      for q in range(n // PACK)]
        tinv = [jnp.where(in_block, spread_out[j // PACK][:, (j % PACK) * LANES:
                                                           (j % PACK + 1) * LANES], 0.0)
                for j in js]
        yield
        span = 1
        while span * SUB < c:
            low_mask = (tb // span == sb // span + 1) & ((tb // span) % 2 == 1)
            inner = [_dot(jnp.where(low_mask, aab[j], 0.0).astype(BF16),
                          block_diag(tinv[j])) for j in js]
            yield
            tinv = [tinv[j] - _dot(tinv[j].astype(BF16), block_diag(inner[j])) for j in js]
            yield
            span *= 2
        wu = [_dot(tinv[j].astype(BF16),
                   jnp.concatenate([kbd_ref[slot, j], block_diag(aakv[j])], axis=1))
              for j in js]
        yield
        w = [x[:, :LANES] for x in wu]
        uloc = [x[:, LANES:] for x in wu]
        arb_b = [arb[j].astype(BF16) for j in js]
        bdl_b = [bdl_ref[slot, cs[j], :] for j in js]
        qt = [rd_ref[slot, cs[j], :] - _dot(arb_b[j], block_diag(w[j])) for j in js]
        yloc = [_dot(jnp.concatenate([ark[j].astype(BF16), -arb_b[j]], axis=1),
                     jnp.concatenate([vbd[j], block_diag(uloc[j])], axis=0))
                for j in js]
        wtb = [_dot_tn(w[j].astype(BF16), bdl_b[j]) for j in js]
        nn = [_dot_tn(jnp.concatenate([vb_ref[slot, cs[j], :], (-uloc[j]).astype(BF16)], axis=0),
                      jnp.concatenate([kdl_ref[slot, cs[j], :], bdl_b[j]], axis=0))
              for j in js]
        yield
        for j in js:
            ci = first_chunk + j
            csl = pl.ds(pl.multiple_of(ci * c, c), c)
            qt_ref[csl, :] = qt[j].astype(BF16)
            yloc_ref[csl, :] = yloc[j]
            mt_ref[ci] = jnp.where(same_head, wtb[j], 0.0).astype(BF16)
            nn_ref[ci] = jnp.where(is_h0, nn[j][:HEAD_DIM], nn[j][HEAD_DIM:])

    def chunk_step(ci, state):
        sb = state.astype(BF16)
        sl = pl.ds(pl.multiple_of(ci * c, c), c)
        y = _dot_nt(qt_ref[sl, :], block_diag(sb)) + yloc_ref[sl, :]
        new_state = state * pe_ref[ci, 0:1, :] + nn_ref[ci] - _dot(sb, mt_ref[ci])
        mu = head_sum(y) * (1.0 / HEAD_DIM)
        d = y - mu
        var = head_sum(d * d) * (1.0 / HEAD_DIM)
        yn = d * lax.rsqrt(var + GN_EPS) * gng + gnb + bon_ref[sl, :]
        z = z_ref[0, sl, :]
        o_ref[0, sl, :] = (yn * (z * jax.nn.sigmoid(z))).astype(o_ref.dtype)
        return new_state

    class Chain:
        def __init__(self, gi, state):
            self.gi, self.state, self.done = gi, state, 0

        def tick(self):
            if self.done < grp:
                self.state = chunk_step(self.gi * grp + self.done, self.state)
                self.done += 1

        def drain(self):
            while self.done < grp:
                self.tick()
            return self.state

    n_groups = n_chunks // grp

    def run_group(gi, tick):
        slot = gi % 2
        nxt = prep(jnp.minimum(gi + 1, n_groups - 1), 1 - slot)
        for step, _ in enumerate(pass_a(gi * grp, slot)):
            if step in PREP_STEPS:
                next(nxt, None)
            tick()
        for _ in nxt:
            pass

    def body(gi, state):
        chain = Chain(gi - 1, state)
        run_group(gi, chain.tick)
        return chain.drain()

    for _ in prep(0, 0):
        pass
    run_group(0, lambda: None)
    state = lax.fori_loop(1, n_groups, body, jnp.zeros((HEAD_DIM, LANES), F32))
    Chain(n_groups - 1, state).drain()


def _attn_kernel(q_ref, k_ref, v_ref, z_ref, o_ref, kb_ref, vb_ref, acc_ref, m_ref, l_ref):
    t_len = q_ref.shape[1]
    blk = ATTN_BLOCK
    kb_ref[...] = k_ref[0].astype(BF16)
    vb_ref[...] = v_ref[0].astype(BF16)

    lane = lax.broadcasted_iota(jnp.int32, (blk, LANES), 1)
    is_h0 = lane < HEAD_DIM
    qi = lax.broadcasted_iota(jnp.int32, (blk, 2 * blk), 0)
    kj = lax.broadcasted_iota(jnp.int32, (blk, 2 * blk), 1)
    nb = ATTN_GROUP
    heads = (is_h0, ~is_h0)
    patterns = tuple(reversed(DILATION_PATTERNS))
    assert patterns[-1][1] == 1
    bias_causal = jnp.where(kj <= qi, 0.0, NEG_BIG).astype(F32)

    for pi, (window, dil) in enumerate(patterns):
        span = window // dil
        nblk = t_len // (dil * blk)
        bias_band = jnp.where((qi + blk - kj >= 0) & (qi + blk - kj <= span), 0.0,
                              NEG_BIG).astype(F32)
        last = pi == len(patterns) - 1

        def body(it, _, pi=pi, dil=dil, nblk=nblk, bias_band=bias_band, last=last):
            rows, q, kw, vw, biases = [], [], [], [], []
            if nblk == 2 and nb % 2 == 0:
                for jr in range(nb // 2):
                    res = it * (nb // 2) + jr
                    seq = pl.ds(res, 2 * blk, stride=dil)
                    q_seq = q_ref[0, seq, :]
                    k_seq = k_ref[0, seq, :].astype(BF16)
                    v_seq = v_ref[0, seq, :].astype(BF16)
                    for n in range(2):
                        rows.append(pl.ds(res + dil * blk * n, blk, stride=dil))
                        q.append(q_seq[n * blk:(n + 1) * blk])
                        kw.append(k_seq)
                        vw.append(v_seq)
                        biases.append(bias_band if n else bias_causal)
            else:
                for j in range(nb):
                    res = (it * nb + j) // nblk
                    n = (it * nb + j) % nblk
                    maybe_first = j == 0 or nblk % nb != 0
                    n_prev = jnp.maximum(n - 1, 0) if maybe_first else n - 1
                    q0 = res + dil * blk * n
                    k0 = res + dil * blk * n_prev
                    if dil == 1:
                        rows.append(pl.ds(pl.multiple_of(q0, blk), blk))
                        krows = pl.ds(pl.multiple_of(k0, blk), 2 * blk)
                        kw.append(kb_ref[krows, :])
                        vw.append(vb_ref[krows, :])
                    else:
                        rows.append(pl.ds(q0, blk, stride=dil))
                        krows = pl.ds(k0, 2 * blk, stride=dil)
                        kw.append(k_ref[0, krows, :].astype(BF16))
                        vw.append(v_ref[0, krows, :].astype(BF16))
                    q.append(q_ref[0, rows[j], :])
                    biases.append(jnp.where(n == 0, bias_causal, bias_band) if maybe_first
                                  else bias_band)
            s = [[_dot_nt(jnp.where(h, q[j], 0.0).astype(BF16), kw[j]) + biases[j] for h in heads]
                 for j in range(nb)]
            m = [[jnp.max(x, axis=1, keepdims=True) for x in sj] for sj in s]
            p = [[jnp.exp2(s[j][h] - m[j][h]) for h in range(2)] for j in range(nb)]
            l = [[jnp.sum(x, axis=1, keepdims=True) for x in pj] for pj in p]
            pv = [[_dot(p[j][h].astype(BF16), vw[j]) for h in range(2)] for j in range(nb)]
            for j in range(nb):
                acc = jnp.where(is_h0, pv[j][0], pv[j][1])
                mj = jnp.where(is_h0, m[j][0], m[j][1])
                lj = jnp.where(is_h0, l[j][0], l[j][1])
                if last:
                    ms = [mj] + [m_ref[pp, rows[j], :] for pp in range(pi)]
                    ls = [lj] + [l_ref[pp, rows[j], :] for pp in range(pi)]
                    accs = [acc] + [acc_ref[pp, rows[j], :] for pp in range(pi)]
                    m_all = functools.reduce(jnp.maximum, ms)
                    ws = [jnp.exp2(mm - m_all) for mm in ms]
                    num = functools.reduce(jnp.add, [w * a for w, a in zip(ws, accs)])
                    den = functools.reduce(jnp.add, [w * x for w, x in zip(ws, ls)])
                    z = z_ref[0, rows[j], :]
                    o_ref[0, rows[j], :] = (num / den * (z * jax.nn.sigmoid(z))).astype(o_ref.dtype)
                else:
                    acc_ref[pi, rows[j], :] = acc
                    m_ref[pi, rows[j], :] = mj
                    l_ref[pi, rows[j], :] = lj
            return 0

        lax.fori_loop(0, t_len // (blk * nb), body, 0)


def _out_proj_kernel(ya_ref, yb_ref, wa_ref, wb_ref, x_ref, g_ref, o_ref):
    y = x_ref[...] + _dot(ya_ref[...], wa_ref[...]) + _dot(yb_ref[...], wb_ref[...])
    o_ref[...] = y * lax.rsqrt(jnp.mean(y * y, axis=-1, keepdims=True) + NORM_EPS) * g_ref[...]


def _rope_tables(t_len):
    half = ROPE_DIM // 2
    inv = ROPE_THETA ** (-jnp.arange(half, dtype=F32) * 2.0 / ROPE_DIM)
    ang = jnp.arange(t_len, dtype=jnp.int32).astype(F32)[:, None] * inv[None, :]
    cos, sin = jnp.cos(ang), jnp.sin(ang)
    ones = jnp.ones((t_len, HEAD_DIM - ROPE_DIM), F32)
    zeros = jnp.zeros((t_len, HEAD_DIM - ROPE_DIM), F32)
    zh = jnp.zeros((t_len, half), F32)
    tile = lambda a: jnp.tile(a, (1, LANES // HEAD_DIM))
    cos_t = tile(jnp.concatenate([cos, cos, ones], axis=1))
    sa_t = tile(jnp.concatenate([-sin, zh, zeros], axis=1))
    sb_t = tile(jnp.concatenate([zh, sin, zeros], axis=1))
    return cos_t, sa_t, sb_t


def kernel(x, norm_gain, w_in, shift_mix, decay_base, decay_up, iclr_base, iclr_up, key_norm_scale,
           key_iclr_mix, bonus, gn_gain, gn_bias, w_out, final_gain):
    bsz, t_len, d_model = x.shape
    depth = w_in.shape[0]
    c_a = decay_base.shape[1]
    n_hp = c_a // LANES
    shift_w = 3 * c_a + DECAY_RANK + ICLR_RANK
    in_w = w_in.shape[2]
    c_b = (in_w - shift_w - c_a) // 4
    assert c_a % LANES == 0 and c_b == c_a
    assert all(t_len % (2 * dil * ATTN_BLOCK) == 0 for _, dil in DILATION_PATTERNS)
    o_za = shift_w
    o_q = o_za + c_a
    o_k, o_v, o_zb = o_q + c_b, o_q + 2 * c_b, o_q + 3 * c_b
    segs = ((0, c_a), (c_a, 2 * c_a), (2 * c_a, 3 * c_a), (3 * c_a, shift_w), (o_za, o_q),
            (o_q, o_k), (o_k, o_v), (o_v, o_zb), (o_zb, in_w))
    cos_t, sa_t, sb_t = _rope_tables(t_len)
    rows = bsz * t_len
    tm = IN_PROJ_ROWS
    tiles_per_seq = t_len // tm
    row2 = lambda a: a.reshape(1, -1).astype(F32)

    for layer in range(depth):
        p = pl.pallas_call(
            functools.partial(_in_proj_kernel, segs=segs, shift_w=shift_w, rope_lo=o_q, q_hi=o_k,
                              rope_hi=o_v, tiles_per_seq=tiles_per_seq),
            grid=(rows // tm,),
            in_specs=[
                pl.BlockSpec((tm, d_model), lambda i: (i, 0)),
                pl.BlockSpec((1, d_model), lambda i: (0, 0)),
                pl.BlockSpec((d_model, in_w), lambda i: (0, 0)),
                pl.BlockSpec((1, shift_w), lambda i: (0, 0)),
                pl.BlockSpec((tm, LANES), lambda i: (i % tiles_per_seq, 0)),
                pl.BlockSpec((tm, LANES), lambda i: (i % tiles_per_seq, 0)),
                pl.BlockSpec((tm, LANES), lambda i: (i % tiles_per_seq, 0)),
            ],
            out_specs=pl.BlockSpec((tm, in_w), lambda i: (i, 0)),
            out_shape=jax.ShapeDtypeStruct((rows, in_w), F32),
            scratch_shapes=[pltpu.VMEM((SUBLANES, shift_w), F32)],
            compiler_params=pltpu.CompilerParams(dimension_semantics=("arbitrary",),
                                                 vmem_limit_bytes=VMEM_LIMIT),
            name="in_proj",
        )(x.reshape(rows, d_model), row2(norm_gain[layer]), w_in[layer].astype(BF16),
          row2(shift_mix[layer]), cos_t, sa_t, sb_t)
        p = p.reshape(bsz, t_len, in_w)

        zeros_up = jnp.zeros((DECAY_RANK, c_a), F32)
        up_full = jnp.concatenate(
            [jnp.concatenate([decay_up[layer].astype(F32), zeros_up], axis=0).reshape(
                DECAY_RANK + ICLR_RANK, n_hp, 1, LANES),
             jnp.concatenate([zeros_up, iclr_up[layer].astype(F32)], axis=0).reshape(
                 DECAY_RANK + ICLR_RANK, n_hp, 1, LANES)], axis=2).reshape(
                     DECAY_RANK + ICLR_RANK, n_hp * 2 * LANES)
        col = lambda off: (lambda b, h: (b, 0, off // LANES + h))
        par = lambda b, h: (0, h)
        seq_spec = lambda off: pl.BlockSpec((1, t_len, LANES), col(off))
        par_spec = pl.BlockSpec((1, LANES), par)
        ya = pl.pallas_call(
            _rwkv_kernel,
            grid=(bsz, n_hp),
            in_specs=[seq_spec(0), seq_spec(c_a), seq_spec(2 * c_a),
                      pl.BlockSpec((1, t_len, LANES), lambda b, h: (b, 0, 3 * c_a // LANES)),
                      seq_spec(o_za),
                      pl.BlockSpec((DECAY_RANK + ICLR_RANK, 2 * LANES), par),
                      par_spec, par_spec, par_spec, par_spec, par_spec, par_spec, par_spec],
            out_specs=pl.BlockSpec((1, t_len, LANES), lambda b, h: (b, 0, h)),
            out_shape=jax.ShapeDtypeStruct((bsz, t_len, c_a), BF16),
            scratch_shapes=_rwkv_scratch(t_len),
            compiler_params=pltpu.CompilerParams(dimension_semantics=("arbitrary", "arbitrary"),
                                                 vmem_limit_bytes=VMEM_LIMIT),
            name="rwkv",
        )(p, p, p, p, p, up_full, row2(decay_base[layer]), row2(iclr_base[layer]),
          row2(key_norm_scale[layer]), row2(key_iclr_mix[layer]), row2(bonus[layer]),
          row2(gn_gain[layer]), row2(gn_bias[layer]))

        yb = pl.pallas_call(
            _attn_kernel,
            grid=(bsz, c_b // LANES),
            in_specs=[seq_spec(o_q), seq_spec(o_k), seq_spec(o_v), seq_spec(o_zb)],
            out_specs=pl.BlockSpec((1, t_len, LANES), lambda b, h: (b, 0, h)),
            out_shape=jax.ShapeDtypeStruct((bsz, t_len, c_b), BF16),
            scratch_shapes=[pltpu.VMEM((t_len, LANES), BF16),
                            pltpu.VMEM((t_len, LANES), BF16),
                            pltpu.VMEM((len(DILATION_PATTERNS) - 1, t_len, LANES), F32),
                            pltpu.VMEM((len(DILATION_PATTERNS) - 1, t_len, LANES), F32),
                            pltpu.VMEM((len(DILATION_PATTERNS) - 1, t_len, LANES), F32)],
            compiler_params=pltpu.CompilerParams(dimension_semantics=("arbitrary", "arbitrary"),
                                                 vmem_limit_bytes=VMEM_LIMIT),
            name="attention",
        )(p, p, p, p)

        assert depth == 1
        tmo = OUT_PROJ_ROWS
        wo = w_out[layer].astype(BF16)
        x = pl.pallas_call(
            _out_proj_kernel,
            grid=(rows // tmo,),
            in_specs=[pl.BlockSpec((tmo, c_a), lambda i: (i, 0)),
                      pl.BlockSpec((tmo, c_b), lambda i: (i, 0)),
                      pl.BlockSpec((c_a, d_model), lambda i: (0, 0)),
                      pl.BlockSpec((c_b, d_model), lambda i: (0, 0)),
                      pl.BlockSpec((tmo, d_model), lambda i: (i, 0)),
                      pl.BlockSpec((1, d_model), lambda i: (0, 0))],
            out_specs=pl.BlockSpec((tmo, d_model), lambda i: (i, 0)),
            out_shape=jax.ShapeDtypeStruct((rows, d_model), F32),
            compiler_params=pltpu.CompilerParams(dimension_semantics=("arbitrary",),
                                                 vmem_limit_bytes=VMEM_LIMIT),
            name="out_proj",
        )(ya.reshape(rows, c_a), yb.reshape(rows, c_b), wo[:c_a], wo[c_a:],
          x.reshape(rows, d_model), row2(final_gain)).reshape(bsz, t_len, d_model)
    return x
```

```python
import functools

import jax
import jax.numpy as jnp
from jax import lax
from jax.experimental import pallas as pl
from jax.experimental.pallas import tpu as pltpu

HEAD_DIM = 64
LANES = 128
SUBLANES = 8
DECAY_RANK = 64
ICLR_RANK = 64
DILATION_PATTERNS = ((128, 1), (512, 4), (2048, 16))
ATTN_BLOCK = 128
ATTN_GROUP = 4
ROPE_THETA = 500000.0
ROPE_DIM = HEAD_DIM // 4
NORM_EPS = 1e-6
GN_EPS = 64e-5
CHUNK = 64
SUB = 16
PACK = LANES // (2 * SUB)
RWKV_GROUP = 8
PREP_STEPS = (2, 5, 7)
IN_PROJ_ROWS = 512
OUT_PROJ_ROWS = 1024
DECAY_SCALE = 0.6065306597126334
LOG2_E = 1.4426950408889634
ATTN_Q_SCALE = HEAD_DIM ** -0.5 * LOG2_E
NEG_BIG = -1e30
VMEM_LIMIT = 56 * 1024 * 1024

F32 = jnp.float32
BF16 = jnp.bfloat16


def _dot(a, b, **kw):
    return jnp.dot(a, b, preferred_element_type=F32, **kw)


def _dot_nt(a, b):
    return lax.dot_general(a, b, (((1,), (1,)), ((), ())), preferred_element_type=F32)


def _split2(x):
    hi = x.astype(BF16)
    return hi, (x - hi.astype(F32)).astype(BF16)


def _dot_tn(a, b):
    return lax.dot_general(a, b, (((0,), (0,)), ((), ())), preferred_element_type=F32)


def _in_proj_kernel(x_ref, g_ref, w_ref, mix_ref, cos_ref, sa_ref, sb_ref, o_ref, carry_ref,
                    *, segs, shift_w, rope_lo, q_hi, rope_hi, tiles_per_seq):
    i = pl.program_id(0)

    @pl.when(i == 0)
    def _():
        carry_ref[...] = jnp.zeros(carry_ref.shape, F32)

    x = x_ref[...]
    h = x * lax.rsqrt(jnp.mean(x * x, axis=-1, keepdims=True) + NORM_EPS) * g_ref[...]
    hb = h.astype(BF16)
    tm = x.shape[0]
    first = (i % tiles_per_seq) == 0
    row0 = lax.broadcasted_iota(jnp.int32, (tm, 1), 0) == 0
    for lo, hi in segs:
        p = _dot(hb, w_ref[:, lo:hi])
        if hi <= shift_w:
            old = jnp.where(first, 0.0, carry_ref[0:1, lo:hi])
            carry_ref[0:1, lo:hi] = p[tm - 1:tm, :]
            prev = jnp.where(row0, old, pltpu.roll(p, 1, axis=0))
            p = p + (prev - p) * mix_ref[:, lo:hi]
            o_ref[:, lo:hi] = p
        elif lo >= rope_lo and hi <= rope_hi:
            if hi <= q_hi:
                p = p * ATTN_Q_SCALE
            for j in range(lo, hi, LANES):
                t = p[:, j - lo:j - lo + LANES]
                t = (t * cos_ref[...] + pltpu.roll(t, LANES - ROPE_DIM // 2, axis=1) * sa_ref[...]
                     + pltpu.roll(t, ROPE_DIM // 2, axis=1) * sb_ref[...])
                o_ref[:, j:j + LANES] = t
        else:
            o_ref[:, lo:hi] = p


def _spread_matrix():
    j = lax.broadcasted_iota(jnp.int32, (LANES, (SUB - 1) * LANES), 0)
    col = lax.broadcasted_iota(jnp.int32, (LANES, (SUB - 1) * LANES), 1)
    sel = j == SUB * ((col % LANES) // SUB) + col // LANES
    return jnp.where(sel, 1.0, 0.0).astype(BF16)


def _fold_matrices():
    r = lax.broadcasted_iota(jnp.int32, (LANES, LANES), 0)
    l = lax.broadcasted_iota(jnp.int32, (LANES, LANES), 1)
    per_chunk = LANES // PACK
    match = ((l % per_chunk) // SUB == r // HEAD_DIM) & (l % SUB == r % SUB)
    fold = [jnp.where(match & (l // per_chunk == c), 1.0, 0.0).astype(BF16) for c in range(PACK)]
    rr = lax.broadcasted_iota(jnp.int32, (LANES, PACK * LANES), 0)
    cc = lax.broadcasted_iota(jnp.int32, (LANES, PACK * LANES), 1)
    src = per_chunk * (cc // LANES) + SUB * ((cc % LANES) // HEAD_DIM) + cc % SUB
    unfold = jnp.where(rr == src, 1.0, 0.0).astype(BF16)
    return fold, unfold


def _substitute(coef):
    rows = coef.shape[0]
    ng = rows // SUBLANES
    gpb = SUB // SUBLANES
    row = lax.broadcasted_iota(jnp.int32, (SUBLANES, LANES), 0)
    col = lax.broadcasted_iota(jnp.int32, (SUBLANES, LANES), 1) % SUB
    xs = [jnp.where(col == (row + g * SUBLANES) % SUB, 1.0, 0.0).astype(F32) for g in range(ng)]
    for ss in range(SUB - 1):
        gs, rs = divmod(ss, SUBLANES)
        for blk in range(ng // gpb):
            xrow = xs[blk * gpb + gs][rs:rs + 1, :]
            for g in range(blk * gpb + (ss + 1) // SUBLANES, (blk + 1) * gpb):
                cf = coef[g * SUBLANES:(g + 1) * SUBLANES, ss * LANES:(ss + 1) * LANES]
                xs[g] = xs[g] - cf * xrow
    return jnp.concatenate(xs, axis=0)


def _rwkv_scratch(t_len):
    n_chunks = t_len // CHUNK
    return [pltpu.VMEM((LANES, (SUB - 1) * LANES), BF16),
            pltpu.VMEM((PACK, LANES, LANES), BF16),
            pltpu.VMEM((LANES, PACK * LANES), BF16),
            pltpu.VMEM((t_len, LANES), BF16),
            pltpu.VMEM((t_len, LANES), F32),
            pltpu.VMEM((t_len, LANES), F32),
            pltpu.VMEM((n_chunks, LANES, LANES), BF16),
            pltpu.VMEM((n_chunks, HEAD_DIM, LANES), F32),
            pltpu.VMEM((n_chunks, SUBLANES, LANES), F32),
            pltpu.VMEM((2, RWKV_GROUP, 2 * CHUNK, LANES), BF16),
            pltpu.VMEM((2, RWKV_GROUP, 4 * CHUNK, LANES), BF16),
            pltpu.VMEM((2, RWKV_GROUP, 2 * CHUNK, LANES), BF16),
            pltpu.VMEM((2, RWKV_GROUP, 2 * CHUNK, LANES), BF16),
            pltpu.VMEM((2, RWKV_GROUP * CHUNK, LANES), F32),
            pltpu.VMEM((2, RWKV_GROUP * CHUNK, LANES), BF16),
            pltpu.VMEM((2, RWKV_GROUP * CHUNK, LANES), BF16),
            pltpu.VMEM((2, RWKV_GROUP * CHUNK, LANES), BF16)]


def _rwkv_kernel(r_ref, k_ref, v_ref, wa_ref, z_ref, up_ref, dbase_ref, abase_ref, kns_ref,
                 kim_ref, bonus_ref, gng_ref, gnb_ref, o_ref,
                 spread_ref, fold_ref, unfold_ref, qt_ref, yloc_ref, bon_ref, mt_ref, nn_ref,
                 pe_ref, lhs_ref, rhs_ref, vbd_ref, kbd_ref, rd_ref, vb_ref, kdl_ref, bdl_ref):
    t_len = r_ref.shape[1]
    c = CHUNK
    grp = RWKV_GROUP
    rows = grp * c
    n_chunks = t_len // c
    lane = lax.broadcasted_iota(jnp.int32, (c, LANES), 1)
    is_h0 = lane < HEAD_DIM
    spread_ref[...] = _spread_matrix()
    fold, unfold = _fold_matrices()
    for i in range(PACK):
        fold_ref[i] = fold[i]
    unfold_ref[...] = unfold
    assert grp % PACK == 0
    trow = lax.broadcasted_iota(jnp.int32, (c, LANES), 0)
    scol = lane % HEAD_DIM
    tb, sb = trow // SUB, scol // SUB
    strict = trow > scol
    incl = trow >= scol
    in_block = tb == sb
    ltri = (lax.broadcasted_iota(jnp.int32, (c, c), 0)
            >= lax.broadcasted_iota(jnp.int32, (c, c), 1)).astype(BF16)
    sq_r = lax.broadcasted_iota(jnp.int32, (LANES, LANES), 0)
    sq_c = lax.broadcasted_iota(jnp.int32, (LANES, LANES), 1)
    same_head = sq_r // HEAD_DIM == sq_c // HEAD_DIM

    dbase, abase = dbase_ref[...], abase_ref[...]
    kns, kim, bonus = kns_ref[...], kim_ref[...], bonus_ref[...]
    gng, gnb = gng_ref[...], gnb_ref[...]
    up_hi, up_lo = _split2(up_ref[...])
    up_hh = jnp.concatenate([up_hi, up_hi], axis=0)

    def head_sum(x):
        h0 = lax.broadcasted_iota(jnp.int32, x.shape, 1) < HEAD_DIM
        s0 = jnp.sum(jnp.where(h0, x, 0.0), axis=1, keepdims=True)
        s1 = jnp.sum(jnp.where(h0, 0.0, x), axis=1, keepdims=True)
        return jnp.where(h0, s0, s1)

    def block_diag(x):
        zero = jnp.zeros_like(x)
        return jnp.concatenate([jnp.where(is_h0, x, zero), jnp.where(is_h0, zero, x)],
                               axis=0).astype(BF16)

    n = grp
    js = range(n)
    cs = [slice(j * c, (j + 1) * c) for j in js]

    def prep(gi, slot):
        first_chunk = gi * grp
        sl = pl.ds(pl.multiple_of(first_chunk * c, n * c), n * c)
        r = r_ref[0, sl, :]
        k = k_ref[0, sl, :]
        v = v_ref[0, sl, :]
        wa = wa_ref[0, sl, :]
        is_h0_s = lax.broadcasted_iota(jnp.int32, (n * c, LANES), 1) < HEAD_DIM
        x_hi, x_lo = _split2(jnp.where(is_h0_s, jnp.tanh(wa), wa))
        lin = _dot(jnp.concatenate([x_hi, x_lo], axis=1), up_hh) + _dot(x_hi, up_lo)
        yield
        logw = -DECAY_SCALE * jax.nn.sigmoid(dbase + lin[:, :LANES])
        a = jax.nn.sigmoid(abase + lin[:, LANES:])
        kk = k * kns
        kk = kk * lax.rsqrt(jnp.maximum(head_sum(kk * kk), 1e-24))
        kmod = k * (1.0 + (a - 1.0) * kim)
        b = kk * a
        bon_ref[sl, :] = head_sum(r * kmod * bonus) * v

        parts = jnp.concatenate(_split2(logw), axis=1)
        gs = [_dot(ltri, parts[cs[j]]) for j in js]
        yield
        g = jnp.concatenate([x[:, LANES:] + x[:, :LANES] for x in gs], axis=0)
        p_end = [jnp.exp(g[(j + 1) * c - 1:(j + 1) * c, :]) for j in js]
        e_inv = jnp.exp(-g)
        kkd = kk * jnp.exp(g - logw)
        rd = r * jnp.exp(g)
        bi = b * e_inv
        ki = kmod * e_inv
        e_end = e_inv * jnp.concatenate([jnp.broadcast_to(x, (c, LANES)) for x in p_end], axis=0)
        rd_ref[slot] = rd
        vb_ref[slot] = v.astype(BF16)
        kdl_ref[slot] = (kmod * e_end).astype(BF16)
        bdl_ref[slot] = (b * e_end).astype(BF16)
        for j in js:
            lhs_ref[slot, j] = jnp.concatenate([kkd[cs[j]], rd[cs[j]]], axis=0).astype(BF16)
            rhs_ref[slot, j] = jnp.concatenate([block_diag(bi[cs[j]]), block_diag(ki[cs[j]])],
                                               axis=0)
            vbd_ref[slot, j] = block_diag(v[cs[j]])
            kbd_ref[slot, j] = block_diag(kkd[cs[j]])
            pe_ref[first_chunk + j] = jnp.broadcast_to(p_end[j], (SUBLANES, LANES))

    def pass_a(first_chunk, slot):
        aab, aak, arb, ark = [], [], [], []
        for j in js:
            pair = _dot_nt(lhs_ref[slot, j], rhs_ref[slot, j])
            aab.append(jnp.where(strict, pair[:c, :LANES], 0.0))
            aak.append(jnp.where(strict, pair[:c, LANES:], 0.0))
            arb.append(jnp.where(incl, pair[c:, :LANES], 0.0))
            ark.append(jnp.where(incl, pair[c:, LANES:], 0.0))
        yield
        vbd = [vbd_ref[slot, j] for j in js]
        aakv = [_dot(aak[j].astype(BF16), vbd[j]) for j in js]
        compact = [functools.reduce(
            jnp.add, [_dot(jnp.where(in_block, aab[q * PACK + i], 0.0).astype(BF16), fold_ref[i])
                      for i in range(PACK)]) for q in range(n // PACK)]
        yield
        coef = _dot(jnp.concatenate(compact, axis=0).astype(BF16), spread_ref[...])
        t_diag = _substitute(coef)
        yield
        spread_out = [_dot(t_diag[q * c:(q + 1) * c].astype(BF16), unfold_ref[...])
                      for q in range(n // PACK)]
        tinv = [jnp.where(in_block, spread_out[j // PACK][:, (j % PACK) * LANES:
                                                           (j % PACK + 1) * LANES], 0.0)
                for j in js]
        yield
        span = 1
        while span * SUB < c:
            low_mask = (tb // span == sb // span + 1) & ((tb // span) % 2 == 1)
            inner = [_dot(jnp.where(low_mask, aab[j], 0.0).astype(BF16),
                          block_diag(tinv[j])) for j in js]
            yield
            tinv = [tinv[j] - _dot(tinv[j].astype(BF16), block_diag(inner[j])) for j in js]
            yield
            span *= 2
        wu = [_dot(tinv[j].astype(BF16),
                   jnp.concatenate([kbd_ref[slot, j], block_diag(aakv[j])], axis=1))
              for j in js]
        yield
        w = [x[:, :LANES] for x in wu]
        uloc = [x[:, LANES:] for x in wu]
        arb_b = [arb[j].astype(BF16) for j in js]
        bdl_b = [bdl_ref[slot, cs[j], :] for j in js]
        qt = [rd_ref[slot, cs[j], :] - _dot(arb_b[j], block_diag(w[j])) for j in js]
        yloc = [_dot(jnp.concatenate([ark[j].astype(BF16), -arb_b[j]], axis=1),
                     jnp.concatenate([vbd[j], block_diag(uloc[j])], axis=0))
                for j in js]
        wtb = [_dot_tn(w[j].astype(BF16), bdl_b[j]) for j in js]
        nn = [_dot_tn(jnp.concatenate([vb_ref[slot, cs[j], :], (-uloc[j]).astype(BF16)], axis=0),
                      jnp.concatenate([kdl_ref[slot, cs[j], :], bdl_b[j]], axis=0))
              for j in js]
        yield
        for j in js:
            ci = first_chunk + j
            csl = pl.ds(pl.multiple_of(ci * c, c), c)
            qt_ref[csl, :] = qt[j].astype(BF16)
            yloc_ref[csl, :] = yloc[j]
            mt_ref[ci] = jnp.where(same_head, wtb[j], 0.0).astype(BF16)
            nn_ref[ci] = jnp.where(is_h0, nn[j][:HEAD_DIM], nn[j][HEAD_DIM:])

    def chunk_step(ci, state):
        sb = state.astype(BF16)
        sl = pl.ds(pl.multiple_of(ci * c, c), c)
        y = _dot_nt(qt_ref[sl, :], block_diag(sb)) + yloc_ref[sl, :]
        new_state = state * pe_ref[ci, 0:1, :] + nn_ref[ci] - _dot(sb, mt_ref[ci])
        mu = head_sum(y) * (1.0 / HEAD_DIM)
        d = y - mu
        var = head_sum(d * d) * (1.0 / HEAD_DIM)
        yn = d * lax.rsqrt(var + GN_EPS) * gng + gnb + bon_ref[sl, :]
        z = z_ref[0, sl, :]
        o_ref[0, sl, :] = (yn * (z * jax.nn.sigmoid(z))).astype(o_ref.dtype)
        return new_state

    class Chain:
        def __init__(self, gi, state):
            self.gi, self.state, self.done = gi, state, 0

        def tick(self):
            if self.done < grp:
                self.state = chunk_step(self.gi * grp + self.done, self.state)
                self.done += 1

        def drain(self):
            while self.done < grp:
                self.tick()
            return self.state

    n_groups = n_chunks // grp

    def run_group(gi, slot, chain, prepare_next=True):
        nxt = prep(gi + 1, 1 - slot) if prepare_next else iter(())
        for step, _ in enumerate(pass_a(gi * grp, slot)):
            if step in PREP_STEPS:
                next(nxt, None)
            if chain is not None:
                chain.tick()
        for _ in nxt:
            pass
        return chain.drain() if chain is not None else None

    def body(pair, state):
        gi = 2 * pair + 1
        state = run_group(gi, 1, Chain(gi - 1, state))
        return run_group(gi + 1, 0, Chain(gi, state))

    assert n_groups % 2 == 0
    for _ in prep(0, 0):
        pass
    run_group(0, 0, None)
    state = lax.fori_loop(0, (n_groups - 2) // 2, body, jnp.zeros((HEAD_DIM, LANES), F32))
    state = run_group(n_groups - 1, 1, Chain(n_groups - 2, state), prepare_next=False)
    Chain(n_groups - 1, state).drain()


def _attn_kernel(q_ref, k_ref, v_ref, z_ref, o_ref, kb_ref, vb_ref, acc_ref, m_ref, l_ref):
    t_len = q_ref.shape[1]
    blk = ATTN_BLOCK
    kb_ref[...] = k_ref[0].astype(BF16)
    vb_ref[...] = v_ref[0].astype(BF16)

    lane = lax.broadcasted_iota(jnp.int32, (blk, LANES), 1)
    is_h0 = lane < HEAD_DIM
    qi = lax.broadcasted_iota(jnp.int32, (blk, 2 * blk), 0)
    kj = lax.broadcasted_iota(jnp.int32, (blk, 2 * blk), 1)
    nb = ATTN_GROUP
    heads = (is_h0, ~is_h0)
    patterns = tuple(reversed(DILATION_PATTERNS))
    assert patterns[-1][1] == 1
    bias_causal = jnp.where(kj <= qi, 0.0, NEG_BIG).astype(F32)

    for pi, (window, dil) in enumerate(patterns):
        span = window // dil
        nblk = t_len // (dil * blk)
        bias_band = jnp.where((qi + blk - kj >= 0) & (qi + blk - kj <= span), 0.0,
                              NEG_BIG).astype(F32)
        last = pi == len(patterns) - 1

        def body(it, _, pi=pi, dil=dil, nblk=nblk, bias_band=bias_band, last=last):
            rows, q, kw, vw, biases = [], [], [], [], []
            if nblk == 2 and nb % 2 == 0:
                for jr in range(nb // 2):
                    res = it * (nb // 2) + jr
                    seq = pl.ds(res, 2 * blk, stride=dil)
                    q_seq = q_ref[0, seq, :]
                    k_seq = k_ref[0, seq, :].astype(BF16)
                    v_seq = v_ref[0, seq, :].astype(BF16)
                    for n in range(2):
                        rows.append(pl.ds(res + dil * blk * n, blk, stride=dil))
                        q.append(q_seq[n * blk:(n + 1) * blk])
                        kw.append(k_seq)
                        vw.append(v_seq)
                        biases.append(bias_band if n else bias_causal)
            else:
                for j in range(nb):
                    res = (it * nb + j) // nblk
                    n = (it * nb + j) % nblk
                    maybe_first = j == 0 or nblk % nb != 0
                    n_prev = jnp.maximum(n - 1, 0) if maybe_first else n - 1
                    q0 = res + dil * blk * n
                    k0 = res + dil * blk * n_prev
                    if dil == 1:
                        rows.append(pl.ds(pl.multiple_of(q0, blk), blk))
                        krows = pl.ds(pl.multiple_of(k0, blk), 2 * blk)
                        kw.append(kb_ref[krows, :])
                        vw.append(vb_ref[krows, :])
                    else:
                        rows.append(pl.ds(q0, blk, stride=dil))
                        krows = pl.ds(k0, 2 * blk, stride=dil)
                        kw.append(k_ref[0, krows, :].astype(BF16))
                        vw.append(v_ref[0, krows, :].astype(BF16))
                    q.append(q_ref[0, rows[j], :])
                    biases.append(jnp.where(n == 0, bias_causal, bias_band) if maybe_first
                                  else bias_band)
            s = [[_dot_nt(jnp.where(h, q[j], 0.0).astype(BF16), kw[j]) + biases[j] for h in heads]
                 for j in range(nb)]
            m = [[jnp.max(x, axis=1, keepdims=True) for x in sj] for sj in s]
            p = [[jnp.exp2(s[j][h] - m[j][h]) for h in range(2)] for j in range(nb)]
            l = [[jnp.sum(x, axis=1, keepdims=True) for x in pj] for pj in p]
            pv = [[_dot(p[j][h].astype(BF16), vw[j]) for h in range(2)] for j in range(nb)]
            for j in range(nb):
                acc = jnp.where(is_h0, pv[j][0], pv[j][1])
                mj = jnp.where(is_h0, m[j][0], m[j][1])
                lj = jnp.where(is_h0, l[j][0], l[j][1])
                if last:
                    ms = [mj] + [m_ref[pp, rows[j], :] for pp in range(pi)]
                    ls = [lj] + [l_ref[pp, rows[j], :] for pp in range(pi)]
                    accs = [acc] + [acc_ref[pp, rows[j], :] for pp in range(pi)]
                    m_all = functools.reduce(jnp.maximum, ms)
                    ws = [jnp.exp2(mm - m_all) for mm in ms]
                    num = functools.reduce(jnp.add, [w * a for w, a in zip(ws, accs)])
                    den = functools.reduce(jnp.add, [w * x for w, x in zip(ws, ls)])
                    z = z_ref[0, rows[j], :]
                    o_ref[0, rows[j], :] = (num / den * (z * jax.nn.sigmoid(z))).astype(o_ref.dtype)
                else:
                    acc_ref[pi, rows[j], :] = acc
                    m_ref[pi, rows[j], :] = mj
                    l_ref[pi, rows[j], :] = lj
            return 0

        lax.fori_loop(0, t_len // (blk * nb), body, 0)


def _out_proj_kernel(ya_ref, yb_ref, wa_ref, wb_ref, x_ref, g_ref, o_ref):
    y = x_ref[...] + _dot(ya_ref[...], wa_ref[...]) + _dot(yb_ref[...], wb_ref[...])
    o_ref[...] = y * lax.rsqrt(jnp.mean(y * y, axis=-1, keepdims=True) + NORM_EPS) * g_ref[...]


def _rope_tables(t_len):
    half = ROPE_DIM // 2
    inv = ROPE_THETA ** (-jnp.arange(half, dtype=F32) * 2.0 / ROPE_DIM)
    ang = jnp.arange(t_len, dtype=jnp.int32).astype(F32)[:, None] * inv[None, :]
    cos, sin = jnp.cos(ang), jnp.sin(ang)
    ones = jnp.ones((t_len, HEAD_DIM - ROPE_DIM), F32)
    zeros = jnp.zeros((t_len, HEAD_DIM - ROPE_DIM), F32)
    zh = jnp.zeros((t_len, half), F32)
    tile = lambda a: jnp.tile(a, (1, LANES // HEAD_DIM))
    cos_t = tile(jnp.concatenate([cos, cos, ones], axis=1))
    sa_t = tile(jnp.concatenate([-sin, zh, zeros], axis=1))
    sb_t = tile(jnp.concatenate([zh, sin, zeros], axis=1))
    return cos_t, sa_t, sb_t


def kernel(x, norm_gain, w_in, shift_mix, decay_base, decay_up, iclr_base, iclr_up, key_norm_scale,
           key_iclr_mix, bonus, gn_gain, gn_bias, w_out, final_gain):
    bsz, t_len, d_model = x.shape
    depth = w_in.shape[0]
    c_a = decay_base.shape[1]
    n_hp = c_a // LANES
    shift_w = 3 * c_a + DECAY_RANK + ICLR_RANK
    in_w = w_in.shape[2]
    c_b = (in_w - shift_w - c_a) // 4
    assert c_a % LANES == 0 and c_b == c_a
    assert all(t_len % (2 * dil * ATTN_BLOCK) == 0 for _, dil in DILATION_PATTERNS)
    o_za = shift_w
    o_q = o_za + c_a
    o_k, o_v, o_zb = o_q + c_b, o_q + 2 * c_b, o_q + 3 * c_b
    segs = ((0, c_a), (c_a, 2 * c_a), (2 * c_a, 3 * c_a), (3 * c_a, shift_w), (o_za, o_q),
            (o_q, o_k), (o_k, o_v), (o_v, o_zb), (o_zb, in_w))
    cos_t, sa_t, sb_t = _rope_tables(t_len)
    rows = bsz * t_len
    tm = IN_PROJ_ROWS
    tiles_per_seq = t_len // tm
    row2 = lambda a: a.reshape(1, -1).astype(F32)

    for layer in range(depth):
        p = pl.pallas_call(
            functools.partial(_in_proj_kernel, segs=segs, shift_w=shift_w, rope_lo=o_q, q_hi=o_k,
                              rope_hi=o_v, tiles_per_seq=tiles_per_seq),
            grid=(rows // tm,),
            in_specs=[
                pl.BlockSpec((tm, d_model), lambda i: (i, 0)),
                pl.BlockSpec((1, d_model), lambda i: (0, 0)),
                pl.BlockSpec((d_model, in_w), lambda i: (0, 0)),
                pl.BlockSpec((1, shift_w), lambda i: (0, 0)),
                pl.BlockSpec((tm, LANES), lambda i: (i % tiles_per_seq, 0)),
                pl.BlockSpec((tm, LANES), lambda i: (i % tiles_per_seq, 0)),
                pl.BlockSpec((tm, LANES), lambda i: (i % tiles_per_seq, 0)),
            ],
            out_specs=pl.BlockSpec((tm, in_w), lambda i: (i, 0)),
            out_shape=jax.ShapeDtypeStruct((rows, in_w), F32),
            scratch_shapes=[pltpu.VMEM((SUBLANES, shift_w), F32)],
            compiler_params=pltpu.CompilerParams(dimension_semantics=("arbitrary",),
                                                 vmem_limit_bytes=VMEM_LIMIT),
            name="in_proj",
        )(x.reshape(rows, d_model), row2(norm_gain[layer]), w_in[layer].astype(BF16),
          row2(shift_mix[layer]), cos_t, sa_t, sb_t)
        p = p.reshape(bsz, t_len, in_w)

        zeros_up = jnp.zeros((DECAY_RANK, c_a), F32)
        up_full = jnp.concatenate(
            [jnp.concatenate([decay_up[layer].astype(F32), zeros_up], axis=0).reshape(
                DECAY_RANK + ICLR_RANK, n_hp, 1, LANES),
             jnp.concatenate([zeros_up, iclr_up[layer].astype(F32)], axis=0).reshape(
                 DECAY_RANK + ICLR_RANK, n_hp, 1, LANES)], axis=2).reshape(
                     DECAY_RANK + ICLR_RANK, n_hp * 2 * LANES)
        col = lambda off: (lambda b, h: (b, 0, off // LANES + h))
        par = lambda b, h: (0, h)
        seq_spec = lambda off: pl.BlockSpec((1, t_len, LANES), col(off))
        par_spec = pl.BlockSpec((1, LANES), par)
        ya = pl.pallas_call(
            _rwkv_kernel,
            grid=(bsz, n_hp),
            in_specs=[seq_spec(0), seq_spec(c_a), seq_spec(2 * c_a),
                      pl.BlockSpec((1, t_len, LANES), lambda b, h: (b, 0, 3 * c_a // LANES)),
                      seq_spec(o_za),
                      pl.BlockSpec((DECAY_RANK + ICLR_RANK, 2 * LANES), par),
                      par_spec, par_spec, par_spec, par_spec, par_spec, par_spec, par_spec],
            out_specs=pl.BlockSpec((1, t_len, LANES), lambda b, h: (b, 0, h)),
            out_shape=jax.ShapeDtypeStruct((bsz, t_len, c_a), BF16),
            scratch_shapes=_rwkv_scratch(t_len),
            compiler_params=pltpu.CompilerParams(dimension_semantics=("arbitrary", "arbitrary"),
                                                 vmem_limit_bytes=VMEM_LIMIT),
            name="rwkv",
        )(p, p, p, p, p, up_full, row2(decay_base[layer]), row2(iclr_base[layer]),
          row2(key_norm_scale[layer]), row2(key_iclr_mix[layer]), row2(bonus[layer]),
          row2(gn_gain[layer]), row2(gn_bias[layer]))

        yb = pl.pallas_call(
            _attn_kernel,
            grid=(bsz, c_b // LANES),
            in_specs=[seq_spec(o_q), seq_spec(o_k), seq_spec(o_v), seq_spec(o_zb)],
            out_specs=pl.BlockSpec((1, t_len, LANES), lambda b, h: (b, 0, h)),
            out_shape=jax.ShapeDtypeStruct((bsz, t_len, c_b), BF16),
            scratch_shapes=[pltpu.VMEM((t_len, LANES), BF16),
                            pltpu.VMEM((t_len, LANES), BF16),
                            pltpu.VMEM((len(DILATION_PATTERNS) - 1, t_len, LANES), F32),
                            pltpu.VMEM((len(DILATION_PATTERNS) - 1, t_len, LANES), F32),
                            pltpu.VMEM((len(DILATION_PATTERNS) - 1, t_len, LANES), F32)],
            compiler_params=pltpu.CompilerParams(dimension_semantics=("arbitrary", "arbitrary"),
                                                 vmem_limit_bytes=VMEM_LIMIT),
            name="attention",
        )(p, p, p, p)

        assert depth == 1
        tmo = OUT_PROJ_ROWS
        wo = w_out[layer].astype(BF16)
        x = pl.pallas_call(
            _out_proj_kernel,
            grid=(rows // tmo,),
            in_specs=[pl.BlockSpec((tmo, c_a), lambda i: (i, 0)),
                      pl.BlockSpec((tmo, c_b), lambda i: (i, 0)),
                      pl.BlockSpec((c_a, d_model), lambda i: (0, 0)),
                      pl.BlockSpec((c_b, d_model), lambda i: (0, 0)),
                      pl.BlockSpec((tmo, d_model), lambda i: (i, 0)),
                      pl.BlockSpec((1, d_model), lambda i: (0, 0))],
            out_specs=pl.BlockSpec((tmo, d_model), lambda i: (i, 0)),
            out_shape=jax.ShapeDtypeStruct((rows, d_model), F32),
            compiler_params=pltpu.CompilerParams(dimension_semantics=("arbitrary",),
                                                 vmem_limit_bytes=VMEM_LIMIT),
            name="out_proj",
        )(ya.reshape(rows, c_a), yb.reshape(rows, c_b), wo[:c_a], wo[c_a:],
          x.reshape(rows, d_model), row2(final_gain)).reshape(bsz, t_len, d_model)
    return x
```

```python
import functools

import jax
import jax.numpy as jnp
from jax import lax
from jax.experimental import pallas as pl
from jax.experimental.pallas import tpu as pltpu

HEAD_DIM = 64
LANES = 128
SUBLANES = 8
DECAY_RANK = 64
ICLR_RANK = 64
DILATION_PATTERNS = ((128, 1), (512, 4), (2048, 16))
ATTN_BLOCK = 128
ATTN_GROUP = 4
ROPE_THETA = 500000.0
ROPE_DIM = HEAD_DIM // 4
NORM_EPS = 1e-6
GN_EPS = 64e-5
CHUNK = 64
SUB = 16
PACK = LANES // (2 * SUB)
RWKV_GROUP = 8
PREP_STEPS = (2, 5, 7)
IN_PROJ_ROWS = 512
OUT_PROJ_ROWS = 1024
DECAY_SCALE = 0.6065306597126334
LOG2_E = 1.4426950408889634
ATTN_Q_SCALE = HEAD_DIM ** -0.5 * LOG2_E
NEG_BIG = -1e30
VMEM_LIMIT = 56 * 1024 * 1024

F32 = jnp.float32
BF16 = jnp.bfloat16


def _dot(a, b, **kw):
    return jnp.dot(a, b, preferred_element_type=F32, **kw)


def _dot_nt(a, b):
    return lax.dot_general(a, b, (((1,), (1,)), ((), ())), preferred_element_type=F32)


def _split2(x):
    hi = x.astype(BF16)
    return hi, (x - hi.astype(F32)).astype(BF16)


def _dot_tn(a, b):
    return lax.dot_general(a, b, (((0,), (0,)), ((), ())), preferred_element_type=F32)


def _in_proj_kernel(x_ref, g_ref, w_ref, mix_ref, cos_ref, sa_ref, sb_ref, o_ref, carry_ref,
                    *, segs, shift_w, rope_lo, q_hi, rope_hi, tiles_per_seq):
    i = pl.program_id(0)

    @pl.when(i == 0)
    def _():
        carry_ref[...] = jnp.zeros(carry_ref.shape, F32)

    x = x_ref[...]
    h = x * lax.rsqrt(jnp.mean(x * x, axis=-1, keepdims=True) + NORM_EPS) * g_ref[...]
    hb = h.astype(BF16)
    tm = x.shape[0]
    first = (i % tiles_per_seq) == 0
    row0 = lax.broadcasted_iota(jnp.int32, (tm, 1), 0) == 0
    for lo, hi in segs:
        p = _dot(hb, w_ref[:, lo:hi])
        if hi <= shift_w:
            old = jnp.where(first, 0.0, carry_ref[0:1, lo:hi])
            carry_ref[0:1, lo:hi] = p[tm - 1:tm, :]
            prev = jnp.where(row0, old, pltpu.roll(p, 1, axis=0))
            p = p + (prev - p) * mix_ref[:, lo:hi]
            o_ref[:, lo:hi] = p
        elif lo >= rope_lo and hi <= rope_hi:
            if hi <= q_hi:
                p = p * ATTN_Q_SCALE
            for j in range(lo, hi, LANES):
                t = p[:, j - lo:j - lo + LANES]
                t = (t * cos_ref[...] + pltpu.roll(t, LANES - ROPE_DIM // 2, axis=1) * sa_ref[...]
                     + pltpu.roll(t, ROPE_DIM // 2, axis=1) * sb_ref[...])
                o_ref[:, j:j + LANES] = t
        else:
            o_ref[:, lo:hi] = p


def _spread_matrix():
    j = lax.broadcasted_iota(jnp.int32, (LANES, (SUB - 1) * LANES), 0)
    col = lax.broadcasted_iota(jnp.int32, (LANES, (SUB - 1) * LANES), 1)
    sel = j == SUB * ((col % LANES) // SUB) + col // LANES
    return jnp.where(sel, 1.0, 0.0).astype(BF16)


def _fold_matrices():
    r = lax.broadcasted_iota(jnp.int32, (LANES, LANES), 0)
    l = lax.broadcasted_iota(jnp.int32, (LANES, LANES), 1)
    per_chunk = LANES // PACK
    match = ((l % per_chunk) // SUB == r // HEAD_DIM) & (l % SUB == r % SUB)
    fold = [jnp.where(match & (l // per_chunk == c), 1.0, 0.0).astype(BF16) for c in range(PACK)]
    rr = lax.broadcasted_iota(jnp.int32, (LANES, PACK * LANES), 0)
    cc = lax.broadcasted_iota(jnp.int32, (LANES, PACK * LANES), 1)
    src = per_chunk * (cc // LANES) + SUB * ((cc % LANES) // HEAD_DIM) + cc % SUB
    unfold = jnp.where(rr == src, 1.0, 0.0).astype(BF16)
    return fold, unfold


def _substitute(coef):
    rows = coef.shape[0]
    ng = rows // SUBLANES
    gpb = SUB // SUBLANES
    row = lax.broadcasted_iota(jnp.int32, (SUBLANES, LANES), 0)
    col = lax.broadcasted_iota(jnp.int32, (SUBLANES, LANES), 1) % SUB
    xs = [jnp.where(col == (row + g * SUBLANES) % SUB, 1.0, 0.0).astype(F32) for g in range(ng)]
    for ss in range(SUB - 1):
        gs, rs = divmod(ss, SUBLANES)
        for blk in range(ng // gpb):
            xrow = xs[blk * gpb + gs][rs:rs + 1, :]
            for g in range(blk * gpb + (ss + 1) // SUBLANES, (blk + 1) * gpb):
                cf = coef[g * SUBLANES:(g + 1) * SUBLANES, ss * LANES:(ss + 1) * LANES]
                xs[g] = xs[g] - cf * xrow
    return jnp.concatenate(xs, axis=0)


def _rwkv_scratch(t_len):
    n_chunks = t_len // CHUNK
    return [pltpu.VMEM((t_len, LANES), BF16),
            pltpu.VMEM((t_len, LANES), F32),
            pltpu.VMEM((t_len, LANES), F32),
            pltpu.VMEM((n_chunks, LANES, LANES), BF16),
            pltpu.VMEM((n_chunks, HEAD_DIM, LANES), F32),
            pltpu.VMEM((n_chunks, SUBLANES, LANES), F32),
            pltpu.VMEM((2, RWKV_GROUP, 2 * CHUNK, LANES), BF16),
            pltpu.VMEM((2, RWKV_GROUP, 4 * CHUNK, LANES), BF16),
            pltpu.VMEM((2, RWKV_GROUP, 2 * CHUNK, LANES), BF16),
            pltpu.VMEM((2, RWKV_GROUP, 2 * CHUNK, LANES), BF16),
            pltpu.VMEM((2, RWKV_GROUP * CHUNK, LANES), F32),
            pltpu.VMEM((2, RWKV_GROUP * CHUNK, LANES), BF16),
            pltpu.VMEM((2, RWKV_GROUP * CHUNK, LANES), BF16),
            pltpu.VMEM((2, RWKV_GROUP * CHUNK, LANES), BF16)]


def _rwkv_kernel(r_ref, k_ref, v_ref, wa_ref, z_ref, up_ref, dbase_ref, abase_ref, kns_ref,
                 kim_ref, bonus_ref, gng_ref, gnb_ref, spread_ref, fold_ref, unfold_ref, o_ref,
                 qt_ref, yloc_ref, bon_ref, mt_ref, nn_ref,
                 pe_ref, lhs_ref, rhs_ref, vbd_ref, kbd_ref, rd_ref, vb_ref, kdl_ref, bdl_ref):
    t_len = r_ref.shape[1]
    c = CHUNK
    grp = RWKV_GROUP
    rows = grp * c
    n_chunks = t_len // c
    lane = lax.broadcasted_iota(jnp.int32, (c, LANES), 1)
    is_h0 = lane < HEAD_DIM
    assert grp % PACK == 0
    trow = lax.broadcasted_iota(jnp.int32, (c, LANES), 0)
    scol = lane % HEAD_DIM
    tb, sb = trow // SUB, scol // SUB
    strict = trow > scol
    incl = trow >= scol
    in_block = tb == sb
    ltri = (lax.broadcasted_iota(jnp.int32, (c, c), 0)
            >= lax.broadcasted_iota(jnp.int32, (c, c), 1)).astype(BF16)
    sq_r = lax.broadcasted_iota(jnp.int32, (LANES, LANES), 0)
    sq_c = lax.broadcasted_iota(jnp.int32, (LANES, LANES), 1)
    same_head = sq_r // HEAD_DIM == sq_c // HEAD_DIM

    dbase, abase = dbase_ref[...], abase_ref[...]
    kns, kim, bonus = kns_ref[...], kim_ref[...], bonus_ref[...]
    gng, gnb = gng_ref[...], gnb_ref[...]
    up_hi, up_lo = _split2(up_ref[...])
    up_hh = jnp.concatenate([up_hi, up_hi], axis=0)

    def head_sum(x):
        h0 = lax.broadcasted_iota(jnp.int32, x.shape, 1) < HEAD_DIM
        s0 = jnp.sum(jnp.where(h0, x, 0.0), axis=1, keepdims=True)
        s1 = jnp.sum(jnp.where(h0, 0.0, x), axis=1, keepdims=True)
        return jnp.where(h0, s0, s1)

    def block_diag(x):
        zero = jnp.zeros_like(x)
        return jnp.concatenate([jnp.where(is_h0, x, zero), jnp.where(is_h0, zero, x)],
                               axis=0).astype(BF16)

    n = grp
    js = range(n)
    cs = [slice(j * c, (j + 1) * c) for j in js]

    def prep(gi, slot):
        first_chunk = gi * grp
        sl = pl.ds(pl.multiple_of(first_chunk * c, n * c), n * c)
        r = r_ref[0, sl, :]
        k = k_ref[0, sl, :]
        v = v_ref[0, sl, :]
        wa = wa_ref[0, sl, :]
        is_h0_s = lax.broadcasted_iota(jnp.int32, (n * c, LANES), 1) < HEAD_DIM
        x_hi, x_lo = _split2(jnp.where(is_h0_s, jnp.tanh(wa), wa))
        lin = _dot(jnp.concatenate([x_hi, x_lo], axis=1), up_hh) + _dot(x_hi, up_lo)
        yield
        logw = -DECAY_SCALE * jax.nn.sigmoid(dbase + lin[:, :LANES])
        a = jax.nn.sigmoid(abase + lin[:, LANES:])
        kk = k * kns
        kk = kk * lax.rsqrt(jnp.maximum(head_sum(kk * kk), 1e-24))
        kmod = k * (1.0 + (a - 1.0) * kim)
        b = kk * a
        bon_ref[sl, :] = head_sum(r * kmod * bonus) * v

        parts = jnp.concatenate(_split2(logw), axis=1)
        gs = [_dot(ltri, parts[cs[j]]) for j in js]
        yield
        g = jnp.concatenate([x[:, LANES:] + x[:, :LANES] for x in gs], axis=0)
        p_end = [jnp.exp(g[(j + 1) * c - 1:(j + 1) * c, :]) for j in js]
        e_inv = jnp.exp(-g)
        kkd = kk * jnp.exp(g - logw)
        rd = r * jnp.exp(g)
        bi = b * e_inv
        ki = kmod * e_inv
        e_end = e_inv * jnp.concatenate([jnp.broadcast_to(x, (c, LANES)) for x in p_end], axis=0)
        rd_ref[slot] = rd
        vb_ref[slot] = v.astype(BF16)
        kdl_ref[slot] = (kmod * e_end).astype(BF16)
        bdl_ref[slot] = (b * e_end).astype(BF16)
        for j in js:
            lhs_ref[slot, j] = jnp.concatenate([kkd[cs[j]], rd[cs[j]]], axis=0).astype(BF16)
            rhs_ref[slot, j] = jnp.concatenate([block_diag(bi[cs[j]]), block_diag(ki[cs[j]])],
                                               axis=0)
            vbd_ref[slot, j] = block_diag(v[cs[j]])
            kbd_ref[slot, j] = block_diag(kkd[cs[j]])
            pe_ref[first_chunk + j] = jnp.broadcast_to(p_end[j], (SUBLANES, LANES))

    def pass_a(first_chunk, slot):
        aab, aak, arb, ark = [], [], [], []
        for j in js:
            pair = _dot_nt(lhs_ref[slot, j], rhs_ref[slot, j])
            aab.append(jnp.where(strict, pair[:c, :LANES], 0.0))
            aak.append(jnp.where(strict, pair[:c, LANES:], 0.0))
            arb.append(jnp.where(incl, pair[c:, :LANES], 0.0))
            ark.append(jnp.where(incl, pair[c:, LANES:], 0.0))
        yield
        vbd = [vbd_ref[slot, j] for j in js]
        aakv = [_dot(aak[j].astype(BF16), vbd[j]) for j in js]
        compact = [functools.reduce(
            jnp.add, [_dot(jnp.where(in_block, aab[q * PACK + i], 0.0).astype(BF16), fold_ref[i])
                      for i in range(PACK)]) for q in range(n // PACK)]
        yield
        coef = _dot(jnp.concatenate(compact, axis=0).astype(BF16), spread_ref[...])
        t_diag = _substitute(coef)
        yield
        spread_out = [_dot(t_diag[q * c:(q + 1) * c].astype(BF16), unfold_ref[...])
                      for q in range(n // PACK)]
        tinv = [jnp.where(in_block, spread_out[j // PACK][:, (j % PACK) * LANES:
                                                           (j % PACK + 1) * LANES], 0.0)
                for j in js]
        yield
        span = 1
        while span * SUB < c:
            low_mask = (tb // span == sb // span + 1) & ((tb // span) % 2 == 1)
            inner = [_dot(jnp.where(low_mask, aab[j], 0.0).astype(BF16),
                          block_diag(tinv[j])) for j in js]
            yield
            tinv = [tinv[j] - _dot(tinv[j].astype(BF16), block_diag(inner[j])) for j in js]
            yield
            span *= 2
        wu = [_dot(tinv[j].astype(BF16),
                   jnp.concatenate([kbd_ref[slot, j], block_diag(aakv[j])], axis=1))
              for j in js]
        yield
        w = [x[:, :LANES] for x in wu]
        uloc = [x[:, LANES:] for x in wu]
        arb_b = [arb[j].astype(BF16) for j in js]
        bdl_b = [bdl_ref[slot, cs[j], :] for j in js]
        qt = [rd_ref[slot, cs[j], :] - _dot(arb_b[j], block_diag(w[j])) for j in js]
        yloc = [_dot(jnp.concatenate([ark[j].astype(BF16), -arb_b[j]], axis=1),
                     jnp.concatenate([vbd[j], block_diag(uloc[j])], axis=0))
                for j in js]
        wtb = [_dot_tn(w[j].astype(BF16), bdl_b[j]) for j in js]
        nn = [_dot_tn(jnp.concatenate([vb_ref[slot, cs[j], :], (-uloc[j]).astype(BF16)], axis=0),
                      jnp.concatenate([kdl_ref[slot, cs[j], :], bdl_b[j]], axis=0))
              for j in js]
        yield
        for j in js:
            ci = first_chunk + j
            csl = pl.ds(pl.multiple_of(ci * c, c), c)
            qt_ref[csl, :] = qt[j].astype(BF16)
            yloc_ref[csl, :] = yloc[j]
            mt_ref[ci] = jnp.where(same_head, wtb[j], 0.0).astype(BF16)
            nn_ref[ci] = jnp.where(is_h0, nn[j][:HEAD_DIM], nn[j][HEAD_DIM:])

    def chunk_step(ci, state):
        sb = state.astype(BF16)
        sl = pl.ds(pl.multiple_of(ci * c, c), c)
        y = _dot_nt(qt_ref[sl, :], block_diag(sb)) + yloc_ref[sl, :]
        new_state = state * pe_ref[ci, 0:1, :] + nn_ref[ci] - _dot(sb, mt_ref[ci])
        mu = head_sum(y) * (1.0 / HEAD_DIM)
        d = y - mu
        var = head_sum(d * d) * (1.0 / HEAD_DIM)
        yn = d * lax.rsqrt(var + GN_EPS) * gng + gnb + bon_ref[sl, :]
        z = z_ref[0, sl, :]
        o_ref[0, sl, :] = (yn * (z * jax.nn.sigmoid(z))).astype(o_ref.dtype)
        return new_state

    class Chain:
        def __init__(self, gi, state):
            self.gi, self.state, self.done = gi, state, 0

        def tick(self):
            if self.done < grp:
                self.state = chunk_step(self.gi * grp + self.done, self.state)
                self.done += 1

        def drain(self):
            while self.done < grp:
                self.tick()
            return self.state

    n_groups = n_chunks // grp

    def run_group(gi, slot, chain, prepare_next=True):
        nxt = prep(gi + 1, 1 - slot) if prepare_next else iter(())
        for step, _ in enumerate(pass_a(gi * grp, slot)):
            if step in PREP_STEPS:
                next(nxt, None)
            if chain is not None:
                chain.tick()
        for _ in nxt:
            pass
        return chain.drain() if chain is not None else None

    def body(pair, state):
        gi = 2 * pair + 1
        state = run_group(gi, 1, Chain(gi - 1, state))
        return run_group(gi + 1, 0, Chain(gi, state))

    assert n_groups % 2 == 0
    for _ in prep(0, 0):
        pass
    run_group(0, 0, None)
    state = lax.fori_loop(0, (n_groups - 2) // 2, body, jnp.zeros((HEAD_DIM, LANES), F32))
    state = run_group(n_groups - 1, 1, Chain(n_groups - 2, state), prepare_next=False)
    Chain(n_groups - 1, state).drain()


def _attn_kernel(q_ref, k_ref, v_ref, z_ref, o_ref, kb_ref, vb_ref, acc_ref, m_ref, l_ref):
    t_len = q_ref.shape[1]
    blk = ATTN_BLOCK
    kb_ref[...] = k_ref[0].astype(BF16)
    vb_ref[...] = v_ref[0].astype(BF16)

    lane = lax.broadcasted_iota(jnp.int32, (blk, LANES), 1)
    is_h0 = lane < HEAD_DIM
    qi = lax.broadcasted_iota(jnp.int32, (blk, 2 * blk), 0)
    kj = lax.broadcasted_iota(jnp.int32, (blk, 2 * blk), 1)
    nb = ATTN_GROUP
    heads = (is_h0, ~is_h0)
    patterns = tuple(reversed(DILATION_PATTERNS))
    assert patterns[-1][1] == 1
    bias_causal = jnp.where(kj <= qi, 0.0, NEG_BIG).astype(F32)

    for pi, (window, dil) in enumerate(patterns):
        span = window // dil
        nblk = t_len // (dil * blk)
        bias_band = jnp.where((qi + blk - kj >= 0) & (qi + blk - kj <= span), 0.0,
                              NEG_BIG).astype(F32)
        last = pi == len(patterns) - 1

        def body(it, _, pi=pi, dil=dil, nblk=nblk, bias_band=bias_band, last=last):
            rows, q, kw, vw, biases = [], [], [], [], []
            if nblk == 2 and nb % 2 == 0:
                for jr in range(nb // 2):
                    res = it * (nb // 2) + jr
                    seq = pl.ds(res, 2 * blk, stride=dil)
                    q_seq = q_ref[0, seq, :]
                    k_seq = k_ref[0, seq, :].astype(BF16)
                    v_seq = v_ref[0, seq, :].astype(BF16)
                    for n in range(2):
                        rows.append(pl.ds(res + dil * blk * n, blk, stride=dil))
                        q.append(q_seq[n * blk:(n + 1) * blk])
                        kw.append(k_seq)
                        vw.append(v_seq)
                        biases.append(bias_band if n else bias_causal)
            else:
                for j in range(nb):
                    res = (it * nb + j) // nblk
                    n = (it * nb + j) % nblk
                    maybe_first = j == 0 or nblk % nb != 0
                    n_prev = jnp.maximum(n - 1, 0) if maybe_first else n - 1
                    q0 = res + dil * blk * n
                    k0 = res + dil * blk * n_prev
                    if dil == 1:
                        rows.append(pl.ds(pl.multiple_of(q0, blk), blk))
                        krows = pl.ds(pl.multiple_of(k0, blk), 2 * blk)
                        kw.append(kb_ref[krows, :])
                        vw.append(vb_ref[krows, :])
                    else:
                        rows.append(pl.ds(q0, blk, stride=dil))
                        krows = pl.ds(k0, 2 * blk, stride=dil)
                        kw.append(k_ref[0, krows, :].astype(BF16))
                        vw.append(v_ref[0, krows, :].astype(BF16))
                    q.append(q_ref[0, rows[j], :])
                    biases.append(jnp.where(n == 0, bias_causal, bias_band) if maybe_first
                                  else bias_band)
            s = [[_dot_nt(jnp.where(h, q[j], 0.0).astype(BF16), kw[j]) + biases[j] for h in heads]
                 for j in range(nb)]
            m = [[jnp.max(x, axis=1, keepdims=True) for x in sj] for sj in s]
            p = [[jnp.exp2(s[j][h] - m[j][h]) for h in range(2)] for j in range(nb)]
            l = [[jnp.sum(x, axis=1, keepdims=True) for x in pj] for pj in p]
            pv = [[_dot(p[j][h].astype(BF16), vw[j]) for h in range(2)] for j in range(nb)]
            for j in range(nb):
                acc = jnp.where(is_h0, pv[j][0], pv[j][1])
                mj = jnp.where(is_h0, m[j][0], m[j][1])
                lj = jnp.where(is_h0, l[j][0], l[j][1])
                if last:
                    ms = [mj] + [m_ref[pp, rows[j], :] for pp in range(pi)]
                    ls = [lj] + [l_ref[pp, rows[j], :] for pp in range(pi)]
                    accs = [acc] + [acc_ref[pp, rows[j], :] for pp in range(pi)]
                    m_all = functools.reduce(jnp.maximum, ms)
                    ws = [jnp.exp2(mm - m_all) for mm in ms]
                    num = functools.reduce(jnp.add, [w * a for w, a in zip(ws, accs)])
                    den = functools.reduce(jnp.add, [w * x for w, x in zip(ws, ls)])
                    z = z_ref[0, rows[j], :]
                    o_ref[0, rows[j], :] = (num / den * (z * jax.nn.sigmoid(z))).astype(o_ref.dtype)
                else:
                    acc_ref[pi, rows[j], :] = acc
                    m_ref[pi, rows[j], :] = mj
                    l_ref[pi, rows[j], :] = lj
            return 0

        lax.fori_loop(0, t_len // (blk * nb), body, 0)


def _out_proj_kernel(ya_ref, yb_ref, wa_ref, wb_ref, x_ref, g_ref, o_ref):
    y = x_ref[...] + _dot(ya_ref[...], wa_ref[...]) + _dot(yb_ref[...], wb_ref[...])
    o_ref[...] = y * lax.rsqrt(jnp.mean(y * y, axis=-1, keepdims=True) + NORM_EPS) * g_ref[...]


def _rope_tables(t_len):
    half = ROPE_DIM // 2
    inv = ROPE_THETA ** (-jnp.arange(half, dtype=F32) * 2.0 / ROPE_DIM)
    ang = jnp.arange(t_len, dtype=jnp.int32).astype(F32)[:, None] * inv[None, :]
    cos, sin = jnp.cos(ang), jnp.sin(ang)
    ones = jnp.ones((t_len, HEAD_DIM - ROPE_DIM), F32)
    zeros = jnp.zeros((t_len, HEAD_DIM - ROPE_DIM), F32)
    zh = jnp.zeros((t_len, half), F32)
    tile = lambda a: jnp.tile(a, (1, LANES // HEAD_DIM))
    cos_t = tile(jnp.concatenate([cos, cos, ones], axis=1))
    sa_t = tile(jnp.concatenate([-sin, zh, zeros], axis=1))
    sb_t = tile(jnp.concatenate([zh, sin, zeros], axis=1))
    return cos_t, sa_t, sb_t


def kernel(x, norm_gain, w_in, shift_mix, decay_base, decay_up, iclr_base, iclr_up, key_norm_scale,
           key_iclr_mix, bonus, gn_gain, gn_bias, w_out, final_gain):
    bsz, t_len, d_model = x.shape
    depth = w_in.shape[0]
    c_a = decay_base.shape[1]
    n_hp = c_a // LANES
    shift_w = 3 * c_a + DECAY_RANK + ICLR_RANK
    in_w = w_in.shape[2]
    c_b = (in_w - shift_w - c_a) // 4
    assert c_a % LANES == 0 and c_b == c_a
    assert all(t_len % (2 * dil * ATTN_BLOCK) == 0 for _, dil in DILATION_PATTERNS)
    o_za = shift_w
    o_q = o_za + c_a
    o_k, o_v, o_zb = o_q + c_b, o_q + 2 * c_b, o_q + 3 * c_b
    segs = ((0, c_a), (c_a, 2 * c_a), (2 * c_a, 3 * c_a), (3 * c_a, shift_w), (o_za, o_q),
            (o_q, o_k), (o_k, o_v), (o_v, o_zb), (o_zb, in_w))
    cos_t, sa_t, sb_t = _rope_tables(t_len)
    rows = bsz * t_len
    tm = IN_PROJ_ROWS
    tiles_per_seq = t_len // tm
    row2 = lambda a: a.reshape(1, -1).astype(F32)

    for layer in range(depth):
        p = pl.pallas_call(
            functools.partial(_in_proj_kernel, segs=segs, shift_w=shift_w, rope_lo=o_q, q_hi=o_k,
                              rope_hi=o_v, tiles_per_seq=tiles_per_seq),
            grid=(rows // tm,),
            in_specs=[
                pl.BlockSpec((tm, d_model), lambda i: (i, 0)),
                pl.BlockSpec((1, d_model), lambda i: (0, 0)),
                pl.BlockSpec((d_model, in_w), lambda i: (0, 0)),
                pl.BlockSpec((1, shift_w), lambda i: (0, 0)),
                pl.BlockSpec((tm, LANES), lambda i: (i % tiles_per_seq, 0)),
                pl.BlockSpec((tm, LANES), lambda i: (i % tiles_per_seq, 0)),
                pl.BlockSpec((tm, LANES), lambda i: (i % tiles_per_seq, 0)),
            ],
            out_specs=pl.BlockSpec((tm, in_w), lambda i: (i, 0)),
            out_shape=jax.ShapeDtypeStruct((rows, in_w), F32),
            scratch_shapes=[pltpu.VMEM((SUBLANES, shift_w), F32)],
            compiler_params=pltpu.CompilerParams(dimension_semantics=("arbitrary",),
                                                 vmem_limit_bytes=VMEM_LIMIT),
            name="in_proj",
        )(x.reshape(rows, d_model), row2(norm_gain[layer]), w_in[layer].astype(BF16),
          row2(shift_mix[layer]), cos_t, sa_t, sb_t)
        p = p.reshape(bsz, t_len, in_w)

        zeros_up = jnp.zeros((DECAY_RANK, c_a), F32)
        up_full = jnp.concatenate(
            [jnp.concatenate([decay_up[layer].astype(F32), zeros_up], axis=0).reshape(
                DECAY_RANK + ICLR_RANK, n_hp, 1, LANES),
             jnp.concatenate([zeros_up, iclr_up[layer].astype(F32)], axis=0).reshape(
                 DECAY_RANK + ICLR_RANK, n_hp, 1, LANES)], axis=2).reshape(
                     DECAY_RANK + ICLR_RANK, n_hp * 2 * LANES)
        col = lambda off: (lambda b, h: (b, 0, off // LANES + h))
        par = lambda b, h: (0, h)
        seq_spec = lambda off: pl.BlockSpec((1, t_len, LANES), col(off))
        par_spec = pl.BlockSpec((1, LANES), par)
        spread = _spread_matrix()
        fold_list, unfold = _fold_matrices()
        fold = jnp.stack(fold_list)
        ya = pl.pallas_call(
            _rwkv_kernel,
            grid=(bsz, n_hp),
            in_specs=[seq_spec(0), seq_spec(c_a), seq_spec(2 * c_a),
                      pl.BlockSpec((1, t_len, LANES), lambda b, h: (b, 0, 3 * c_a // LANES)),
                      seq_spec(o_za),
                      pl.BlockSpec((DECAY_RANK + ICLR_RANK, 2 * LANES), par),
                      par_spec, par_spec, par_spec, par_spec, par_spec, par_spec, par_spec,
                      pl.BlockSpec(spread.shape, lambda b, h: (0, 0)),
                      pl.BlockSpec(fold.shape, lambda b, h: (0, 0, 0)),
                      pl.BlockSpec(unfold.shape, lambda b, h: (0, 0))],
            out_specs=pl.BlockSpec((1, t_len, LANES), lambda b, h: (b, 0, h)),
            out_shape=jax.ShapeDtypeStruct((bsz, t_len, c_a), BF16),
            scratch_shapes=_rwkv_scratch(t_len),
            compiler_params=pltpu.CompilerParams(dimension_semantics=("arbitrary", "arbitrary"),
                                                 vmem_limit_bytes=VMEM_LIMIT),
            name="rwkv",
        )(p, p, p, p, p, up_full, row2(decay_base[layer]), row2(iclr_base[layer]),
          row2(key_norm_scale[layer]), row2(key_iclr_mix[layer]), row2(bonus[layer]),
          row2(gn_gain[layer]), row2(gn_bias[layer]), spread, fold, unfold)

        yb = pl.pallas_call(
            _attn_kernel,
            grid=(bsz, c_b // LANES),
            in_specs=[seq_spec(o_q), seq_spec(o_k), seq_spec(o_v), seq_spec(o_zb)],
            out_specs=pl.BlockSpec((1, t_len, LANES), lambda b, h: (b, 0, h)),
            out_shape=jax.ShapeDtypeStruct((bsz, t_len, c_b), BF16),
            scratch_shapes=[pltpu.VMEM((t_len, LANES), BF16),
                            pltpu.VMEM((t_len, LANES), BF16),
                            pltpu.VMEM((len(DILATION_PATTERNS) - 1, t_len, LANES), F32),
                            pltpu.VMEM((len(DILATION_PATTERNS) - 1, t_len, LANES), F32),
                            pltpu.VMEM((len(DILATION_PATTERNS) - 1, t_len, LANES), F32)],
            compiler_params=pltpu.CompilerParams(dimension_semantics=("arbitrary", "arbitrary"),
                                                 vmem_limit_bytes=VMEM_LIMIT),
            name="attention",
        )(p, p, p, p)

        assert depth == 1
        tmo = OUT_PROJ_ROWS
        wo = w_out[layer].astype(BF16)
        x = pl.pallas_call(
            _out_proj_kernel,
            grid=(rows // tmo,),
            in_specs=[pl.BlockSpec((tmo, c_a), lambda i: (i, 0)),
                      pl.BlockSpec((tmo, c_b), lambda i: (i, 0)),
                      pl.BlockSpec((c_a, d_model), lambda i: (0, 0)),
                      pl.BlockSpec((c_b, d_model), lambda i: (0, 0)),
                      pl.BlockSpec((tmo, d_model), lambda i: (i, 0)),
                      pl.BlockSpec((1, d_model), lambda i: (0, 0))],
            out_specs=pl.BlockSpec((tmo, d_model), lambda i: (i, 0)),
            out_shape=jax.ShapeDtypeStruct((rows, d_model), F32),
            compiler_params=pltpu.CompilerParams(dimension_semantics=("arbitrary",),
                                                 vmem_limit_bytes=VMEM_LIMIT),
            name="out_proj",
        )(ya.reshape(rows, c_a), yb.reshape(rows, c_b), wo[:c_a], wo[c_a:],
          x.reshape(rows, d_model), row2(final_gain)).reshape(bsz, t_len, d_model)
    return x
```

```python
import functools

import jax
import jax.numpy as jnp
from jax import lax
from jax.experimental import pallas as pl
from jax.experimental.pallas import tpu as pltpu

HEAD_DIM = 64
LANES = 128
SUBLANES = 8
DECAY_RANK = 64
ICLR_RANK = 64
DILATION_PATTERNS = ((128, 1), (512, 4), (2048, 16))
ATTN_BLOCK = 128
ATTN_GROUP = 4
ROPE_THETA = 500000.0
ROPE_DIM = HEAD_DIM // 4
NORM_EPS = 1e-6
GN_EPS = 64e-5
CHUNK = 64
SUB = 16
PACK = LANES // (2 * SUB)
RWKV_GROUP = 8
PREP_STEPS = (2, 4, 6)
IN_PROJ_ROWS = 512
OUT_PROJ_ROWS = 1024
DECAY_SCALE = 0.6065306597126334
LOG2_E = 1.4426950408889634
ATTN_Q_SCALE = HEAD_DIM ** -0.5 * LOG2_E
NEG_BIG = -1e30
VMEM_LIMIT = 56 * 1024 * 1024

F32 = jnp.float32
BF16 = jnp.bfloat16


def _dot(a, b, **kw):
    return jnp.dot(a, b, preferred_element_type=F32, **kw)


def _dot_nt(a, b):
    return lax.dot_general(a, b, (((1,), (1,)), ((), ())), preferred_element_type=F32)


def _split2(x):
    hi = x.astype(BF16)
    return hi, (x - hi.astype(F32)).astype(BF16)


def _dot_tn(a, b):
    return lax.dot_general(a, b, (((0,), (0,)), ((), ())), preferred_element_type=F32)


def _in_proj_kernel(x_ref, g_ref, w_ref, mix_ref, cos_ref, sa_ref, sb_ref, o_ref, carry_ref,
                    *, segs, shift_w, rope_lo, q_hi, rope_hi, tiles_per_seq):
    i = pl.program_id(0)

    @pl.when(i == 0)
    def _():
        carry_ref[...] = jnp.zeros(carry_ref.shape, F32)

    x = x_ref[...]
    h = x * lax.rsqrt(jnp.mean(x * x, axis=-1, keepdims=True) + NORM_EPS) * g_ref[...]
    hb = h.astype(BF16)
    tm = x.shape[0]
    first = (i % tiles_per_seq) == 0
    row0 = lax.broadcasted_iota(jnp.int32, (tm, 1), 0) == 0
    for lo, hi in segs:
        p = _dot(hb, w_ref[:, lo:hi])
        if hi <= shift_w:
            old = jnp.where(first, 0.0, carry_ref[0:1, lo:hi])
            carry_ref[0:1, lo:hi] = p[tm - 1:tm, :]
            prev = jnp.where(row0, old, pltpu.roll(p, 1, axis=0))
            p = p + (prev - p) * mix_ref[:, lo:hi]
            o_ref[:, lo:hi] = p
        elif lo >= rope_lo and hi <= rope_hi:
            if hi <= q_hi:
                p = p * ATTN_Q_SCALE
            for j in range(lo, hi, LANES):
                t = p[:, j - lo:j - lo + LANES]
                t = (t * cos_ref[...] + pltpu.roll(t, LANES - ROPE_DIM // 2, axis=1) * sa_ref[...]
                     + pltpu.roll(t, ROPE_DIM // 2, axis=1) * sb_ref[...])
                o_ref[:, j:j + LANES] = t
        else:
            o_ref[:, lo:hi] = p


def _spread_matrix():
    j = lax.broadcasted_iota(jnp.int32, (LANES, (SUB - 1) * LANES), 0)
    col = lax.broadcasted_iota(jnp.int32, (LANES, (SUB - 1) * LANES), 1)
    sel = j == SUB * ((col % LANES) // SUB) + col // LANES
    return jnp.where(sel, 1.0, 0.0).astype(BF16)


def _fold_matrices():
    r = lax.broadcasted_iota(jnp.int32, (LANES, LANES), 0)
    l = lax.broadcasted_iota(jnp.int32, (LANES, LANES), 1)
    per_chunk = LANES // PACK
    match = ((l % per_chunk) // SUB == r // HEAD_DIM) & (l % SUB == r % SUB)
    fold = [jnp.where(match & (l // per_chunk == c), 1.0, 0.0).astype(BF16) for c in range(PACK)]
    rr = lax.broadcasted_iota(jnp.int32, (LANES, PACK * LANES), 0)
    cc = lax.broadcasted_iota(jnp.int32, (LANES, PACK * LANES), 1)
    src = per_chunk * (cc // LANES) + SUB * ((cc % LANES) // HEAD_DIM) + cc % SUB
    unfold = jnp.where(rr == src, 1.0, 0.0).astype(BF16)
    return fold, unfold


def _substitute(coef):
    rows = coef.shape[0]
    ng = rows // SUBLANES
    gpb = SUB // SUBLANES
    row = lax.broadcasted_iota(jnp.int32, (SUBLANES, LANES), 0)
    col = lax.broadcasted_iota(jnp.int32, (SUBLANES, LANES), 1) % SUB
    xs = [jnp.where(col == (row + g * SUBLANES) % SUB, 1.0, 0.0).astype(F32) for g in range(ng)]
    for ss in range(SUB - 1):
        gs, rs = divmod(ss, SUBLANES)
        for blk in range(ng // gpb):
            xrow = xs[blk * gpb + gs][rs:rs + 1, :]
            for g in range(blk * gpb + (ss + 1) // SUBLANES, (blk + 1) * gpb):
                cf = coef[g * SUBLANES:(g + 1) * SUBLANES, ss * LANES:(ss + 1) * LANES]
                xs[g] = xs[g] - cf * xrow
    return jnp.concatenate(xs, axis=0)


def _rwkv_scratch(t_len):
    n_chunks = t_len // CHUNK
    return [pltpu.VMEM((LANES, (SUB - 1) * LANES), BF16),
            pltpu.VMEM((PACK, LANES, LANES), BF16),
            pltpu.VMEM((LANES, PACK * LANES), BF16),
            pltpu.VMEM((t_len, LANES), BF16),
            pltpu.VMEM((t_len, LANES), F32),
            pltpu.VMEM((t_len, LANES), F32),
            pltpu.VMEM((n_chunks, LANES, LANES), BF16),
            pltpu.VMEM((n_chunks, HEAD_DIM, LANES), F32),
            pltpu.VMEM((n_chunks, SUBLANES, LANES), F32),
            pltpu.VMEM((2, RWKV_GROUP, 2 * CHUNK, LANES), BF16),
            pltpu.VMEM((2, RWKV_GROUP, 4 * CHUNK, LANES), BF16),
            pltpu.VMEM((2, RWKV_GROUP, 2 * CHUNK, LANES), BF16),
            pltpu.VMEM((2, RWKV_GROUP, 2 * CHUNK, LANES), BF16),
            pltpu.VMEM((2, RWKV_GROUP * CHUNK, LANES), F32),
            pltpu.VMEM((2, RWKV_GROUP * CHUNK, LANES), BF16),
            pltpu.VMEM((2, RWKV_GROUP * CHUNK, LANES), BF16),
            pltpu.VMEM((2, RWKV_GROUP * CHUNK, LANES), BF16)]


def _rwkv_kernel(r_ref, k_ref, v_ref, wa_ref, z_ref, up_ref, dbase_ref, abase_ref, kns_ref,
                 kim_ref, bonus_ref, gng_ref, gnb_ref, o_ref,
                 spread_ref, fold_ref, unfold_ref, qt_ref, yloc_ref, bon_ref, mt_ref, nn_ref,
                 pe_ref, lhs_ref, rhs_ref, vbd_ref, kbd_ref, rd_ref, vb_ref, kdl_ref, bdl_ref):
    t_len = r_ref.shape[1]
    c = CHUNK
    grp = RWKV_GROUP
    rows = grp * c
    n_chunks = t_len // c
    lane = lax.broadcasted_iota(jnp.int32, (c, LANES), 1)
    is_h0 = lane < HEAD_DIM
    spread_ref[...] = _spread_matrix()
    fold, unfold = _fold_matrices()
    for i in range(PACK):
        fold_ref[i] = fold[i]
    unfold_ref[...] = unfold
    assert grp % PACK == 0
    trow = lax.broadcasted_iota(jnp.int32, (c, LANES), 0)
    scol = lane % HEAD_DIM
    tb, sb = trow // SUB, scol // SUB
    strict = trow > scol
    incl = trow >= scol
    in_block = tb == sb
    ltri = (lax.broadcasted_iota(jnp.int32, (c, c), 0)
            >= lax.broadcasted_iota(jnp.int32, (c, c), 1)).astype(BF16)
    sq_r = lax.broadcasted_iota(jnp.int32, (LANES, LANES), 0)
    sq_c = lax.broadcasted_iota(jnp.int32, (LANES, LANES), 1)
    same_head = sq_r // HEAD_DIM == sq_c // HEAD_DIM

    dbase, abase = dbase_ref[...], abase_ref[...]
    kns, kim, bonus = kns_ref[...], kim_ref[...], bonus_ref[...]
    gng, gnb = gng_ref[...], gnb_ref[...]
    up_hi, up_lo = _split2(up_ref[...])
    up_hh = jnp.concatenate([up_hi, up_hi], axis=0)

    def head_sum(x):
        h0 = lax.broadcasted_iota(jnp.int32, x.shape, 1) < HEAD_DIM
        s0 = jnp.sum(jnp.where(h0, x, 0.0), axis=1, keepdims=True)
        s1 = jnp.sum(jnp.where(h0, 0.0, x), axis=1, keepdims=True)
        return jnp.where(h0, s0, s1)

    def block_diag(x):
        zero = jnp.zeros_like(x)
        return jnp.concatenate([jnp.where(is_h0, x, zero), jnp.where(is_h0, zero, x)],
                               axis=0).astype(BF16)

    n = grp
    js = range(n)
    cs = [slice(j * c, (j + 1) * c) for j in js]

    def prep(gi, slot):
        first_chunk = gi * grp
        sl = pl.ds(pl.multiple_of(first_chunk * c, n * c), n * c)
        r = r_ref[0, sl, :]
        k = k_ref[0, sl, :]
        v = v_ref[0, sl, :]
        wa = wa_ref[0, sl, :]
        is_h0_s = lax.broadcasted_iota(jnp.int32, (n * c, LANES), 1) < HEAD_DIM
        x_hi, x_lo = _split2(jnp.where(is_h0_s, jnp.tanh(wa), wa))
        lin = _dot(jnp.concatenate([x_hi, x_lo], axis=1), up_hh) + _dot(x_hi, up_lo)
        yield
        logw = -DECAY_SCALE * jax.nn.sigmoid(dbase + lin[:, :LANES])
        a = jax.nn.sigmoid(abase + lin[:, LANES:])
        kk = k * kns
        kk = kk * lax.rsqrt(jnp.maximum(head_sum(kk * kk), 1e-24))
        kmod = k * (1.0 + (a - 1.0) * kim)
        b = kk * a
        bon_ref[sl, :] = head_sum(r * kmod * bonus) * v

        parts = jnp.concatenate(_split2(logw), axis=1)
        gs = [_dot(ltri, parts[cs[j]]) for j in js]
        yield
        g = jnp.concatenate([x[:, LANES:] + x[:, :LANES] for x in gs], axis=0)
        p_end = [jnp.exp(g[(j + 1) * c - 1:(j + 1) * c, :]) for j in js]
        e_inv = jnp.exp(-g)
        kkd = kk * jnp.exp(g - logw)
        rd = r * jnp.exp(g)
        bi = b * e_inv
        ki = kmod * e_inv
        e_end = e_inv * jnp.concatenate([jnp.broadcast_to(x, (c, LANES)) for x in p_end], axis=0)
        rd_ref[slot] = rd
        vb_ref[slot] = v.astype(BF16)
        kdl_ref[slot] = (kmod * e_end).astype(BF16)
        bdl_ref[slot] = (b * e_end).astype(BF16)
        for j in js:
            lhs_ref[slot, j] = jnp.concatenate([kkd[cs[j]], rd[cs[j]]], axis=0).astype(BF16)
            rhs_ref[slot, j] = jnp.concatenate([block_diag(bi[cs[j]]), block_diag(ki[cs[j]])],
                                               axis=0)
            vbd_ref[slot, j] = block_diag(v[cs[j]])
            kbd_ref[slot, j] = block_diag(kkd[cs[j]])
            pe_ref[first_chunk + j] = jnp.broadcast_to(p_end[j], (SUBLANES, LANES))

    def pass_a(first_chunk, slot):
        aab, aak, arb, ark = [], [], [], []
        for j in js:
            pair = _dot_nt(lhs_ref[slot, j], rhs_ref[slot, j])
            aab.append(jnp.where(strict, pair[:c, :LANES], 0.0))
            aak.append(jnp.where(strict, pair[:c, LANES:], 0.0))
            arb.append(jnp.where(incl, pair[c:, :LANES], 0.0))
            ark.append(jnp.where(incl, pair[c:, LANES:], 0.0))
        yield
        akv = [_dot(jnp.concatenate([aak[j], ark[j]], axis=0).astype(BF16), vbd_ref[slot, j])
               for j in js]
        aakv = [x[:c] for x in akv]
        arkv = [x[c:] for x in akv]
        compact = [functools.reduce(
            jnp.add, [_dot(jnp.where(in_block, aab[q * PACK + i], 0.0).astype(BF16), fold_ref[i])
                      for i in range(PACK)]) for q in range(n // PACK)]
        yield
        coef = _dot(jnp.concatenate(compact, axis=0).astype(BF16), spread_ref[...])
        t_diag = _substitute(coef)
        yield
        spread_out = [_dot(t_diag[q * c:(q + 1) * c].astype(BF16), unfold_ref[...])
                      for q in range(n // PACK)]
        tinv = [jnp.where(in_block, spread_out[j // PACK][:, (j % PACK) * LANES:
                                                           (j % PACK + 1) * LANES], 0.0)
                for j in js]
        yield
        span = 1
        while span * SUB < c:
            low_mask = (tb // span == sb // span + 1) & ((tb // span) % 2 == 1)
            inner = [_dot(jnp.where(low_mask, aab[j], 0.0).astype(BF16),
                          block_diag(tinv[j])) for j in js]
            yield
            tinv = [tinv[j] - _dot(tinv[j].astype(BF16), block_diag(inner[j])) for j in js]
            yield
            span *= 2
        wu = [_dot(tinv[j].astype(BF16),
                   jnp.concatenate([kbd_ref[slot, j], block_diag(aakv[j])], axis=1))
              for j in js]
        yield
        w = [x[:, :LANES] for x in wu]
        uloc = [x[:, LANES:] for x in wu]
        arb_b = [arb[j].astype(BF16) for j in js]
        bdl_b = [bdl_ref[slot, cs[j], :] for j in js]
        awu = [_dot(arb_b[j], jnp.concatenate([block_diag(w[j]), block_diag(uloc[j])], axis=1))
               for j in js]
        qt = [rd_ref[slot, cs[j], :] - awu[j][:, :LANES] for j in js]
        yloc = [arkv[j] - awu[j][:, LANES:] for j in js]
        wtb = [_dot_tn(w[j].astype(BF16), bdl_b[j]) for j in js]
        nn = [_dot_tn(jnp.concatenate([vb_ref[slot, cs[j], :], (-uloc[j]).astype(BF16)], axis=0),
                      jnp.concatenate([kdl_ref[slot, cs[j], :], bdl_b[j]], axis=0))
              for j in js]
        yield
        for j in js:
            ci = first_chunk + j
            csl = pl.ds(pl.multiple_of(ci * c, c), c)
            qt_ref[csl, :] = qt[j].astype(BF16)
            yloc_ref[csl, :] = yloc[j]
            mt_ref[ci] = jnp.where(same_head, wtb[j], 0.0).astype(BF16)
            nn_ref[ci] = jnp.where(is_h0, nn[j][:HEAD_DIM], nn[j][HEAD_DIM:])

    def chunk_step(ci, state):
        sb = state.astype(BF16)
        sl = pl.ds(pl.multiple_of(ci * c, c), c)
        y = _dot_nt(qt_ref[sl, :], block_diag(sb)) + yloc_ref[sl, :]
        new_state = state * pe_ref[ci, 0:1, :] + nn_ref[ci] - _dot(sb, mt_ref[ci])
        mu = head_sum(y) * (1.0 / HEAD_DIM)
        d = y - mu
        var = head_sum(d * d) * (1.0 / HEAD_DIM)
        yn = d * lax.rsqrt(var + GN_EPS) * gng + gnb + bon_ref[sl, :]
        z = z_ref[0, sl, :]
        o_ref[0, sl, :] = (yn * (z * jax.nn.sigmoid(z))).astype(o_ref.dtype)
        return new_state

    class Chain:
        def __init__(self, gi, state):
            self.gi, self.state, self.done = gi, state, 0

        def tick(self):
            if self.done < grp:
                self.state = chunk_step(self.gi * grp + self.done, self.state)
                self.done += 1

        def drain(self):
            while self.done < grp:
                self.tick()
            return self.state

    n_groups = n_chunks // grp

    def run_group(gi, slot, chain, prepare_next=True):
        nxt = prep(gi + 1, 1 - slot) if prepare_next else iter(())
        for step, _ in enumerate(pass_a(gi * grp, slot)):
            if step in PREP_STEPS:
                next(nxt, None)
            if chain is not None:
                chain.tick()
        for _ in nxt:
            pass
        return chain.drain() if chain is not None else None

    def body(pair, state):
        gi = 2 * pair + 1
        state = run_group(gi, 1, Chain(gi - 1, state))
        return run_group(gi + 1, 0, Chain(gi, state))

    assert n_groups % 2 == 0
    for _ in prep(0, 0):
        pass
    run_group(0, 0, None)
    state = lax.fori_loop(0, (n_groups - 2) // 2, body, jnp.zeros((HEAD_DIM, LANES), F32))
    state = run_group(n_groups - 1, 1, Chain(n_groups - 2, state), prepare_next=False)
    Chain(n_groups - 1, state).drain()


def _attn_kernel(q_ref, k_ref, v_ref, z_ref, o_ref, kb_ref, vb_ref, acc_ref, m_ref, l_ref):
    t_len = q_ref.shape[1]
    blk = ATTN_BLOCK
    kb_ref[...] = k_ref[0].astype(BF16)
    vb_ref[...] = v_ref[0].astype(BF16)

    lane = lax.broadcasted_iota(jnp.int32, (blk, LANES), 1)
    is_h0 = lane < HEAD_DIM
    qi = lax.broadcasted_iota(jnp.int32, (blk, 2 * blk), 0)
    kj = lax.broadcasted_iota(jnp.int32, (blk, 2 * blk), 1)
    nb = ATTN_GROUP
    heads = (is_h0, ~is_h0)
    patterns = tuple(reversed(DILATION_PATTERNS))
    assert patterns[-1][1] == 1
    bias_causal = jnp.where(kj <= qi, 0.0, NEG_BIG).astype(F32)

    for pi, (window, dil) in enumerate(patterns):
        span = window // dil
        nblk = t_len // (dil * blk)
        bias_band = jnp.where((qi + blk - kj >= 0) & (qi + blk - kj <= span), 0.0,
                              NEG_BIG).astype(F32)
        last = pi == len(patterns) - 1

        def body(it, _, pi=pi, dil=dil, nblk=nblk, bias_band=bias_band, last=last):
            rows, q, kw, vw, biases = [], [], [], [], []
            if nblk == 2 and nb % 2 == 0:
                for jr in range(nb // 2):
                    res = it * (nb // 2) + jr
                    seq = pl.ds(res, 2 * blk, stride=dil)
                    q_seq = q_ref[0, seq, :]
                    k_seq = k_ref[0, seq, :].astype(BF16)
                    v_seq = v_ref[0, seq, :].astype(BF16)
                    for n in range(2):
                        rows.append(pl.ds(res + dil * blk * n, blk, stride=dil))
                        q.append(q_seq[n * blk:(n + 1) * blk])
                        kw.append(k_seq)
                        vw.append(v_seq)
                        biases.append(bias_band if n else bias_causal)
            else:
                for j in range(nb):
                    res = (it * nb + j) // nblk
                    n = (it * nb + j) % nblk
                    maybe_first = j == 0 or nblk % nb != 0
                    n_prev = jnp.maximum(n - 1, 0) if maybe_first else n - 1
                    q0 = res + dil * blk * n
                    k0 = res + dil * blk * n_prev
                    if dil == 1:
                        rows.append(pl.ds(pl.multiple_of(q0, blk), blk))
                        krows = pl.ds(pl.multiple_of(k0, blk), 2 * blk)
                        kw.append(kb_ref[krows, :])
                        vw.append(vb_ref[krows, :])
                    else:
                        rows.append(pl.ds(q0, blk, stride=dil))
                        krows = pl.ds(k0, 2 * blk, stride=dil)
                        kw.append(k_ref[0, krows, :].astype(BF16))
                        vw.append(v_ref[0, krows, :].astype(BF16))
                    q.append(q_ref[0, rows[j], :])
                    biases.append(jnp.where(n == 0, bias_causal, bias_band) if maybe_first
                                  else bias_band)
            s = [[_dot_nt(jnp.where(h, q[j], 0.0).astype(BF16), kw[j]) + biases[j] for h in heads]
                 for j in range(nb)]
            m = [[jnp.max(x, axis=1, keepdims=True) for x in sj] for sj in s]
            p = [[jnp.exp2(s[j][h] - m[j][h]) for h in range(2)] for j in range(nb)]
            l = [[jnp.sum(x, axis=1, keepdims=True) for x in pj] for pj in p]
            pv = [[_dot(p[j][h].astype(BF16), vw[j]) for h in range(2)] for j in range(nb)]
            for j in range(nb):
                acc = jnp.where(is_h0, pv[j][0], pv[j][1])
                mj = jnp.where(is_h0, m[j][0], m[j][1])
                lj = jnp.where(is_h0, l[j][0], l[j][1])
                if last:
                    ms = [mj] + [m_ref[pp, rows[j], :] for pp in range(pi)]
                    ls = [lj] + [l_ref[pp, rows[j], :] for pp in range(pi)]
                    accs = [acc] + [acc_ref[pp, rows[j], :] for pp in range(pi)]
                    m_all = functools.reduce(jnp.maximum, ms)
                    ws = [jnp.exp2(mm - m_all) for mm in ms]
                    num = functools.reduce(jnp.add, [w * a for w, a in zip(ws, accs)])
                    den = functools.reduce(jnp.add, [w * x for w, x in zip(ws, ls)])
                    z = z_ref[0, rows[j], :]
                    o_ref[0, rows[j], :] = (num / den * (z * jax.nn.sigmoid(z))).astype(o_ref.dtype)
                else:
                    acc_ref[pi, rows[j], :] = acc
                    m_ref[pi, rows[j], :] = mj
                    l_ref[pi, rows[j], :] = lj
            return 0

        lax.fori_loop(0, t_len // (blk * nb), body, 0)


def _out_proj_kernel(ya_ref, yb_ref, wa_ref, wb_ref, x_ref, g_ref, o_ref):
    y = x_ref[...] + _dot(ya_ref[...], wa_ref[...]) + _dot(yb_ref[...], wb_ref[...])
    o_ref[...] = y * lax.rsqrt(jnp.mean(y * y, axis=-1, keepdims=True) + NORM_EPS) * g_ref[...]


def _rope_tables(t_len):
    half = ROPE_DIM // 2
    inv = ROPE_THETA ** (-jnp.arange(half, dtype=F32) * 2.0 / ROPE_DIM)
    ang = jnp.arange(t_len, dtype=jnp.int32).astype(F32)[:, None] * inv[None, :]
    cos, sin = jnp.cos(ang), jnp.sin(ang)
    ones = jnp.ones((t_len, HEAD_DIM - ROPE_DIM), F32)
    zeros = jnp.zeros((t_len, HEAD_DIM - ROPE_DIM), F32)
    zh = jnp.zeros((t_len, half), F32)
    tile = lambda a: jnp.tile(a, (1, LANES // HEAD_DIM))
    cos_t = tile(jnp.concatenate([cos, cos, ones], axis=1))
    sa_t = tile(jnp.concatenate([-sin, zh, zeros], axis=1))
    sb_t = tile(jnp.concatenate([zh, sin, zeros], axis=1))
    return cos_t, sa_t, sb_t


def kernel(x, norm_gain, w_in, shift_mix, decay_base, decay_up, iclr_base, iclr_up, key_norm_scale,
           key_iclr_mix, bonus, gn_gain, gn_bias, w_out, final_gain):
    bsz, t_len, d_model = x.shape
    depth = w_in.shape[0]
    c_a = decay_base.shape[1]
    n_hp = c_a // LANES
    shift_w = 3 * c_a + DECAY_RANK + ICLR_RANK
    in_w = w_in.shape[2]
    c_b = (in_w - shift_w - c_a) // 4
    assert c_a % LANES == 0 and c_b == c_a
    assert all(t_len % (2 * dil * ATTN_BLOCK) == 0 for _, dil in DILATION_PATTERNS)
    o_za = shift_w
    o_q = o_za + c_a
    o_k, o_v, o_zb = o_q + c_b, o_q + 2 * c_b, o_q + 3 * c_b
    segs = ((0, c_a), (c_a, 2 * c_a), (2 * c_a, 3 * c_a), (3 * c_a, shift_w), (o_za, o_q),
            (o_q, o_k), (o_k, o_v), (o_v, o_zb), (o_zb, in_w))
    cos_t, sa_t, sb_t = _rope_tables(t_len)
    rows = bsz * t_len
    tm = IN_PROJ_ROWS
    tiles_per_seq = t_len // tm
    row2 = lambda a: a.reshape(1, -1).astype(F32)

    for layer in range(depth):
        p = pl.pallas_call(
            functools.partial(_in_proj_kernel, segs=segs, shift_w=shift_w, rope_lo=o_q, q_hi=o_k,
                              rope_hi=o_v, tiles_per_seq=tiles_per_seq),
            grid=(rows // tm,),
            in_specs=[
                pl.BlockSpec((tm, d_model), lambda i: (i, 0)),
                pl.BlockSpec((1, d_model), lambda i: (0, 0)),
                pl.BlockSpec((d_model, in_w), lambda i: (0, 0)),
                pl.BlockSpec((1, shift_w), lambda i: (0, 0)),
                pl.BlockSpec((tm, LANES), lambda i: (i % tiles_per_seq, 0)),
                pl.BlockSpec((tm, LANES), lambda i: (i % tiles_per_seq, 0)),
                pl.BlockSpec((tm, LANES), lambda i: (i % tiles_per_seq, 0)),
            ],
            out_specs=pl.BlockSpec((tm, in_w), lambda i: (i, 0)),
            out_shape=jax.ShapeDtypeStruct((rows, in_w), F32),
            scratch_shapes=[pltpu.VMEM((SUBLANES, shift_w), F32)],
            compiler_params=pltpu.CompilerParams(dimension_semantics=("arbitrary",),
                                                 vmem_limit_bytes=VMEM_LIMIT),
            name="in_proj",
        )(x.reshape(rows, d_model), row2(norm_gain[layer]), w_in[layer].astype(BF16),
          row2(shift_mix[layer]), cos_t, sa_t, sb_t)
        p = p.reshape(bsz, t_len, in_w)

        zeros_up = jnp.zeros((DECAY_RANK, c_a), F32)
        up_full = jnp.concatenate(
            [jnp.concatenate([decay_up[layer].astype(F32), zeros_up], axis=0).reshape(
                DECAY_RANK + ICLR_RANK, n_hp, 1, LANES),
             jnp.concatenate([zeros_up, iclr_up[layer].astype(F32)], axis=0).reshape(
                 DECAY_RANK + ICLR_RANK, n_hp, 1, LANES)], axis=2).reshape(
                     DECAY_RANK + ICLR_RANK, n_hp * 2 * LANES)
        col = lambda off: (lambda b, h: (b, 0, off // LANES + h))
        par = lambda b, h: (0, h)
        seq_spec = lambda off: pl.BlockSpec((1, t_len, LANES), col(off))
        par_spec = pl.BlockSpec((1, LANES), par)
        ya = pl.pallas_call(
            _rwkv_kernel,
            grid=(bsz, n_hp),
            in_specs=[seq_spec(0), seq_spec(c_a), seq_spec(2 * c_a),
                      pl.BlockSpec((1, t_len, LANES), lambda b, h: (b, 0, 3 * c_a // LANES)),
                      seq_spec(o_za),
                      pl.BlockSpec((DECAY_RANK + ICLR_RANK, 2 * LANES), par),
                      par_spec, par_spec, par_spec, par_spec, par_spec, par_spec, par_spec],
            out_specs=pl.BlockSpec((1, t_len, LANES), lambda b, h: (b, 0, h)),
            out_shape=jax.ShapeDtypeStruct((bsz, t_len, c_a), BF16),
            scratch_shapes=_rwkv_scratch(t_len),
            compiler_params=pltpu.CompilerParams(dimension_semantics=("arbitrary", "arbitrary"),
                                                 vmem_limit_bytes=VMEM_LIMIT),
            name="rwkv",
        )(p, p, p, p, p, up_full, row2(decay_base[layer]), row2(iclr_base[layer]),
          row2(key_norm_scale[layer]), row2(key_iclr_mix[layer]), row2(bonus[layer]),
          row2(gn_gain[layer]), row2(gn_bias[layer]))

        yb = pl.pallas_call(
            _attn_kernel,
            grid=(bsz, c_b // LANES),
            in_specs=[seq_spec(o_q), seq_spec(o_k), seq_spec(o_v), seq_spec(o_zb)],
            out_specs=pl.BlockSpec((1, t_len, LANES), lambda b, h: (b, 0, h)),
            out_shape=jax.ShapeDtypeStruct((bsz, t_len, c_b), BF16),
            scratch_shapes=[pltpu.VMEM((t_len, LANES), BF16),
                            pltpu.VMEM((t_len, LANES), BF16),
                            pltpu.VMEM((len(DILATION_PATTERNS) - 1, t_len, LANES), F32),
                            pltpu.VMEM((len(DILATION_PATTERNS) - 1, t_len, LANES), F32),
                            pltpu.VMEM((len(DILATION_PATTERNS) - 1, t_len, LANES), F32)],
            compiler_params=pltpu.CompilerParams(dimension_semantics=("arbitrary", "arbitrary"),
                                                 vmem_limit_bytes=VMEM_LIMIT),
            name="attention",
        )(p, p, p, p)

        assert depth == 1
        tmo = OUT_PROJ_ROWS
        wo = w_out[layer].astype(BF16)
        x = pl.pallas_call(
            _out_proj_kernel,
            grid=(rows // tmo,),
            in_specs=[pl.BlockSpec((tmo, c_a), lambda i: (i, 0)),
                      pl.BlockSpec((tmo, c_b), lambda i: (i, 0)),
                      pl.BlockSpec((c_a, d_model), lambda i: (0, 0)),
                      pl.BlockSpec((c_b, d_model), lambda i: (0, 0)),
                      pl.BlockSpec((tmo, d_model), lambda i: (i, 0)),
                      pl.BlockSpec((1, d_model), lambda i: (0, 0))],
            out_specs=pl.BlockSpec((tmo, d_model), lambda i: (i, 0)),
            out_shape=jax.ShapeDtypeStruct((rows, d_model), F32),
            compiler_params=pltpu.CompilerParams(dimension_semantics=("arbitrary",),
                                                 vmem_limit_bytes=VMEM_LIMIT),
            name="out_proj",
        )(ya.reshape(rows, c_a), yb.reshape(rows, c_b), wo[:c_a], wo[c_a:],
          x.reshape(rows, d_model), row2(final_gain)).reshape(bsz, t_len, d_model)
    return x
```

```python
import functools

import jax
import jax.numpy as jnp
from jax import lax
from jax.experimental import pallas as pl
from jax.experimental.pallas import tpu as pltpu

HEAD_DIM = 64
LANES = 128
SUBLANES = 8
DECAY_RANK = 64
ICLR_RANK = 64
DILATION_PATTERNS = ((128, 1), (512, 4), (2048, 16))
ATTN_BLOCK = 128
ATTN_GROUP = 4
ROPE_THETA = 500000.0
ROPE_DIM = HEAD_DIM // 4
NORM_EPS = 1e-6
GN_EPS = 64e-5
CHUNK = 64
SUB = 16
PACK = LANES // (2 * SUB)
RWKV_GROUP = 8
PREP_STEPS = (2, 4, 6)
IN_PROJ_ROWS = 512
OUT_PROJ_ROWS = 1024
DECAY_SCALE = 0.6065306597126334
LOG2_E = 1.4426950408889634
ATTN_Q_SCALE = HEAD_DIM ** -0.5 * LOG2_E
NEG_BIG = -1e30
VMEM_LIMIT = 56 * 1024 * 1024

F32 = jnp.float32
BF16 = jnp.bfloat16


def _dot(a, b, **kw):
    return jnp.dot(a, b, preferred_element_type=F32, **kw)


def _dot_nt(a, b):
    return lax.dot_general(a, b, (((1,), (1,)), ((), ())), preferred_element_type=F32)


def _split2(x):
    hi = x.astype(BF16)
    return hi, (x - hi.astype(F32)).astype(BF16)


def _dot_tn(a, b):
    return lax.dot_general(a, b, (((0,), (0,)), ((), ())), preferred_element_type=F32)


def _in_proj_kernel(x_ref, g_ref, w_ref, mix_ref, cos_ref, sa_ref, sb_ref, o_ref, carry_ref,
                    *, segs, shift_w, rope_lo, q_hi, rope_hi, tiles_per_seq):
    i = pl.program_id(0)

    @pl.when(i == 0)
    def _():
        carry_ref[...] = jnp.zeros(carry_ref.shape, F32)

    x = x_ref[...]
    h = x * lax.rsqrt(jnp.mean(x * x, axis=-1, keepdims=True) + NORM_EPS) * g_ref[...]
    hb = h.astype(BF16)
    tm = x.shape[0]
    first = (i % tiles_per_seq) == 0
    row0 = lax.broadcasted_iota(jnp.int32, (tm, 1), 0) == 0
    for lo, hi in segs:
        p = _dot(hb, w_ref[:, lo:hi])
        if hi <= shift_w:
            old = jnp.where(first, 0.0, carry_ref[0:1, lo:hi])
            carry_ref[0:1, lo:hi] = p[tm - 1:tm, :]
            prev = jnp.where(row0, old, pltpu.roll(p, 1, axis=0))
            p = p + (prev - p) * mix_ref[:, lo:hi]
            o_ref[:, lo:hi] = p
        elif lo >= rope_lo and hi <= rope_hi:
            if hi <= q_hi:
                p = p * ATTN_Q_SCALE
            for j in range(lo, hi, LANES):
                t = p[:, j - lo:j - lo + LANES]
                t = (t * cos_ref[...] + pltpu.roll(t, LANES - ROPE_DIM // 2, axis=1) * sa_ref[...]
                     + pltpu.roll(t, ROPE_DIM // 2, axis=1) * sb_ref[...])
                o_ref[:, j:j + LANES] = t
        else:
            o_ref[:, lo:hi] = p


def _spread_matrix():
    j = lax.broadcasted_iota(jnp.int32, (LANES, (SUB - 1) * LANES), 0)
    col = lax.broadcasted_iota(jnp.int32, (LANES, (SUB - 1) * LANES), 1)
    sel = j == SUB * ((col % LANES) // SUB) + col // LANES
    return jnp.where(sel, 1.0, 0.0).astype(BF16)


def _fold_matrices():
    r = lax.broadcasted_iota(jnp.int32, (LANES, LANES), 0)
    l = lax.broadcasted_iota(jnp.int32, (LANES, LANES), 1)
    per_chunk = LANES // PACK
    match = ((l % per_chunk) // SUB == r // HEAD_DIM) & (l % SUB == r % SUB)
    fold = [jnp.where(match & (l // per_chunk == c), 1.0, 0.0).astype(BF16) for c in range(PACK)]
    rr = lax.broadcasted_iota(jnp.int32, (LANES, PACK * LANES), 0)
    cc = lax.broadcasted_iota(jnp.int32, (LANES, PACK * LANES), 1)
    src = per_chunk * (cc // LANES) + SUB * ((cc % LANES) // HEAD_DIM) + cc % SUB
    unfold = jnp.where(rr == src, 1.0, 0.0).astype(BF16)
    return fold, unfold


def _substitute(coef):
    rows = coef.shape[0]
    ng = rows // SUBLANES
    gpb = SUB // SUBLANES
    row = lax.broadcasted_iota(jnp.int32, (SUBLANES, LANES), 0)
    col = lax.broadcasted_iota(jnp.int32, (SUBLANES, LANES), 1) % SUB
    xs = [jnp.where(col == (row + g * SUBLANES) % SUB, 1.0, 0.0).astype(F32) for g in range(ng)]
    for ss in range(SUB - 1):
        gs, rs = divmod(ss, SUBLANES)
        for blk in range(ng // gpb):
            xrow = xs[blk * gpb + gs][rs:rs + 1, :]
            for g in range(blk * gpb + (ss + 1) // SUBLANES, (blk + 1) * gpb):
                cf = coef[g * SUBLANES:(g + 1) * SUBLANES, ss * LANES:(ss + 1) * LANES]
                xs[g] = xs[g] - cf * xrow
    return jnp.concatenate(xs, axis=0)


def _rwkv_scratch(t_len):
    n_chunks = t_len // CHUNK
    return [pltpu.VMEM((LANES, (SUB - 1) * LANES), BF16),
            pltpu.VMEM((PACK * LANES, LANES), BF16),
            pltpu.VMEM((LANES, PACK * LANES), BF16),
            pltpu.VMEM((t_len, LANES), BF16),
            pltpu.VMEM((t_len, LANES), F32),
            pltpu.VMEM((t_len, LANES), F32),
            pltpu.VMEM((n_chunks, LANES, LANES), BF16),
            pltpu.VMEM((n_chunks, HEAD_DIM, LANES), F32),
            pltpu.VMEM((n_chunks, SUBLANES, LANES), F32),
            pltpu.VMEM((2, RWKV_GROUP, 2 * CHUNK, LANES), BF16),
            pltpu.VMEM((2, RWKV_GROUP, 4 * CHUNK, LANES), BF16),
            pltpu.VMEM((2, RWKV_GROUP, 2 * CHUNK, LANES), BF16),
            pltpu.VMEM((2, RWKV_GROUP, 2 * CHUNK, LANES), BF16),
            pltpu.VMEM((2, RWKV_GROUP * CHUNK, LANES), F32),
            pltpu.VMEM((2, RWKV_GROUP * CHUNK, LANES), BF16),
            pltpu.VMEM((2, RWKV_GROUP * CHUNK, LANES), BF16),
            pltpu.VMEM((2, RWKV_GROUP * CHUNK, LANES), BF16)]


def _rwkv_kernel(r_ref, k_ref, v_ref, wa_ref, z_ref, up_ref, dbase_ref, abase_ref, kns_ref,
                 kim_ref, bonus_ref, gng_ref, gnb_ref, o_ref,
                 spread_ref, fold_ref, unfold_ref, qt_ref, yloc_ref, bon_ref, mt_ref, nn_ref,
                 pe_ref, lhs_ref, rhs_ref, vbd_ref, kbd_ref, rd_ref, vb_ref, kdl_ref, bdl_ref):
    t_len = r_ref.shape[1]
    c = CHUNK
    grp = RWKV_GROUP
    rows = grp * c
    n_chunks = t_len // c
    lane = lax.broadcasted_iota(jnp.int32, (c, LANES), 1)
    is_h0 = lane < HEAD_DIM
    spread_ref[...] = _spread_matrix()
    fold, unfold = _fold_matrices()
    fold_ref[...] = jnp.concatenate(fold, axis=0)
    unfold_ref[...] = unfold
    assert grp % PACK == 0
    trow = lax.broadcasted_iota(jnp.int32, (c, LANES), 0)
    scol = lane % HEAD_DIM
    tb, sb = trow // SUB, scol // SUB
    strict = trow > scol
    incl = trow >= scol
    in_block = tb == sb
    ltri = (lax.broadcasted_iota(jnp.int32, (c, c), 0)
            >= lax.broadcasted_iota(jnp.int32, (c, c), 1)).astype(BF16)
    sq_r = lax.broadcasted_iota(jnp.int32, (LANES, LANES), 0)
    sq_c = lax.broadcasted_iota(jnp.int32, (LANES, LANES), 1)
    same_head = sq_r // HEAD_DIM == sq_c // HEAD_DIM

    dbase, abase = dbase_ref[...], abase_ref[...]
    kns, kim, bonus = kns_ref[...], kim_ref[...], bonus_ref[...]
    gng, gnb = gng_ref[...], gnb_ref[...]
    up_hi, up_lo = _split2(up_ref[...])
    up_hh = jnp.concatenate([up_hi, up_hi], axis=0)

    def head_sum(x):
        h0 = lax.broadcasted_iota(jnp.int32, x.shape, 1) < HEAD_DIM
        s0 = jnp.sum(jnp.where(h0, x, 0.0), axis=1, keepdims=True)
        s1 = jnp.sum(jnp.where(h0, 0.0, x), axis=1, keepdims=True)
        return jnp.where(h0, s0, s1)

    def block_diag(x):
        zero = jnp.zeros_like(x)
        return jnp.concatenate([jnp.where(is_h0, x, zero), jnp.where(is_h0, zero, x)],
                               axis=0).astype(BF16)

    n = grp
    js = range(n)
    cs = [slice(j * c, (j + 1) * c) for j in js]

    def prep(gi, slot):
        first_chunk = gi * grp
        sl = pl.ds(pl.multiple_of(first_chunk * c, n * c), n * c)
        r = r_ref[0, sl, :]
        k = k_ref[0, sl, :]
        v = v_ref[0, sl, :]
        wa = wa_ref[0, sl, :]
        is_h0_s = lax.broadcasted_iota(jnp.int32, (n * c, LANES), 1) < HEAD_DIM
        x_hi, x_lo = _split2(jnp.where(is_h0_s, jnp.tanh(wa), wa))
        lin = _dot(jnp.concatenate([x_hi, x_lo], axis=1), up_hh) + _dot(x_hi, up_lo)
        yield
        logw = -DECAY_SCALE * jax.nn.sigmoid(dbase + lin[:, :LANES])
        a = jax.nn.sigmoid(abase + lin[:, LANES:])
        kk = k * kns
        kk = kk * lax.rsqrt(jnp.maximum(head_sum(kk * kk), 1e-24))
        kmod = k * (1.0 + (a - 1.0) * kim)
        b = kk * a
        bon_ref[sl, :] = head_sum(r * kmod * bonus) * v

        parts = jnp.concatenate(_split2(logw), axis=1)
        gs = [_dot(ltri, parts[cs[j]]) for j in js]
        yield
        g = jnp.concatenate([x[:, LANES:] + x[:, :LANES] for x in gs], axis=0)
        p_end = [jnp.exp(g[(j + 1) * c - 1:(j + 1) * c, :]) for j in js]
        e_inv = jnp.exp(-g)
        kkd = kk * jnp.exp(g - logw)
        rd = r * jnp.exp(g)
        bi = b * e_inv
        ki = kmod * e_inv
        e_end = e_inv * jnp.concatenate([jnp.broadcast_to(x, (c, LANES)) for x in p_end], axis=0)
        rd_ref[slot] = rd
        vb_ref[slot] = v.astype(BF16)
        kdl_ref[slot] = (kmod * e_end).astype(BF16)
        bdl_ref[slot] = (b * e_end).astype(BF16)
        for j in js:
            lhs_ref[slot, j] = jnp.concatenate([kkd[cs[j]], rd[cs[j]]], axis=0).astype(BF16)
            rhs_ref[slot, j] = jnp.concatenate([block_diag(bi[cs[j]]), block_diag(ki[cs[j]])],
                                               axis=0)
            vbd_ref[slot, j] = block_diag(v[cs[j]])
            kbd_ref[slot, j] = block_diag(kkd[cs[j]])
            pe_ref[first_chunk + j] = jnp.broadcast_to(p_end[j], (SUBLANES, LANES))

    def pass_a(first_chunk, slot):
        aab, aak, arb, ark = [], [], [], []
        for j in js:
            pair = _dot_nt(lhs_ref[slot, j], rhs_ref[slot, j])
            aab.append(jnp.where(strict, pair[:c, :LANES], 0.0))
            aak.append(jnp.where(strict, pair[:c, LANES:], 0.0))
            arb.append(jnp.where(incl, pair[c:, :LANES], 0.0))
            ark.append(jnp.where(incl, pair[c:, LANES:], 0.0))
        yield
        akv = [_dot(jnp.concatenate([aak[j], ark[j]], axis=0).astype(BF16), vbd_ref[slot, j])
               for j in js]
        aakv = [x[:c] for x in akv]
        arkv = [x[c:] for x in akv]
        n_diag = jnp.concatenate(
            [jnp.concatenate([jnp.where(in_block, aab[q * PACK + i], 0.0).astype(BF16)
                              for i in range(PACK)], axis=1)
             for q in range(n // PACK)], axis=0)
        compact = _dot(n_diag, fold_ref[...])
        yield
        coef = _dot(compact.astype(BF16), spread_ref[...])
        t_diag = _substitute(coef)
        yield
        spread_out = _dot(t_diag.astype(BF16), unfold_ref[...])
        tinv = [jnp.where(in_block, spread_out[(j // PACK) * c:(j // PACK + 1) * c,
                                               (j % PACK) * LANES:(j % PACK + 1) * LANES], 0.0)
                for j in js]
        yield
        span = 1
        while span * SUB < c:
            low_mask = (tb // span == sb // span + 1) & ((tb // span) % 2 == 1)
            inner = [_dot(jnp.where(low_mask, aab[j], 0.0).astype(BF16),
                          block_diag(tinv[j])) for j in js]
            yield
            tinv = [tinv[j] - _dot(tinv[j].astype(BF16), block_diag(inner[j])) for j in js]
            yield
            span *= 2
        wu = [_dot(tinv[j].astype(BF16),
                   jnp.concatenate([kbd_ref[slot, j], block_diag(aakv[j])], axis=1))
              for j in js]
        yield
        w = [x[:, :LANES] for x in wu]
        uloc = [x[:, LANES:] for x in wu]
        arb_b = [arb[j].astype(BF16) for j in js]
        bdl_b = [bdl_ref[slot, cs[j], :] for j in js]
        awu = [_dot(arb_b[j], jnp.concatenate([block_diag(w[j]), block_diag(uloc[j])], axis=1))
               for j in js]
        qt = [rd_ref[slot, cs[j], :] - awu[j][:, :LANES] for j in js]
        yloc = [arkv[j] - awu[j][:, LANES:] for j in js]
        wtb = [_dot_tn(w[j].astype(BF16), bdl_b[j]) for j in js]
        nn = [_dot_tn(jnp.concatenate([vb_ref[slot, cs[j], :], (-uloc[j]).astype(BF16)], axis=0),
                      jnp.concatenate([kdl_ref[slot, cs[j], :], bdl_b[j]], axis=0))
              for j in js]
        yield
        for j in js:
            ci = first_chunk + j
            csl = pl.ds(pl.multiple_of(ci * c, c), c)
            qt_ref[csl, :] = qt[j].astype(BF16)
            yloc_ref[csl, :] = yloc[j]
            mt_ref[ci] = jnp.where(same_head, wtb[j], 0.0).astype(BF16)
            nn_ref[ci] = jnp.where(is_h0, nn[j][:HEAD_DIM], nn[j][HEAD_DIM:])

    def chunk_step(ci, state):
        sb = state.astype(BF16)
        sl = pl.ds(pl.multiple_of(ci * c, c), c)
        y = _dot_nt(qt_ref[sl, :], block_diag(sb)) + yloc_ref[sl, :]
        new_state = state * pe_ref[ci, 0:1, :] + nn_ref[ci] - _dot(sb, mt_ref[ci])
        mu = head_sum(y) * (1.0 / HEAD_DIM)
        d = y - mu
        var = head_sum(d * d) * (1.0 / HEAD_DIM)
        yn = d * lax.rsqrt(var + GN_EPS) * gng + gnb + bon_ref[sl, :]
        z = z_ref[0, sl, :]
        o_ref[0, sl, :] = (yn * (z * jax.nn.sigmoid(z))).astype(o_ref.dtype)
        return new_state

    class Chain:
        def __init__(self, gi, state):
            self.gi, self.state, self.done = gi, state, 0

        def tick(self):
            if self.done < grp:
                self.state = chunk_step(self.gi * grp + self.done, self.state)
                self.done += 1

        def drain(self):
            while self.done < grp:
                self.tick()
            return self.state

    n_groups = n_chunks // grp

    def run_group(gi, slot, chain, prepare_next=True):
        nxt = prep(gi + 1, 1 - slot) if prepare_next else iter(())
        for step, _ in enumerate(pass_a(gi * grp, slot)):
            if step in PREP_STEPS:
                next(nxt, None)
            if chain is not None:
                chain.tick()
        for _ in nxt:
            pass
        return chain.drain() if chain is not None else None

    def body(pair, state):
        gi = 2 * pair + 1
        state = run_group(gi, 1, Chain(gi - 1, state))
        return run_group(gi + 1, 0, Chain(gi, state))

    assert n_groups % 2 == 0
    for _ in prep(0, 0):
        pass
    run_group(0, 0, None)
    state = lax.fori_loop(0, (n_groups - 2) // 2, body, jnp.zeros((HEAD_DIM, LANES), F32))
    state = run_group(n_groups - 1, 1, Chain(n_groups - 2, state), prepare_next=False)
    Chain(n_groups - 1, state).drain()


def _attn_kernel(q_ref, k_ref, v_ref, z_ref, o_ref, kb_ref, vb_ref, acc_ref, m_ref, l_ref):
    t_len = q_ref.shape[1]
    blk = ATTN_BLOCK
    kb_ref[...] = k_ref[0].astype(BF16)
    vb_ref[...] = v_ref[0].astype(BF16)

    lane = lax.broadcasted_iota(jnp.int32, (blk, LANES), 1)
    is_h0 = lane < HEAD_DIM
    qi = lax.broadcasted_iota(jnp.int32, (blk, 2 * blk), 0)
    kj = lax.broadcasted_iota(jnp.int32, (blk, 2 * blk), 1)
    nb = ATTN_GROUP
    heads = (is_h0, ~is_h0)
    patterns = tuple(reversed(DILATION_PATTERNS))
    assert patterns[-1][1] == 1
    bias_causal = jnp.where(kj <= qi, 0.0, NEG_BIG).astype(F32)

    for pi, (window, dil) in enumerate(patterns):
        span = window // dil
        nblk = t_len // (dil * blk)
        bias_band = jnp.where((qi + blk - kj >= 0) & (qi + blk - kj <= span), 0.0,
                              NEG_BIG).astype(F32)
        last = pi == len(patterns) - 1

        def body(it, _, pi=pi, dil=dil, nblk=nblk, bias_band=bias_band, last=last):
            rows, q, kw, vw, biases = [], [], [], [], []
            if nblk == 2 and nb % 2 == 0:
                for jr in range(nb // 2):
                    res = it * (nb // 2) + jr
                    seq = pl.ds(res, 2 * blk, stride=dil)
                    q_seq = q_ref[0, seq, :]
                    k_seq = k_ref[0, seq, :].astype(BF16)
                    v_seq = v_ref[0, seq, :].astype(BF16)
                    for n in range(2):
                        rows.append(pl.ds(res + dil * blk * n, blk, stride=dil))
                        q.append(q_seq[n * blk:(n + 1) * blk])
                        kw.append(k_seq)
                        vw.append(v_seq)
                        biases.append(bias_band if n else bias_causal)
            else:
                for j in range(nb):
                    res = (it * nb + j) // nblk
                    n = (it * nb + j) % nblk
                    maybe_first = j == 0 or nblk % nb != 0
                    n_prev = jnp.maximum(n - 1, 0) if maybe_first else n - 1
                    q0 = res + dil * blk * n
                    k0 = res + dil * blk * n_prev
                    if dil == 1:
                        rows.append(pl.ds(pl.multiple_of(q0, blk), blk))
                        krows = pl.ds(pl.multiple_of(k0, blk), 2 * blk)
                        kw.append(kb_ref[krows, :])
                        vw.append(vb_ref[krows, :])
                    else:
                        rows.append(pl.ds(q0, blk, stride=dil))
                        krows = pl.ds(k0, 2 * blk, stride=dil)
                        kw.append(k_ref[0, krows, :].astype(BF16))
                        vw.append(v_ref[0, krows, :].astype(BF16))
                    q.append(q_ref[0, rows[j], :])
                    biases.append(jnp.where(n == 0, bias_causal, bias_band) if maybe_first
                                  else bias_band)
            s = [[_dot_nt(jnp.where(h, q[j], 0.0).astype(BF16), kw[j]) + biases[j] for h in heads]
                 for j in range(nb)]
            m = [[jnp.max(x, axis=1, keepdims=True) for x in sj] for sj in s]
            p = [[jnp.exp2(s[j][h] - m[j][h]) for h in range(2)] for j in range(nb)]
            l = [[jnp.sum(x, axis=1, keepdims=True) for x in pj] for pj in p]
            pv = [[_dot(p[j][h].astype(BF16), vw[j]) for h in range(2)] for j in range(nb)]
            for j in range(nb):
                acc = jnp.where(is_h0, pv[j][0], pv[j][1])
                mj = jnp.where(is_h0, m[j][0], m[j][1])
                lj = jnp.where(is_h0, l[j][0], l[j][1])
                if last:
                    ms = [mj] + [m_ref[pp, rows[j], :] for pp in range(pi)]
                    ls = [lj] + [l_ref[pp, rows[j], :] for pp in range(pi)]
                    accs = [acc] + [acc_ref[pp, rows[j], :] for pp in range(pi)]
                    m_all = functools.reduce(jnp.maximum, ms)
                    ws = [jnp.exp2(mm - m_all) for mm in ms]
                    num = functools.reduce(jnp.add, [w * a for w, a in zip(ws, accs)])
                    den = functools.reduce(jnp.add, [w * x for w, x in zip(ws, ls)])
                    z = z_ref[0, rows[j], :]
                    o_ref[0, rows[j], :] = (num / den * (z * jax.nn.sigmoid(z))).astype(o_ref.dtype)
                else:
                    acc_ref[pi, rows[j], :] = acc
                    m_ref[pi, rows[j], :] = mj
                    l_ref[pi, rows[j], :] = lj
            return 0

        lax.fori_loop(0, t_len // (blk * nb), body, 0)


def _out_proj_kernel(ya_ref, yb_ref, wa_ref, wb_ref, x_ref, g_ref, o_ref):
    y = x_ref[...] + _dot(ya_ref[...], wa_ref[...]) + _dot(yb_ref[...], wb_ref[...])
    o_ref[...] = y * lax.rsqrt(jnp.mean(y * y, axis=-1, keepdims=True) + NORM_EPS) * g_ref[...]


def _rope_tables(t_len):
    half = ROPE_DIM // 2
    inv = ROPE_THETA ** (-jnp.arange(half, dtype=F32) * 2.0 / ROPE_DIM)
    ang = jnp.arange(t_len, dtype=jnp.int32).astype(F32)[:, None] * inv[None, :]
    cos, sin = jnp.cos(ang), jnp.sin(ang)
    ones = jnp.ones((t_len, HEAD_DIM - ROPE_DIM), F32)
    zeros = jnp.zeros((t_len, HEAD_DIM - ROPE_DIM), F32)
    zh = jnp.zeros((t_len, half), F32)
    tile = lambda a: jnp.tile(a, (1, LANES // HEAD_DIM))
    cos_t = tile(jnp.concatenate([cos, cos, ones], axis=1))
    sa_t = tile(jnp.concatenate([-sin, zh, zeros], axis=1))
    sb_t = tile(jnp.concatenate([zh, sin, zeros], axis=1))
    return cos_t, sa_t, sb_t


def kernel(x, norm_gain, w_in, shift_mix, decay_base, decay_up, iclr_base, iclr_up, key_norm_scale,
           key_iclr_mix, bonus, gn_gain, gn_bias, w_out, final_gain):
    bsz, t_len, d_model = x.shape
    depth = w_in.shape[0]
    c_a = decay_base.shape[1]
    n_hp = c_a // LANES
    shift_w = 3 * c_a + DECAY_RANK + ICLR_RANK
    in_w = w_in.shape[2]
    c_b = (in_w - shift_w - c_a) // 4
    assert c_a % LANES == 0 and c_b == c_a
    assert all(t_len % (2 * dil * ATTN_BLOCK) == 0 for _, dil in DILATION_PATTERNS)
    o_za = shift_w
    o_q = o_za + c_a
    o_k, o_v, o_zb = o_q + c_b, o_q + 2 * c_b, o_q + 3 * c_b
    segs = ((0, c_a), (c_a, 2 * c_a), (2 * c_a, 3 * c_a), (3 * c_a, shift_w), (o_za, o_q),
            (o_q, o_k), (o_k, o_v), (o_v, o_zb), (o_zb, in_w))
    cos_t, sa_t, sb_t = _rope_tables(t_len)
    rows = bsz * t_len
    tm = IN_PROJ_ROWS
    tiles_per_seq = t_len // tm
    row2 = lambda a: a.reshape(1, -1).astype(F32)

    for layer in range(depth):
        p = pl.pallas_call(
            functools.partial(_in_proj_kernel, segs=segs, shift_w=shift_w, rope_lo=o_q, q_hi=o_k,
                              rope_hi=o_v, tiles_per_seq=tiles_per_seq),
            grid=(rows // tm,),
            in_specs=[
                pl.BlockSpec((tm, d_model), lambda i: (i, 0)),
                pl.BlockSpec((1, d_model), lambda i: (0, 0)),
                pl.BlockSpec((d_model, in_w), lambda i: (0, 0)),
                pl.BlockSpec((1, shift_w), lambda i: (0, 0)),
                pl.BlockSpec((tm, LANES), lambda i: (i % tiles_per_seq, 0)),
                pl.BlockSpec((tm, LANES), lambda i: (i % tiles_per_seq, 0)),
                pl.BlockSpec((tm, LANES), lambda i: (i % tiles_per_seq, 0)),
            ],
            out_specs=pl.BlockSpec((tm, in_w), lambda i: (i, 0)),
            out_shape=jax.ShapeDtypeStruct((rows, in_w), F32),
            scratch_shapes=[pltpu.VMEM((SUBLANES, shift_w), F32)],
            compiler_params=pltpu.CompilerParams(dimension_semantics=("arbitrary",),
                                                 vmem_limit_bytes=VMEM_LIMIT),
            name="in_proj",
        )(x.reshape(rows, d_model), row2(norm_gain[layer]), w_in[layer].astype(BF16),
          row2(shift_mix[layer]), cos_t, sa_t, sb_t)
        p = p.reshape(bsz, t_len, in_w)

        zeros_up = jnp.zeros((DECAY_RANK, c_a), F32)
        up_full = jnp.concatenate(
            [jnp.concatenate([decay_up[layer].astype(F32), zeros_up], axis=0).reshape(
                DECAY_RANK + ICLR_RANK, n_hp, 1, LANES),
             jnp.concatenate([zeros_up, iclr_up[layer].astype(F32)], axis=0).reshape(
                 DECAY_RANK + ICLR_RANK, n_hp, 1, LANES)], axis=2).reshape(
                     DECAY_RANK + ICLR_RANK, n_hp * 2 * LANES)
        col = lambda off: (lambda b, h: (b, 0, off // LANES + h))
        par = lambda b, h: (0, h)
        seq_spec = lambda off: pl.BlockSpec((1, t_len, LANES), col(off))
        par_spec = pl.BlockSpec((1, LANES), par)
        ya = pl.pallas_call(
            _rwkv_kernel,
            grid=(bsz, n_hp),
            in_specs=[seq_spec(0), seq_spec(c_a), seq_spec(2 * c_a),
                      pl.BlockSpec((1, t_len, LANES), lambda b, h: (b, 0, 3 * c_a // LANES)),
                      seq_spec(o_za),
                      pl.BlockSpec((DECAY_RANK + ICLR_RANK, 2 * LANES), par),
                      par_spec, par_spec, par_spec, par_spec, par_spec, par_spec, par_spec],
            out_specs=pl.BlockSpec((1, t_len, LANES), lambda b, h: (b, 0, h)),
            out_shape=jax.ShapeDtypeStruct((bsz, t_len, c_a), BF16),
            scratch_shapes=_rwkv_scratch(t_len),
            compiler_params=pltpu.CompilerParams(dimension_semantics=("arbitrary", "arbitrary"),
                                                 vmem_limit_bytes=VMEM_LIMIT),
            name="rwkv",
        )(p, p, p, p, p, up_full, row2(decay_base[layer]), row2(iclr_base[layer]),
          row2(key_norm_scale[layer]), row2(key_iclr_mix[layer]), row2(bonus[layer]),
          row2(gn_gain[layer]), row2(gn_bias[layer]))

        yb = pl.pallas_call(
            _attn_kernel,
            grid=(bsz, c_b // LANES),
            in_specs=[seq_spec(o_q), seq_spec(o_k), seq_spec(o_v), seq_spec(o_zb)],
            out_specs=pl.BlockSpec((1, t_len, LANES), lambda b, h: (b, 0, h)),
            out_shape=jax.ShapeDtypeStruct((bsz, t_len, c_b), BF16),
            scratch_shapes=[pltpu.VMEM((t_len, LANES), BF16),
                            pltpu.VMEM((t_len, LANES), BF16),
                            pltpu.VMEM((len(DILATION_PATTERNS) - 1, t_len, LANES), F32),
                            pltpu.VMEM((len(DILATION_PATTERNS) - 1, t_len, LANES), F32),
                            pltpu.VMEM((len(DILATION_PATTERNS) - 1, t_len, LANES), F32)],
            compiler_params=pltpu.CompilerParams(dimension_semantics=("arbitrary", "arbitrary"),
                                                 vmem_limit_bytes=VMEM_LIMIT),
            name="attention",
        )(p, p, p, p)

        assert depth == 1
        tmo = OUT_PROJ_ROWS
        wo = w_out[layer].astype(BF16)
        x = pl.pallas_call(
            _out_proj_kernel,
            grid=(rows // tmo,),
            in_specs=[pl.BlockSpec((tmo, c_a), lambda i: (i, 0)),
                      pl.BlockSpec((tmo, c_b), lambda i: (i, 0)),
                      pl.BlockSpec((c_a, d_model), lambda i: (0, 0)),
                      pl.BlockSpec((c_b, d_model), lambda i: (0, 0)),
                      pl.BlockSpec((tmo, d_model), lambda i: (i, 0)),
                      pl.BlockSpec((1, d_model), lambda i: (0, 0))],
            out_specs=pl.BlockSpec((tmo, d_model), lambda i: (i, 0)),
            out_shape=jax.ShapeDtypeStruct((rows, d_model), F32),
            compiler_params=pltpu.CompilerParams(dimension_semantics=("arbitrary",),
                                                 vmem_limit_bytes=VMEM_LIMIT),
            name="out_proj",
        )(ya.reshape(rows, c_a), yb.reshape(rows, c_b), wo[:c_a], wo[c_a:],
          x.reshape(rows, d_model), row2(final_gain)).reshape(bsz, t_len, d_model)
    return x
```

```python
import functools

import jax
import jax.numpy as jnp
from jax import lax
from jax.experimental import pallas as pl
from jax.experimental.pallas import tpu as pltpu

HEAD_DIM = 64
LANES = 128
SUBLANES = 8
DECAY_RANK = 64
ICLR_RANK = 64
DILATION_PATTERNS = ((128, 1), (512, 4), (2048, 16))
ATTN_BLOCK = 128
ATTN_GROUP = 4
ROPE_THETA = 500000.0
ROPE_DIM = HEAD_DIM // 4
NORM_EPS = 1e-6
GN_EPS = 64e-5
CHUNK = 64
SUB = 16
PACK = LANES // (2 * SUB)
RWKV_GROUP = 8
PREP_STEPS = (2, 4, 6)
IN_PROJ_ROWS = 512
OUT_PROJ_ROWS = 1024
DECAY_SCALE = 0.6065306597126334
LOG2_E = 1.4426950408889634
ATTN_Q_SCALE = HEAD_DIM ** -0.5 * LOG2_E
NEG_BIG = -1e30
VMEM_LIMIT = 56 * 1024 * 1024

F32 = jnp.float32
BF16 = jnp.bfloat16


def _dot(a, b, **kw):
    return jnp.dot(a, b, preferred_element_type=F32, **kw)


def _dot_nt(a, b):
    return lax.dot_general(a, b, (((1,), (1,)), ((), ())), preferred_element_type=F32)


def _split2(x):
    hi = x.astype(BF16)
    return hi, (x - hi.astype(F32)).astype(BF16)


def _dot_tn(a, b):
    return lax.dot_general(a, b, (((0,), (0,)), ((), ())), preferred_element_type=F32)


def _in_proj_kernel(x_ref, g_ref, w_ref, mix_ref, cos_ref, sa_ref, sb_ref, o_ref, carry_ref,
                    *, segs, shift_w, rope_lo, q_hi, rope_hi, tiles_per_seq):
    i = pl.program_id(0)

    @pl.when(i == 0)
    def _():
        carry_ref[...] = jnp.zeros(carry_ref.shape, F32)

    x = x_ref[...]
    h = x * lax.rsqrt(jnp.mean(x * x, axis=-1, keepdims=True) + NORM_EPS) * g_ref[...]
    hb = h.astype(BF16)
    tm = x.shape[0]
    first = (i % tiles_per_seq) == 0
    row0 = lax.broadcasted_iota(jnp.int32, (tm, 1), 0) == 0
    for lo, hi in segs:
        p = _dot(hb, w_ref[:, lo:hi])
        if hi <= shift_w:
            old = jnp.where(first, 0.0, carry_ref[0:1, lo:hi])
            carry_ref[0:1, lo:hi] = p[tm - 1:tm, :]
            prev = jnp.where(row0, old, pltpu.roll(p, 1, axis=0))
            p = p + (prev - p) * mix_ref[:, lo:hi]
            o_ref[:, lo:hi] = p
        elif lo >= rope_lo and hi <= rope_hi:
            if hi <= q_hi:
                p = p * ATTN_Q_SCALE
            for j in range(lo, hi, LANES):
                t = p[:, j - lo:j - lo + LANES]
                t = (t * cos_ref[...] + pltpu.roll(t, LANES - ROPE_DIM // 2, axis=1) * sa_ref[...]
                     + pltpu.roll(t, ROPE_DIM // 2, axis=1) * sb_ref[...])
                o_ref[:, j:j + LANES] = t
        else:
            o_ref[:, lo:hi] = p


def _spread_matrix():
    j = lax.broadcasted_iota(jnp.int32, (LANES, (SUB - 1) * LANES), 0)
    col = lax.broadcasted_iota(jnp.int32, (LANES, (SUB - 1) * LANES), 1)
    sel = j == SUB * ((col % LANES) // SUB) + col // LANES
    return jnp.where(sel, 1.0, 0.0).astype(BF16)


def _fold_matrices():
    r = lax.broadcasted_iota(jnp.int32, (LANES, LANES), 0)
    l = lax.broadcasted_iota(jnp.int32, (LANES, LANES), 1)
    per_chunk = LANES // PACK
    match = ((l % per_chunk) // SUB == r // HEAD_DIM) & (l % SUB == r % SUB)
    fold = [jnp.where(match & (l // per_chunk == c), 1.0, 0.0).astype(BF16) for c in range(PACK)]
    rr = lax.broadcasted_iota(jnp.int32, (LANES, PACK * LANES), 0)
    cc = lax.broadcasted_iota(jnp.int32, (LANES, PACK * LANES), 1)
    src = per_chunk * (cc // LANES) + SUB * ((cc % LANES) // HEAD_DIM) + cc % SUB
    unfold = jnp.where(rr == src, 1.0, 0.0).astype(BF16)
    return fold, unfold


def _substitute(coef):
    rows = coef.shape[0]
    ng = rows // SUBLANES
    gpb = SUB // SUBLANES
    row = lax.broadcasted_iota(jnp.int32, (SUBLANES, LANES), 0)
    col = lax.broadcasted_iota(jnp.int32, (SUBLANES, LANES), 1) % SUB
    xs = [jnp.where(col == (row + g * SUBLANES) % SUB, 1.0, 0.0).astype(F32) for g in range(ng)]
    for ss in range(SUB - 1):
        gs, rs = divmod(ss, SUBLANES)
        for blk in range(ng // gpb):
            xrow = xs[blk * gpb + gs][rs:rs + 1, :]
            for g in range(blk * gpb + (ss + 1) // SUBLANES, (blk + 1) * gpb):
                cf = coef[g * SUBLANES:(g + 1) * SUBLANES, ss * LANES:(ss + 1) * LANES]
                xs[g] = xs[g] - cf * xrow
    return jnp.concatenate(xs, axis=0)


def _rwkv_scratch(t_len):
    n_chunks = t_len // CHUNK
    return [pltpu.VMEM((LANES, (SUB - 1) * LANES), BF16),
            pltpu.VMEM((PACK * LANES, LANES), BF16),
            pltpu.VMEM((LANES, PACK * LANES), BF16),
            pltpu.VMEM((t_len, LANES), BF16),
            pltpu.VMEM((t_len, LANES), F32),
            pltpu.VMEM((t_len, LANES), F32),
            pltpu.VMEM((n_chunks, LANES, LANES), BF16),
            pltpu.VMEM((n_chunks, HEAD_DIM, LANES), F32),
            pltpu.VMEM((n_chunks, SUBLANES, LANES), F32),
            pltpu.VMEM((2, RWKV_GROUP, 2 * CHUNK, LANES), BF16),
            pltpu.VMEM((2, RWKV_GROUP, 4 * CHUNK, LANES), BF16),
            pltpu.VMEM((2, RWKV_GROUP, 2 * CHUNK, LANES), BF16),
            pltpu.VMEM((2, RWKV_GROUP, 2 * CHUNK, LANES), BF16),
            pltpu.VMEM((2, RWKV_GROUP * CHUNK, LANES), F32),
            pltpu.VMEM((2, RWKV_GROUP * CHUNK, LANES), BF16),
            pltpu.VMEM((2, RWKV_GROUP * CHUNK, LANES), BF16),
            pltpu.VMEM((2, RWKV_GROUP * CHUNK, LANES), BF16)]


def _rwkv_kernel(r_ref, k_ref, v_ref, wa_ref, z_ref, up_ref, dbase_ref, abase_ref, kns_ref,
                 kim_ref, bonus_ref, gng_ref, gnb_ref, o_ref,
                 spread_ref, fold_ref, unfold_ref, qt_ref, yloc_ref, bon_ref, mt_ref, nn_ref,
                 pe_ref, lhs_ref, rhs_ref, vbd_ref, kbd_ref, rd_ref, vb_ref, kdl_ref, bdl_ref):
    t_len = r_ref.shape[1]
    c = CHUNK
    grp = RWKV_GROUP
    rows = grp * c
    n_chunks = t_len // c
    lane = lax.broadcasted_iota(jnp.int32, (c, LANES), 1)
    is_h0 = lane < HEAD_DIM
    spread_ref[...] = _spread_matrix()
    fold, unfold = _fold_matrices()
    fold_ref[...] = jnp.concatenate(fold, axis=0)
    unfold_ref[...] = unfold
    assert grp % PACK == 0
    trow = lax.broadcasted_iota(jnp.int32, (c, LANES), 0)
    scol = lane % HEAD_DIM
    tb, sb = trow // SUB, scol // SUB
    strict = trow > scol
    incl = trow >= scol
    in_block = tb == sb
    ltri = (lax.broadcasted_iota(jnp.int32, (c, c), 0)
            >= lax.broadcasted_iota(jnp.int32, (c, c), 1)).astype(BF16)
    sq_r = lax.broadcasted_iota(jnp.int32, (LANES, LANES), 0)
    sq_c = lax.broadcasted_iota(jnp.int32, (LANES, LANES), 1)
    same_head = sq_r // HEAD_DIM == sq_c // HEAD_DIM

    dbase, abase = dbase_ref[...], abase_ref[...]
    kns, kim, bonus = kns_ref[...], kim_ref[...], bonus_ref[...]
    gng, gnb = gng_ref[...], gnb_ref[...]
    up_hi, up_lo = _split2(up_ref[...])
    up_hh = jnp.concatenate([up_hi, up_hi], axis=0)

    def head_sum(x):
        h0 = lax.broadcasted_iota(jnp.int32, x.shape, 1) < HEAD_DIM
        s0 = jnp.sum(jnp.where(h0, x, 0.0), axis=1, keepdims=True)
        s1 = jnp.sum(jnp.where(h0, 0.0, x), axis=1, keepdims=True)
        return jnp.where(h0, s0, s1)

    def block_diag(x):
        zero = jnp.zeros_like(x)
        return jnp.concatenate([jnp.where(is_h0, x, zero), jnp.where(is_h0, zero, x)],
                               axis=0).astype(BF16)

    n = grp
    js = range(n)
    cs = [slice(j * c, (j + 1) * c) for j in js]

    def prep(gi, slot):
        first_chunk = gi * grp
        sl = pl.ds(pl.multiple_of(first_chunk * c, n * c), n * c)
        r = r_ref[0, sl, :]
        k = k_ref[0, sl, :]
        v = v_ref[0, sl, :]
        wa = wa_ref[0, sl, :]
        is_h0_s = lax.broadcasted_iota(jnp.int32, (n * c, LANES), 1) < HEAD_DIM
        x_hi, x_lo = _split2(jnp.where(is_h0_s, jnp.tanh(wa), wa))
        lin = _dot(jnp.concatenate([x_hi, x_lo], axis=1), up_hh) + _dot(x_hi, up_lo)
        yield
        logw = -DECAY_SCALE * jax.nn.sigmoid(dbase + lin[:, :LANES])
        a = jax.nn.sigmoid(abase + lin[:, LANES:])
        kk = k * kns
        kk = kk * lax.rsqrt(jnp.maximum(head_sum(kk * kk), 1e-24))
        kmod = k * (1.0 + (a - 1.0) * kim)
        b = kk * a
        bon_ref[sl, :] = head_sum(r * kmod * bonus) * v

        parts = jnp.concatenate(_split2(logw), axis=1)
        gs = [_dot(ltri, parts[cs[j]]) for j in js]
        yield
        g = jnp.concatenate([x[:, LANES:] + x[:, :LANES] for x in gs], axis=0)
        p_end = [jnp.exp(g[(j + 1) * c - 1:(j + 1) * c, :]) for j in js]
        e_inv = jnp.exp(-g)
        kkd = kk * jnp.exp(g - logw)
        rd = r * jnp.exp(g)
        bi = b * e_inv
        ki = kmod * e_inv
        e_end = e_inv * jnp.concatenate([jnp.broadcast_to(x, (c, LANES)) for x in p_end], axis=0)
        rd_ref[slot] = rd
        vb_ref[slot] = v.astype(BF16)
        kdl_ref[slot] = (kmod * e_end).astype(BF16)
        bdl_ref[slot] = (b * e_end).astype(BF16)
        for j in js:
            lhs_ref[slot, j] = jnp.concatenate([kkd[cs[j]], rd[cs[j]]], axis=0).astype(BF16)
            rhs_ref[slot, j] = jnp.concatenate([block_diag(bi[cs[j]]), block_diag(ki[cs[j]])],
                                               axis=0)
            vbd_ref[slot, j] = block_diag(v[cs[j]])
            kbd_ref[slot, j] = block_diag(kkd[cs[j]])
            pe_ref[first_chunk + j] = jnp.broadcast_to(p_end[j], (SUBLANES, LANES))

    def pass_a(first_chunk, slot):
        aab, aak, arb, ark = [], [], [], []
        for j in js:
            pair = _dot_nt(lhs_ref[slot, j], rhs_ref[slot, j])
            aab.append(jnp.where(strict, pair[:c, :LANES], 0.0))
            aak.append(jnp.where(strict, pair[:c, LANES:], 0.0))
            arb.append(jnp.where(incl, pair[c:, :LANES], 0.0))
            ark.append(jnp.where(incl, pair[c:, LANES:], 0.0))
        yield
        akv = [_dot(jnp.concatenate([aak[j], ark[j]], axis=0).astype(BF16), vbd_ref[slot, j])
               for j in js]
        aakv = [x[:c] for x in akv]
        arkv = [x[c:] for x in akv]
        n_diag = jnp.concatenate(
            [jnp.concatenate([jnp.where(in_block, aab[q * PACK + i], 0.0).astype(BF16)
                              for i in range(PACK)], axis=1)
             for q in range(n // PACK)], axis=0)
        compact = _dot(n_diag, fold_ref[...])
        yield
        coef = _dot(compact.astype(BF16), spread_ref[...])
        t_diag = _substitute(coef)
        yield
        spread_out = _dot(t_diag.astype(BF16), unfold_ref[...])
        tinv = [jnp.where(in_block, spread_out[(j // PACK) * c:(j // PACK + 1) * c,
                                               (j % PACK) * LANES:(j % PACK + 1) * LANES], 0.0)
                for j in js]
        yield
        span = 1
        while span * SUB < c:
            low_mask = (tb // span == sb // span + 1) & ((tb // span) % 2 == 1)
            inner = [_dot(jnp.where(low_mask, aab[j], 0.0).astype(BF16),
                          block_diag(tinv[j])) for j in js]
            yield
            tinv = [tinv[j] - _dot(tinv[j].astype(BF16), block_diag(inner[j])) for j in js]
            yield
            span *= 2
        wu = [_dot(tinv[j].astype(BF16),
                   jnp.concatenate([kbd_ref[slot, j], block_diag(aakv[j])], axis=1))
              for j in js]
        yield
        w = [x[:, :LANES] for x in wu]
        uloc = [x[:, LANES:] for x in wu]
        arb_b = [arb[j].astype(BF16) for j in js]
        bdl_b = [bdl_ref[slot, cs[j], :] for j in js]
        awu = [_dot(arb_b[j], jnp.concatenate([block_diag(w[j]), block_diag(uloc[j])], axis=1))
               for j in js]
        qt = [rd_ref[slot, cs[j], :] - awu[j][:, :LANES] for j in js]
        yloc = [arkv[j] - awu[j][:, LANES:] for j in js]
        wtb = [_dot_tn(w[j].astype(BF16), bdl_b[j]) for j in js]
        nn = [_dot_tn(jnp.concatenate([vb_ref[slot, cs[j], :], (-uloc[j]).astype(BF16)], axis=0),
                      jnp.concatenate([kdl_ref[slot, cs[j], :], bdl_b[j]], axis=0))
              for j in js]
        yield
        for j in js:
            ci = first_chunk + j
            csl = pl.ds(pl.multiple_of(ci * c, c), c)
            qt_ref[csl, :] = qt[j].astype(BF16)
            yloc_ref[csl, :] = yloc[j]
            mt_ref[ci] = jnp.where(same_head, wtb[j], 0.0).astype(BF16)
            nn_ref[ci] = jnp.where(is_h0, nn[j][:HEAD_DIM], nn[j][HEAD_DIM:])

    def chunk_step(ci, state):
        sb = state.astype(BF16)
        sl = pl.ds(pl.multiple_of(ci * c, c), c)
        y = _dot_nt(qt_ref[sl, :], block_diag(sb)) + yloc_ref[sl, :]
        new_state = state * pe_ref[ci, 0:1, :] + nn_ref[ci] - _dot(sb, mt_ref[ci])
        mu = head_sum(y) * (1.0 / HEAD_DIM)
        d = y - mu
        var = head_sum(d * d) * (1.0 / HEAD_DIM)
        yn = d * lax.rsqrt(var + GN_EPS) * gng + gnb + bon_ref[sl, :]
        z = z_ref[0, sl, :]
        o_ref[0, sl, :] = (yn * (z * jax.nn.sigmoid(z))).astype(o_ref.dtype)
        return new_state

    class Chain:
        def __init__(self, gi, state):
            self.gi, self.state, self.done = gi, state, 0

        def tick(self):
            if self.done < grp:
                self.state = chunk_step(self.gi * grp + self.done, self.state)
                self.done += 1

        def drain(self):
            while self.done < grp:
                self.tick()
            return self.state

    n_groups = n_chunks // grp

    def run_group(gi, slot, chain, prepare_next=True):
        nxt = prep(gi + 1, 1 - slot) if prepare_next else iter(())
        for step, _ in enumerate(pass_a(gi * grp, slot)):
            if step in PREP_STEPS:
                next(nxt, None)
            if chain is not None:
                chain.tick()
        for _ in nxt:
            pass
        return chain.drain() if chain is not None else None

    def body(pair, state):
        gi = 2 * pair + 1
        state = run_group(gi, 1, Chain(gi - 1, state))
        return run_group(gi + 1, 0, Chain(gi, state))

    assert n_groups % 2 == 0
    for _ in prep(0, 0):
        pass
    run_group(0, 0, None)
    state = lax.fori_loop(0, (n_groups - 2) // 2, body, jnp.zeros((HEAD_DIM, LANES), F32))
    state = run_group(n_groups - 1, 1, Chain(n_groups - 2, state), prepare_next=False)
    Chain(n_groups - 1, state).drain()


def _attn_kernel(q_ref, k_ref, v_ref, z_ref, o_ref, kb_ref, vb_ref, acc_ref, m_ref, l_ref):
    t_len = q_ref.shape[1]
    blk = ATTN_BLOCK
    kb_ref[...] = k_ref[0].astype(BF16)
    vb_ref[...] = v_ref[0].astype(BF16)

    lane = lax.broadcasted_iota(jnp.int32, (blk, LANES), 1)
    is_h0 = lane < HEAD_DIM
    qi = lax.broadcasted_iota(jnp.int32, (blk, 2 * blk), 0)
    kj = lax.broadcasted_iota(jnp.int32, (blk, 2 * blk), 1)
    nb = ATTN_GROUP
    heads = (is_h0, ~is_h0)
    patterns = tuple(reversed(DILATION_PATTERNS))
    assert patterns[-1][1] == 1
    bias_causal = jnp.where(kj <= qi, 0.0, NEG_BIG).astype(F32)

    bodies = []
    for pi, (window, dil) in enumerate(patterns):
        span = window // dil
        nblk = t_len // (dil * blk)
        bias_band = jnp.where((qi + blk - kj >= 0) & (qi + blk - kj <= span), 0.0,
                              NEG_BIG).astype(F32)
        last = pi == len(patterns) - 1

        def body(it, pi=pi, dil=dil, nblk=nblk, bias_band=bias_band, last=last):
            rows, q, kw, vw, biases = [], [], [], [], []
            if nblk == 2 and nb % 2 == 0:
                for jr in range(nb // 2):
                    res = it * (nb // 2) + jr
                    seq = pl.ds(res, 2 * blk, stride=dil)
                    q_seq = q_ref[0, seq, :]
                    k_seq = k_ref[0, seq, :].astype(BF16)
                    v_seq = v_ref[0, seq, :].astype(BF16)
                    for n in range(2):
                        rows.append(pl.ds(res + dil * blk * n, blk, stride=dil))
                        q.append(q_seq[n * blk:(n + 1) * blk])
                        kw.append(k_seq)
                        vw.append(v_seq)
                        biases.append(bias_band if n else bias_causal)
            else:
                for j in range(nb):
                    res = (it * nb + j) // nblk
                    n = (it * nb + j) % nblk
                    maybe_first = j == 0 or nblk % nb != 0
                    n_prev = jnp.maximum(n - 1, 0) if maybe_first else n - 1
                    q0 = res + dil * blk * n
                    k0 = res + dil * blk * n_prev
                    if dil == 1:
                        rows.append(pl.ds(pl.multiple_of(q0, blk), blk))
                        krows = pl.ds(pl.multiple_of(k0, blk), 2 * blk)
                        kw.append(kb_ref[krows, :])
                        vw.append(vb_ref[krows, :])
                    else:
                        rows.append(pl.ds(q0, blk, stride=dil))
                        krows = pl.ds(k0, 2 * blk, stride=dil)
                        kw.append(k_ref[0, krows, :].astype(BF16))
                        vw.append(v_ref[0, krows, :].astype(BF16))
                    q.append(q_ref[0, rows[j], :])
                    biases.append(jnp.where(n == 0, bias_causal, bias_band) if maybe_first
                                  else bias_band)
            s = [[_dot_nt(jnp.where(h, q[j], 0.0).astype(BF16), kw[j]) + biases[j] for h in heads]
                 for j in range(nb)]
            m = [[jnp.max(x, axis=1, keepdims=True) for x in sj] for sj in s]
            p = [[jnp.exp2(s[j][h] - m[j][h]) for h in range(2)] for j in range(nb)]
            l = [[jnp.sum(x, axis=1, keepdims=True) for x in pj] for pj in p]
            pv = [[_dot(p[j][h].astype(BF16), vw[j]) for h in range(2)] for j in range(nb)]
            for j in range(nb):
                acc = jnp.where(is_h0, pv[j][0], pv[j][1])
                mj = jnp.where(is_h0, m[j][0], m[j][1])
                lj = jnp.where(is_h0, l[j][0], l[j][1])
                if last:
                    ms = [mj] + [m_ref[pp, rows[j], :] for pp in range(pi)]
                    ls = [lj] + [l_ref[pp, rows[j], :] for pp in range(pi)]
                    accs = [acc] + [acc_ref[pp, rows[j], :] for pp in range(pi)]
                    m_all = functools.reduce(jnp.maximum, ms)
                    ws = [jnp.exp2(mm - m_all) for mm in ms]
                    num = functools.reduce(jnp.add, [w * a for w, a in zip(ws, accs)])
                    den = functools.reduce(jnp.add, [w * x for w, x in zip(ws, ls)])
                    z = z_ref[0, rows[j], :]
                    o_ref[0, rows[j], :] = (num / den * (z * jax.nn.sigmoid(z))).astype(o_ref.dtype)
                else:
                    acc_ref[pi, rows[j], :] = acc
                    m_ref[pi, rows[j], :] = mj
                    l_ref[pi, rows[j], :] = lj

        bodies.append(body)

    def strided_body(it, carry):
        for body in bodies[:-1]:
            body(it)
        return carry

    def contiguous_body(it, carry):
        bodies[-1](it)
        return carry

    n_it = t_len // (blk * nb)
    lax.fori_loop(0, n_it, strided_body, 0)
    lax.fori_loop(0, n_it, contiguous_body, 0)


def _out_proj_kernel(ya_ref, yb_ref, wa_ref, wb_ref, x_ref, g_ref, o_ref):
    y = x_ref[...] + _dot(ya_ref[...], wa_ref[...]) + _dot(yb_ref[...], wb_ref[...])
    o_ref[...] = y * lax.rsqrt(jnp.mean(y * y, axis=-1, keepdims=True) + NORM_EPS) * g_ref[...]


def _rope_tables(t_len):
    half = ROPE_DIM // 2
    inv = ROPE_THETA ** (-jnp.arange(half, dtype=F32) * 2.0 / ROPE_DIM)
    ang = jnp.arange(t_len, dtype=jnp.int32).astype(F32)[:, None] * inv[None, :]
    cos, sin = jnp.cos(ang), jnp.sin(ang)
    ones = jnp.ones((t_len, HEAD_DIM - ROPE_DIM), F32)
    zeros = jnp.zeros((t_len, HEAD_DIM - ROPE_DIM), F32)
    zh = jnp.zeros((t_len, half), F32)
    tile = lambda a: jnp.tile(a, (1, LANES // HEAD_DIM))
    cos_t = tile(jnp.concatenate([cos, cos, ones], axis=1))
    sa_t = tile(jnp.concatenate([-sin, zh, zeros], axis=1))
    sb_t = tile(jnp.concatenate([zh, sin, zeros], axis=1))
    return cos_t, sa_t, sb_t


def kernel(x, norm_gain, w_in, shift_mix, decay_base, decay_up, iclr_base, iclr_up, key_norm_scale,
           key_iclr_mix, bonus, gn_gain, gn_bias, w_out, final_gain):
    bsz, t_len, d_model = x.shape
    depth = w_in.shape[0]
    c_a = decay_base.shape[1]
    n_hp = c_a // LANES
    shift_w = 3 * c_a + DECAY_RANK + ICLR_RANK
    in_w = w_in.shape[2]
    c_b = (in_w - shift_w - c_a) // 4
    assert c_a % LANES == 0 and c_b == c_a
    assert all(t_len % (2 * dil * ATTN_BLOCK) == 0 for _, dil in DILATION_PATTERNS)
    o_za = shift_w
    o_q = o_za + c_a
    o_k, o_v, o_zb = o_q + c_b, o_q + 2 * c_b, o_q + 3 * c_b
    segs = ((0, c_a), (c_a, 2 * c_a), (2 * c_a, 3 * c_a), (3 * c_a, shift_w), (o_za, o_q),
            (o_q, o_k), (o_k, o_v), (o_v, o_zb), (o_zb, in_w))
    cos_t, sa_t, sb_t = _rope_tables(t_len)
    rows = bsz * t_len
    tm = IN_PROJ_ROWS
    tiles_per_seq = t_len // tm
    row2 = lambda a: a.reshape(1, -1).astype(F32)

    for layer in range(depth):
        p = pl.pallas_call(
            functools.partial(_in_proj_kernel, segs=segs, shift_w=shift_w, rope_lo=o_q, q_hi=o_k,
                              rope_hi=o_v, tiles_per_seq=tiles_per_seq),
            grid=(rows // tm,),
            in_specs=[
                pl.BlockSpec((tm, d_model), lambda i: (i, 0)),
                pl.BlockSpec((1, d_model), lambda i: (0, 0)),
                pl.BlockSpec((d_model, in_w), lambda i: (0, 0)),
                pl.BlockSpec((1, shift_w), lambda i: (0, 0)),
                pl.BlockSpec((tm, LANES), lambda i: (i % tiles_per_seq, 0)),
                pl.BlockSpec((tm, LANES), lambda i: (i % tiles_per_seq, 0)),
                pl.BlockSpec((tm, LANES), lambda i: (i % tiles_per_seq, 0)),
            ],
            out_specs=pl.BlockSpec((tm, in_w), lambda i: (i, 0)),
            out_shape=jax.ShapeDtypeStruct((rows, in_w), F32),
            scratch_shapes=[pltpu.VMEM((SUBLANES, shift_w), F32)],
            compiler_params=pltpu.CompilerParams(dimension_semantics=("arbitrary",),
                                                 vmem_limit_bytes=VMEM_LIMIT),
            name="in_proj",
        )(x.reshape(rows, d_model), row2(norm_gain[layer]), w_in[layer].astype(BF16),
          row2(shift_mix[layer]), cos_t, sa_t, sb_t)
        p = p.reshape(bsz, t_len, in_w)

        zeros_up = jnp.zeros((DECAY_RANK, c_a), F32)
        up_full = jnp.concatenate(
            [jnp.concatenate([decay_up[layer].astype(F32), zeros_up], axis=0).reshape(
                DECAY_RANK + ICLR_RANK, n_hp, 1, LANES),
             jnp.concatenate([zeros_up, iclr_up[layer].astype(F32)], axis=0).reshape(
                 DECAY_RANK + ICLR_RANK, n_hp, 1, LANES)], axis=2).reshape(
                     DECAY_RANK + ICLR_RANK, n_hp * 2 * LANES)
        col = lambda off: (lambda b, h: (b, 0, off // LANES + h))
        par = lambda b, h: (0, h)
        seq_spec = lambda off: pl.BlockSpec((1, t_len, LANES), col(off))
        par_spec = pl.BlockSpec((1, LANES), par)
        ya = pl.pallas_call(
            _rwkv_kernel,
            grid=(bsz, n_hp),
            in_specs=[seq_spec(0), seq_spec(c_a), seq_spec(2 * c_a),
                      pl.BlockSpec((1, t_len, LANES), lambda b, h: (b, 0, 3 * c_a // LANES)),
                      seq_spec(o_za),
                      pl.BlockSpec((DECAY_RANK + ICLR_RANK, 2 * LANES), par),
                      par_spec, par_spec, par_spec, par_spec, par_spec, par_spec, par_spec],
            out_specs=pl.BlockSpec((1, t_len, LANES), lambda b, h: (b, 0, h)),
            out_shape=jax.ShapeDtypeStruct((bsz, t_len, c_a), BF16),
            scratch_shapes=_rwkv_scratch(t_len),
            compiler_params=pltpu.CompilerParams(dimension_semantics=("arbitrary", "arbitrary"),
                                                 vmem_limit_bytes=VMEM_LIMIT),
            name="rwkv",
        )(p, p, p, p, p, up_full, row2(decay_base[layer]), row2(iclr_base[layer]),
          row2(key_norm_scale[layer]), row2(key_iclr_mix[layer]), row2(bonus[layer]),
          row2(gn_gain[layer]), row2(gn_bias[layer]))

        yb = pl.pallas_call(
            _attn_kernel,
            grid=(bsz, c_b // LANES),
            in_specs=[seq_spec(o_q), seq_spec(o_k), seq_spec(o_v), seq_spec(o_zb)],
            out_specs=pl.BlockSpec((1, t_len, LANES), lambda b, h: (b, 0, h)),
            out_shape=jax.ShapeDtypeStruct((bsz, t_len, c_b), BF16),
            scratch_shapes=[pltpu.VMEM((t_len, LANES), BF16),
                            pltpu.VMEM((t_len, LANES), BF16),
                            pltpu.VMEM((len(DILATION_PATTERNS) - 1, t_len, LANES), F32),
                            pltpu.VMEM((len(DILATION_PATTERNS) - 1, t_len, LANES), F32),
                            pltpu.VMEM((len(DILATION_PATTERNS) - 1, t_len, LANES), F32)],
            compiler_params=pltpu.CompilerParams(dimension_semantics=("arbitrary", "arbitrary"),
                                                 vmem_limit_bytes=VMEM_LIMIT),
            name="attention",
        )(p, p, p, p)

        assert depth == 1
        tmo = OUT_PROJ_ROWS
        wo = w_out[layer].astype(BF16)
        x = pl.pallas_call(
            _out_proj_kernel,
            grid=(rows // tmo,),
            in_specs=[pl.BlockSpec((tmo, c_a), lambda i: (i, 0)),
                      pl.BlockSpec((tmo, c_b), lambda i: (i, 0)),
                      pl.BlockSpec((c_a, d_model), lambda i: (0, 0)),
                      pl.BlockSpec((c_b, d_model), lambda i: (0, 0)),
                      pl.BlockSpec((tmo, d_model), lambda i: (i, 0)),
                      pl.BlockSpec((1, d_model), lambda i: (0, 0))],
            out_specs=pl.BlockSpec((tmo, d_model), lambda i: (i, 0)),
            out_shape=jax.ShapeDtypeStruct((rows, d_model), F32),
            compiler_params=pltpu.CompilerParams(dimension_semantics=("arbitrary",),
                                                 vmem_limit_bytes=VMEM_LIMIT),
            name="out_proj",
        )(ya.reshape(rows, c_a), yb.reshape(rows, c_b), wo[:c_a], wo[c_a:],
          x.reshape(rows, d_model), row2(final_gain)).reshape(bsz, t_len, d_model)
    return x
```

```python
import functools

import jax
import jax.numpy as jnp
from jax import lax
from jax.experimental import pallas as pl
from jax.experimental.pallas import tpu as pltpu

HEAD_DIM = 64
LANES = 128
SUBLANES = 8
DECAY_RANK = 64
ICLR_RANK = 64
DILATION_PATTERNS = ((128, 1), (512, 4), (2048, 16))
ATTN_BLOCK = 128
ATTN_GROUP = 4
ATTN_UNROLL = 4
ROPE_THETA = 500000.0
ROPE_DIM = HEAD_DIM // 4
NORM_EPS = 1e-6
GN_EPS = 64e-5
CHUNK = 64
SUB = 16
PACK = LANES // (2 * SUB)
RWKV_GROUP = 8
PREP_STEPS = (2, 4, 6)
IN_PROJ_ROWS = 512
OUT_PROJ_ROWS = 1024
DECAY_SCALE = 0.6065306597126334
LOG2_E = 1.4426950408889634
ATTN_Q_SCALE = HEAD_DIM ** -0.5 * LOG2_E
NEG_BIG = -1e30
VMEM_LIMIT = 56 * 1024 * 1024

F32 = jnp.float32
BF16 = jnp.bfloat16


def _dot(a, b, **kw):
    return jnp.dot(a, b, preferred_element_type=F32, **kw)


def _dot_nt(a, b):
    return lax.dot_general(a, b, (((1,), (1,)), ((), ())), preferred_element_type=F32)


def _split2(x):
    hi = x.astype(BF16)
    return hi, (x - hi.astype(F32)).astype(BF16)


def _dot_tn(a, b):
    return lax.dot_general(a, b, (((0,), (0,)), ((), ())), preferred_element_type=F32)


def _in_proj_kernel(x_ref, g_ref, w_ref, mix_ref, cos_ref, sa_ref, sb_ref, o_ref, carry_ref,
                    *, segs, shift_w, rope_lo, q_hi, rope_hi, tiles_per_seq):
    i = pl.program_id(0)

    @pl.when(i == 0)
    def _():
        carry_ref[...] = jnp.zeros(carry_ref.shape, F32)

    x = x_ref[...]
    h = x * lax.rsqrt(jnp.mean(x * x, axis=-1, keepdims=True) + NORM_EPS) * g_ref[...]
    hb = h.astype(BF16)
    tm = x.shape[0]
    first = (i % tiles_per_seq) == 0
    row0 = lax.broadcasted_iota(jnp.int32, (tm, 1), 0) == 0
    for lo, hi in segs:
        p = _dot(hb, w_ref[:, lo:hi])
        if hi <= shift_w:
            old = jnp.where(first, 0.0, carry_ref[0:1, lo:hi])
            carry_ref[0:1, lo:hi] = p[tm - 1:tm, :]
            prev = jnp.where(row0, old, pltpu.roll(p, 1, axis=0))
            p = p + (prev - p) * mix_ref[:, lo:hi]
            o_ref[:, lo:hi] = p
        elif lo >= rope_lo and hi <= rope_hi:
            if hi <= q_hi:
                p = p * ATTN_Q_SCALE
            for j in range(lo, hi, LANES):
                t = p[:, j - lo:j - lo + LANES]
                t = (t * cos_ref[...] + pltpu.roll(t, LANES - ROPE_DIM // 2, axis=1) * sa_ref[...]
                     + pltpu.roll(t, ROPE_DIM // 2, axis=1) * sb_ref[...])
                o_ref[:, j:j + LANES] = t
        else:
            o_ref[:, lo:hi] = p


def _spread_matrix():
    j = lax.broadcasted_iota(jnp.int32, (LANES, (SUB - 1) * LANES), 0)
    col = lax.broadcasted_iota(jnp.int32, (LANES, (SUB - 1) * LANES), 1)
    sel = j == SUB * ((col % LANES) // SUB) + col // LANES
    return jnp.where(sel, 1.0, 0.0).astype(BF16)


def _fold_matrices():
    r = lax.broadcasted_iota(jnp.int32, (LANES, LANES), 0)
    l = lax.broadcasted_iota(jnp.int32, (LANES, LANES), 1)
    per_chunk = LANES // PACK
    match = ((l % per_chunk) // SUB == r // HEAD_DIM) & (l % SUB == r % SUB)
    fold = [jnp.where(match & (l // per_chunk == c), 1.0, 0.0).astype(BF16) for c in range(PACK)]
    rr = lax.broadcasted_iota(jnp.int32, (LANES, PACK * LANES), 0)
    cc = lax.broadcasted_iota(jnp.int32, (LANES, PACK * LANES), 1)
    src = per_chunk * (cc // LANES) + SUB * ((cc % LANES) // HEAD_DIM) + cc % SUB
    unfold = jnp.where(rr == src, 1.0, 0.0).astype(BF16)
    return fold, unfold


def _substitute(coef):
    rows = coef.shape[0]
    ng = rows // SUBLANES
    gpb = SUB // SUBLANES
    row = lax.broadcasted_iota(jnp.int32, (SUBLANES, LANES), 0)
    col = lax.broadcasted_iota(jnp.int32, (SUBLANES, LANES), 1) % SUB
    xs = [jnp.where(col == (row + g * SUBLANES) % SUB, 1.0, 0.0).astype(F32) for g in range(ng)]
    for ss in range(SUB - 1):
        gs, rs = divmod(ss, SUBLANES)
        for blk in range(ng // gpb):
            xrow = xs[blk * gpb + gs][rs:rs + 1, :]
            for g in range(blk * gpb + (ss + 1) // SUBLANES, (blk + 1) * gpb):
                cf = coef[g * SUBLANES:(g + 1) * SUBLANES, ss * LANES:(ss + 1) * LANES]
                xs[g] = xs[g] - cf * xrow
    return jnp.concatenate(xs, axis=0)


def _rwkv_scratch(t_len):
    n_chunks = t_len // CHUNK
    return [pltpu.VMEM((LANES, (SUB - 1) * LANES), BF16),
            pltpu.VMEM((PACK * LANES, LANES), BF16),
            pltpu.VMEM((LANES, PACK * LANES), BF16),
            pltpu.VMEM((t_len, LANES), BF16),
            pltpu.VMEM((t_len, LANES), F32),
            pltpu.VMEM((t_len, LANES), F32),
            pltpu.VMEM((n_chunks, LANES, LANES), BF16),
            pltpu.VMEM((n_chunks, HEAD_DIM, LANES), F32),
            pltpu.VMEM((n_chunks, SUBLANES, LANES), F32),
            pltpu.VMEM((2, RWKV_GROUP, 2 * CHUNK, LANES), BF16),
            pltpu.VMEM((2, RWKV_GROUP, 4 * CHUNK, LANES), BF16),
            pltpu.VMEM((2, RWKV_GROUP, 2 * CHUNK, LANES), BF16),
            pltpu.VMEM((2, RWKV_GROUP, 2 * CHUNK, LANES), BF16),
            pltpu.VMEM((2, RWKV_GROUP * CHUNK, LANES), F32),
            pltpu.VMEM((2, RWKV_GROUP * CHUNK, LANES), BF16),
            pltpu.VMEM((2, RWKV_GROUP * CHUNK, LANES), BF16),
            pltpu.VMEM((2, RWKV_GROUP * CHUNK, LANES), BF16)]


def _rwkv_kernel(r_ref, k_ref, v_ref, wa_ref, z_ref, up_ref, dbase_ref, abase_ref, kns_ref,
                 kim_ref, bonus_ref, gng_ref, gnb_ref, o_ref,
                 spread_ref, fold_ref, unfold_ref, qt_ref, yloc_ref, bon_ref, mt_ref, nn_ref,
                 pe_ref, lhs_ref, rhs_ref, vbd_ref, kbd_ref, rd_ref, vb_ref, kdl_ref, bdl_ref):
    t_len = r_ref.shape[1]
    c = CHUNK
    grp = RWKV_GROUP
    rows = grp * c
    n_chunks = t_len // c
    lane = lax.broadcasted_iota(jnp.int32, (c, LANES), 1)
    is_h0 = lane < HEAD_DIM
    spread_ref[...] = _spread_matrix()
    fold, unfold = _fold_matrices()
    fold_ref[...] = jnp.concatenate(fold, axis=0)
    unfold_ref[...] = unfold
    assert grp % PACK == 0
    trow = lax.broadcasted_iota(jnp.int32, (c, LANES), 0)
    scol = lane % HEAD_DIM
    tb, sb = trow // SUB, scol // SUB
    strict = trow > scol
    incl = trow >= scol
    in_block = tb == sb
    ltri = (lax.broadcasted_iota(jnp.int32, (c, c), 0)
            >= lax.broadcasted_iota(jnp.int32, (c, c), 1)).astype(BF16)
    sq_r = lax.broadcasted_iota(jnp.int32, (LANES, LANES), 0)
    sq_c = lax.broadcasted_iota(jnp.int32, (LANES, LANES), 1)
    same_head = sq_r // HEAD_DIM == sq_c // HEAD_DIM

    dbase, abase = dbase_ref[...], abase_ref[...]
    kns, kim, bonus = kns_ref[...], kim_ref[...], bonus_ref[...]
    gng, gnb = gng_ref[...], gnb_ref[...]
    up_hi, up_lo = _split2(up_ref[...])
    up_hh = jnp.concatenate([up_hi, up_hi], axis=0)

    def head_sum(x):
        h0 = lax.broadcasted_iota(jnp.int32, x.shape, 1) < HEAD_DIM
        s0 = jnp.sum(jnp.where(h0, x, 0.0), axis=1, keepdims=True)
        s1 = jnp.sum(jnp.where(h0, 0.0, x), axis=1, keepdims=True)
        return jnp.where(h0, s0, s1)

    def block_diag(x):
        zero = jnp.zeros_like(x)
        return jnp.concatenate([jnp.where(is_h0, x, zero), jnp.where(is_h0, zero, x)],
                               axis=0).astype(BF16)

    n = grp
    js = range(n)
    cs = [slice(j * c, (j + 1) * c) for j in js]

    def prep(gi, slot):
        first_chunk = gi * grp
        sl = pl.ds(pl.multiple_of(first_chunk * c, n * c), n * c)
        r = r_ref[0, sl, :]
        k = k_ref[0, sl, :]
        v = v_ref[0, sl, :]
        wa = wa_ref[0, sl, :]
        is_h0_s = lax.broadcasted_iota(jnp.int32, (n * c, LANES), 1) < HEAD_DIM
        x_hi, x_lo = _split2(jnp.where(is_h0_s, jnp.tanh(wa), wa))
        lin = _dot(jnp.concatenate([x_hi, x_lo], axis=1), up_hh) + _dot(x_hi, up_lo)
        yield
        logw = -DECAY_SCALE * jax.nn.sigmoid(dbase + lin[:, :LANES])
        a = jax.nn.sigmoid(abase + lin[:, LANES:])
        kk = k * kns
        kk = kk * lax.rsqrt(jnp.maximum(head_sum(kk * kk), 1e-24))
        kmod = k * (1.0 + (a - 1.0) * kim)
        b = kk * a
        bon_ref[sl, :] = head_sum(r * kmod * bonus) * v

        parts = jnp.concatenate(_split2(logw), axis=1)
        gs = [_dot(ltri, parts[cs[j]]) for j in js]
        yield
        g = jnp.concatenate([x[:, LANES:] + x[:, :LANES] for x in gs], axis=0)
        p_end = [jnp.exp(g[(j + 1) * c - 1:(j + 1) * c, :]) for j in js]
        e_inv = jnp.exp(-g)
        kkd = kk * jnp.exp(g - logw)
        rd = r * jnp.exp(g)
        bi = b * e_inv
        ki = kmod * e_inv
        e_end = e_inv * jnp.concatenate([jnp.broadcast_to(x, (c, LANES)) for x in p_end], axis=0)
        rd_ref[slot] = rd
        vb_ref[slot] = v.astype(BF16)
        kdl_ref[slot] = (kmod * e_end).astype(BF16)
        bdl_ref[slot] = (b * e_end).astype(BF16)
        for j in js:
            lhs_ref[slot, j] = jnp.concatenate([kkd[cs[j]], rd[cs[j]]], axis=0).astype(BF16)
            rhs_ref[slot, j] = jnp.concatenate([block_diag(bi[cs[j]]), block_diag(ki[cs[j]])],
                                               axis=0)
            vbd_ref[slot, j] = block_diag(v[cs[j]])
            kbd_ref[slot, j] = block_diag(kkd[cs[j]])
            pe_ref[first_chunk + j] = jnp.broadcast_to(p_end[j], (SUBLANES, LANES))

    def pass_a(first_chunk, slot):
        aab, aak, arb, ark = [], [], [], []
        for j in js:
            pair = _dot_nt(lhs_ref[slot, j], rhs_ref[slot, j])
            aab.append(jnp.where(strict, pair[:c, :LANES], 0.0))
            aak.append(jnp.where(strict, pair[:c, LANES:], 0.0))
            arb.append(jnp.where(incl, pair[c:, :LANES], 0.0))
            ark.append(jnp.where(incl, pair[c:, LANES:], 0.0))
        yield
        akv = [_dot(jnp.concatenate([aak[j], ark[j]], axis=0).astype(BF16), vbd_ref[slot, j])
               for j in js]
        aakv = [x[:c] for x in akv]
        arkv = [x[c:] for x in akv]
        n_diag = jnp.concatenate(
            [jnp.concatenate([jnp.where(in_block, aab[q * PACK + i], 0.0).astype(BF16)
                              for i in range(PACK)], axis=1)
             for q in range(n // PACK)], axis=0)
        compact = _dot(n_diag, fold_ref[...])
        yield
        coef = _dot(compact.astype(BF16), spread_ref[...])
        t_diag = _substitute(coef)
        yield
        spread_out = _dot(t_diag.astype(BF16), unfold_ref[...])
        tinv = [jnp.where(in_block, spread_out[(j // PACK) * c:(j // PACK + 1) * c,
                                               (j % PACK) * LANES:(j % PACK + 1) * LANES], 0.0)
                for j in js]
        yield
        span = 1
        while span * SUB < c:
            low_mask = (tb // span == sb // span + 1) & ((tb // span) % 2 == 1)
            inner = [_dot(jnp.where(low_mask, aab[j], 0.0).astype(BF16),
                          block_diag(tinv[j])) for j in js]
            yield
            tinv = [tinv[j] - _dot(tinv[j].astype(BF16), block_diag(inner[j])) for j in js]
            yield
            span *= 2
        wu = [_dot(tinv[j].astype(BF16),
                   jnp.concatenate([kbd_ref[slot, j], block_diag(aakv[j])], axis=1))
              for j in js]
        yield
        w = [x[:, :LANES] for x in wu]
        uloc = [x[:, LANES:] for x in wu]
        arb_b = [arb[j].astype(BF16) for j in js]
        bdl_b = [bdl_ref[slot, cs[j], :] for j in js]
        awu = [_dot(arb_b[j], jnp.concatenate([block_diag(w[j]), block_diag(uloc[j])], axis=1))
               for j in js]
        qt = [rd_ref[slot, cs[j], :] - awu[j][:, :LANES] for j in js]
        yloc = [arkv[j] - awu[j][:, LANES:] for j in js]
        wtb = [_dot_tn(w[j].astype(BF16), bdl_b[j]) for j in js]
        nn = [_dot_tn(jnp.concatenate([vb_ref[slot, cs[j], :], (-uloc[j]).astype(BF16)], axis=0),
                      jnp.concatenate([kdl_ref[slot, cs[j], :], bdl_b[j]], axis=0))
              for j in js]
        yield
        for j in js:
            ci = first_chunk + j
            csl = pl.ds(pl.multiple_of(ci * c, c), c)
            qt_ref[csl, :] = qt[j].astype(BF16)
            yloc_ref[csl, :] = yloc[j]
            mt_ref[ci] = jnp.where(same_head, wtb[j], 0.0).astype(BF16)
            nn_ref[ci] = jnp.where(is_h0, nn[j][:HEAD_DIM], nn[j][HEAD_DIM:])

    def chunk_step(ci, state):
        sb = state.astype(BF16)
        sl = pl.ds(pl.multiple_of(ci * c, c), c)
        y = _dot_nt(qt_ref[sl, :], block_diag(sb)) + yloc_ref[sl, :]
        new_state = state * pe_ref[ci, 0:1, :] + nn_ref[ci] - _dot(sb, mt_ref[ci])
        mu = head_sum(y) * (1.0 / HEAD_DIM)
        d = y - mu
        var = head_sum(d * d) * (1.0 / HEAD_DIM)
        yn = d * lax.rsqrt(var + GN_EPS) * gng + gnb + bon_ref[sl, :]
        z = z_ref[0, sl, :]
        o_ref[0, sl, :] = (yn * (z * jax.nn.sigmoid(z))).astype(o_ref.dtype)
        return new_state

    class Chain:
        def __init__(self, gi, state):
            self.gi, self.state, self.done = gi, state, 0

        def tick(self):
            if self.done < grp:
                self.state = chunk_step(self.gi * grp + self.done, self.state)
                self.done += 1

        def drain(self):
            while self.done < grp:
                self.tick()
            return self.state

    n_groups = n_chunks // grp

    def run_group(gi, slot, chain, prepare_next=True):
        nxt = prep(gi + 1, 1 - slot) if prepare_next else iter(())
        for step, _ in enumerate(pass_a(gi * grp, slot)):
            if step in PREP_STEPS:
                next(nxt, None)
            if chain is not None:
                chain.tick()
        for _ in nxt:
            pass
        return chain.drain() if chain is not None else None

    def body(pair, state):
        gi = 2 * pair + 1
        state = run_group(gi, 1, Chain(gi - 1, state))
        return run_group(gi + 1, 0, Chain(gi, state))

    assert n_groups % 2 == 0
    for _ in prep(0, 0):
        pass
    run_group(0, 0, None)
    state = lax.fori_loop(0, (n_groups - 2) // 2, body, jnp.zeros((HEAD_DIM, LANES), F32))
    state = run_group(n_groups - 1, 1, Chain(n_groups - 2, state), prepare_next=False)
    Chain(n_groups - 1, state).drain()


def _attn_kernel(q_ref, k_ref, v_ref, z_ref, o_ref, kb_ref, vb_ref, acc_ref, m_ref, l_ref):
    t_len = q_ref.shape[1]
    blk = ATTN_BLOCK
    kb_ref[...] = k_ref[0].astype(BF16)
    vb_ref[...] = v_ref[0].astype(BF16)

    lane = lax.broadcasted_iota(jnp.int32, (blk, LANES), 1)
    is_h0 = lane < HEAD_DIM
    qi = lax.broadcasted_iota(jnp.int32, (blk, 2 * blk), 0)
    kj = lax.broadcasted_iota(jnp.int32, (blk, 2 * blk), 1)
    nb = ATTN_GROUP
    heads = (is_h0, ~is_h0)
    patterns = tuple(reversed(DILATION_PATTERNS))
    assert patterns[-1][1] == 1
    bias_causal = jnp.where(kj <= qi, 0.0, NEG_BIG).astype(F32)

    bodies = []
    for pi, (window, dil) in enumerate(patterns):
        span = window // dil
        nblk = t_len // (dil * blk)
        bias_band = jnp.where((qi + blk - kj >= 0) & (qi + blk - kj <= span), 0.0,
                              NEG_BIG).astype(F32)
        last = pi == len(patterns) - 1

        def body(it, pi=pi, dil=dil, nblk=nblk, bias_band=bias_band, last=last):
            rows, q, kw, vw, biases = [], [], [], [], []
            if nblk == 2 and nb % 2 == 0:
                for jr in range(nb // 2):
                    res = it * (nb // 2) + jr
                    seq = pl.ds(res, 2 * blk, stride=dil)
                    q_seq = q_ref[0, seq, :]
                    k_seq = k_ref[0, seq, :].astype(BF16)
                    v_seq = v_ref[0, seq, :].astype(BF16)
                    for n in range(2):
                        rows.append(pl.ds(res + dil * blk * n, blk, stride=dil))
                        q.append(q_seq[n * blk:(n + 1) * blk])
                        kw.append(k_seq)
                        vw.append(v_seq)
                        biases.append(bias_band if n else bias_causal)
            else:
                for j in range(nb):
                    res = (it * nb + j) // nblk
                    n = (it * nb + j) % nblk
                    maybe_first = j == 0 or nblk % nb != 0
                    n_prev = jnp.maximum(n - 1, 0) if maybe_first else n - 1
                    q0 = res + dil * blk * n
                    k0 = res + dil * blk * n_prev
                    if dil == 1:
                        rows.append(pl.ds(pl.multiple_of(q0, blk), blk))
                        krows = pl.ds(pl.multiple_of(k0, blk), 2 * blk)
                        kw.append(kb_ref[krows, :])
                        vw.append(vb_ref[krows, :])
                    else:
                        rows.append(pl.ds(q0, blk, stride=dil))
                        krows = pl.ds(k0, 2 * blk, stride=dil)
                        kw.append(k_ref[0, krows, :].astype(BF16))
                        vw.append(v_ref[0, krows, :].astype(BF16))
                    q.append(q_ref[0, rows[j], :])
                    biases.append(jnp.where(n == 0, bias_causal, bias_band) if maybe_first
                                  else bias_band)
            s = [[_dot_nt(jnp.where(h, q[j], 0.0).astype(BF16), kw[j]) + biases[j] for h in heads]
                 for j in range(nb)]
            m = [[jnp.max(x, axis=1, keepdims=True) for x in sj] for sj in s]
            p = [[jnp.exp2(s[j][h] - m[j][h]) for h in range(2)] for j in range(nb)]
            l = [[jnp.sum(x, axis=1, keepdims=True) for x in pj] for pj in p]
            pv = [[_dot(p[j][h].astype(BF16), vw[j]) for h in range(2)] for j in range(nb)]
            for j in range(nb):
                acc = jnp.where(is_h0, pv[j][0], pv[j][1])
                mj = jnp.where(is_h0, m[j][0], m[j][1])
                lj = jnp.where(is_h0, l[j][0], l[j][1])
                if last:
                    ms = [mj] + [m_ref[pp, rows[j], :] for pp in range(pi)]
                    ls = [lj] + [l_ref[pp, rows[j], :] for pp in range(pi)]
                    accs = [acc] + [acc_ref[pp, rows[j], :] for pp in range(pi)]
                    m_all = functools.reduce(jnp.maximum, ms)
                    ws = [jnp.exp2(mm - m_all) for mm in ms]
                    num = functools.reduce(jnp.add, [w * a for w, a in zip(ws, accs)])
                    den = functools.reduce(jnp.add, [w * x for w, x in zip(ws, ls)])
                    z = z_ref[0, rows[j], :]
                    o_ref[0, rows[j], :] = (num / den * (z * jax.nn.sigmoid(z))).astype(o_ref.dtype)
                else:
                    acc_ref[pi, rows[j], :] = acc
                    m_ref[pi, rows[j], :] = mj
                    l_ref[pi, rows[j], :] = lj

        bodies.append(body)

    unroll = ATTN_UNROLL

    def strided_body(trip, carry):
        for u in range(unroll):
            for body in bodies[:-1]:
                body(trip * unroll + u)
        return carry

    def contiguous_body(trip, carry):
        for u in range(unroll):
            bodies[-1](trip * unroll + u)
        return carry

    n_it = t_len // (blk * nb)
    assert n_it % unroll == 0
    lax.fori_loop(0, n_it // unroll, strided_body, 0)
    lax.fori_loop(0, n_it // unroll, contiguous_body, 0)


def _out_proj_kernel(ya_ref, yb_ref, wa_ref, wb_ref, x_ref, g_ref, o_ref):
    y = x_ref[...] + _dot(ya_ref[...], wa_ref[...]) + _dot(yb_ref[...], wb_ref[...])
    o_ref[...] = y * lax.rsqrt(jnp.mean(y * y, axis=-1, keepdims=True) + NORM_EPS) * g_ref[...]


def _rope_tables(t_len):
    half = ROPE_DIM // 2
    inv = ROPE_THETA ** (-jnp.arange(half, dtype=F32) * 2.0 / ROPE_DIM)
    ang = jnp.arange(t_len, dtype=jnp.int32).astype(F32)[:, None] * inv[None, :]
    cos, sin = jnp.cos(ang), jnp.sin(ang)
    ones = jnp.ones((t_len, HEAD_DIM - ROPE_DIM), F32)
    zeros = jnp.zeros((t_len, HEAD_DIM - ROPE_DIM), F32)
    zh = jnp.zeros((t_len, half), F32)
    tile = lambda a: jnp.tile(a, (1, LANES // HEAD_DIM))
    cos_t = tile(jnp.concatenate([cos, cos, ones], axis=1))
    sa_t = tile(jnp.concatenate([-sin, zh, zeros], axis=1))
    sb_t = tile(jnp.concatenate([zh, sin, zeros], axis=1))
    return cos_t, sa_t, sb_t


def kernel(x, norm_gain, w_in, shift_mix, decay_base, decay_up, iclr_base, iclr_up, key_norm_scale,
           key_iclr_mix, bonus, gn_gain, gn_bias, w_out, final_gain):
    bsz, t_len, d_model = x.shape
    depth = w_in.shape[0]
    c_a = decay_base.shape[1]
    n_hp = c_a // LANES
    shift_w = 3 * c_a + DECAY_RANK + ICLR_RANK
    in_w = w_in.shape[2]
    c_b = (in_w - shift_w - c_a) // 4
    assert c_a % LANES == 0 and c_b == c_a
    assert all(t_len % (2 * dil * ATTN_BLOCK) == 0 for _, dil in DILATION_PATTERNS)
    o_za = shift_w
    o_q = o_za + c_a
    o_k, o_v, o_zb = o_q + c_b, o_q + 2 * c_b, o_q + 3 * c_b
    segs = ((0, c_a), (c_a, 2 * c_a), (2 * c_a, 3 * c_a), (3 * c_a, shift_w), (o_za, o_q),
            (o_q, o_k), (o_k, o_v), (o_v, o_zb), (o_zb, in_w))
    cos_t, sa_t, sb_t = _rope_tables(t_len)
    rows = bsz * t_len
    tm = IN_PROJ_ROWS
    tiles_per_seq = t_len // tm
    row2 = lambda a: a.reshape(1, -1).astype(F32)

    for layer in range(depth):
        p = pl.pallas_call(
            functools.partial(_in_proj_kernel, segs=segs, shift_w=shift_w, rope_lo=o_q, q_hi=o_k,
                              rope_hi=o_v, tiles_per_seq=tiles_per_seq),
            grid=(rows // tm,),
            in_specs=[
                pl.BlockSpec((tm, d_model), lambda i: (i, 0)),
                pl.BlockSpec((1, d_model), lambda i: (0, 0)),
                pl.BlockSpec((d_model, in_w), lambda i: (0, 0)),
                pl.BlockSpec((1, shift_w), lambda i: (0, 0)),
                pl.BlockSpec((tm, LANES), lambda i: (i % tiles_per_seq, 0)),
                pl.BlockSpec((tm, LANES), lambda i: (i % tiles_per_seq, 0)),
                pl.BlockSpec((tm, LANES), lambda i: (i % tiles_per_seq, 0)),
            ],
            out_specs=pl.BlockSpec((tm, in_w), lambda i: (i, 0)),
            out_shape=jax.ShapeDtypeStruct((rows, in_w), F32),
            scratch_shapes=[pltpu.VMEM((SUBLANES, shift_w), F32)],
            compiler_params=pltpu.CompilerParams(dimension_semantics=("arbitrary",),
                                                 vmem_limit_bytes=VMEM_LIMIT),
            name="in_proj",
        )(x.reshape(rows, d_model), row2(norm_gain[layer]), w_in[layer].astype(BF16),
          row2(shift_mix[layer]), cos_t, sa_t, sb_t)
        p = p.reshape(bsz, t_len, in_w)

        zeros_up = jnp.zeros((DECAY_RANK, c_a), F32)
        up_full = jnp.concatenate(
            [jnp.concatenate([decay_up[layer].astype(F32), zeros_up], axis=0).reshape(
                DECAY_RANK + ICLR_RANK, n_hp, 1, LANES),
             jnp.concatenate([zeros_up, iclr_up[layer].astype(F32)], axis=0).reshape(
                 DECAY_RANK + ICLR_RANK, n_hp, 1, LANES)], axis=2).reshape(
                     DECAY_RANK + ICLR_RANK, n_hp * 2 * LANES)
        col = lambda off: (lambda b, h: (b, 0, off // LANES + h))
        par = lambda b, h: (0, h)
        seq_spec = lambda off: pl.BlockSpec((1, t_len, LANES), col(off))
        par_spec = pl.BlockSpec((1, LANES), par)
        ya = pl.pallas_call(
            _rwkv_kernel,
            grid=(bsz, n_hp),
            in_specs=[seq_spec(0), seq_spec(c_a), seq_spec(2 * c_a),
                      pl.BlockSpec((1, t_len, LANES), lambda b, h: (b, 0, 3 * c_a // LANES)),
                      seq_spec(o_za),
                      pl.BlockSpec((DECAY_RANK + ICLR_RANK, 2 * LANES), par),
                      par_spec, par_spec, par_spec, par_spec, par_spec, par_spec, par_spec],
            out_specs=pl.BlockSpec((1, t_len, LANES), lambda b, h: (b, 0, h)),
            out_shape=jax.ShapeDtypeStruct((bsz, t_len, c_a), BF16),
            scratch_shapes=_rwkv_scratch(t_len),
            compiler_params=pltpu.CompilerParams(dimension_semantics=("arbitrary", "arbitrary"),
                                                 vmem_limit_bytes=VMEM_LIMIT),
            name="rwkv",
        )(p, p, p, p, p, up_full, row2(decay_base[layer]), row2(iclr_base[layer]),
          row2(key_norm_scale[layer]), row2(key_iclr_mix[layer]), row2(bonus[layer]),
          row2(gn_gain[layer]), row2(gn_bias[layer]))

        yb = pl.pallas_call(
            _attn_kernel,
            grid=(bsz, c_b // LANES),
            in_specs=[seq_spec(o_q), seq_spec(o_k), seq_spec(o_v), seq_spec(o_zb)],
            out_specs=pl.BlockSpec((1, t_len, LANES), lambda b, h: (b, 0, h)),
            out_shape=jax.ShapeDtypeStruct((bsz, t_len, c_b), BF16),
            scratch_shapes=[pltpu.VMEM((t_len, LANES), BF16),
                            pltpu.VMEM((t_len, LANES), BF16),
                            pltpu.VMEM((len(DILATION_PATTERNS) - 1, t_len, LANES), F32),
                            pltpu.VMEM((len(DILATION_PATTERNS) - 1, t_len, LANES), F32),
                            pltpu.VMEM((len(DILATION_PATTERNS) - 1, t_len, LANES), F32)],
            compiler_params=pltpu.CompilerParams(dimension_semantics=("arbitrary", "arbitrary"),
                                                 vmem_limit_bytes=VMEM_LIMIT),
            name="attention",
        )(p, p, p, p)

        assert depth == 1
        tmo = OUT_PROJ_ROWS
        wo = w_out[layer].astype(BF16)
        x = pl.pallas_call(
            _out_proj_kernel,
            grid=(rows // tmo,),
            in_specs=[pl.BlockSpec((tmo, c_a), lambda i: (i, 0)),
                      pl.BlockSpec((tmo, c_b), lambda i: (i, 0)),
                      pl.BlockSpec((c_a, d_model), lambda i: (0, 0)),
                      pl.BlockSpec((c_b, d_model), lambda i: (0, 0)),
                      pl.BlockSpec((tmo, d_model), lambda i: (i, 0)),
                      pl.BlockSpec((1, d_model), lambda i: (0, 0))],
            out_specs=pl.BlockSpec((tmo, d_model), lambda i: (i, 0)),
            out_shape=jax.ShapeDtypeStruct((rows, d_model), F32),
            compiler_params=pltpu.CompilerParams(dimension_semantics=("arbitrary",),
                                                 vmem_limit_bytes=VMEM_LIMIT),
            name="out_proj",
        )(ya.reshape(rows, c_a), yb.reshape(rows, c_b), wo[:c_a], wo[c_a:],
          x.reshape(rows, d_model), row2(final_gain)).reshape(bsz, t_len, d_model)
    return x
```

```python
import functools

import jax
import jax.numpy as jnp
from jax import lax
from jax.experimental import pallas as pl
from jax.experimental.pallas import tpu as pltpu

HEAD_DIM = 64
LANES = 128
SUBLANES = 8
DECAY_RANK = 64
ICLR_RANK = 64
DILATION_PATTERNS = ((128, 1), (512, 4), (2048, 16))
ATTN_BLOCK = 128
ATTN_GROUP = 4
ATTN_UNROLL = 8
ROPE_THETA = 500000.0
ROPE_DIM = HEAD_DIM // 4
NORM_EPS = 1e-6
GN_EPS = 64e-5
CHUNK = 64
SUB = 16
PACK = LANES // (2 * SUB)
RWKV_GROUP = 8
PREP_STEPS = (2, 4, 6)
IN_PROJ_ROWS = 512
OUT_PROJ_ROWS = 1024
DECAY_SCALE = 0.6065306597126334
LOG2_E = 1.4426950408889634
ATTN_Q_SCALE = HEAD_DIM ** -0.5 * LOG2_E
NEG_BIG = -1e30
VMEM_LIMIT = 56 * 1024 * 1024

F32 = jnp.float32
BF16 = jnp.bfloat16


def _dot(a, b, **kw):
    return jnp.dot(a, b, preferred_element_type=F32, **kw)


def _dot_nt(a, b):
    return lax.dot_general(a, b, (((1,), (1,)), ((), ())), preferred_element_type=F32)


def _split2(x):
    hi = x.astype(BF16)
    return hi, (x - hi.astype(F32)).astype(BF16)


def _dot_tn(a, b):
    return lax.dot_general(a, b, (((0,), (0,)), ((), ())), preferred_element_type=F32)


def _in_proj_kernel(x_ref, g_ref, w_ref, mix_ref, cos_ref, sa_ref, sb_ref, o_ref, carry_ref,
                    *, segs, shift_w, rope_lo, q_hi, rope_hi, tiles_per_seq):
    i = pl.program_id(0)

    @pl.when(i == 0)
    def _():
        carry_ref[...] = jnp.zeros(carry_ref.shape, F32)

    x = x_ref[...]
    h = x * lax.rsqrt(jnp.mean(x * x, axis=-1, keepdims=True) + NORM_EPS) * g_ref[...]
    hb = h.astype(BF16)
    tm = x.shape[0]
    first = (i % tiles_per_seq) == 0
    row0 = lax.broadcasted_iota(jnp.int32, (tm, 1), 0) == 0
    for lo, hi in segs:
        p = _dot(hb, w_ref[:, lo:hi])
        if hi <= shift_w:
            old = jnp.where(first, 0.0, carry_ref[0:1, lo:hi])
            carry_ref[0:1, lo:hi] = p[tm - 1:tm, :]
            prev = jnp.where(row0, old, pltpu.roll(p, 1, axis=0))
            p = p + (prev - p) * mix_ref[:, lo:hi]
            o_ref[:, lo:hi] = p
        elif lo >= rope_lo and hi <= rope_hi:
            if hi <= q_hi:
                p = p * ATTN_Q_SCALE
            for j in range(lo, hi, LANES):
                t = p[:, j - lo:j - lo + LANES]
                t = (t * cos_ref[...] + pltpu.roll(t, LANES - ROPE_DIM // 2, axis=1) * sa_ref[...]
                     + pltpu.roll(t, ROPE_DIM // 2, axis=1) * sb_ref[...])
                o_ref[:, j:j + LANES] = t
        else:
            o_ref[:, lo:hi] = p


def _spread_matrix():
    j = lax.broadcasted_iota(jnp.int32, (LANES, (SUB - 1) * LANES), 0)
    col = lax.broadcasted_iota(jnp.int32, (LANES, (SUB - 1) * LANES), 1)
    sel = j == SUB * ((col % LANES) // SUB) + col // LANES
    return jnp.where(sel, 1.0, 0.0).astype(BF16)


def _fold_matrices():
    r = lax.broadcasted_iota(jnp.int32, (LANES, LANES), 0)
    l = lax.broadcasted_iota(jnp.int32, (LANES, LANES), 1)
    per_chunk = LANES // PACK
    match = ((l % per_chunk) // SUB == r // HEAD_DIM) & (l % SUB == r % SUB)
    fold = [jnp.where(match & (l // per_chunk == c), 1.0, 0.0).astype(BF16) for c in range(PACK)]
    rr = lax.broadcasted_iota(jnp.int32, (LANES, PACK * LANES), 0)
    cc = lax.broadcasted_iota(jnp.int32, (LANES, PACK * LANES), 1)
    src = per_chunk * (cc // LANES) + SUB * ((cc % LANES) // HEAD_DIM) + cc % SUB
    unfold = jnp.where(rr == src, 1.0, 0.0).astype(BF16)
    return fold, unfold


def _substitute(coef):
    rows = coef.shape[0]
    ng = rows // SUBLANES
    gpb = SUB // SUBLANES
    row = lax.broadcasted_iota(jnp.int32, (SUBLANES, LANES), 0)
    col = lax.broadcasted_iota(jnp.int32, (SUBLANES, LANES), 1) % SUB
    xs = [jnp.where(col == (row + g * SUBLANES) % SUB, 1.0, 0.0).astype(F32) for g in range(ng)]
    for ss in range(SUB - 1):
        gs, rs = divmod(ss, SUBLANES)
        for blk in range(ng // gpb):
            xrow = xs[blk * gpb + gs][rs:rs + 1, :]
            for g in range(blk * gpb + (ss + 1) // SUBLANES, (blk + 1) * gpb):
                cf = coef[g * SUBLANES:(g + 1) * SUBLANES, ss * LANES:(ss + 1) * LANES]
                xs[g] = xs[g] - cf * xrow
    return jnp.concatenate(xs, axis=0)


def _rwkv_scratch(t_len):
    n_chunks = t_len // CHUNK
    return [pltpu.VMEM((LANES, (SUB - 1) * LANES), BF16),
            pltpu.VMEM((PACK * LANES, LANES), BF16),
            pltpu.VMEM((LANES, PACK * LANES), BF16),
            pltpu.VMEM((t_len, LANES), BF16),
            pltpu.VMEM((t_len, LANES), F32),
            pltpu.VMEM((t_len, LANES), F32),
            pltpu.VMEM((n_chunks, LANES, LANES), BF16),
            pltpu.VMEM((n_chunks, HEAD_DIM, LANES), F32),
            pltpu.VMEM((n_chunks, SUBLANES, LANES), F32),
            pltpu.VMEM((2, RWKV_GROUP, 2 * CHUNK, LANES), BF16),
            pltpu.VMEM((2, RWKV_GROUP, 4 * CHUNK, LANES), BF16),
            pltpu.VMEM((2, RWKV_GROUP, 2 * CHUNK, LANES), BF16),
            pltpu.VMEM((2, RWKV_GROUP, 2 * CHUNK, LANES), BF16),
            pltpu.VMEM((2, RWKV_GROUP * CHUNK, LANES), F32),
            pltpu.VMEM((2, RWKV_GROUP * CHUNK, LANES), BF16),
            pltpu.VMEM((2, RWKV_GROUP * CHUNK, LANES), BF16),
            pltpu.VMEM((2, RWKV_GROUP * CHUNK, LANES), BF16)]


def _rwkv_kernel(r_ref, k_ref, v_ref, wa_ref, z_ref, up_ref, dbase_ref, abase_ref, kns_ref,
                 kim_ref, bonus_ref, gng_ref, gnb_ref, o_ref,
                 spread_ref, fold_ref, unfold_ref, qt_ref, yloc_ref, bon_ref, mt_ref, nn_ref,
                 pe_ref, lhs_ref, rhs_ref, vbd_ref, kbd_ref, rd_ref, vb_ref, kdl_ref, bdl_ref):
    t_len = r_ref.shape[1]
    c = CHUNK
    grp = RWKV_GROUP
    rows = grp * c
    n_chunks = t_len // c
    lane = lax.broadcasted_iota(jnp.int32, (c, LANES), 1)
    is_h0 = lane < HEAD_DIM
    spread_ref[...] = _spread_matrix()
    fold, unfold = _fold_matrices()
    fold_ref[...] = jnp.concatenate(fold, axis=0)
    unfold_ref[...] = unfold
    assert grp % PACK == 0
    trow = lax.broadcasted_iota(jnp.int32, (c, LANES), 0)
    scol = lane % HEAD_DIM
    tb, sb = trow // SUB, scol // SUB
    strict = trow > scol
    incl = trow >= scol
    in_block = tb == sb
    ltri = (lax.broadcasted_iota(jnp.int32, (c, c), 0)
            >= lax.broadcasted_iota(jnp.int32, (c, c), 1)).astype(BF16)
    sq_r = lax.broadcasted_iota(jnp.int32, (LANES, LANES), 0)
    sq_c = lax.broadcasted_iota(jnp.int32, (LANES, LANES), 1)
    same_head = sq_r // HEAD_DIM == sq_c // HEAD_DIM

    dbase, abase = dbase_ref[...], abase_ref[...]
    kns, kim, bonus = kns_ref[...], kim_ref[...], bonus_ref[...]
    gng, gnb = gng_ref[...], gnb_ref[...]
    up_hi, up_lo = _split2(up_ref[...])
    up_hh = jnp.concatenate([up_hi, up_hi], axis=0)

    def head_sum(x):
        h0 = lax.broadcasted_iota(jnp.int32, x.shape, 1) < HEAD_DIM
        s0 = jnp.sum(jnp.where(h0, x, 0.0), axis=1, keepdims=True)
        s1 = jnp.sum(jnp.where(h0, 0.0, x), axis=1, keepdims=True)
        return jnp.where(h0, s0, s1)

    def block_diag(x):
        zero = jnp.zeros_like(x)
        return jnp.concatenate([jnp.where(is_h0, x, zero), jnp.where(is_h0, zero, x)],
                               axis=0).astype(BF16)

    n = grp
    js = range(n)
    cs = [slice(j * c, (j + 1) * c) for j in js]

    def prep(gi, slot):
        first_chunk = gi * grp
        sl = pl.ds(pl.multiple_of(first_chunk * c, n * c), n * c)
        r = r_ref[0, sl, :]
        k = k_ref[0, sl, :]
        v = v_ref[0, sl, :]
        wa = wa_ref[0, sl, :]
        is_h0_s = lax.broadcasted_iota(jnp.int32, (n * c, LANES), 1) < HEAD_DIM
        x_hi, x_lo = _split2(jnp.where(is_h0_s, jnp.tanh(wa), wa))
        lin = _dot(jnp.concatenate([x_hi, x_lo], axis=1), up_hh) + _dot(x_hi, up_lo)
        yield
        logw = -DECAY_SCALE * jax.nn.sigmoid(dbase + lin[:, :LANES])
        a = jax.nn.sigmoid(abase + lin[:, LANES:])
        kk = k * kns
        kk = kk * lax.rsqrt(jnp.maximum(head_sum(kk * kk), 1e-24))
        kmod = k * (1.0 + (a - 1.0) * kim)
        b = kk * a
        bon_ref[sl, :] = head_sum(r * kmod * bonus) * v

        parts = jnp.concatenate(_split2(logw), axis=1)
        gs = [_dot(ltri, parts[cs[j]]) for j in js]
        yield
        g = jnp.concatenate([x[:, LANES:] + x[:, :LANES] for x in gs], axis=0)
        p_end = [jnp.exp(g[(j + 1) * c - 1:(j + 1) * c, :]) for j in js]
        e_inv = jnp.exp(-g)
        kkd = kk * jnp.exp(g - logw)
        rd = r * jnp.exp(g)
        bi = b * e_inv
        ki = kmod * e_inv
        e_end = e_inv * jnp.concatenate([jnp.broadcast_to(x, (c, LANES)) for x in p_end], axis=0)
        rd_ref[slot] = rd
        vb_ref[slot] = v.astype(BF16)
        kdl_ref[slot] = (kmod * e_end).astype(BF16)
        bdl_ref[slot] = (b * e_end).astype(BF16)
        for j in js:
            lhs_ref[slot, j] = jnp.concatenate([kkd[cs[j]], rd[cs[j]]], axis=0).astype(BF16)
            rhs_ref[slot, j] = jnp.concatenate([block_diag(bi[cs[j]]), block_diag(ki[cs[j]])],
                                               axis=0)
            vbd_ref[slot, j] = block_diag(v[cs[j]])
            kbd_ref[slot, j] = block_diag(kkd[cs[j]])
            pe_ref[first_chunk + j] = jnp.broadcast_to(p_end[j], (SUBLANES, LANES))

    def pass_a(first_chunk, slot):
        aab, aak, arb, ark = [], [], [], []
        for j in js:
            pair = _dot_nt(lhs_ref[slot, j], rhs_ref[slot, j])
            aab.append(jnp.where(strict, pair[:c, :LANES], 0.0))
            aak.append(jnp.where(strict, pair[:c, LANES:], 0.0))
            arb.append(jnp.where(incl, pair[c:, :LANES], 0.0))
            ark.append(jnp.where(incl, pair[c:, LANES:], 0.0))
        yield
        akv = [_dot(jnp.concatenate([aak[j], ark[j]], axis=0).astype(BF16), vbd_ref[slot, j])
               for j in js]
        aakv = [x[:c] for x in akv]
        arkv = [x[c:] for x in akv]
        n_diag = jnp.concatenate(
            [jnp.concatenate([jnp.where(in_block, aab[q * PACK + i], 0.0).astype(BF16)
                              for i in range(PACK)], axis=1)
             for q in range(n // PACK)], axis=0)
        compact = _dot(n_diag, fold_ref[...])
        yield
        coef = _dot(compact.astype(BF16), spread_ref[...])
        t_diag = _substitute(coef)
        yield
        spread_out = _dot(t_diag.astype(BF16), unfold_ref[...])
        tinv = [jnp.where(in_block, spread_out[(j // PACK) * c:(j // PACK + 1) * c,
                                               (j % PACK) * LANES:(j % PACK + 1) * LANES], 0.0)
                for j in js]
        yield
        span = 1
        while span * SUB < c:
            low_mask = (tb // span == sb // span + 1) & ((tb // span) % 2 == 1)
            inner = [_dot(jnp.where(low_mask, aab[j], 0.0).astype(BF16),
                          block_diag(tinv[j])) for j in js]
            yield
            tinv = [tinv[j] - _dot(tinv[j].astype(BF16), block_diag(inner[j])) for j in js]
            yield
            span *= 2
        wu = [_dot(tinv[j].astype(BF16),
                   jnp.concatenate([kbd_ref[slot, j], block_diag(aakv[j])], axis=1))
              for j in js]
        yield
        w = [x[:, :LANES] for x in wu]
        uloc = [x[:, LANES:] for x in wu]
        arb_b = [arb[j].astype(BF16) for j in js]
        bdl_b = [bdl_ref[slot, cs[j], :] for j in js]
        awu = [_dot(arb_b[j], jnp.concatenate([block_diag(w[j]), block_diag(uloc[j])], axis=1))
               for j in js]
        qt = [rd_ref[slot, cs[j], :] - awu[j][:, :LANES] for j in js]
        yloc = [arkv[j] - awu[j][:, LANES:] for j in js]
        wtb = [_dot_tn(w[j].astype(BF16), bdl_b[j]) for j in js]
        nn = [_dot_tn(jnp.concatenate([vb_ref[slot, cs[j], :], (-uloc[j]).astype(BF16)], axis=0),
                      jnp.concatenate([kdl_ref[slot, cs[j], :], bdl_b[j]], axis=0))
              for j in js]
        yield
        for j in js:
            ci = first_chunk + j
            csl = pl.ds(pl.multiple_of(ci * c, c), c)
            qt_ref[csl, :] = qt[j].astype(BF16)
            yloc_ref[csl, :] = yloc[j]
            mt_ref[ci] = jnp.where(same_head, wtb[j], 0.0).astype(BF16)
            nn_ref[ci] = jnp.where(is_h0, nn[j][:HEAD_DIM], nn[j][HEAD_DIM:])

    def chunk_step(ci, state):
        sb = state.astype(BF16)
        sl = pl.ds(pl.multiple_of(ci * c, c), c)
        y = _dot_nt(qt_ref[sl, :], block_diag(sb)) + yloc_ref[sl, :]
        new_state = state * pe_ref[ci, 0:1, :] + nn_ref[ci] - _dot(sb, mt_ref[ci])
        mu = head_sum(y) * (1.0 / HEAD_DIM)
        d = y - mu
        var = head_sum(d * d) * (1.0 / HEAD_DIM)
        yn = d * lax.rsqrt(var + GN_EPS) * gng + gnb + bon_ref[sl, :]
        z = z_ref[0, sl, :]
        o_ref[0, sl, :] = (yn * (z * jax.nn.sigmoid(z))).astype(o_ref.dtype)
        return new_state

    class Chain:
        def __init__(self, gi, state):
            self.gi, self.state, self.done = gi, state, 0

        def tick(self):
            if self.done < grp:
                self.state = chunk_step(self.gi * grp + self.done, self.state)
                self.done += 1

        def drain(self):
            while self.done < grp:
                self.tick()
            return self.state

    n_groups = n_chunks // grp

    def run_group(gi, slot, chain, prepare_next=True):
        nxt = prep(gi + 1, 1 - slot) if prepare_next else iter(())
        for step, _ in enumerate(pass_a(gi * grp, slot)):
            if step in PREP_STEPS:
                next(nxt, None)
            if chain is not None:
                chain.tick()
        for _ in nxt:
            pass
        return chain.drain() if chain is not None else None

    def body(pair, state):
        gi = 2 * pair + 1
        state = run_group(gi, 1, Chain(gi - 1, state))
        return run_group(gi + 1, 0, Chain(gi, state))

    assert n_groups % 2 == 0
    for _ in prep(0, 0):
        pass
    run_group(0, 0, None)
    state = lax.fori_loop(0, (n_groups - 2) // 2, body, jnp.zeros((HEAD_DIM, LANES), F32))
    state = run_group(n_groups - 1, 1, Chain(n_groups - 2, state), prepare_next=False)
    Chain(n_groups - 1, state).drain()


def _attn_kernel(q_ref, k_ref, v_ref, z_ref, o_ref, kb_ref, vb_ref, acc_ref, m_ref, l_ref):
    t_len = q_ref.shape[1]
    blk = ATTN_BLOCK
    kb_ref[...] = k_ref[0].astype(BF16)
    vb_ref[...] = v_ref[0].astype(BF16)

    lane = lax.broadcasted_iota(jnp.int32, (blk, LANES), 1)
    is_h0 = lane < HEAD_DIM
    qi = lax.broadcasted_iota(jnp.int32, (blk, 2 * blk), 0)
    kj = lax.broadcasted_iota(jnp.int32, (blk, 2 * blk), 1)
    nb = ATTN_GROUP
    heads = (is_h0, ~is_h0)
    patterns = tuple(reversed(DILATION_PATTERNS))
    assert patterns[-1][1] == 1
    bias_causal = jnp.where(kj <= qi, 0.0, NEG_BIG).astype(F32)

    bodies = []
    for pi, (window, dil) in enumerate(patterns):
        span = window // dil
        nblk = t_len // (dil * blk)
        bias_band = jnp.where((qi + blk - kj >= 0) & (qi + blk - kj <= span), 0.0,
                              NEG_BIG).astype(F32)
        last = pi == len(patterns) - 1

        def body(it, pi=pi, dil=dil, nblk=nblk, bias_band=bias_band, last=last):
            rows, q, kw, vw, biases = [], [], [], [], []
            if nblk == 2 and nb % 2 == 0:
                for jr in range(nb // 2):
                    res = it * (nb // 2) + jr
                    seq = pl.ds(res, 2 * blk, stride=dil)
                    q_seq = q_ref[0, seq, :]
                    k_seq = k_ref[0, seq, :].astype(BF16)
                    v_seq = v_ref[0, seq, :].astype(BF16)
                    for n in range(2):
                        rows.append(pl.ds(res + dil * blk * n, blk, stride=dil))
                        q.append(q_seq[n * blk:(n + 1) * blk])
                        kw.append(k_seq)
                        vw.append(v_seq)
                        biases.append(bias_band if n else bias_causal)
            else:
                for j in range(nb):
                    res = (it * nb + j) // nblk
                    n = (it * nb + j) % nblk
                    maybe_first = j == 0 or nblk % nb != 0
                    n_prev = jnp.maximum(n - 1, 0) if maybe_first else n - 1
                    q0 = res + dil * blk * n
                    k0 = res + dil * blk * n_prev
                    if dil == 1:
                        rows.append(pl.ds(pl.multiple_of(q0, blk), blk))
                        krows = pl.ds(pl.multiple_of(k0, blk), 2 * blk)
                        kw.append(kb_ref[krows, :])
                        vw.append(vb_ref[krows, :])
                    else:
                        rows.append(pl.ds(q0, blk, stride=dil))
                        krows = pl.ds(k0, 2 * blk, stride=dil)
                        kw.append(k_ref[0, krows, :].astype(BF16))
                        vw.append(v_ref[0, krows, :].astype(BF16))
                    q.append(q_ref[0, rows[j], :])
                    biases.append(jnp.where(n == 0, bias_causal, bias_band) if maybe_first
                                  else bias_band)
            s = [[_dot_nt(jnp.where(h, q[j], 0.0).astype(BF16), kw[j]) + biases[j] for h in heads]
                 for j in range(nb)]
            m = [[jnp.max(x, axis=1, keepdims=True) for x in sj] for sj in s]
            p = [[jnp.exp2(s[j][h] - m[j][h]) for h in range(2)] for j in range(nb)]
            l = [[jnp.sum(x, axis=1, keepdims=True) for x in pj] for pj in p]
            pv = [[_dot(p[j][h].astype(BF16), vw[j]) for h in range(2)] for j in range(nb)]
            for j in range(nb):
                acc = jnp.where(is_h0, pv[j][0], pv[j][1])
                mj = jnp.where(is_h0, m[j][0], m[j][1])
                lj = jnp.where(is_h0, l[j][0], l[j][1])
                if last:
                    ms = [mj] + [m_ref[pp, rows[j], :] for pp in range(pi)]
                    ls = [lj] + [l_ref[pp, rows[j], :] for pp in range(pi)]
                    accs = [acc] + [acc_ref[pp, rows[j], :] for pp in range(pi)]
                    m_all = functools.reduce(jnp.maximum, ms)
                    ws = [jnp.exp2(mm - m_all) for mm in ms]
                    num = functools.reduce(jnp.add, [w * a for w, a in zip(ws, accs)])
                    den = functools.reduce(jnp.add, [w * x for w, x in zip(ws, ls)])
                    z = z_ref[0, rows[j], :]
                    o_ref[0, rows[j], :] = (num / den * (z * jax.nn.sigmoid(z))).astype(o_ref.dtype)
                else:
                    acc_ref[pi, rows[j], :] = acc
                    m_ref[pi, rows[j], :] = mj
                    l_ref[pi, rows[j], :] = lj

        bodies.append(body)

    unroll = ATTN_UNROLL

    def strided_body(trip, carry):
        for u in range(unroll):
            for body in bodies[:-1]:
                body(trip * unroll + u)
        return carry

    def contiguous_body(trip, carry):
        for u in range(unroll):
            bodies[-1](trip * unroll + u)
        return carry

    n_it = t_len // (blk * nb)
    assert n_it % unroll == 0
    lax.fori_loop(0, n_it // unroll, strided_body, 0)
    lax.fori_loop(0, n_it // unroll, contiguous_body, 0)


def _out_proj_kernel(ya_ref, yb_ref, wa_ref, wb_ref, x_ref, g_ref, o_ref):
    y = x_ref[...] + _dot(ya_ref[...], wa_ref[...]) + _dot(yb_ref[...], wb_ref[...])
    o_ref[...] = y * lax.rsqrt(jnp.mean(y * y, axis=-1, keepdims=True) + NORM_EPS) * g_ref[...]


def _rope_tables(t_len):
    half = ROPE_DIM // 2
    inv = ROPE_THETA ** (-jnp.arange(half, dtype=F32) * 2.0 / ROPE_DIM)
    ang = jnp.arange(t_len, dtype=jnp.int32).astype(F32)[:, None] * inv[None, :]
    cos, sin = jnp.cos(ang), jnp.sin(ang)
    ones = jnp.ones((t_len, HEAD_DIM - ROPE_DIM), F32)
    zeros = jnp.zeros((t_len, HEAD_DIM - ROPE_DIM), F32)
    zh = jnp.zeros((t_len, half), F32)
    tile = lambda a: jnp.tile(a, (1, LANES // HEAD_DIM))
    cos_t = tile(jnp.concatenate([cos, cos, ones], axis=1))
    sa_t = tile(jnp.concatenate([-sin, zh, zeros], axis=1))
    sb_t = tile(jnp.concatenate([zh, sin, zeros], axis=1))
    return cos_t, sa_t, sb_t


def kernel(x, norm_gain, w_in, shift_mix, decay_base, decay_up, iclr_base, iclr_up, key_norm_scale,
           key_iclr_mix, bonus, gn_gain, gn_bias, w_out, final_gain):
    bsz, t_len, d_model = x.shape
    depth = w_in.shape[0]
    c_a = decay_base.shape[1]
    n_hp = c_a // LANES
    shift_w = 3 * c_a + DECAY_RANK + ICLR_RANK
    in_w = w_in.shape[2]
    c_b = (in_w - shift_w - c_a) // 4
    assert c_a % LANES == 0 and c_b == c_a
    assert all(t_len % (2 * dil * ATTN_BLOCK) == 0 for _, dil in DILATION_PATTERNS)
    o_za = shift_w
    o_q = o_za + c_a
    o_k, o_v, o_zb = o_q + c_b, o_q + 2 * c_b, o_q + 3 * c_b
    segs = ((0, c_a), (c_a, 2 * c_a), (2 * c_a, 3 * c_a), (3 * c_a, shift_w), (o_za, o_q),
            (o_q, o_k), (o_k, o_v), (o_v, o_zb), (o_zb, in_w))
    cos_t, sa_t, sb_t = _rope_tables(t_len)
    rows = bsz * t_len
    tm = IN_PROJ_ROWS
    tiles_per_seq = t_len // tm
    row2 = lambda a: a.reshape(1, -1).astype(F32)

    for layer in range(depth):
        p = pl.pallas_call(
            functools.partial(_in_proj_kernel, segs=segs, shift_w=shift_w, rope_lo=o_q, q_hi=o_k,
                              rope_hi=o_v, tiles_per_seq=tiles_per_seq),
            grid=(rows // tm,),
            in_specs=[
                pl.BlockSpec((tm, d_model), lambda i: (i, 0)),
                pl.BlockSpec((1, d_model), lambda i: (0, 0)),
                pl.BlockSpec((d_model, in_w), lambda i: (0, 0)),
                pl.BlockSpec((1, shift_w), lambda i: (0, 0)),
                pl.BlockSpec((tm, LANES), lambda i: (i % tiles_per_seq, 0)),
                pl.BlockSpec((tm, LANES), lambda i: (i % tiles_per_seq, 0)),
                pl.BlockSpec((tm, LANES), lambda i: (i % tiles_per_seq, 0)),
            ],
            out_specs=pl.BlockSpec((tm, in_w), lambda i: (i, 0)),
            out_shape=jax.ShapeDtypeStruct((rows, in_w), F32),
            scratch_shapes=[pltpu.VMEM((SUBLANES, shift_w), F32)],
            compiler_params=pltpu.CompilerParams(dimension_semantics=("arbitrary",),
                                                 vmem_limit_bytes=VMEM_LIMIT),
            name="in_proj",
        )(x.reshape(rows, d_model), row2(norm_gain[layer]), w_in[layer].astype(BF16),
          row2(shift_mix[layer]), cos_t, sa_t, sb_t)
        p = p.reshape(bsz, t_len, in_w)

        zeros_up = jnp.zeros((DECAY_RANK, c_a), F32)
        up_full = jnp.concatenate(
            [jnp.concatenate([decay_up[layer].astype(F32), zeros_up], axis=0).reshape(
                DECAY_RANK + ICLR_RANK, n_hp, 1, LANES),
             jnp.concatenate([zeros_up, iclr_up[layer].astype(F32)], axis=0).reshape(
                 DECAY_RANK + ICLR_RANK, n_hp, 1, LANES)], axis=2).reshape(
                     DECAY_RANK + ICLR_RANK, n_hp * 2 * LANES)
        col = lambda off: (lambda b, h: (b, 0, off // LANES + h))
        par = lambda b, h: (0, h)
        seq_spec = lambda off: pl.BlockSpec((1, t_len, LANES), col(off))
        par_spec = pl.BlockSpec((1, LANES), par)
        ya = pl.pallas_call(
            _rwkv_kernel,
            grid=(bsz, n_hp),
            in_specs=[seq_spec(0), seq_spec(c_a), seq_spec(2 * c_a),
                      pl.BlockSpec((1, t_len, LANES), lambda b, h: (b, 0, 3 * c_a // LANES)),
                      seq_spec(o_za),
                      pl.BlockSpec((DECAY_RANK + ICLR_RANK, 2 * LANES), par),
                      par_spec, par_spec, par_spec, par_spec, par_spec, par_spec, par_spec],
            out_specs=pl.BlockSpec((1, t_len, LANES), lambda b, h: (b, 0, h)),
            out_shape=jax.ShapeDtypeStruct((bsz, t_len, c_a), BF16),
            scratch_shapes=_rwkv_scratch(t_len),
            compiler_params=pltpu.CompilerParams(dimension_semantics=("arbitrary", "arbitrary"),
                                                 vmem_limit_bytes=VMEM_LIMIT),
            name="rwkv",
        )(p, p, p, p, p, up_full, row2(decay_base[layer]), row2(iclr_base[layer]),
          row2(key_norm_scale[layer]), row2(key_iclr_mix[layer]), row2(bonus[layer]),
          row2(gn_gain[layer]), row2(gn_bias[layer]))

        yb = pl.pallas_call(
            _attn_kernel,
            grid=(bsz, c_b // LANES),
            in_specs=[seq_spec(o_q), seq_spec(o_k), seq_spec(o_v), seq_spec(o_zb)],
            out_specs=pl.BlockSpec((1, t_len, LANES), lambda b, h: (b, 0, h)),
            out_shape=jax.ShapeDtypeStruct((bsz, t_len, c_b), BF16),
            scratch_shapes=[pltpu.VMEM((t_len, LANES), BF16),
                            pltpu.VMEM((t_len, LANES), BF16),
                            pltpu.VMEM((len(DILATION_PATTERNS) - 1, t_len, LANES), F32),
                            pltpu.VMEM((len(DILATION_PATTERNS) - 1, t_len, LANES), F32),
                            pltpu.VMEM((len(DILATION_PATTERNS) - 1, t_len, LANES), F32)],
            compiler_params=pltpu.CompilerParams(dimension_semantics=("arbitrary", "arbitrary"),
                                                 vmem_limit_bytes=VMEM_LIMIT),
            name="attention",
        )(p, p, p, p)

        assert depth == 1
        tmo = OUT_PROJ_ROWS
        wo = w_out[layer].astype(BF16)
        x = pl.pallas_call(
            _out_proj_kernel,
            grid=(rows // tmo,),
            in_specs=[pl.BlockSpec((tmo, c_a), lambda i: (i, 0)),
                      pl.BlockSpec((tmo, c_b), lambda i: (i, 0)),
                      pl.BlockSpec((c_a, d_model), lambda i: (0, 0)),
                      pl.BlockSpec((c_b, d_model), lambda i: (0, 0)),
                      pl.BlockSpec((tmo, d_model), lambda i: (i, 0)),
                      pl.BlockSpec((1, d_model), lambda i: (0, 0))],
            out_specs=pl.BlockSpec((tmo, d_model), lambda i: (i, 0)),
            out_shape=jax.ShapeDtypeStruct((rows, d_model), F32),
            compiler_params=pltpu.CompilerParams(dimension_semantics=("arbitrary",),
                                                 vmem_limit_bytes=VMEM_LIMIT),
            name="out_proj",
        )(ya.reshape(rows, c_a), yb.reshape(rows, c_b), wo[:c_a], wo[c_a:],
          x.reshape(rows, d_model), row2(final_gain)).reshape(bsz, t_len, d_model)
    return x
```

```python
import functools

import jax
import jax.numpy as jnp
from jax import lax
from jax.experimental import pallas as pl
from jax.experimental.pallas import tpu as pltpu

HEAD_DIM = 64
LANES = 128
SUBLANES = 8
DECAY_RANK = 64
ICLR_RANK = 64
DILATION_PATTERNS = ((128, 1), (512, 4), (2048, 16))
ATTN_BLOCK = 128
ATTN_GROUP = 4
ATTN_UNROLL = 8
ROPE_THETA = 500000.0
ROPE_DIM = HEAD_DIM // 4
NORM_EPS = 1e-6
GN_EPS = 64e-5
CHUNK = 64
SUB = 16
PACK = LANES // (2 * SUB)
RWKV_GROUP = 8
PREP_STEPS = (2, 4, 6)
IN_PROJ_ROWS = 1024
OUT_PROJ_ROWS = 1024
DECAY_SCALE = 0.6065306597126334
LOG2_E = 1.4426950408889634
ATTN_Q_SCALE = HEAD_DIM ** -0.5 * LOG2_E
NEG_BIG = -1e30
VMEM_LIMIT = 56 * 1024 * 1024

F32 = jnp.float32
BF16 = jnp.bfloat16


def _dot(a, b, **kw):
    return jnp.dot(a, b, preferred_element_type=F32, **kw)


def _dot_nt(a, b):
    return lax.dot_general(a, b, (((1,), (1,)), ((), ())), preferred_element_type=F32)


def _split2(x):
    hi = x.astype(BF16)
    return hi, (x - hi.astype(F32)).astype(BF16)


def _dot_tn(a, b):
    return lax.dot_general(a, b, (((0,), (0,)), ((), ())), preferred_element_type=F32)


def _in_proj_kernel(x_ref, g_ref, w_ref, mix_ref, cos_ref, sa_ref, sb_ref, o_ref, carry_ref,
                    *, segs, shift_w, rope_lo, q_hi, rope_hi, tiles_per_seq):
    i = pl.program_id(0)

    @pl.when(i == 0)
    def _():
        carry_ref[...] = jnp.zeros(carry_ref.shape, F32)

    x = x_ref[...]
    h = x * lax.rsqrt(jnp.mean(x * x, axis=-1, keepdims=True) + NORM_EPS) * g_ref[...]
    hb = h.astype(BF16)
    tm = x.shape[0]
    first = (i % tiles_per_seq) == 0
    row0 = lax.broadcasted_iota(jnp.int32, (tm, 1), 0) == 0
    for lo, hi in segs:
        p = _dot(hb, w_ref[:, lo:hi])
        if hi <= shift_w:
            old = jnp.where(first, 0.0, carry_ref[0:1, lo:hi])
            carry_ref[0:1, lo:hi] = p[tm - 1:tm, :]
            prev = jnp.where(row0, old, pltpu.roll(p, 1, axis=0))
            p = p + (prev - p) * mix_ref[:, lo:hi]
            o_ref[:, lo:hi] = p
        elif lo >= rope_lo and hi <= rope_hi:
            if hi <= q_hi:
                p = p * ATTN_Q_SCALE
            for j in range(lo, hi, LANES):
                t = p[:, j - lo:j - lo + LANES]
                t = (t * cos_ref[...] + pltpu.roll(t, LANES - ROPE_DIM // 2, axis=1) * sa_ref[...]
                     + pltpu.roll(t, ROPE_DIM // 2, axis=1) * sb_ref[...])
                o_ref[:, j:j + LANES] = t
        else:
            o_ref[:, lo:hi] = p


def _spread_matrix():
    j = lax.broadcasted_iota(jnp.int32, (LANES, (SUB - 1) * LANES), 0)
    col = lax.broadcasted_iota(jnp.int32, (LANES, (SUB - 1) * LANES), 1)
    sel = j == SUB * ((col % LANES) // SUB) + col // LANES
    return jnp.where(sel, 1.0, 0.0).astype(BF16)


def _fold_matrices():
    r = lax.broadcasted_iota(jnp.int32, (LANES, LANES), 0)
    l = lax.broadcasted_iota(jnp.int32, (LANES, LANES), 1)
    per_chunk = LANES // PACK
    match = ((l % per_chunk) // SUB == r // HEAD_DIM) & (l % SUB == r % SUB)
    fold = [jnp.where(match & (l // per_chunk == c), 1.0, 0.0).astype(BF16) for c in range(PACK)]
    rr = lax.broadcasted_iota(jnp.int32, (LANES, PACK * LANES), 0)
    cc = lax.broadcasted_iota(jnp.int32, (LANES, PACK * LANES), 1)
    src = per_chunk * (cc // LANES) + SUB * ((cc % LANES) // HEAD_DIM) + cc % SUB
    unfold = jnp.where(rr == src, 1.0, 0.0).astype(BF16)
    return fold, unfold


def _substitute(coef):
    rows = coef.shape[0]
    ng = rows // SUBLANES
    gpb = SUB // SUBLANES
    row = lax.broadcasted_iota(jnp.int32, (SUBLANES, LANES), 0)
    col = lax.broadcasted_iota(jnp.int32, (SUBLANES, LANES), 1) % SUB
    xs = [jnp.where(col == (row + g * SUBLANES) % SUB, 1.0, 0.0).astype(F32) for g in range(ng)]
    for ss in range(SUB - 1):
        gs, rs = divmod(ss, SUBLANES)
        for blk in range(ng // gpb):
            xrow = xs[blk * gpb + gs][rs:rs + 1, :]
            for g in range(blk * gpb + (ss + 1) // SUBLANES, (blk + 1) * gpb):
                cf = coef[g * SUBLANES:(g + 1) * SUBLANES, ss * LANES:(ss + 1) * LANES]
                xs[g] = xs[g] - cf * xrow
    return jnp.concatenate(xs, axis=0)


def _rwkv_scratch(t_len):
    n_chunks = t_len // CHUNK
    return [pltpu.VMEM((LANES, (SUB - 1) * LANES), BF16),
            pltpu.VMEM((PACK * LANES, LANES), BF16),
            pltpu.VMEM((LANES, PACK * LANES), BF16),
            pltpu.VMEM((t_len, LANES), BF16),
            pltpu.VMEM((t_len, LANES), F32),
            pltpu.VMEM((t_len, LANES), F32),
            pltpu.VMEM((n_chunks, LANES, LANES), BF16),
            pltpu.VMEM((n_chunks, HEAD_DIM, LANES), F32),
            pltpu.VMEM((n_chunks, SUBLANES, LANES), F32),
            pltpu.VMEM((2, RWKV_GROUP, 2 * CHUNK, LANES), BF16),
            pltpu.VMEM((2, RWKV_GROUP, 4 * CHUNK, LANES), BF16),
            pltpu.VMEM((2, RWKV_GROUP, 2 * CHUNK, LANES), BF16),
            pltpu.VMEM((2, RWKV_GROUP, 2 * CHUNK, LANES), BF16),
            pltpu.VMEM((2, RWKV_GROUP * CHUNK, LANES), F32),
            pltpu.VMEM((2, RWKV_GROUP * CHUNK, LANES), BF16),
            pltpu.VMEM((2, RWKV_GROUP * CHUNK, LANES), BF16),
            pltpu.VMEM((2, RWKV_GROUP * CHUNK, LANES), BF16)]


def _rwkv_kernel(r_ref, k_ref, v_ref, wa_ref, z_ref, up_ref, dbase_ref, abase_ref, kns_ref,
                 kim_ref, bonus_ref, gng_ref, gnb_ref, o_ref,
                 spread_ref, fold_ref, unfold_ref, qt_ref, yloc_ref, bon_ref, mt_ref, nn_ref,
                 pe_ref, lhs_ref, rhs_ref, vbd_ref, kbd_ref, rd_ref, vb_ref, kdl_ref, bdl_ref):
    t_len = r_ref.shape[1]
    c = CHUNK
    grp = RWKV_GROUP
    rows = grp * c
    n_chunks = t_len // c
    lane = lax.broadcasted_iota(jnp.int32, (c, LANES), 1)
    is_h0 = lane < HEAD_DIM
    spread_ref[...] = _spread_matrix()
    fold, unfold = _fold_matrices()
    fold_ref[...] = jnp.concatenate(fold, axis=0)
    unfold_ref[...] = unfold
    assert grp % PACK == 0
    trow = lax.broadcasted_iota(jnp.int32, (c, LANES), 0)
    scol = lane % HEAD_DIM
    tb, sb = trow // SUB, scol // SUB
    strict = trow > scol
    incl = trow >= scol
    in_block = tb == sb
    ltri = (lax.broadcasted_iota(jnp.int32, (c, c), 0)
            >= lax.broadcasted_iota(jnp.int32, (c, c), 1)).astype(BF16)
    sq_r = lax.broadcasted_iota(jnp.int32, (LANES, LANES), 0)
    sq_c = lax.broadcasted_iota(jnp.int32, (LANES, LANES), 1)
    same_head = sq_r // HEAD_DIM == sq_c // HEAD_DIM

    dbase, abase = dbase_ref[...], abase_ref[...]
    kns, kim, bonus = kns_ref[...], kim_ref[...], bonus_ref[...]
    gng, gnb = gng_ref[...], gnb_ref[...]
    up_hi, up_lo = _split2(up_ref[...])
    up_hh = jnp.concatenate([up_hi, up_hi], axis=0)

    def head_sum(x):
        h0 = lax.broadcasted_iota(jnp.int32, x.shape, 1) < HEAD_DIM
        s0 = jnp.sum(jnp.where(h0, x, 0.0), axis=1, keepdims=True)
        s1 = jnp.sum(jnp.where(h0, 0.0, x), axis=1, keepdims=True)
        return jnp.where(h0, s0, s1)

    def block_diag(x):
        zero = jnp.zeros_like(x)
        return jnp.concatenate([jnp.where(is_h0, x, zero), jnp.where(is_h0, zero, x)],
                               axis=0).astype(BF16)

    n = grp
    js = range(n)
    cs = [slice(j * c, (j + 1) * c) for j in js]

    def prep(gi, slot):
        first_chunk = gi * grp
        sl = pl.ds(pl.multiple_of(first_chunk * c, n * c), n * c)
        r = r_ref[0, sl, :]
        k = k_ref[0, sl, :]
        v = v_ref[0, sl, :]
        wa = wa_ref[0, sl, :]
        is_h0_s = lax.broadcasted_iota(jnp.int32, (n * c, LANES), 1) < HEAD_DIM
        x_hi, x_lo = _split2(jnp.where(is_h0_s, jnp.tanh(wa), wa))
        lin = _dot(jnp.concatenate([x_hi, x_lo], axis=1), up_hh) + _dot(x_hi, up_lo)
        yield
        logw = -DECAY_SCALE * jax.nn.sigmoid(dbase + lin[:, :LANES])
        a = jax.nn.sigmoid(abase + lin[:, LANES:])
        kk = k * kns
        kk = kk * lax.rsqrt(jnp.maximum(head_sum(kk * kk), 1e-24))
        kmod = k * (1.0 + (a - 1.0) * kim)
        b = kk * a
        bon_ref[sl, :] = head_sum(r * kmod * bonus) * v

        parts = jnp.concatenate(_split2(logw), axis=1)
        gs = [_dot(ltri, parts[cs[j]]) for j in js]
        yield
        g = jnp.concatenate([x[:, LANES:] + x[:, :LANES] for x in gs], axis=0)
        p_end = [jnp.exp(g[(j + 1) * c - 1:(j + 1) * c, :]) for j in js]
        e_inv = jnp.exp(-g)
        kkd = kk * jnp.exp(g - logw)
        rd = r * jnp.exp(g)
        bi = b * e_inv
        ki = kmod * e_inv
        e_end = e_inv * jnp.concatenate([jnp.broadcast_to(x, (c, LANES)) for x in p_end], axis=0)
        rd_ref[slot] = rd
        vb_ref[slot] = v.astype(BF16)
        kdl_ref[slot] = (kmod * e_end).astype(BF16)
        bdl_ref[slot] = (b * e_end).astype(BF16)
        for j in js:
            lhs_ref[slot, j] = jnp.concatenate([kkd[cs[j]], rd[cs[j]]], axis=0).astype(BF16)
            rhs_ref[slot, j] = jnp.concatenate([block_diag(bi[cs[j]]), block_diag(ki[cs[j]])],
                                               axis=0)
            vbd_ref[slot, j] = block_diag(v[cs[j]])
            kbd_ref[slot, j] = block_diag(kkd[cs[j]])
            pe_ref[first_chunk + j] = jnp.broadcast_to(p_end[j], (SUBLANES, LANES))

    def pass_a(first_chunk, slot):
        aab, aak, arb, ark = [], [], [], []
        for j in js:
            pair = _dot_nt(lhs_ref[slot, j], rhs_ref[slot, j])
            aab.append(jnp.where(strict, pair[:c, :LANES], 0.0))
            aak.append(jnp.where(strict, pair[:c, LANES:], 0.0))
            arb.append(jnp.where(incl, pair[c:, :LANES], 0.0))
            ark.append(jnp.where(incl, pair[c:, LANES:], 0.0))
        yield
        akv = [_dot(jnp.concatenate([aak[j], ark[j]], axis=0).astype(BF16), vbd_ref[slot, j])
               for j in js]
        aakv = [x[:c] for x in akv]
        arkv = [x[c:] for x in akv]
        n_diag = jnp.concatenate(
            [jnp.concatenate([jnp.where(in_block, aab[q * PACK + i], 0.0).astype(BF16)
                              for i in range(PACK)], axis=1)
             for q in range(n // PACK)], axis=0)
        compact = _dot(n_diag, fold_ref[...])
        yield
        coef = _dot(compact.astype(BF16), spread_ref[...])
        t_diag = _substitute(coef)
        yield
        spread_out = _dot(t_diag.astype(BF16), unfold_ref[...])
        tinv = [jnp.where(in_block, spread_out[(j // PACK) * c:(j // PACK + 1) * c,
                                               (j % PACK) * LANES:(j % PACK + 1) * LANES], 0.0)
                for j in js]
        yield
        span = 1
        while span * SUB < c:
            low_mask = (tb // span == sb // span + 1) & ((tb // span) % 2 == 1)
            inner = [_dot(jnp.where(low_mask, aab[j], 0.0).astype(BF16),
                          block_diag(tinv[j])) for j in js]
            yield
            tinv = [tinv[j] - _dot(tinv[j].astype(BF16), block_diag(inner[j])) for j in js]
            yield
            span *= 2
        wu = [_dot(tinv[j].astype(BF16),
                   jnp.concatenate([kbd_ref[slot, j], block_diag(aakv[j])], axis=1))
              for j in js]
        yield
        w = [x[:, :LANES] for x in wu]
        uloc = [x[:, LANES:] for x in wu]
        arb_b = [arb[j].astype(BF16) for j in js]
        bdl_b = [bdl_ref[slot, cs[j], :] for j in js]
        awu = [_dot(arb_b[j], jnp.concatenate([block_diag(w[j]), block_diag(uloc[j])], axis=1))
               for j in js]
        qt = [rd_ref[slot, cs[j], :] - awu[j][:, :LANES] for j in js]
        yloc = [arkv[j] - awu[j][:, LANES:] for j in js]
        wtb = [_dot_tn(w[j].astype(BF16), bdl_b[j]) for j in js]
        nn = [_dot_tn(jnp.concatenate([vb_ref[slot, cs[j], :], (-uloc[j]).astype(BF16)], axis=0),
                      jnp.concatenate([kdl_ref[slot, cs[j], :], bdl_b[j]], axis=0))
              for j in js]
        yield
        for j in js:
            ci = first_chunk + j
            csl = pl.ds(pl.multiple_of(ci * c, c), c)
            qt_ref[csl, :] = qt[j].astype(BF16)
            yloc_ref[csl, :] = yloc[j]
            mt_ref[ci] = jnp.where(same_head, wtb[j], 0.0).astype(BF16)
            nn_ref[ci] = jnp.where(is_h0, nn[j][:HEAD_DIM], nn[j][HEAD_DIM:])

    def chunk_step(ci, state):
        sb = state.astype(BF16)
        sl = pl.ds(pl.multiple_of(ci * c, c), c)
        y = _dot_nt(qt_ref[sl, :], block_diag(sb)) + yloc_ref[sl, :]
        new_state = state * pe_ref[ci, 0:1, :] + nn_ref[ci] - _dot(sb, mt_ref[ci])
        mu = head_sum(y) * (1.0 / HEAD_DIM)
        d = y - mu
        var = head_sum(d * d) * (1.0 / HEAD_DIM)
        yn = d * lax.rsqrt(var + GN_EPS) * gng + gnb + bon_ref[sl, :]
        z = z_ref[0, sl, :]
        o_ref[0, sl, :] = (yn * (z * jax.nn.sigmoid(z))).astype(o_ref.dtype)
        return new_state

    class Chain:
        def __init__(self, gi, state):
            self.gi, self.state, self.done = gi, state, 0

        def tick(self):
            if self.done < grp:
                self.state = chunk_step(self.gi * grp + self.done, self.state)
                self.done += 1

        def drain(self):
            while self.done < grp:
                self.tick()
            return self.state

    n_groups = n_chunks // grp

    def run_group(gi, slot, chain, prepare_next=True):
        nxt = prep(gi + 1, 1 - slot) if prepare_next else iter(())
        for step, _ in enumerate(pass_a(gi * grp, slot)):
            if step in PREP_STEPS:
                next(nxt, None)
            if chain is not None:
                chain.tick()
        for _ in nxt:
            pass
        return chain.drain() if chain is not None else None

    def body(pair, state):
        gi = 2 * pair + 1
        state = run_group(gi, 1, Chain(gi - 1, state))
        return run_group(gi + 1, 0, Chain(gi, state))

    assert n_groups % 2 == 0
    for _ in prep(0, 0):
        pass
    run_group(0, 0, None)
    state = lax.fori_loop(0, (n_groups - 2) // 2, body, jnp.zeros((HEAD_DIM, LANES), F32))
    state = run_group(n_groups - 1, 1, Chain(n_groups - 2, state), prepare_next=False)
    Chain(n_groups - 1, state).drain()


def _attn_kernel(q_ref, k_ref, v_ref, z_ref, o_ref, kb_ref, vb_ref, acc_ref, m_ref, l_ref):
    t_len = q_ref.shape[1]
    blk = ATTN_BLOCK
    kb_ref[...] = k_ref[0].astype(BF16)
    vb_ref[...] = v_ref[0].astype(BF16)

    lane = lax.broadcasted_iota(jnp.int32, (blk, LANES), 1)
    is_h0 = lane < HEAD_DIM
    qi = lax.broadcasted_iota(jnp.int32, (blk, 2 * blk), 0)
    kj = lax.broadcasted_iota(jnp.int32, (blk, 2 * blk), 1)
    nb = ATTN_GROUP
    heads = (is_h0, ~is_h0)
    patterns = tuple(reversed(DILATION_PATTERNS))
    assert patterns[-1][1] == 1
    bias_causal = jnp.where(kj <= qi, 0.0, NEG_BIG).astype(F32)

    bodies = []
    for pi, (window, dil) in enumerate(patterns):
        span = window // dil
        nblk = t_len // (dil * blk)
        bias_band = jnp.where((qi + blk - kj >= 0) & (qi + blk - kj <= span), 0.0,
                              NEG_BIG).astype(F32)
        last = pi == len(patterns) - 1

        def body(it, pi=pi, dil=dil, nblk=nblk, bias_band=bias_band, last=last):
            rows, q, kw, vw, biases = [], [], [], [], []
            if nblk == 2 and nb % 2 == 0:
                for jr in range(nb // 2):
                    res = it * (nb // 2) + jr
                    seq = pl.ds(res, 2 * blk, stride=dil)
                    q_seq = q_ref[0, seq, :]
                    k_seq = k_ref[0, seq, :].astype(BF16)
                    v_seq = v_ref[0, seq, :].astype(BF16)
                    for n in range(2):
                        rows.append(pl.ds(res + dil * blk * n, blk, stride=dil))
                        q.append(q_seq[n * blk:(n + 1) * blk])
                        kw.append(k_seq)
                        vw.append(v_seq)
                        biases.append(bias_band if n else bias_causal)
            else:
                for j in range(nb):
                    res = (it * nb + j) // nblk
                    n = (it * nb + j) % nblk
                    maybe_first = j == 0 or nblk % nb != 0
                    n_prev = jnp.maximum(n - 1, 0) if maybe_first else n - 1
                    q0 = res + dil * blk * n
                    k0 = res + dil * blk * n_prev
                    if dil == 1:
                        rows.append(pl.ds(pl.multiple_of(q0, blk), blk))
                        krows = pl.ds(pl.multiple_of(k0, blk), 2 * blk)
                        kw.append(kb_ref[krows, :])
                        vw.append(vb_ref[krows, :])
                    else:
                        rows.append(pl.ds(q0, blk, stride=dil))
                        krows = pl.ds(k0, 2 * blk, stride=dil)
                        kw.append(k_ref[0, krows, :].astype(BF16))
                        vw.append(v_ref[0, krows, :].astype(BF16))
                    q.append(q_ref[0, rows[j], :])
                    biases.append(jnp.where(n == 0, bias_causal, bias_band) if maybe_first
                                  else bias_band)
            s = [[_dot_nt(jnp.where(h, q[j], 0.0).astype(BF16), kw[j]) + biases[j] for h in heads]
                 for j in range(nb)]
            m = [[jnp.max(x, axis=1, keepdims=True) for x in sj] for sj in s]
            p = [[jnp.exp2(s[j][h] - m[j][h]) for h in range(2)] for j in range(nb)]
            l = [[jnp.sum(x, axis=1, keepdims=True) for x in pj] for pj in p]
            pv = [[_dot(p[j][h].astype(BF16), vw[j]) for h in range(2)] for j in range(nb)]
            for j in range(nb):
                acc = jnp.where(is_h0, pv[j][0], pv[j][1])
                mj = jnp.where(is_h0, m[j][0], m[j][1])
                lj = jnp.where(is_h0, l[j][0], l[j][1])
                if last:
                    ms = [mj] + [m_ref[pp, rows[j], :] for pp in range(pi)]
                    ls = [lj] + [l_ref[pp, rows[j], :] for pp in range(pi)]
                    accs = [acc] + [acc_ref[pp, rows[j], :] for pp in range(pi)]
                    m_all = functools.reduce(jnp.maximum, ms)
                    ws = [jnp.exp2(mm - m_all) for mm in ms]
                    num = functools.reduce(jnp.add, [w * a for w, a in zip(ws, accs)])
                    den = functools.reduce(jnp.add, [w * x for w, x in zip(ws, ls)])
                    z = z_ref[0, rows[j], :]
                    o_ref[0, rows[j], :] = (num / den * (z * jax.nn.sigmoid(z))).astype(o_ref.dtype)
                else:
                    acc_ref[pi, rows[j], :] = acc
                    m_ref[pi, rows[j], :] = mj
                    l_ref[pi, rows[j], :] = lj

        bodies.append(body)

    unroll = ATTN_UNROLL

    def strided_body(trip, carry):
        for u in range(unroll):
            for body in bodies[:-1]:
                body(trip * unroll + u)
        return carry

    def contiguous_body(trip, carry):
        for u in range(unroll):
            bodies[-1](trip * unroll + u)
        return carry

    n_it = t_len // (blk * nb)
    assert n_it % unroll == 0
    lax.fori_loop(0, n_it // unroll, strided_body, 0)
    lax.fori_loop(0, n_it // unroll, contiguous_body, 0)


def _out_proj_kernel(ya_ref, yb_ref, wa_ref, wb_ref, x_ref, g_ref, o_ref):
    y = x_ref[...] + _dot(ya_ref[...], wa_ref[...]) + _dot(yb_ref[...], wb_ref[...])
    o_ref[...] = y * lax.rsqrt(jnp.mean(y * y, axis=-1, keepdims=True) + NORM_EPS) * g_ref[...]


def _rope_tables(t_len):
    half = ROPE_DIM // 2
    inv = ROPE_THETA ** (-jnp.arange(half, dtype=F32) * 2.0 / ROPE_DIM)
    ang = jnp.arange(t_len, dtype=jnp.int32).astype(F32)[:, None] * inv[None, :]
    cos, sin = jnp.cos(ang), jnp.sin(ang)
    ones = jnp.ones((t_len, HEAD_DIM - ROPE_DIM), F32)
    zeros = jnp.zeros((t_len, HEAD_DIM - ROPE_DIM), F32)
    zh = jnp.zeros((t_len, half), F32)
    tile = lambda a: jnp.tile(a, (1, LANES // HEAD_DIM))
    cos_t = tile(jnp.concatenate([cos, cos, ones], axis=1))
    sa_t = tile(jnp.concatenate([-sin, zh, zeros], axis=1))
    sb_t = tile(jnp.concatenate([zh, sin, zeros], axis=1))
    return cos_t, sa_t, sb_t


def kernel(x, norm_gain, w_in, shift_mix, decay_base, decay_up, iclr_base, iclr_up, key_norm_scale,
           key_iclr_mix, bonus, gn_gain, gn_bias, w_out, final_gain):
    bsz, t_len, d_model = x.shape
    depth = w_in.shape[0]
    c_a = decay_base.shape[1]
    n_hp = c_a // LANES
    shift_w = 3 * c_a + DECAY_RANK + ICLR_RANK
    in_w = w_in.shape[2]
    c_b = (in_w - shift_w - c_a) // 4
    assert c_a % LANES == 0 and c_b == c_a
    assert all(t_len % (2 * dil * ATTN_BLOCK) == 0 for _, dil in DILATION_PATTERNS)
    o_za = shift_w
    o_q = o_za + c_a
    o_k, o_v, o_zb = o_q + c_b, o_q + 2 * c_b, o_q + 3 * c_b
    segs = ((0, c_a), (c_a, 2 * c_a), (2 * c_a, 3 * c_a), (3 * c_a, shift_w), (o_za, o_q),
            (o_q, o_k), (o_k, o_v), (o_v, o_zb), (o_zb, in_w))
    cos_t, sa_t, sb_t = _rope_tables(t_len)
    rows = bsz * t_len
    tm = IN_PROJ_ROWS
    tiles_per_seq = t_len // tm
    row2 = lambda a: a.reshape(1, -1).astype(F32)

    for layer in range(depth):
        p = pl.pallas_call(
            functools.partial(_in_proj_kernel, segs=segs, shift_w=shift_w, rope_lo=o_q, q_hi=o_k,
                              rope_hi=o_v, tiles_per_seq=tiles_per_seq),
            grid=(rows // tm,),
            in_specs=[
                pl.BlockSpec((tm, d_model), lambda i: (i, 0)),
                pl.BlockSpec((1, d_model), lambda i: (0, 0)),
                pl.BlockSpec((d_model, in_w), lambda i: (0, 0), pipeline_mode=pl.Buffered(1)),
                pl.BlockSpec((1, shift_w), lambda i: (0, 0)),
                pl.BlockSpec((tm, LANES), lambda i: (i % tiles_per_seq, 0)),
                pl.BlockSpec((tm, LANES), lambda i: (i % tiles_per_seq, 0)),
                pl.BlockSpec((tm, LANES), lambda i: (i % tiles_per_seq, 0)),
            ],
            out_specs=pl.BlockSpec((tm, in_w), lambda i: (i, 0)),
            out_shape=jax.ShapeDtypeStruct((rows, in_w), F32),
            scratch_shapes=[pltpu.VMEM((SUBLANES, shift_w), F32)],
            compiler_params=pltpu.CompilerParams(dimension_semantics=("arbitrary",),
                                                 vmem_limit_bytes=VMEM_LIMIT),
            name="in_proj",
        )(x.reshape(rows, d_model), row2(norm_gain[layer]), w_in[layer].astype(BF16),
          row2(shift_mix[layer]), cos_t, sa_t, sb_t)
        p = p.reshape(bsz, t_len, in_w)

        zeros_up = jnp.zeros((DECAY_RANK, c_a), F32)
        up_full = jnp.concatenate(
            [jnp.concatenate([decay_up[layer].astype(F32), zeros_up], axis=0).reshape(
                DECAY_RANK + ICLR_RANK, n_hp, 1, LANES),
             jnp.concatenate([zeros_up, iclr_up[layer].astype(F32)], axis=0).reshape(
                 DECAY_RANK + ICLR_RANK, n_hp, 1, LANES)], axis=2).reshape(
                     DECAY_RANK + ICLR_RANK, n_hp * 2 * LANES)
        col = lambda off: (lambda b, h: (b, 0, off // LANES + h))
        par = lambda b, h: (0, h)
        seq_spec = lambda off: pl.BlockSpec((1, t_len, LANES), col(off))
        par_spec = pl.BlockSpec((1, LANES), par)
        ya = pl.pallas_call(
            _rwkv_kernel,
            grid=(bsz, n_hp),
            in_specs=[seq_spec(0), seq_spec(c_a), seq_spec(2 * c_a),
                      pl.BlockSpec((1, t_len, LANES), lambda b, h: (b, 0, 3 * c_a // LANES)),
                      seq_spec(o_za),
                      pl.BlockSpec((DECAY_RANK + ICLR_RANK, 2 * LANES), par),
                      par_spec, par_spec, par_spec, par_spec, par_spec, par_spec, par_spec],
            out_specs=pl.BlockSpec((1, t_len, LANES), lambda b, h: (b, 0, h)),
            out_shape=jax.ShapeDtypeStruct((bsz, t_len, c_a), BF16),
            scratch_shapes=_rwkv_scratch(t_len),
            compiler_params=pltpu.CompilerParams(dimension_semantics=("arbitrary", "arbitrary"),
                                                 vmem_limit_bytes=VMEM_LIMIT),
            name="rwkv",
        )(p, p, p, p, p, up_full, row2(decay_base[layer]), row2(iclr_base[layer]),
          row2(key_norm_scale[layer]), row2(key_iclr_mix[layer]), row2(bonus[layer]),
          row2(gn_gain[layer]), row2(gn_bias[layer]))

        yb = pl.pallas_call(
            _attn_kernel,
            grid=(bsz, c_b // LANES),
            in_specs=[seq_spec(o_q), seq_spec(o_k), seq_spec(o_v), seq_spec(o_zb)],
            out_specs=pl.BlockSpec((1, t_len, LANES), lambda b, h: (b, 0, h)),
            out_shape=jax.ShapeDtypeStruct((bsz, t_len, c_b), BF16),
            scratch_shapes=[pltpu.VMEM((t_len, LANES), BF16),
                            pltpu.VMEM((t_len, LANES), BF16),
                            pltpu.VMEM((len(DILATION_PATTERNS) - 1, t_len, LANES), F32),
                            pltpu.VMEM((len(DILATION_PATTERNS) - 1, t_len, LANES), F32),
                            pltpu.VMEM((len(DILATION_PATTERNS) - 1, t_len, LANES), F32)],
            compiler_params=pltpu.CompilerParams(dimension_semantics=("arbitrary", "arbitrary"),
                                                 vmem_limit_bytes=VMEM_LIMIT),
            name="attention",
        )(p, p, p, p)

        assert depth == 1
        tmo = OUT_PROJ_ROWS
        wo = w_out[layer].astype(BF16)
        x = pl.pallas_call(
            _out_proj_kernel,
            grid=(rows // tmo,),
            in_specs=[pl.BlockSpec((tmo, c_a), lambda i: (i, 0)),
                      pl.BlockSpec((tmo, c_b), lambda i: (i, 0)),
                      pl.BlockSpec((c_a, d_model), lambda i: (0, 0)),
                      pl.BlockSpec((c_b, d_model), lambda i: (0, 0)),
                      pl.BlockSpec((tmo, d_model), lambda i: (i, 0)),
                      pl.BlockSpec((1, d_model), lambda i: (0, 0))],
            out_specs=pl.BlockSpec((tmo, d_model), lambda i: (i, 0)),
            out_shape=jax.ShapeDtypeStruct((rows, d_model), F32),
            compiler_params=pltpu.CompilerParams(dimension_semantics=("arbitrary",),
                                                 vmem_limit_bytes=VMEM_LIMIT),
            name="out_proj",
        )(ya.reshape(rows, c_a), yb.reshape(rows, c_b), wo[:c_a], wo[c_a:],
          x.reshape(rows, d_model), row2(final_gain)).reshape(bsz, t_len, d_model)
    return x
```

```python
import functools

import jax
import jax.numpy as jnp
from jax import lax
from jax.experimental import pallas as pl
from jax.experimental.pallas import tpu as pltpu

HEAD_DIM = 64
LANES = 128
SUBLANES = 8
DECAY_RANK = 64
ICLR_RANK = 64
DILATION_PATTERNS = ((128, 1), (512, 4), (2048, 16))
ATTN_BLOCK = 128
ATTN_GROUP = 4
ATTN_UNROLL = 8
ROPE_THETA = 500000.0
ROPE_DIM = HEAD_DIM // 4
NORM_EPS = 1e-6
GN_EPS = 64e-5
CHUNK = 64
SUB = 16
PACK = LANES // (2 * SUB)
RWKV_GROUP = 8
PREP_STEPS = (2, 4, 6)
IN_PROJ_ROWS = 1024
OUT_PROJ_ROWS = 2048
DECAY_SCALE = 0.6065306597126334
LOG2_E = 1.4426950408889634
ATTN_Q_SCALE = HEAD_DIM ** -0.5 * LOG2_E
NEG_BIG = -1e30
VMEM_LIMIT = 56 * 1024 * 1024

F32 = jnp.float32
BF16 = jnp.bfloat16


def _dot(a, b, **kw):
    return jnp.dot(a, b, preferred_element_type=F32, **kw)


def _dot_nt(a, b):
    return lax.dot_general(a, b, (((1,), (1,)), ((), ())), preferred_element_type=F32)


def _split2(x):
    hi = x.astype(BF16)
    return hi, (x - hi.astype(F32)).astype(BF16)


def _dot_tn(a, b):
    return lax.dot_general(a, b, (((0,), (0,)), ((), ())), preferred_element_type=F32)


def _in_proj_kernel(x_ref, g_ref, w_ref, mix_ref, cos_ref, sa_ref, sb_ref, o_ref, carry_ref,
                    *, segs, shift_w, rope_lo, q_hi, rope_hi, tiles_per_seq):
    i = pl.program_id(0)

    @pl.when(i == 0)
    def _():
        carry_ref[...] = jnp.zeros(carry_ref.shape, F32)

    x = x_ref[...]
    h = x * lax.rsqrt(jnp.mean(x * x, axis=-1, keepdims=True) + NORM_EPS) * g_ref[...]
    hb = h.astype(BF16)
    tm = x.shape[0]
    first = (i % tiles_per_seq) == 0
    row0 = lax.broadcasted_iota(jnp.int32, (tm, 1), 0) == 0
    for lo, hi in segs:
        p = _dot(hb, w_ref[:, lo:hi])
        if hi <= shift_w:
            old = jnp.where(first, 0.0, carry_ref[0:1, lo:hi])
            carry_ref[0:1, lo:hi] = p[tm - 1:tm, :]
            prev = jnp.where(row0, old, pltpu.roll(p, 1, axis=0))
            p = p + (prev - p) * mix_ref[:, lo:hi]
            o_ref[:, lo:hi] = p
        elif lo >= rope_lo and hi <= rope_hi:
            if hi <= q_hi:
                p = p * ATTN_Q_SCALE
            for j in range(lo, hi, LANES):
                t = p[:, j - lo:j - lo + LANES]
                t = (t * cos_ref[...] + pltpu.roll(t, LANES - ROPE_DIM // 2, axis=1) * sa_ref[...]
                     + pltpu.roll(t, ROPE_DIM // 2, axis=1) * sb_ref[...])
                o_ref[:, j:j + LANES] = t
        else:
            o_ref[:, lo:hi] = p


def _spread_matrix():
    j = lax.broadcasted_iota(jnp.int32, (LANES, (SUB - 1) * LANES), 0)
    col = lax.broadcasted_iota(jnp.int32, (LANES, (SUB - 1) * LANES), 1)
    sel = j == SUB * ((col % LANES) // SUB) + col // LANES
    return jnp.where(sel, 1.0, 0.0).astype(BF16)


def _fold_matrices():
    r = lax.broadcasted_iota(jnp.int32, (LANES, LANES), 0)
    l = lax.broadcasted_iota(jnp.int32, (LANES, LANES), 1)
    per_chunk = LANES // PACK
    match = ((l % per_chunk) // SUB == r // HEAD_DIM) & (l % SUB == r % SUB)
    fold = [jnp.where(match & (l // per_chunk == c), 1.0, 0.0).astype(BF16) for c in range(PACK)]
    rr = lax.broadcasted_iota(jnp.int32, (LANES, PACK * LANES), 0)
    cc = lax.broadcasted_iota(jnp.int32, (LANES, PACK * LANES), 1)
    src = per_chunk * (cc // LANES) + SUB * ((cc % LANES) // HEAD_DIM) + cc % SUB
    unfold = jnp.where(rr == src, 1.0, 0.0).astype(BF16)
    return fold, unfold


def _substitute(coef):
    rows = coef.shape[0]
    ng = rows // SUBLANES
    gpb = SUB // SUBLANES
    row = lax.broadcasted_iota(jnp.int32, (SUBLANES, LANES), 0)
    col = lax.broadcasted_iota(jnp.int32, (SUBLANES, LANES), 1) % SUB
    xs = [jnp.where(col == (row + g * SUBLANES) % SUB, 1.0, 0.0).astype(F32) for g in range(ng)]
    for ss in range(SUB - 1):
        gs, rs = divmod(ss, SUBLANES)
        for blk in range(ng // gpb):
            xrow = xs[blk * gpb + gs][rs:rs + 1, :]
            for g in range(blk * gpb + (ss + 1) // SUBLANES, (blk + 1) * gpb):
                cf = coef[g * SUBLANES:(g + 1) * SUBLANES, ss * LANES:(ss + 1) * LANES]
                xs[g] = xs[g] - cf * xrow
    return jnp.concatenate(xs, axis=0)


def _rwkv_scratch(t_len):
    n_chunks = t_len // CHUNK
    return [pltpu.VMEM((LANES, (SUB - 1) * LANES), BF16),
            pltpu.VMEM((PACK * LANES, LANES), BF16),
            pltpu.VMEM((LANES, PACK * LANES), BF16),
            pltpu.VMEM((t_len, LANES), BF16),
            pltpu.VMEM((t_len, LANES), F32),
            pltpu.VMEM((t_len, LANES), F32),
            pltpu.VMEM((n_chunks, LANES, LANES), BF16),
            pltpu.VMEM((n_chunks, HEAD_DIM, LANES), F32),
            pltpu.VMEM((n_chunks, SUBLANES, LANES), F32),
            pltpu.VMEM((2, RWKV_GROUP, 2 * CHUNK, LANES), BF16),
            pltpu.VMEM((2, RWKV_GROUP, 4 * CHUNK, LANES), BF16),
            pltpu.VMEM((2, RWKV_GROUP, 2 * CHUNK, LANES), BF16),
            pltpu.VMEM((2, RWKV_GROUP, 2 * CHUNK, LANES), BF16),
            pltpu.VMEM((2, RWKV_GROUP * CHUNK, LANES), F32),
            pltpu.VMEM((2, RWKV_GROUP * CHUNK, LANES), BF16),
            pltpu.VMEM((2, RWKV_GROUP * CHUNK, LANES), BF16),
            pltpu.VMEM((2, RWKV_GROUP * CHUNK, LANES), BF16)]


def _rwkv_kernel(r_ref, k_ref, v_ref, wa_ref, z_ref, up_ref, dbase_ref, abase_ref, kns_ref,
                 kim_ref, bonus_ref, gng_ref, gnb_ref, o_ref,
                 spread_ref, fold_ref, unfold_ref, qt_ref, yloc_ref, bon_ref, mt_ref, nn_ref,
                 pe_ref, lhs_ref, rhs_ref, vbd_ref, kbd_ref, rd_ref, vb_ref, kdl_ref, bdl_ref):
    t_len = r_ref.shape[1]
    c = CHUNK
    grp = RWKV_GROUP
    rows = grp * c
    n_chunks = t_len // c
    lane = lax.broadcasted_iota(jnp.int32, (c, LANES), 1)
    is_h0 = lane < HEAD_DIM
    spread_ref[...] = _spread_matrix()
    fold, unfold = _fold_matrices()
    fold_ref[...] = jnp.concatenate(fold, axis=0)
    unfold_ref[...] = unfold
    assert grp % PACK == 0
    trow = lax.broadcasted_iota(jnp.int32, (c, LANES), 0)
    scol = lane % HEAD_DIM
    tb, sb = trow // SUB, scol // SUB
    strict = trow > scol
    incl = trow >= scol
    in_block = tb == sb
    ltri = (lax.broadcasted_iota(jnp.int32, (c, c), 0)
            >= lax.broadcasted_iota(jnp.int32, (c, c), 1)).astype(BF16)
    sq_r = lax.broadcasted_iota(jnp.int32, (LANES, LANES), 0)
    sq_c = lax.broadcasted_iota(jnp.int32, (LANES, LANES), 1)
    same_head = sq_r // HEAD_DIM == sq_c // HEAD_DIM

    dbase, abase = dbase_ref[...], abase_ref[...]
    kns, kim, bonus = kns_ref[...], kim_ref[...], bonus_ref[...]
    gng, gnb = gng_ref[...], gnb_ref[...]
    up_hi, up_lo = _split2(up_ref[...])
    up_hh = jnp.concatenate([up_hi, up_hi], axis=0)

    def head_sum(x):
        h0 = lax.broadcasted_iota(jnp.int32, x.shape, 1) < HEAD_DIM
        s0 = jnp.sum(jnp.where(h0, x, 0.0), axis=1, keepdims=True)
        s1 = jnp.sum(jnp.where(h0, 0.0, x), axis=1, keepdims=True)
        return jnp.where(h0, s0, s1)

    def block_diag(x):
        zero = jnp.zeros_like(x)
        return jnp.concatenate([jnp.where(is_h0, x, zero), jnp.where(is_h0, zero, x)],
                               axis=0).astype(BF16)

    n = grp
    js = range(n)
    cs = [slice(j * c, (j + 1) * c) for j in js]

    def prep(gi, slot):
        first_chunk = gi * grp
        sl = pl.ds(pl.multiple_of(first_chunk * c, n * c), n * c)
        r = r_ref[0, sl, :]
        k = k_ref[0, sl, :]
        v = v_ref[0, sl, :]
        wa = wa_ref[0, sl, :]
        is_h0_s = lax.broadcasted_iota(jnp.int32, (n * c, LANES), 1) < HEAD_DIM
        x_hi, x_lo = _split2(jnp.where(is_h0_s, jnp.tanh(wa), wa))
        lin = _dot(jnp.concatenate([x_hi, x_lo], axis=1), up_hh) + _dot(x_hi, up_lo)
        yield
        logw = -DECAY_SCALE * jax.nn.sigmoid(dbase + lin[:, :LANES])
        a = jax.nn.sigmoid(abase + lin[:, LANES:])
        kk = k * kns
        kk = kk * lax.rsqrt(jnp.maximum(head_sum(kk * kk), 1e-24))
        kmod = k * (1.0 + (a - 1.0) * kim)
        b = kk * a
        bon_ref[sl, :] = head_sum(r * kmod * bonus) * v

        parts = jnp.concatenate(_split2(logw), axis=1)
        gs = [_dot(ltri, parts[cs[j]]) for j in js]
        yield
        g = jnp.concatenate([x[:, LANES:] + x[:, :LANES] for x in gs], axis=0)
        p_end = [jnp.exp(g[(j + 1) * c - 1:(j + 1) * c, :]) for j in js]
        e_inv = jnp.exp(-g)
        kkd = kk * jnp.exp(g - logw)
        rd = r * jnp.exp(g)
        bi = b * e_inv
        ki = kmod * e_inv
        e_end = e_inv * jnp.concatenate([jnp.broadcast_to(x, (c, LANES)) for x in p_end], axis=0)
        rd_ref[slot] = rd
        vb_ref[slot] = v.astype(BF16)
        kdl_ref[slot] = (kmod * e_end).astype(BF16)
        bdl_ref[slot] = (b * e_end).astype(BF16)
        for j in js:
            lhs_ref[slot, j] = jnp.concatenate([kkd[cs[j]], rd[cs[j]]], axis=0).astype(BF16)
            rhs_ref[slot, j] = jnp.concatenate([block_diag(bi[cs[j]]), block_diag(ki[cs[j]])],
                                               axis=0)
            vbd_ref[slot, j] = block_diag(v[cs[j]])
            kbd_ref[slot, j] = block_diag(kkd[cs[j]])
            pe_ref[first_chunk + j] = jnp.broadcast_to(p_end[j], (SUBLANES, LANES))

    def pass_a(first_chunk, slot):
        aab, aak, arb, ark = [], [], [], []
        for j in js:
            pair = _dot_nt(lhs_ref[slot, j], rhs_ref[slot, j])
            aab.append(jnp.where(strict, pair[:c, :LANES], 0.0))
            aak.append(jnp.where(strict, pair[:c, LANES:], 0.0))
            arb.append(jnp.where(incl, pair[c:, :LANES], 0.0))
            ark.append(jnp.where(incl, pair[c:, LANES:], 0.0))
        yield
        akv = [_dot(jnp.concatenate([aak[j], ark[j]], axis=0).astype(BF16), vbd_ref[slot, j])
               for j in js]
        aakv = [x[:c] for x in akv]
        arkv = [x[c:] for x in akv]
        n_diag = jnp.concatenate(
            [jnp.concatenate([jnp.where(in_block, aab[q * PACK + i], 0.0).astype(BF16)
                              for i in range(PACK)], axis=1)
             for q in range(n // PACK)], axis=0)
        compact = _dot(n_diag, fold_ref[...])
        yield
        coef = _dot(compact.astype(BF16), spread_ref[...])
        t_diag = _substitute(coef)
        yield
        spread_out = _dot(t_diag.astype(BF16), unfold_ref[...])
        tinv = [jnp.where(in_block, spread_out[(j // PACK) * c:(j // PACK + 1) * c,
                                               (j % PACK) * LANES:(j % PACK + 1) * LANES], 0.0)
                for j in js]
        yield
        span = 1
        while span * SUB < c:
            low_mask = (tb // span == sb // span + 1) & ((tb // span) % 2 == 1)
            inner = [_dot(jnp.where(low_mask, aab[j], 0.0).astype(BF16),
                          block_diag(tinv[j])) for j in js]
            yield
            tinv = [tinv[j] - _dot(tinv[j].astype(BF16), block_diag(inner[j])) for j in js]
            yield
            span *= 2
        wu = [_dot(tinv[j].astype(BF16),
                   jnp.concatenate([kbd_ref[slot, j], block_diag(aakv[j])], axis=1))
              for j in js]
        yield
        w = [x[:, :LANES] for x in wu]
        uloc = [x[:, LANES:] for x in wu]
        arb_b = [arb[j].astype(BF16) for j in js]
        bdl_b = [bdl_ref[slot, cs[j], :] for j in js]
        awu = [_dot(arb_b[j], jnp.concatenate([block_diag(w[j]), block_diag(uloc[j])], axis=1))
               for j in js]
        qt = [rd_ref[slot, cs[j], :] - awu[j][:, :LANES] for j in js]
        yloc = [arkv[j] - awu[j][:, LANES:] for j in js]
        wtb = [_dot_tn(w[j].astype(BF16), bdl_b[j]) for j in js]
        nn = [_dot_tn(jnp.concatenate([vb_ref[slot, cs[j], :], (-uloc[j]).astype(BF16)], axis=0),
                      jnp.concatenate([kdl_ref[slot, cs[j], :], bdl_b[j]], axis=0))
              for j in js]
        yield
        for j in js:
            ci = first_chunk + j
            csl = pl.ds(pl.multiple_of(ci * c, c), c)
            qt_ref[csl, :] = qt[j].astype(BF16)
            yloc_ref[csl, :] = yloc[j]
            mt_ref[ci] = jnp.where(same_head, wtb[j], 0.0).astype(BF16)
            nn_ref[ci] = jnp.where(is_h0, nn[j][:HEAD_DIM], nn[j][HEAD_DIM:])

    def chunk_step(ci, state):
        sb = state.astype(BF16)
        sl = pl.ds(pl.multiple_of(ci * c, c), c)
        y = _dot_nt(qt_ref[sl, :], block_diag(sb)) + yloc_ref[sl, :]
        new_state = state * pe_ref[ci, 0:1, :] + nn_ref[ci] - _dot(sb, mt_ref[ci])
        mu = head_sum(y) * (1.0 / HEAD_DIM)
        d = y - mu
        var = head_sum(d * d) * (1.0 / HEAD_DIM)
        yn = d * lax.rsqrt(var + GN_EPS) * gng + gnb + bon_ref[sl, :]
        z = z_ref[0, sl, :]
        o_ref[0, sl, :] = (yn * (z * jax.nn.sigmoid(z))).astype(o_ref.dtype)
        return new_state

    class Chain:
        def __init__(self, gi, state):
            self.gi, self.state, self.done = gi, state, 0

        def tick(self):
            if self.done < grp:
                self.state = chunk_step(self.gi * grp + self.done, self.state)
                self.done += 1

        def drain(self):
            while self.done < grp:
                self.tick()
            return self.state

    n_groups = n_chunks // grp

    def run_group(gi, slot, chain, prepare_next=True):
        nxt = prep(gi + 1, 1 - slot) if prepare_next else iter(())
        for step, _ in enumerate(pass_a(gi * grp, slot)):
            if step in PREP_STEPS:
                next(nxt, None)
            if chain is not None:
                chain.tick()
        for _ in nxt:
            pass
        return chain.drain() if chain is not None else None

    def body(pair, state):
        gi = 2 * pair + 1
        state = run_group(gi, 1, Chain(gi - 1, state))
        return run_group(gi + 1, 0, Chain(gi, state))

    assert n_groups % 2 == 0
    for _ in prep(0, 0):
        pass
    run_group(0, 0, None)
    state = lax.fori_loop(0, (n_groups - 2) // 2, body, jnp.zeros((HEAD_DIM, LANES), F32))
    state = run_group(n_groups - 1, 1, Chain(n_groups - 2, state), prepare_next=False)
    Chain(n_groups - 1, state).drain()


def _attn_kernel(q_ref, k_ref, v_ref, z_ref, o_ref, kb_ref, vb_ref, acc_ref, m_ref, l_ref):
    t_len = q_ref.shape[1]
    blk = ATTN_BLOCK
    kb_ref[...] = k_ref[0].astype(BF16)
    vb_ref[...] = v_ref[0].astype(BF16)

    lane = lax.broadcasted_iota(jnp.int32, (blk, LANES), 1)
    is_h0 = lane < HEAD_DIM
    qi = lax.broadcasted_iota(jnp.int32, (blk, 2 * blk), 0)
    kj = lax.broadcasted_iota(jnp.int32, (blk, 2 * blk), 1)
    nb = ATTN_GROUP
    heads = (is_h0, ~is_h0)
    patterns = tuple(reversed(DILATION_PATTERNS))
    assert patterns[-1][1] == 1
    bias_causal = jnp.where(kj <= qi, 0.0, NEG_BIG).astype(F32)

    bodies = []
    for pi, (window, dil) in enumerate(patterns):
        span = window // dil
        nblk = t_len // (dil * blk)
        bias_band = jnp.where((qi + blk - kj >= 0) & (qi + blk - kj <= span), 0.0,
                              NEG_BIG).astype(F32)
        last = pi == len(patterns) - 1

        def body(it, pi=pi, dil=dil, nblk=nblk, bias_band=bias_band, last=last):
            rows, q, kw, vw, biases = [], [], [], [], []
            if nblk == 2 and nb % 2 == 0:
                for jr in range(nb // 2):
                    res = it * (nb // 2) + jr
                    seq = pl.ds(res, 2 * blk, stride=dil)
                    q_seq = q_ref[0, seq, :]
                    k_seq = k_ref[0, seq, :].astype(BF16)
                    v_seq = v_ref[0, seq, :].astype(BF16)
                    for n in range(2):
                        rows.append(pl.ds(res + dil * blk * n, blk, stride=dil))
                        q.append(q_seq[n * blk:(n + 1) * blk])
                        kw.append(k_seq)
                        vw.append(v_seq)
                        biases.append(bias_band if n else bias_causal)
            else:
                for j in range(nb):
                    res = (it * nb + j) // nblk
                    n = (it * nb + j) % nblk
                    maybe_first = j == 0 or nblk % nb != 0
                    n_prev = jnp.maximum(n - 1, 0) if maybe_first else n - 1
                    q0 = res + dil * blk * n
                    k0 = res + dil * blk * n_prev
                    if dil == 1:
                        rows.append(pl.ds(pl.multiple_of(q0, blk), blk))
                        krows = pl.ds(pl.multiple_of(k0, blk), 2 * blk)
                        kw.append(kb_ref[krows, :])
                        vw.append(vb_ref[krows, :])
                    else:
                        rows.append(pl.ds(q0, blk, stride=dil))
                        krows = pl.ds(k0, 2 * blk, stride=dil)
                        kw.append(k_ref[0, krows, :].astype(BF16))
                        vw.append(v_ref[0, krows, :].astype(BF16))
                    q.append(q_ref[0, rows[j], :])
                    biases.append(jnp.where(n == 0, bias_causal, bias_band) if maybe_first
                                  else bias_band)
            s = [[_dot_nt(jnp.where(h, q[j], 0.0).astype(BF16), kw[j]) + biases[j] for h in heads]
                 for j in range(nb)]
            m = [[jnp.max(x, axis=1, keepdims=True) for x in sj] for sj in s]
            p = [[jnp.exp2(s[j][h] - m[j][h]) for h in range(2)] for j in range(nb)]
            l = [[jnp.sum(x, axis=1, keepdims=True) for x in pj] for pj in p]
            pv = [[_dot(p[j][h].astype(BF16), vw[j]) for h in range(2)] for j in range(nb)]
            for j in range(nb):
                acc = jnp.where(is_h0, pv[j][0], pv[j][1])
                mj = jnp.where(is_h0, m[j][0], m[j][1])
                lj = jnp.where(is_h0, l[j][0], l[j][1])
                if last:
                    ms = [mj] + [m_ref[pp, rows[j], :] for pp in range(pi)]
                    ls = [lj] + [l_ref[pp, rows[j], :] for pp in range(pi)]
                    accs = [acc] + [acc_ref[pp, rows[j], :] for pp in range(pi)]
                    m_all = functools.reduce(jnp.maximum, ms)
                    ws = [jnp.exp2(mm - m_all) for mm in ms]
                    num = functools.reduce(jnp.add, [w * a for w, a in zip(ws, accs)])
                    den = functools.reduce(jnp.add, [w * x for w, x in zip(ws, ls)])
                    z = z_ref[0, rows[j], :]
                    o_ref[0, rows[j], :] = (num / den * (z * jax.nn.sigmoid(z))).astype(o_ref.dtype)
                else:
                    acc_ref[pi, rows[j], :] = acc
                    m_ref[pi, rows[j], :] = mj
                    l_ref[pi, rows[j], :] = lj

        bodies.append(body)

    unroll = ATTN_UNROLL

    def strided_body(trip, carry):
        for u in range(unroll):
            for body in bodies[:-1]:
                body(trip * unroll + u)
        return carry

    def contiguous_body(trip, carry):
        for u in range(unroll):
            bodies[-1](trip * unroll + u)
        return carry

    n_it = t_len // (blk * nb)
    assert n_it % unroll == 0
    lax.fori_loop(0, n_it // unroll, strided_body, 0)
    lax.fori_loop(0, n_it // unroll, contiguous_body, 0)


def _out_proj_kernel(ya_ref, yb_ref, wa_ref, wb_ref, x_ref, g_ref, o_ref):
    y = x_ref[...] + _dot(ya_ref[...], wa_ref[...]) + _dot(yb_ref[...], wb_ref[...])
    o_ref[...] = y * lax.rsqrt(jnp.mean(y * y, axis=-1, keepdims=True) + NORM_EPS) * g_ref[...]


def _rope_tables(t_len):
    half = ROPE_DIM // 2
    inv = ROPE_THETA ** (-jnp.arange(half, dtype=F32) * 2.0 / ROPE_DIM)
    ang = jnp.arange(t_len, dtype=jnp.int32).astype(F32)[:, None] * inv[None, :]
    cos, sin = jnp.cos(ang), jnp.sin(ang)
    ones = jnp.ones((t_len, HEAD_DIM - ROPE_DIM), F32)
    zeros = jnp.zeros((t_len, HEAD_DIM - ROPE_DIM), F32)
    zh = jnp.zeros((t_len, half), F32)
    tile = lambda a: jnp.tile(a, (1, LANES // HEAD_DIM))
    cos_t = tile(jnp.concatenate([cos, cos, ones], axis=1))
    sa_t = tile(jnp.concatenate([-sin, zh, zeros], axis=1))
    sb_t = tile(jnp.concatenate([zh, sin, zeros], axis=1))
    return cos_t, sa_t, sb_t


def kernel(x, norm_gain, w_in, shift_mix, decay_base, decay_up, iclr_base, iclr_up, key_norm_scale,
           key_iclr_mix, bonus, gn_gain, gn_bias, w_out, final_gain):
    bsz, t_len, d_model = x.shape
    depth = w_in.shape[0]
    c_a = decay_base.shape[1]
    n_hp = c_a // LANES
    shift_w = 3 * c_a + DECAY_RANK + ICLR_RANK
    in_w = w_in.shape[2]
    c_b = (in_w - shift_w - c_a) // 4
    assert c_a % LANES == 0 and c_b == c_a
    assert all(t_len % (2 * dil * ATTN_BLOCK) == 0 for _, dil in DILATION_PATTERNS)
    o_za = shift_w
    o_q = o_za + c_a
    o_k, o_v, o_zb = o_q + c_b, o_q + 2 * c_b, o_q + 3 * c_b
    segs = ((0, c_a), (c_a, 2 * c_a), (2 * c_a, 3 * c_a), (3 * c_a, shift_w), (o_za, o_q),
            (o_q, o_k), (o_k, o_v), (o_v, o_zb), (o_zb, in_w))
    cos_t, sa_t, sb_t = _rope_tables(t_len)
    rows = bsz * t_len
    tm = IN_PROJ_ROWS
    tiles_per_seq = t_len // tm
    row2 = lambda a: a.reshape(1, -1).astype(F32)

    for layer in range(depth):
        p = pl.pallas_call(
            functools.partial(_in_proj_kernel, segs=segs, shift_w=shift_w, rope_lo=o_q, q_hi=o_k,
                              rope_hi=o_v, tiles_per_seq=tiles_per_seq),
            grid=(rows // tm,),
            in_specs=[
                pl.BlockSpec((tm, d_model), lambda i: (i, 0)),
                pl.BlockSpec((1, d_model), lambda i: (0, 0)),
                pl.BlockSpec((d_model, in_w), lambda i: (0, 0), pipeline_mode=pl.Buffered(1)),
                pl.BlockSpec((1, shift_w), lambda i: (0, 0)),
                pl.BlockSpec((tm, LANES), lambda i: (i % tiles_per_seq, 0)),
                pl.BlockSpec((tm, LANES), lambda i: (i % tiles_per_seq, 0)),
                pl.BlockSpec((tm, LANES), lambda i: (i % tiles_per_seq, 0)),
            ],
            out_specs=pl.BlockSpec((tm, in_w), lambda i: (i, 0)),
            out_shape=jax.ShapeDtypeStruct((rows, in_w), F32),
            scratch_shapes=[pltpu.VMEM((SUBLANES, shift_w), F32)],
            compiler_params=pltpu.CompilerParams(dimension_semantics=("arbitrary",),
                                                 vmem_limit_bytes=VMEM_LIMIT),
            name="in_proj",
        )(x.reshape(rows, d_model), row2(norm_gain[layer]), w_in[layer].astype(BF16),
          row2(shift_mix[layer]), cos_t, sa_t, sb_t)
        p = p.reshape(bsz, t_len, in_w)

        zeros_up = jnp.zeros((DECAY_RANK, c_a), F32)
        up_full = jnp.concatenate(
            [jnp.concatenate([decay_up[layer].astype(F32), zeros_up], axis=0).reshape(
                DECAY_RANK + ICLR_RANK, n_hp, 1, LANES),
             jnp.concatenate([zeros_up, iclr_up[layer].astype(F32)], axis=0).reshape(
                 DECAY_RANK + ICLR_RANK, n_hp, 1, LANES)], axis=2).reshape(
                     DECAY_RANK + ICLR_RANK, n_hp * 2 * LANES)
        col = lambda off: (lambda b, h: (b, 0, off // LANES + h))
        par = lambda b, h: (0, h)
        seq_spec = lambda off: pl.BlockSpec((1, t_len, LANES), col(off))
        par_spec = pl.BlockSpec((1, LANES), par)
        ya = pl.pallas_call(
            _rwkv_kernel,
            grid=(bsz, n_hp),
            in_specs=[seq_spec(0), seq_spec(c_a), seq_spec(2 * c_a),
                      pl.BlockSpec((1, t_len, LANES), lambda b, h: (b, 0, 3 * c_a // LANES)),
                      seq_spec(o_za),
                      pl.BlockSpec((DECAY_RANK + ICLR_RANK, 2 * LANES), par),
                      par_spec, par_spec, par_spec, par_spec, par_spec, par_spec, par_spec],
            out_specs=pl.BlockSpec((1, t_len, LANES), lambda b, h: (b, 0, h)),
            out_shape=jax.ShapeDtypeStruct((bsz, t_len, c_a), BF16),
            scratch_shapes=_rwkv_scratch(t_len),
            compiler_params=pltpu.CompilerParams(dimension_semantics=("arbitrary", "arbitrary"),
                                                 vmem_limit_bytes=VMEM_LIMIT),
            name="rwkv",
        )(p, p, p, p, p, up_full, row2(decay_base[layer]), row2(iclr_base[layer]),
          row2(key_norm_scale[layer]), row2(key_iclr_mix[layer]), row2(bonus[layer]),
          row2(gn_gain[layer]), row2(gn_bias[layer]))

        yb = pl.pallas_call(
            _attn_kernel,
            grid=(bsz, c_b // LANES),
            in_specs=[seq_spec(o_q), seq_spec(o_k), seq_spec(o_v), seq_spec(o_zb)],
            out_specs=pl.BlockSpec((1, t_len, LANES), lambda b, h: (b, 0, h)),
            out_shape=jax.ShapeDtypeStruct((bsz, t_len, c_b), BF16),
            scratch_shapes=[pltpu.VMEM((t_len, LANES), BF16),
                            pltpu.VMEM((t_len, LANES), BF16),
                            pltpu.VMEM((len(DILATION_PATTERNS) - 1, t_len, LANES), F32),
                            pltpu.VMEM((len(DILATION_PATTERNS) - 1, t_len, LANES), F32),
                            pltpu.VMEM((len(DILATION_PATTERNS) - 1, t_len, LANES), F32)],
            compiler_params=pltpu.CompilerParams(dimension_semantics=("arbitrary", "arbitrary"),
                                                 vmem_limit_bytes=VMEM_LIMIT),
            name="attention",
        )(p, p, p, p)

        assert depth == 1
        tmo = OUT_PROJ_ROWS
        wo = w_out[layer].astype(BF16)
        x = pl.pallas_call(
            _out_proj_kernel,
            grid=(rows // tmo,),
            in_specs=[pl.BlockSpec((tmo, c_a), lambda i: (i, 0)),
                      pl.BlockSpec((tmo, c_b), lambda i: (i, 0)),
                      pl.BlockSpec((c_a, d_model), lambda i: (0, 0)),
                      pl.BlockSpec((c_b, d_model), lambda i: (0, 0)),
                      pl.BlockSpec((tmo, d_model), lambda i: (i, 0)),
                      pl.BlockSpec((1, d_model), lambda i: (0, 0))],
            out_specs=pl.BlockSpec((tmo, d_model), lambda i: (i, 0)),
            out_shape=jax.ShapeDtypeStruct((rows, d_model), F32),
            compiler_params=pltpu.CompilerParams(dimension_semantics=("arbitrary",),
                                                 vmem_limit_bytes=VMEM_LIMIT),
            name="out_proj",
        )(ya.reshape(rows, c_a), yb.reshape(rows, c_b), wo[:c_a], wo[c_a:],
          x.reshape(rows, d_model), row2(final_gain)).reshape(bsz, t_len, d_model)
    return x
```

```python
import functools

import jax
import jax.numpy as jnp
from jax import lax
from jax.experimental import pallas as pl
from jax.experimental.pallas import tpu as pltpu

HEAD_DIM = 64
LANES = 128
SUBLANES = 8
DECAY_RANK = 64
ICLR_RANK = 64
DILATION_PATTERNS = ((128, 1), (512, 4), (2048, 16))
ATTN_BLOCK = 128
ATTN_GROUP = 4
ATTN_UNROLL = 8
ROPE_THETA = 500000.0
ROPE_DIM = HEAD_DIM // 4
NORM_EPS = 1e-6
GN_EPS = 64e-5
CHUNK = 64
SUB = 16
PACK = LANES // (2 * SUB)
RWKV_GROUP = 8
PREP_STEPS = (2, 4, 6)
IN_PROJ_ROWS = 1024
OUT_PROJ_ROWS = 2048
DECAY_SCALE = 0.6065306597126334
LOG2_E = 1.4426950408889634
ATTN_Q_SCALE = HEAD_DIM ** -0.5 * LOG2_E
NEG_BIG = -1e30
VMEM_LIMIT = 56 * 1024 * 1024

F32 = jnp.float32
BF16 = jnp.bfloat16


def _dot(a, b, **kw):
    return jnp.dot(a, b, preferred_element_type=F32, **kw)


def _dot_nt(a, b):
    return lax.dot_general(a, b, (((1,), (1,)), ((), ())), preferred_element_type=F32)


def _split2(x):
    hi = x.astype(BF16)
    return hi, (x - hi.astype(F32)).astype(BF16)


def _dot_tn(a, b):
    return lax.dot_general(a, b, (((0,), (0,)), ((), ())), preferred_element_type=F32)


def _in_proj_kernel(x_ref, g_ref, w_ref, mix_ref, cos_ref, sa_ref, sb_ref, o_ref, carry_ref,
                    *, segs, shift_w, rope_lo, q_hi, rope_hi, tiles_per_seq):
    i = pl.program_id(0)

    @pl.when(i == 0)
    def _():
        carry_ref[...] = jnp.zeros(carry_ref.shape, F32)

    x = x_ref[...]
    h = x * lax.rsqrt(jnp.mean(x * x, axis=-1, keepdims=True) + NORM_EPS) * g_ref[...]
    hb = h.astype(BF16)
    tm = x.shape[0]
    first = (i % tiles_per_seq) == 0
    row0 = lax.broadcasted_iota(jnp.int32, (tm, 1), 0) == 0
    for lo, hi in segs:
        p = _dot(hb, w_ref[:, lo:hi])
        if hi <= shift_w:
            old = jnp.where(first, 0.0, carry_ref[0:1, lo:hi])
            carry_ref[0:1, lo:hi] = p[tm - 1:tm, :]
            prev = jnp.where(row0, old, pltpu.roll(p, 1, axis=0))
            p = p + (prev - p) * mix_ref[:, lo:hi]
            o_ref[:, lo:hi] = p
        elif lo >= rope_lo and hi <= rope_hi:
            if hi <= q_hi:
                p = p * ATTN_Q_SCALE
            for j in range(lo, hi, LANES):
                t = p[:, j - lo:j - lo + LANES]
                t = (t * cos_ref[...] + pltpu.roll(t, LANES - ROPE_DIM // 2, axis=1) * sa_ref[...]
                     + pltpu.roll(t, ROPE_DIM // 2, axis=1) * sb_ref[...])
                o_ref[:, j:j + LANES] = t
        else:
            o_ref[:, lo:hi] = p


def _spread_matrix():
    j = lax.broadcasted_iota(jnp.int32, (LANES, (SUB - 1) * LANES), 0)
    col = lax.broadcasted_iota(jnp.int32, (LANES, (SUB - 1) * LANES), 1)
    sel = j == SUB * ((col % LANES) // SUB) + col // LANES
    return jnp.where(sel, 1.0, 0.0).astype(BF16)


def _fold_matrices():
    r = lax.broadcasted_iota(jnp.int32, (LANES, LANES), 0)
    l = lax.broadcasted_iota(jnp.int32, (LANES, LANES), 1)
    per_chunk = LANES // PACK
    match = ((l % per_chunk) // SUB == r // HEAD_DIM) & (l % SUB == r % SUB)
    fold = [jnp.where(match & (l // per_chunk == c), 1.0, 0.0).astype(BF16) for c in range(PACK)]
    rr = lax.broadcasted_iota(jnp.int32, (LANES, PACK * LANES), 0)
    cc = lax.broadcasted_iota(jnp.int32, (LANES, PACK * LANES), 1)
    src = per_chunk * (cc // LANES) + SUB * ((cc % LANES) // HEAD_DIM) + cc % SUB
    unfold = jnp.where(rr == src, 1.0, 0.0).astype(BF16)
    return fold, unfold


def _substitute(coef):
    rows = coef.shape[0]
    ng = rows // SUBLANES
    gpb = SUB // SUBLANES
    row = lax.broadcasted_iota(jnp.int32, (SUBLANES, LANES), 0)
    col = lax.broadcasted_iota(jnp.int32, (SUBLANES, LANES), 1) % SUB
    xs = [jnp.where(col == (row + g * SUBLANES) % SUB, 1.0, 0.0).astype(F32) for g in range(ng)]
    for ss in range(SUB - 1):
        gs, rs = divmod(ss, SUBLANES)
        for blk in range(ng // gpb):
            xrow = xs[blk * gpb + gs][rs:rs + 1, :]
            for g in range(blk * gpb + (ss + 1) // SUBLANES, (blk + 1) * gpb):
                cf = coef[g * SUBLANES:(g + 1) * SUBLANES, ss * LANES:(ss + 1) * LANES]
                xs[g] = xs[g] - cf * xrow
    return jnp.concatenate(xs, axis=0)


def _rwkv_scratch(t_len):
    n_chunks = t_len // CHUNK
    return [pltpu.VMEM((LANES, (SUB - 1) * LANES), BF16),
            pltpu.VMEM((PACK * LANES, LANES), BF16),
            pltpu.VMEM((LANES, PACK * LANES), BF16),
            pltpu.VMEM((t_len, LANES), BF16),
            pltpu.VMEM((t_len, LANES), F32),
            pltpu.VMEM((t_len, LANES), F32),
            pltpu.VMEM((n_chunks, LANES, LANES), BF16),
            pltpu.VMEM((n_chunks, HEAD_DIM, LANES), F32),
            pltpu.VMEM((n_chunks, SUBLANES, LANES), F32),
            pltpu.VMEM((2, RWKV_GROUP, 2 * CHUNK, LANES), BF16),
            pltpu.VMEM((2, RWKV_GROUP, 4 * CHUNK, LANES), BF16),
            pltpu.VMEM((2, RWKV_GROUP, 2 * CHUNK, LANES), BF16),
            pltpu.VMEM((2, RWKV_GROUP, 2 * CHUNK, LANES), BF16),
            pltpu.VMEM((2, RWKV_GROUP * CHUNK, LANES), F32),
            pltpu.VMEM((2, RWKV_GROUP * CHUNK, LANES), BF16),
            pltpu.VMEM((2, RWKV_GROUP * CHUNK, LANES), BF16),
            pltpu.VMEM((2, RWKV_GROUP * CHUNK, LANES), BF16),
            pltpu.VMEM((2, RWKV_GROUP * CHUNK, LANES), F32),
            pltpu.VMEM((2, RWKV_GROUP, SUBLANES, LANES), F32)]


def _rwkv_kernel(r_ref, k_ref, v_ref, wa_ref, z_ref, up_ref, dbase_ref, abase_ref, kns_ref,
                 kim_ref, bonus_ref, gng_ref, gnb_ref, o_ref,
                 spread_ref, fold_ref, unfold_ref, qt_ref, yloc_ref, bon_ref, mt_ref, nn_ref,
                 pe_ref, lhs_ref, rhs_ref, vbd_ref, kbd_ref, rd_ref, vb_ref, kdl_ref, bdl_ref,
                 bonr_ref, per_ref):
    t_len = r_ref.shape[1]
    c = CHUNK
    grp = RWKV_GROUP
    rows = grp * c
    n_chunks = t_len // c
    lane = lax.broadcasted_iota(jnp.int32, (c, LANES), 1)
    is_h0 = lane < HEAD_DIM
    spread_ref[...] = _spread_matrix()
    fold, unfold = _fold_matrices()
    fold_ref[...] = jnp.concatenate(fold, axis=0)
    unfold_ref[...] = unfold
    assert grp % PACK == 0
    trow = lax.broadcasted_iota(jnp.int32, (c, LANES), 0)
    scol = lane % HEAD_DIM
    tb, sb = trow // SUB, scol // SUB
    strict = trow > scol
    incl = trow >= scol
    in_block = tb == sb
    ltri = (lax.broadcasted_iota(jnp.int32, (c, c), 0)
            >= lax.broadcasted_iota(jnp.int32, (c, c), 1)).astype(BF16)
    sq_r = lax.broadcasted_iota(jnp.int32, (LANES, LANES), 0)
    sq_c = lax.broadcasted_iota(jnp.int32, (LANES, LANES), 1)
    same_head = sq_r // HEAD_DIM == sq_c // HEAD_DIM

    dbase, abase = dbase_ref[...], abase_ref[...]
    kns, kim, bonus = kns_ref[...], kim_ref[...], bonus_ref[...]
    gng, gnb = gng_ref[...], gnb_ref[...]
    up_hi, up_lo = _split2(up_ref[...])
    up_hh = jnp.concatenate([up_hi, up_hi], axis=0)

    def head_sum(x):
        h0 = lax.broadcasted_iota(jnp.int32, x.shape, 1) < HEAD_DIM
        s0 = jnp.sum(jnp.where(h0, x, 0.0), axis=1, keepdims=True)
        s1 = jnp.sum(jnp.where(h0, 0.0, x), axis=1, keepdims=True)
        return jnp.where(h0, s0, s1)

    def block_diag(x):
        zero = jnp.zeros_like(x)
        return jnp.concatenate([jnp.where(is_h0, x, zero), jnp.where(is_h0, zero, x)],
                               axis=0).astype(BF16)

    n = grp
    js = range(n)
    cs = [slice(j * c, (j + 1) * c) for j in js]

    def prep(gi, slot):
        first_chunk = gi * grp
        sl = pl.ds(pl.multiple_of(first_chunk * c, n * c), n * c)
        r = r_ref[0, sl, :]
        k = k_ref[0, sl, :]
        v = v_ref[0, sl, :]
        wa = wa_ref[0, sl, :]
        is_h0_s = lax.broadcasted_iota(jnp.int32, (n * c, LANES), 1) < HEAD_DIM
        x_hi, x_lo = _split2(jnp.where(is_h0_s, jnp.tanh(wa), wa))
        lin = _dot(jnp.concatenate([x_hi, x_lo], axis=1), up_hh) + _dot(x_hi, up_lo)
        yield
        logw = -DECAY_SCALE * jax.nn.sigmoid(dbase + lin[:, :LANES])
        a = jax.nn.sigmoid(abase + lin[:, LANES:])
        kk = k * kns
        kk = kk * lax.rsqrt(jnp.maximum(head_sum(kk * kk), 1e-24))
        kmod = k * (1.0 + (a - 1.0) * kim)
        b = kk * a
        bonr_ref[slot] = head_sum(r * kmod * bonus) * v

        parts = jnp.concatenate(_split2(logw), axis=1)
        gs = [_dot(ltri, parts[cs[j]]) for j in js]
        yield
        g = jnp.concatenate([x[:, LANES:] + x[:, :LANES] for x in gs], axis=0)
        p_end = [jnp.exp(g[(j + 1) * c - 1:(j + 1) * c, :]) for j in js]
        e_inv = jnp.exp(-g)
        kkd = kk * jnp.exp(g - logw)
        rd = r * jnp.exp(g)
        bi = b * e_inv
        ki = kmod * e_inv
        e_end = e_inv * jnp.concatenate([jnp.broadcast_to(x, (c, LANES)) for x in p_end], axis=0)
        rd_ref[slot] = rd
        vb_ref[slot] = v.astype(BF16)
        kdl_ref[slot] = (kmod * e_end).astype(BF16)
        bdl_ref[slot] = (b * e_end).astype(BF16)
        for j in js:
            lhs_ref[slot, j] = jnp.concatenate([kkd[cs[j]], rd[cs[j]]], axis=0).astype(BF16)
            rhs_ref[slot, j] = jnp.concatenate([block_diag(bi[cs[j]]), block_diag(ki[cs[j]])],
                                               axis=0)
            vbd_ref[slot, j] = block_diag(v[cs[j]])
            kbd_ref[slot, j] = block_diag(kkd[cs[j]])
            per_ref[slot, j] = jnp.broadcast_to(p_end[j], (SUBLANES, LANES))

    def pass_a(first_chunk, slot):
        aab, aak, arb, ark = [], [], [], []
        for j in js:
            pair = _dot_nt(lhs_ref[slot, j], rhs_ref[slot, j])
            aab.append(jnp.where(strict, pair[:c, :LANES], 0.0))
            aak.append(jnp.where(strict, pair[:c, LANES:], 0.0))
            arb.append(jnp.where(incl, pair[c:, :LANES], 0.0))
            ark.append(jnp.where(incl, pair[c:, LANES:], 0.0))
        yield
        akv = [_dot(jnp.concatenate([aak[j], ark[j]], axis=0).astype(BF16), vbd_ref[slot, j])
               for j in js]
        aakv = [x[:c] for x in akv]
        arkv = [x[c:] for x in akv]
        n_diag = jnp.concatenate(
            [jnp.concatenate([jnp.where(in_block, aab[q * PACK + i], 0.0).astype(BF16)
                              for i in range(PACK)], axis=1)
             for q in range(n // PACK)], axis=0)
        compact = _dot(n_diag, fold_ref[...])
        yield
        coef = _dot(compact.astype(BF16), spread_ref[...])
        t_diag = _substitute(coef)
        yield
        spread_out = _dot(t_diag.astype(BF16), unfold_ref[...])
        tinv = [jnp.where(in_block, spread_out[(j // PACK) * c:(j // PACK + 1) * c,
                                               (j % PACK) * LANES:(j % PACK + 1) * LANES], 0.0)
                for j in js]
        yield
        span = 1
        while span * SUB < c:
            low_mask = (tb // span == sb // span + 1) & ((tb // span) % 2 == 1)
            inner = [_dot(jnp.where(low_mask, aab[j], 0.0).astype(BF16),
                          block_diag(tinv[j])) for j in js]
            yield
            tinv = [tinv[j] - _dot(tinv[j].astype(BF16), block_diag(inner[j])) for j in js]
            yield
            span *= 2
        wu = [_dot(tinv[j].astype(BF16),
                   jnp.concatenate([kbd_ref[slot, j], block_diag(aakv[j])], axis=1))
              for j in js]
        yield
        w = [x[:, :LANES] for x in wu]
        uloc = [x[:, LANES:] for x in wu]
        arb_b = [arb[j].astype(BF16) for j in js]
        bdl_b = [bdl_ref[slot, cs[j], :] for j in js]
        awu = [_dot(arb_b[j], jnp.concatenate([block_diag(w[j]), block_diag(uloc[j])], axis=1))
               for j in js]
        qt = [rd_ref[slot, cs[j], :] - awu[j][:, :LANES] for j in js]
        yloc = [arkv[j] - awu[j][:, LANES:] for j in js]
        wtb = [_dot_tn(w[j].astype(BF16), bdl_b[j]) for j in js]
        nn = [_dot_tn(jnp.concatenate([vb_ref[slot, cs[j], :], (-uloc[j]).astype(BF16)], axis=0),
                      jnp.concatenate([kdl_ref[slot, cs[j], :], bdl_b[j]], axis=0))
              for j in js]
        yield
        for j in js:
            ci = first_chunk + j
            csl = pl.ds(pl.multiple_of(ci * c, c), c)
            qt_ref[csl, :] = qt[j].astype(BF16)
            yloc_ref[csl, :] = yloc[j]
            mt_ref[ci] = jnp.where(same_head, wtb[j], 0.0).astype(BF16)
            nn_ref[ci] = jnp.where(is_h0, nn[j][:HEAD_DIM], nn[j][HEAD_DIM:])
            pe_ref[ci] = per_ref[slot, j]
        gsl = pl.ds(pl.multiple_of(first_chunk * c, n * c), n * c)
        bon_ref[gsl, :] = bonr_ref[slot]

    def chunk_step(ci, state):
        sb = state.astype(BF16)
        sl = pl.ds(pl.multiple_of(ci * c, c), c)
        y = _dot_nt(qt_ref[sl, :], block_diag(sb)) + yloc_ref[sl, :]
        new_state = state * pe_ref[ci, 0:1, :] + nn_ref[ci] - _dot(sb, mt_ref[ci])
        mu = head_sum(y) * (1.0 / HEAD_DIM)
        d = y - mu
        var = head_sum(d * d) * (1.0 / HEAD_DIM)
        yn = d * lax.rsqrt(var + GN_EPS) * gng + gnb + bon_ref[sl, :]
        z = z_ref[0, sl, :]
        o_ref[0, sl, :] = (yn * (z * jax.nn.sigmoid(z))).astype(o_ref.dtype)
        return new_state

    class Chain:
        def __init__(self, gi, state):
            self.gi, self.state, self.done = gi, state, 0

        def tick(self):
            if self.done < grp:
                self.state = chunk_step(self.gi * grp + self.done, self.state)
                self.done += 1

    n_groups = n_chunks // grp

    @pl.when(pl.program_id(0) == 0)
    def _():
        for ref in (qt_ref, yloc_ref, bon_ref, mt_ref, nn_ref, pe_ref):
            ref[...] = jnp.zeros(ref.shape, ref.dtype)

    def run_group(gi, slot, state, prepare_next=True):
        chain = Chain(gi, state)
        nxt = prep(gi + 1, 1 - slot) if prepare_next else iter(())
        for step, _ in enumerate(pass_a(gi * grp, slot)):
            if step in PREP_STEPS:
                next(nxt, None)
            chain.tick()
        assert chain.done == grp
        for _ in nxt:
            pass
        return chain.state

    def body(pair, state):
        state = run_group(2 * pair, 0, state)
        return run_group(2 * pair + 1, 1, state)

    assert n_groups % 2 == 0
    for _ in prep(0, 0):
        pass
    state = lax.fori_loop(0, (n_groups - 2) // 2, body, jnp.zeros((HEAD_DIM, LANES), F32))
    state = run_group(n_groups - 2, 0, state)
    run_group(n_groups - 1, 1, state, prepare_next=False)


def _attn_kernel(q_ref, k_ref, v_ref, z_ref, o_ref, kb_ref, vb_ref, acc_ref, m_ref, l_ref):
    t_len = q_ref.shape[1]
    blk = ATTN_BLOCK
    kb_ref[...] = k_ref[0].astype(BF16)
    vb_ref[...] = v_ref[0].astype(BF16)

    lane = lax.broadcasted_iota(jnp.int32, (blk, LANES), 1)
    is_h0 = lane < HEAD_DIM
    qi = lax.broadcasted_iota(jnp.int32, (blk, 2 * blk), 0)
    kj = lax.broadcasted_iota(jnp.int32, (blk, 2 * blk), 1)
    nb = ATTN_GROUP
    heads = (is_h0, ~is_h0)
    patterns = tuple(reversed(DILATION_PATTERNS))
    assert patterns[-1][1] == 1
    bias_causal = jnp.where(kj <= qi, 0.0, NEG_BIG).astype(F32)

    bodies = []
    for pi, (window, dil) in enumerate(patterns):
        span = window // dil
        nblk = t_len // (dil * blk)
        bias_band = jnp.where((qi + blk - kj >= 0) & (qi + blk - kj <= span), 0.0,
                              NEG_BIG).astype(F32)
        last = pi == len(patterns) - 1

        def body(it, pi=pi, dil=dil, nblk=nblk, bias_band=bias_band, last=last):
            rows, q, kw, vw, biases = [], [], [], [], []
            if nblk == 2 and nb % 2 == 0:
                for jr in range(nb // 2):
                    res = it * (nb // 2) + jr
                    seq = pl.ds(res, 2 * blk, stride=dil)
                    q_seq = q_ref[0, seq, :]
                    k_seq = k_ref[0, seq, :].astype(BF16)
                    v_seq = v_ref[0, seq, :].astype(BF16)
                    for n in range(2):
                        rows.append(pl.ds(res + dil * blk * n, blk, stride=dil))
                        q.append(q_seq[n * blk:(n + 1) * blk])
                        kw.append(k_seq)
                        vw.append(v_seq)
                        biases.append(bias_band if n else bias_causal)
            else:
                for j in range(nb):
                    res = (it * nb + j) // nblk
                    n = (it * nb + j) % nblk
                    maybe_first = j == 0 or nblk % nb != 0
                    n_prev = jnp.maximum(n - 1, 0) if maybe_first else n - 1
                    q0 = res + dil * blk * n
                    k0 = res + dil * blk * n_prev
                    if dil == 1:
                        rows.append(pl.ds(pl.multiple_of(q0, blk), blk))
                        krows = pl.ds(pl.multiple_of(k0, blk), 2 * blk)
                        kw.append(kb_ref[krows, :])
                        vw.append(vb_ref[krows, :])
                    else:
                        rows.append(pl.ds(q0, blk, stride=dil))
                        krows = pl.ds(k0, 2 * blk, stride=dil)
                        kw.append(k_ref[0, krows, :].astype(BF16))
                        vw.append(v_ref[0, krows, :].astype(BF16))
                    q.append(q_ref[0, rows[j], :])
                    biases.append(jnp.where(n == 0, bias_causal, bias_band) if maybe_first
                                  else bias_band)
            s = [[_dot_nt(jnp.where(h, q[j], 0.0).astype(BF16), kw[j]) + biases[j] for h in heads]
                 for j in range(nb)]
            m = [[jnp.max(x, axis=1, keepdims=True) for x in sj] for sj in s]
            p = [[jnp.exp2(s[j][h] - m[j][h]) for h in range(2)] for j in range(nb)]
            l = [[jnp.sum(x, axis=1, keepdims=True) for x in pj] for pj in p]
            pv = [[_dot(p[j][h].astype(BF16), vw[j]) for h in range(2)] for j in range(nb)]
            for j in range(nb):
                acc = jnp.where(is_h0, pv[j][0], pv[j][1])
                mj = jnp.where(is_h0, m[j][0], m[j][1])
                lj = jnp.where(is_h0, l[j][0], l[j][1])
                if last:
                    ms = [mj] + [m_ref[pp, rows[j], :] for pp in range(pi)]
                    ls = [lj] + [l_ref[pp, rows[j], :] for pp in range(pi)]
                    accs = [acc] + [acc_ref[pp, rows[j], :] for pp in range(pi)]
                    m_all = functools.reduce(jnp.maximum, ms)
                    ws = [jnp.exp2(mm - m_all) for mm in ms]
                    num = functools.reduce(jnp.add, [w * a for w, a in zip(ws, accs)])
                    den = functools.reduce(jnp.add, [w * x for w, x in zip(ws, ls)])
                    z = z_ref[0, rows[j], :]
                    o_ref[0, rows[j], :] = (num / den * (z * jax.nn.sigmoid(z))).astype(o_ref.dtype)
                else:
                    acc_ref[pi, rows[j], :] = acc
                    m_ref[pi, rows[j], :] = mj
                    l_ref[pi, rows[j], :] = lj

        bodies.append(body)

    unroll = ATTN_UNROLL

    def strided_body(trip, carry):
        for u in range(unroll):
            for body in bodies[:-1]:
                body(trip * unroll + u)
        return carry

    def contiguous_body(trip, carry):
        for u in range(unroll):
            bodies[-1](trip * unroll + u)
        return carry

    n_it = t_len // (blk * nb)
    assert n_it % unroll == 0
    lax.fori_loop(0, n_it // unroll, strided_body, 0)
    lax.fori_loop(0, n_it // unroll, contiguous_body, 0)


def _out_proj_kernel(ya_ref, yb_ref, wa_ref, wb_ref, x_ref, g_ref, o_ref):
    y = x_ref[...] + _dot(ya_ref[...], wa_ref[...]) + _dot(yb_ref[...], wb_ref[...])
    o_ref[...] = y * lax.rsqrt(jnp.mean(y * y, axis=-1, keepdims=True) + NORM_EPS) * g_ref[...]


def _rope_tables(t_len):
    half = ROPE_DIM // 2
    inv = ROPE_THETA ** (-jnp.arange(half, dtype=F32) * 2.0 / ROPE_DIM)
    ang = jnp.arange(t_len, dtype=jnp.int32).astype(F32)[:, None] * inv[None, :]
    cos, sin = jnp.cos(ang), jnp.sin(ang)
    ones = jnp.ones((t_len, HEAD_DIM - ROPE_DIM), F32)
    zeros = jnp.zeros((t_len, HEAD_DIM - ROPE_DIM), F32)
    zh = jnp.zeros((t_len, half), F32)
    tile = lambda a: jnp.tile(a, (1, LANES // HEAD_DIM))
    cos_t = tile(jnp.concatenate([cos, cos, ones], axis=1))
    sa_t = tile(jnp.concatenate([-sin, zh, zeros], axis=1))
    sb_t = tile(jnp.concatenate([zh, sin, zeros], axis=1))
    return cos_t, sa_t, sb_t


def kernel(x, norm_gain, w_in, shift_mix, decay_base, decay_up, iclr_base, iclr_up, key_norm_scale,
           key_iclr_mix, bonus, gn_gain, gn_bias, w_out, final_gain):
    bsz, t_len, d_model = x.shape
    depth = w_in.shape[0]
    c_a = decay_base.shape[1]
    n_hp = c_a // LANES
    shift_w = 3 * c_a + DECAY_RANK + ICLR_RANK
    in_w = w_in.shape[2]
    c_b = (in_w - shift_w - c_a) // 4
    assert c_a % LANES == 0 and c_b == c_a
    assert all(t_len % (2 * dil * ATTN_BLOCK) == 0 for _, dil in DILATION_PATTERNS)
    o_za = shift_w
    o_q = o_za + c_a
    o_k, o_v, o_zb = o_q + c_b, o_q + 2 * c_b, o_q + 3 * c_b
    segs = ((0, c_a), (c_a, 2 * c_a), (2 * c_a, 3 * c_a), (3 * c_a, shift_w), (o_za, o_q),
            (o_q, o_k), (o_k, o_v), (o_v, o_zb), (o_zb, in_w))
    cos_t, sa_t, sb_t = _rope_tables(t_len)
    rows = bsz * t_len
    tm = IN_PROJ_ROWS
    tiles_per_seq = t_len // tm
    row2 = lambda a: a.reshape(1, -1).astype(F32)

    for layer in range(depth):
        p = pl.pallas_call(
            functools.partial(_in_proj_kernel, segs=segs, shift_w=shift_w, rope_lo=o_q, q_hi=o_k,
                              rope_hi=o_v, tiles_per_seq=tiles_per_seq),
            grid=(rows // tm,),
            in_specs=[
                pl.BlockSpec((tm, d_model), lambda i: (i, 0)),
                pl.BlockSpec((1, d_model), lambda i: (0, 0)),
                pl.BlockSpec((d_model, in_w), lambda i: (0, 0), pipeline_mode=pl.Buffered(1)),
                pl.BlockSpec((1, shift_w), lambda i: (0, 0)),
                pl.BlockSpec((tm, LANES), lambda i: (i % tiles_per_seq, 0)),
                pl.BlockSpec((tm, LANES), lambda i: (i % tiles_per_seq, 0)),
                pl.BlockSpec((tm, LANES), lambda i: (i % tiles_per_seq, 0)),
            ],
            out_specs=pl.BlockSpec((tm, in_w), lambda i: (i, 0)),
            out_shape=jax.ShapeDtypeStruct((rows, in_w), F32),
            scratch_shapes=[pltpu.VMEM((SUBLANES, shift_w), F32)],
            compiler_params=pltpu.CompilerParams(dimension_semantics=("arbitrary",),
                                                 vmem_limit_bytes=VMEM_LIMIT),
            name="in_proj",
        )(x.reshape(rows, d_model), row2(norm_gain[layer]), w_in[layer].astype(BF16),
          row2(shift_mix[layer]), cos_t, sa_t, sb_t)
        p = p.reshape(bsz, t_len, in_w)

        zeros_up = jnp.zeros((DECAY_RANK, c_a), F32)
        up_full = jnp.concatenate(
            [jnp.concatenate([decay_up[layer].astype(F32), zeros_up], axis=0).reshape(
                DECAY_RANK + ICLR_RANK, n_hp, 1, LANES),
             jnp.concatenate([zeros_up, iclr_up[layer].astype(F32)], axis=0).reshape(
                 DECAY_RANK + ICLR_RANK, n_hp, 1, LANES)], axis=2).reshape(
                     DECAY_RANK + ICLR_RANK, n_hp * 2 * LANES)
        col = lambda off: (lambda b, h: (b, 0, off // LANES + h))
        seq_spec = lambda off: pl.BlockSpec((1, t_len, LANES), col(off))
        n_pairs = bsz * n_hp
        cur = lambda s: jnp.minimum(s, n_pairs - 1)
        lag = lambda s: jnp.maximum(s - 1, 0)
        pair_col = lambda pick, off: (
            lambda s: (pick(s) // n_hp, 0, off // LANES + pick(s) % n_hp))
        cur_spec = lambda off: pl.BlockSpec((1, t_len, LANES), pair_col(cur, off))
        lag_spec = lambda off: pl.BlockSpec((1, t_len, LANES), pair_col(lag, off))
        cur_par = pl.BlockSpec((1, LANES), lambda s: (0, cur(s) % n_hp))
        lag_par = pl.BlockSpec((1, LANES), lambda s: (0, lag(s) % n_hp))
        ya = pl.pallas_call(
            _rwkv_kernel,
            grid=(n_pairs + 1,),
            in_specs=[cur_spec(0), cur_spec(c_a), cur_spec(2 * c_a),
                      pl.BlockSpec((1, t_len, LANES),
                                   lambda s: (cur(s) // n_hp, 0, 3 * c_a // LANES)),
                      lag_spec(o_za),
                      pl.BlockSpec((DECAY_RANK + ICLR_RANK, 2 * LANES),
                                   lambda s: (0, cur(s) % n_hp)),
                      cur_par, cur_par, cur_par, cur_par, cur_par, lag_par, lag_par],
            out_specs=lag_spec(0),
            out_shape=jax.ShapeDtypeStruct((bsz, t_len, c_a), BF16),
            scratch_shapes=_rwkv_scratch(t_len),
            compiler_params=pltpu.CompilerParams(dimension_semantics=("arbitrary",),
                                                 vmem_limit_bytes=VMEM_LIMIT),
            name="rwkv",
        )(p, p, p, p, p, up_full, row2(decay_base[layer]), row2(iclr_base[layer]),
          row2(key_norm_scale[layer]), row2(key_iclr_mix[layer]), row2(bonus[layer]),
          row2(gn_gain[layer]), row2(gn_bias[layer]))

        yb = pl.pallas_call(
            _attn_kernel,
            grid=(bsz, c_b // LANES),
            in_specs=[seq_spec(o_q), seq_spec(o_k), seq_spec(o_v), seq_spec(o_zb)],
            out_specs=pl.BlockSpec((1, t_len, LANES), lambda b, h: (b, 0, h)),
            out_shape=jax.ShapeDtypeStruct((bsz, t_len, c_b), BF16),
            scratch_shapes=[pltpu.VMEM((t_len, LANES), BF16),
                            pltpu.VMEM((t_len, LANES), BF16),
                            pltpu.VMEM((len(DILATION_PATTERNS) - 1, t_len, LANES), F32),
                            pltpu.VMEM((len(DILATION_PATTERNS) - 1, t_len, LANES), F32),
                            pltpu.VMEM((len(DILATION_PATTERNS) - 1, t_len, LANES), F32)],
            compiler_params=pltpu.CompilerParams(dimension_semantics=("arbitrary", "arbitrary"),
                                                 vmem_limit_bytes=VMEM_LIMIT),
            name="attention",
        )(p, p, p, p)

        assert depth == 1
        tmo = OUT_PROJ_ROWS
        wo = w_out[layer].astype(BF16)
        x = pl.pallas_call(
            _out_proj_kernel,
            grid=(rows // tmo,),
            in_specs=[pl.BlockSpec((tmo, c_a), lambda i: (i, 0)),
                      pl.BlockSpec((tmo, c_b), lambda i: (i, 0)),
                      pl.BlockSpec((c_a, d_model), lambda i: (0, 0)),
                      pl.BlockSpec((c_b, d_model), lambda i: (0, 0)),
                      pl.BlockSpec((tmo, d_model), lambda i: (i, 0)),
                      pl.BlockSpec((1, d_model), lambda i: (0, 0))],
            out_specs=pl.BlockSpec((tmo, d_model), lambda i: (i, 0)),
            out_shape=jax.ShapeDtypeStruct((rows, d_model), F32),
            compiler_params=pltpu.CompilerParams(dimension_semantics=("arbitrary",),
                                                 vmem_limit_bytes=VMEM_LIMIT),
            name="out_proj",
        )(ya.reshape(rows, c_a), yb.reshape(rows, c_b), wo[:c_a], wo[c_a:],
          x.reshape(rows, d_model), row2(final_gain)).reshape(bsz, t_len, d_model)
    return x
```

```python
import functools

import jax
import jax.numpy as jnp
from jax import lax
from jax.experimental import pallas as pl
from jax.experimental.pallas import tpu as pltpu

HEAD_DIM = 64
LANES = 128
SUBLANES = 8
DECAY_RANK = 64
ICLR_RANK = 64
DILATION_PATTERNS = ((128, 1), (512, 4), (2048, 16))
ATTN_BLOCK = 128
ATTN_GROUP = 4
ATTN_UNROLL = 8
ROPE_THETA = 500000.0
ROPE_DIM = HEAD_DIM // 4
NORM_EPS = 1e-6
GN_EPS = 64e-5
CHUNK = 64
SUB = 16
PACK = LANES // (2 * SUB)
RWKV_GROUP = 8
PREP_STEPS = (2, 4, 6)
IN_PROJ_ROWS = 1024
OUT_PROJ_ROWS = 2048
DECAY_SCALE = 0.6065306597126334
LOG2_E = 1.4426950408889634
ATTN_Q_SCALE = HEAD_DIM ** -0.5 * LOG2_E
NEG_BIG = -1e30
VMEM_LIMIT = 56 * 1024 * 1024

F32 = jnp.float32
BF16 = jnp.bfloat16


def _dot(a, b, **kw):
    return jnp.dot(a, b, preferred_element_type=F32, **kw)


def _dot_nt(a, b):
    return lax.dot_general(a, b, (((1,), (1,)), ((), ())), preferred_element_type=F32)


def _split2(x):
    hi = x.astype(BF16)
    return hi, (x - hi.astype(F32)).astype(BF16)


def _dot_tn(a, b):
    return lax.dot_general(a, b, (((0,), (0,)), ((), ())), preferred_element_type=F32)


def _in_proj_kernel(x_ref, g_ref, w_ref, mix_ref, cos_ref, sa_ref, sb_ref, o_ref, carry_ref,
                    *, segs, shift_w, rope_lo, q_hi, rope_hi, tiles_per_seq):
    i = pl.program_id(0)

    @pl.when(i == 0)
    def _():
        carry_ref[...] = jnp.zeros(carry_ref.shape, F32)

    x = x_ref[...]
    h = x * lax.rsqrt(jnp.mean(x * x, axis=-1, keepdims=True) + NORM_EPS) * g_ref[...]
    hb = h.astype(BF16)
    tm = x.shape[0]
    first = (i % tiles_per_seq) == 0
    row0 = lax.broadcasted_iota(jnp.int32, (tm, 1), 0) == 0
    for lo, hi in segs:
        p = _dot(hb, w_ref[:, lo:hi])
        if hi <= shift_w:
            old = jnp.where(first, 0.0, carry_ref[0:1, lo:hi])
            carry_ref[0:1, lo:hi] = p[tm - 1:tm, :]
            prev = jnp.where(row0, old, pltpu.roll(p, 1, axis=0))
            p = p + (prev - p) * mix_ref[:, lo:hi]
            o_ref[:, lo:hi] = p
        elif lo >= rope_lo and hi <= rope_hi:
            if hi <= q_hi:
                p = p * ATTN_Q_SCALE
            for j in range(lo, hi, LANES):
                t = p[:, j - lo:j - lo + LANES]
                t = (t * cos_ref[...] + pltpu.roll(t, LANES - ROPE_DIM // 2, axis=1) * sa_ref[...]
                     + pltpu.roll(t, ROPE_DIM // 2, axis=1) * sb_ref[...])
                o_ref[:, j:j + LANES] = t
        else:
            o_ref[:, lo:hi] = p


def _spread_matrix():
    j = lax.broadcasted_iota(jnp.int32, (LANES, (SUB - 1) * LANES), 0)
    col = lax.broadcasted_iota(jnp.int32, (LANES, (SUB - 1) * LANES), 1)
    sel = j == SUB * ((col % LANES) // SUB) + col // LANES
    return jnp.where(sel, 1.0, 0.0).astype(BF16)


def _fold_matrices():
    r = lax.broadcasted_iota(jnp.int32, (LANES, LANES), 0)
    l = lax.broadcasted_iota(jnp.int32, (LANES, LANES), 1)
    per_chunk = LANES // PACK
    match = ((l % per_chunk) // SUB == r // HEAD_DIM) & (l % SUB == r % SUB)
    fold = [jnp.where(match & (l // per_chunk == c), 1.0, 0.0).astype(BF16) for c in range(PACK)]
    rr = lax.broadcasted_iota(jnp.int32, (LANES, PACK * LANES), 0)
    cc = lax.broadcasted_iota(jnp.int32, (LANES, PACK * LANES), 1)
    src = per_chunk * (cc // LANES) + SUB * ((cc % LANES) // HEAD_DIM) + cc % SUB
    unfold = jnp.where(rr == src, 1.0, 0.0).astype(BF16)
    return fold, unfold


def _substitute(coef):
    rows = coef.shape[0]
    ng = rows // SUBLANES
    gpb = SUB // SUBLANES
    row = lax.broadcasted_iota(jnp.int32, (SUBLANES, LANES), 0)
    col = lax.broadcasted_iota(jnp.int32, (SUBLANES, LANES), 1) % SUB
    xs = [jnp.where(col == (row + g * SUBLANES) % SUB, 1.0, 0.0).astype(F32) for g in range(ng)]
    for ss in range(SUB - 1):
        gs, rs = divmod(ss, SUBLANES)
        for blk in range(ng // gpb):
            xrow = xs[blk * gpb + gs][rs:rs + 1, :]
            for g in range(blk * gpb + (ss + 1) // SUBLANES, (blk + 1) * gpb):
                cf = coef[g * SUBLANES:(g + 1) * SUBLANES, ss * LANES:(ss + 1) * LANES]
                xs[g] = xs[g] - cf * xrow
    return jnp.concatenate(xs, axis=0)


def _rwkv_scratch(t_len):
    n_chunks = t_len // CHUNK
    return [pltpu.VMEM((LANES, (SUB - 1) * LANES), BF16),
            pltpu.VMEM((PACK * LANES, LANES), BF16),
            pltpu.VMEM((LANES, PACK * LANES), BF16),
            pltpu.VMEM((t_len, LANES), BF16),
            pltpu.VMEM((t_len, LANES), F32),
            pltpu.VMEM((t_len, LANES), F32),
            pltpu.VMEM((n_chunks, LANES, LANES), BF16),
            pltpu.VMEM((n_chunks, HEAD_DIM, LANES), F32),
            pltpu.VMEM((n_chunks, SUBLANES, LANES), F32),
            pltpu.VMEM((2, RWKV_GROUP, 2 * CHUNK, LANES), BF16),
            pltpu.VMEM((2, RWKV_GROUP, 4 * CHUNK, LANES), BF16),
            pltpu.VMEM((2, RWKV_GROUP, 2 * CHUNK, LANES), BF16),
            pltpu.VMEM((2, RWKV_GROUP, 2 * CHUNK, LANES), BF16),
            pltpu.VMEM((2, RWKV_GROUP * CHUNK, LANES), F32),
            pltpu.VMEM((2, RWKV_GROUP * CHUNK, LANES), BF16),
            pltpu.VMEM((2, RWKV_GROUP * CHUNK, LANES), BF16),
            pltpu.VMEM((2, RWKV_GROUP * CHUNK, LANES), BF16),
            pltpu.VMEM((2, RWKV_GROUP * CHUNK, LANES), F32),
            pltpu.VMEM((2, RWKV_GROUP, SUBLANES, LANES), F32)]


def _rwkv_kernel(r_ref, k_ref, v_ref, wa_ref, up_ref, dbase_ref, abase_ref, kns_ref, kim_ref,
                 bonus_ref, rn_ref, kn_ref, vn_ref, wan_ref, upn_ref, dbasen_ref, abasen_ref,
                 knsn_ref, kimn_ref, bonusn_ref, z_ref, gng_ref, gnb_ref, o_ref,
                 spread_ref, fold_ref, unfold_ref, qt_ref, yloc_ref, bon_ref, mt_ref, nn_ref,
                 pe_ref, lhs_ref, rhs_ref, vbd_ref, kbd_ref, rd_ref, vb_ref, kdl_ref, bdl_ref,
                 bonr_ref, per_ref):
    t_len = r_ref.shape[1]
    c = CHUNK
    grp = RWKV_GROUP
    rows = grp * c
    n_chunks = t_len // c
    lane = lax.broadcasted_iota(jnp.int32, (c, LANES), 1)
    is_h0 = lane < HEAD_DIM
    spread_ref[...] = _spread_matrix()
    fold, unfold = _fold_matrices()
    fold_ref[...] = jnp.concatenate(fold, axis=0)
    unfold_ref[...] = unfold
    assert grp % PACK == 0
    trow = lax.broadcasted_iota(jnp.int32, (c, LANES), 0)
    scol = lane % HEAD_DIM
    tb, sb = trow // SUB, scol // SUB
    strict = trow > scol
    incl = trow >= scol
    in_block = tb == sb
    ltri = (lax.broadcasted_iota(jnp.int32, (c, c), 0)
            >= lax.broadcasted_iota(jnp.int32, (c, c), 1)).astype(BF16)
    sq_r = lax.broadcasted_iota(jnp.int32, (LANES, LANES), 0)
    sq_c = lax.broadcasted_iota(jnp.int32, (LANES, LANES), 1)
    same_head = sq_r // HEAD_DIM == sq_c // HEAD_DIM

    gng, gnb = gng_ref[...], gnb_ref[...]

    def prep_source(refs, up, dbase, abase, kns, kim, bonus):
        up_hi, up_lo = _split2(up[...])
        return (refs, jnp.concatenate([up_hi, up_hi], axis=0), up_lo,
                dbase[...], abase[...], kns[...], kim[...], bonus[...])

    src_cur = prep_source((r_ref, k_ref, v_ref, wa_ref), up_ref, dbase_ref, abase_ref, kns_ref,
                          kim_ref, bonus_ref)
    src_next = prep_source((rn_ref, kn_ref, vn_ref, wan_ref), upn_ref, dbasen_ref, abasen_ref,
                           knsn_ref, kimn_ref, bonusn_ref)

    def head_sum(x):
        h0 = lax.broadcasted_iota(jnp.int32, x.shape, 1) < HEAD_DIM
        s0 = jnp.sum(jnp.where(h0, x, 0.0), axis=1, keepdims=True)
        s1 = jnp.sum(jnp.where(h0, 0.0, x), axis=1, keepdims=True)
        return jnp.where(h0, s0, s1)

    def block_diag(x):
        zero = jnp.zeros_like(x)
        return jnp.concatenate([jnp.where(is_h0, x, zero), jnp.where(is_h0, zero, x)],
                               axis=0).astype(BF16)

    n = grp
    js = range(n)
    cs = [slice(j * c, (j + 1) * c) for j in js]

    def prep(gi, slot, src=src_cur):
        (rs_ref, ks_ref, vs_ref, was_ref), up_hh, up_lo, dbase, abase, kns, kim, bonus = src
        sl = pl.ds(pl.multiple_of(gi * n * c, n * c), n * c)
        r = rs_ref[0, sl, :]
        k = ks_ref[0, sl, :]
        v = vs_ref[0, sl, :]
        wa = was_ref[0, sl, :]
        is_h0_s = lax.broadcasted_iota(jnp.int32, (n * c, LANES), 1) < HEAD_DIM
        x_hi, x_lo = _split2(jnp.where(is_h0_s, jnp.tanh(wa), wa))
        lin = _dot(jnp.concatenate([x_hi, x_lo], axis=1), up_hh) + _dot(x_hi, up_lo)
        yield
        logw = -DECAY_SCALE * jax.nn.sigmoid(dbase + lin[:, :LANES])
        a = jax.nn.sigmoid(abase + lin[:, LANES:])
        kk = k * kns
        kk = kk * lax.rsqrt(jnp.maximum(head_sum(kk * kk), 1e-24))
        kmod = k * (1.0 + (a - 1.0) * kim)
        b = kk * a
        bonr_ref[slot] = head_sum(r * kmod * bonus) * v

        parts = jnp.concatenate(_split2(logw), axis=1)
        gs = [_dot(ltri, parts[cs[j]]) for j in js]
        yield
        g = jnp.concatenate([x[:, LANES:] + x[:, :LANES] for x in gs], axis=0)
        p_end = [jnp.exp(g[(j + 1) * c - 1:(j + 1) * c, :]) for j in js]
        e_inv = jnp.exp(-g)
        kkd = kk * jnp.exp(g - logw)
        rd = r * jnp.exp(g)
        bi = b * e_inv
        ki = kmod * e_inv
        e_end = e_inv * jnp.concatenate([jnp.broadcast_to(x, (c, LANES)) for x in p_end], axis=0)
        rd_ref[slot] = rd
        vb_ref[slot] = v.astype(BF16)
        kdl_ref[slot] = (kmod * e_end).astype(BF16)
        bdl_ref[slot] = (b * e_end).astype(BF16)
        for j in js:
            lhs_ref[slot, j] = jnp.concatenate([kkd[cs[j]], rd[cs[j]]], axis=0).astype(BF16)
            rhs_ref[slot, j] = jnp.concatenate([block_diag(bi[cs[j]]), block_diag(ki[cs[j]])],
                                               axis=0)
            vbd_ref[slot, j] = block_diag(v[cs[j]])
            kbd_ref[slot, j] = block_diag(kkd[cs[j]])
            per_ref[slot, j] = jnp.broadcast_to(p_end[j], (SUBLANES, LANES))

    def pass_a(first_chunk, slot):
        aab, aak, arb, ark = [], [], [], []
        for j in js:
            pair = _dot_nt(lhs_ref[slot, j], rhs_ref[slot, j])
            aab.append(jnp.where(strict, pair[:c, :LANES], 0.0))
            aak.append(jnp.where(strict, pair[:c, LANES:], 0.0))
            arb.append(jnp.where(incl, pair[c:, :LANES], 0.0))
            ark.append(jnp.where(incl, pair[c:, LANES:], 0.0))
        yield
        akv = [_dot(jnp.concatenate([aak[j], ark[j]], axis=0).astype(BF16), vbd_ref[slot, j])
               for j in js]
        aakv = [x[:c] for x in akv]
        arkv = [x[c:] for x in akv]
        n_diag = jnp.concatenate(
            [jnp.concatenate([jnp.where(in_block, aab[q * PACK + i], 0.0).astype(BF16)
                              for i in range(PACK)], axis=1)
             for q in range(n // PACK)], axis=0)
        compact = _dot(n_diag, fold_ref[...])
        yield
        coef = _dot(compact.astype(BF16), spread_ref[...])
        t_diag = _substitute(coef)
        yield
        spread_out = _dot(t_diag.astype(BF16), unfold_ref[...])
        tinv = [jnp.where(in_block, spread_out[(j // PACK) * c:(j // PACK + 1) * c,
                                               (j % PACK) * LANES:(j % PACK + 1) * LANES], 0.0)
                for j in js]
        yield
        span = 1
        while span * SUB < c:
            low_mask = (tb // span == sb // span + 1) & ((tb // span) % 2 == 1)
            inner = [_dot(jnp.where(low_mask, aab[j], 0.0).astype(BF16),
                          block_diag(tinv[j])) for j in js]
            yield
            tinv = [tinv[j] - _dot(tinv[j].astype(BF16), block_diag(inner[j])) for j in js]
            yield
            span *= 2
        wu = [_dot(tinv[j].astype(BF16),
                   jnp.concatenate([kbd_ref[slot, j], block_diag(aakv[j])], axis=1))
              for j in js]
        yield
        w = [x[:, :LANES] for x in wu]
        uloc = [x[:, LANES:] for x in wu]
        arb_b = [arb[j].astype(BF16) for j in js]
        bdl_b = [bdl_ref[slot, cs[j], :] for j in js]
        awu = [_dot(arb_b[j], jnp.concatenate([block_diag(w[j]), block_diag(uloc[j])], axis=1))
               for j in js]
        qt = [rd_ref[slot, cs[j], :] - awu[j][:, :LANES] for j in js]
        yloc = [arkv[j] - awu[j][:, LANES:] for j in js]
        wtb = [_dot_tn(w[j].astype(BF16), bdl_b[j]) for j in js]
        nn = [_dot_tn(jnp.concatenate([vb_ref[slot, cs[j], :], (-uloc[j]).astype(BF16)], axis=0),
                      jnp.concatenate([kdl_ref[slot, cs[j], :], bdl_b[j]], axis=0))
              for j in js]
        yield
        for j in js:
            ci = first_chunk + j
            csl = pl.ds(pl.multiple_of(ci * c, c), c)
            qt_ref[csl, :] = qt[j].astype(BF16)
            yloc_ref[csl, :] = yloc[j]
            mt_ref[ci] = jnp.where(same_head, wtb[j], 0.0).astype(BF16)
            nn_ref[ci] = jnp.where(is_h0, nn[j][:HEAD_DIM], nn[j][HEAD_DIM:])
            pe_ref[ci] = per_ref[slot, j]
        gsl = pl.ds(pl.multiple_of(first_chunk * c, n * c), n * c)
        bon_ref[gsl, :] = bonr_ref[slot]

    def chunk_step(ci, state):
        sb = state.astype(BF16)
        sl = pl.ds(pl.multiple_of(ci * c, c), c)
        y = _dot_nt(qt_ref[sl, :], block_diag(sb)) + yloc_ref[sl, :]
        new_state = state * pe_ref[ci, 0:1, :] + nn_ref[ci] - _dot(sb, mt_ref[ci])
        mu = head_sum(y) * (1.0 / HEAD_DIM)
        d = y - mu
        var = head_sum(d * d) * (1.0 / HEAD_DIM)
        yn = d * lax.rsqrt(var + GN_EPS) * gng + gnb + bon_ref[sl, :]
        z = z_ref[0, sl, :]
        o_ref[0, sl, :] = (yn * (z * jax.nn.sigmoid(z))).astype(o_ref.dtype)
        return new_state

    class Chain:
        def __init__(self, gi, state):
            self.gi, self.state, self.done = gi, state, 0

        def tick(self):
            if self.done < grp:
                self.state = chunk_step(self.gi * grp + self.done, self.state)
                self.done += 1

    n_groups = n_chunks // grp

    step_id = pl.program_id(0)
    last_step = pl.num_programs(0) - 1
    zero_state = jnp.zeros((HEAD_DIM, LANES), F32)

    @pl.when(step_id == 0)
    def _():
        for ref in (qt_ref, yloc_ref, bon_ref, mt_ref, nn_ref, pe_ref):
            ref[...] = jnp.zeros(ref.shape, ref.dtype)
        for _ in prep(0, 0):
            pass

    def run_group(gi, slot, state, nxt):
        chain = Chain(gi, state)
        for step, _ in enumerate(pass_a(gi * grp, slot)):
            if step in PREP_STEPS:
                next(nxt, None)
            chain.tick()
        assert chain.done == grp
        for _ in nxt:
            pass
        return chain.state

    def body(pair, state):
        state = run_group(2 * pair, 0, state, prep(2 * pair + 1, 1))
        return run_group(2 * pair + 1, 1, state, prep(2 * pair + 2, 0))

    assert n_groups % 2 == 0

    @pl.when(step_id < last_step)
    def _():
        state = lax.fori_loop(0, (n_groups - 2) // 2, body, zero_state)
        state = run_group(n_groups - 2, 0, state, prep(n_groups - 1, 1))
        run_group(n_groups - 1, 1, state, prep(0, 0, src_next))

    @pl.when(step_id == last_step)
    def _():
        def chain_only(gi, state):
            chain = Chain(gi, state)
            for _ in range(grp):
                chain.tick()
            return chain.state

        lax.fori_loop(0, n_groups, chain_only, zero_state)


def _attn_kernel(q_ref, k_ref, v_ref, z_ref, o_ref, kb_ref, vb_ref, acc_ref, m_ref, l_ref):
    t_len = q_ref.shape[1]
    blk = ATTN_BLOCK
    kb_ref[...] = k_ref[0].astype(BF16)
    vb_ref[...] = v_ref[0].astype(BF16)

    lane = lax.broadcasted_iota(jnp.int32, (blk, LANES), 1)
    is_h0 = lane < HEAD_DIM
    qi = lax.broadcasted_iota(jnp.int32, (blk, 2 * blk), 0)
    kj = lax.broadcasted_iota(jnp.int32, (blk, 2 * blk), 1)
    nb = ATTN_GROUP
    heads = (is_h0, ~is_h0)
    patterns = tuple(reversed(DILATION_PATTERNS))
    assert patterns[-1][1] == 1
    bias_causal = jnp.where(kj <= qi, 0.0, NEG_BIG).astype(F32)

    bodies = []
    for pi, (window, dil) in enumerate(patterns):
        span = window // dil
        nblk = t_len // (dil * blk)
        bias_band = jnp.where((qi + blk - kj >= 0) & (qi + blk - kj <= span), 0.0,
                              NEG_BIG).astype(F32)
        last = pi == len(patterns) - 1

        def body(it, pi=pi, dil=dil, nblk=nblk, bias_band=bias_band, last=last):
            rows, q, kw, vw, biases = [], [], [], [], []
            if nblk == 2 and nb % 2 == 0:
                for jr in range(nb // 2):
                    res = it * (nb // 2) + jr
                    seq = pl.ds(res, 2 * blk, stride=dil)
                    q_seq = q_ref[0, seq, :]
                    k_seq = k_ref[0, seq, :].astype(BF16)
                    v_seq = v_ref[0, seq, :].astype(BF16)
                    for n in range(2):
                        rows.append(pl.ds(res + dil * blk * n, blk, stride=dil))
                        q.append(q_seq[n * blk:(n + 1) * blk])
                        kw.append(k_seq)
                        vw.append(v_seq)
                        biases.append(bias_band if n else bias_causal)
            else:
                for j in range(nb):
                    res = (it * nb + j) // nblk
                    n = (it * nb + j) % nblk
                    maybe_first = j == 0 or nblk % nb != 0
                    n_prev = jnp.maximum(n - 1, 0) if maybe_first else n - 1
                    q0 = res + dil * blk * n
                    k0 = res + dil * blk * n_prev
                    if dil == 1:
                        rows.append(pl.ds(pl.multiple_of(q0, blk), blk))
                        krows = pl.ds(pl.multiple_of(k0, blk), 2 * blk)
                        kw.append(kb_ref[krows, :])
                        vw.append(vb_ref[krows, :])
                    else:
                        rows.append(pl.ds(q0, blk, stride=dil))
                        krows = pl.ds(k0, 2 * blk, stride=dil)
                        kw.append(k_ref[0, krows, :].astype(BF16))
                        vw.append(v_ref[0, krows, :].astype(BF16))
                    q.append(q_ref[0, rows[j], :])
                    biases.append(jnp.where(n == 0, bias_causal, bias_band) if maybe_first
                                  else bias_band)
            s = [[_dot_nt(jnp.where(h, q[j], 0.0).astype(BF16), kw[j]) + biases[j] for h in heads]
                 for j in range(nb)]
            m = [[jnp.max(x, axis=1, keepdims=True) for x in sj] for sj in s]
            p = [[jnp.exp2(s[j][h] - m[j][h]) for h in range(2)] for j in range(nb)]
            l = [[jnp.sum(x, axis=1, keepdims=True) for x in pj] for pj in p]
            pv = [[_dot(p[j][h].astype(BF16), vw[j]) for h in range(2)] for j in range(nb)]
            for j in range(nb):
                acc = jnp.where(is_h0, pv[j][0], pv[j][1])
                mj = jnp.where(is_h0, m[j][0], m[j][1])
                lj = jnp.where(is_h0, l[j][0], l[j][1])
                if last:
                    ms = [mj] + [m_ref[pp, rows[j], :] for pp in range(pi)]
                    ls = [lj] + [l_ref[pp, rows[j], :] for pp in range(pi)]
                    accs = [acc] + [acc_ref[pp, rows[j], :] for pp in range(pi)]
                    m_all = functools.reduce(jnp.maximum, ms)
                    ws = [jnp.exp2(mm - m_all) for mm in ms]
                    num = functools.reduce(jnp.add, [w * a for w, a in zip(ws, accs)])
                    den = functools.reduce(jnp.add, [w * x for w, x in zip(ws, ls)])
                    z = z_ref[0, rows[j], :]
                    o_ref[0, rows[j], :] = (num / den * (z * jax.nn.sigmoid(z))).astype(o_ref.dtype)
                else:
                    acc_ref[pi, rows[j], :] = acc
                    m_ref[pi, rows[j], :] = mj
                    l_ref[pi, rows[j], :] = lj

        bodies.append(body)

    unroll = ATTN_UNROLL

    def strided_body(trip, carry):
        for u in range(unroll):
            for body in bodies[:-1]:
                body(trip * unroll + u)
        return carry

    def contiguous_body(trip, carry):
        for u in range(unroll):
            bodies[-1](trip * unroll + u)
        return carry

    n_it = t_len // (blk * nb)
    assert n_it % unroll == 0
    lax.fori_loop(0, n_it // unroll, strided_body, 0)
    lax.fori_loop(0, n_it // unroll, contiguous_body, 0)


def _out_proj_kernel(ya_ref, yb_ref, wa_ref, wb_ref, x_ref, g_ref, o_ref):
    y = x_ref[...] + _dot(ya_ref[...], wa_ref[...]) + _dot(yb_ref[...], wb_ref[...])
    o_ref[...] = y * lax.rsqrt(jnp.mean(y * y, axis=-1, keepdims=True) + NORM_EPS) * g_ref[...]


def _rope_tables(t_len):
    half = ROPE_DIM // 2
    inv = ROPE_THETA ** (-jnp.arange(half, dtype=F32) * 2.0 / ROPE_DIM)
    ang = jnp.arange(t_len, dtype=jnp.int32).astype(F32)[:, None] * inv[None, :]
    cos, sin = jnp.cos(ang), jnp.sin(ang)
    ones = jnp.ones((t_len, HEAD_DIM - ROPE_DIM), F32)
    zeros = jnp.zeros((t_len, HEAD_DIM - ROPE_DIM), F32)
    zh = jnp.zeros((t_len, half), F32)
    tile = lambda a: jnp.tile(a, (1, LANES // HEAD_DIM))
    cos_t = tile(jnp.concatenate([cos, cos, ones], axis=1))
    sa_t = tile(jnp.concatenate([-sin, zh, zeros], axis=1))
    sb_t = tile(jnp.concatenate([zh, sin, zeros], axis=1))
    return cos_t, sa_t, sb_t


def kernel(x, norm_gain, w_in, shift_mix, decay_base, decay_up, iclr_base, iclr_up, key_norm_scale,
           key_iclr_mix, bonus, gn_gain, gn_bias, w_out, final_gain):
    bsz, t_len, d_model = x.shape
    depth = w_in.shape[0]
    c_a = decay_base.shape[1]
    n_hp = c_a // LANES
    shift_w = 3 * c_a + DECAY_RANK + ICLR_RANK
    in_w = w_in.shape[2]
    c_b = (in_w - shift_w - c_a) // 4
    assert c_a % LANES == 0 and c_b == c_a
    assert all(t_len % (2 * dil * ATTN_BLOCK) == 0 for _, dil in DILATION_PATTERNS)
    o_za = shift_w
    o_q = o_za + c_a
    o_k, o_v, o_zb = o_q + c_b, o_q + 2 * c_b, o_q + 3 * c_b
    segs = ((0, c_a), (c_a, 2 * c_a), (2 * c_a, 3 * c_a), (3 * c_a, shift_w), (o_za, o_q),
            (o_q, o_k), (o_k, o_v), (o_v, o_zb), (o_zb, in_w))
    cos_t, sa_t, sb_t = _rope_tables(t_len)
    rows = bsz * t_len
    tm = IN_PROJ_ROWS
    tiles_per_seq = t_len // tm
    row2 = lambda a: a.reshape(1, -1).astype(F32)

    for layer in range(depth):
        p = pl.pallas_call(
            functools.partial(_in_proj_kernel, segs=segs, shift_w=shift_w, rope_lo=o_q, q_hi=o_k,
                              rope_hi=o_v, tiles_per_seq=tiles_per_seq),
            grid=(rows // tm,),
            in_specs=[
                pl.BlockSpec((tm, d_model), lambda i: (i, 0)),
                pl.BlockSpec((1, d_model), lambda i: (0, 0)),
                pl.BlockSpec((d_model, in_w), lambda i: (0, 0), pipeline_mode=pl.Buffered(1)),
                pl.BlockSpec((1, shift_w), lambda i: (0, 0)),
                pl.BlockSpec((tm, LANES), lambda i: (i % tiles_per_seq, 0)),
                pl.BlockSpec((tm, LANES), lambda i: (i % tiles_per_seq, 0)),
                pl.BlockSpec((tm, LANES), lambda i: (i % tiles_per_seq, 0)),
            ],
            out_specs=pl.BlockSpec((tm, in_w), lambda i: (i, 0)),
            out_shape=jax.ShapeDtypeStruct((rows, in_w), F32),
            scratch_shapes=[pltpu.VMEM((SUBLANES, shift_w), F32)],
            compiler_params=pltpu.CompilerParams(dimension_semantics=("arbitrary",),
                                                 vmem_limit_bytes=VMEM_LIMIT),
            name="in_proj",
        )(x.reshape(rows, d_model), row2(norm_gain[layer]), w_in[layer].astype(BF16),
          row2(shift_mix[layer]), cos_t, sa_t, sb_t)
        p = p.reshape(bsz, t_len, in_w)

        zeros_up = jnp.zeros((DECAY_RANK, c_a), F32)
        up_full = jnp.concatenate(
            [jnp.concatenate([decay_up[layer].astype(F32), zeros_up], axis=0).reshape(
                DECAY_RANK + ICLR_RANK, n_hp, 1, LANES),
             jnp.concatenate([zeros_up, iclr_up[layer].astype(F32)], axis=0).reshape(
                 DECAY_RANK + ICLR_RANK, n_hp, 1, LANES)], axis=2).reshape(
                     DECAY_RANK + ICLR_RANK, n_hp * 2 * LANES)
        col = lambda off: (lambda b, h: (b, 0, off // LANES + h))
        seq_spec = lambda off: pl.BlockSpec((1, t_len, LANES), col(off))
        n_pairs = bsz * n_hp
        g_rows = RWKV_GROUP * CHUNK
        cur = lambda s: jnp.minimum(s, n_pairs - 1)
        nxt = lambda s: jnp.minimum(s + 1, n_pairs - 1)
        lag = lambda s: jnp.maximum(s - 1, 0)

        def pair_specs(pick, n_rows):
            seq = lambda off: pl.BlockSpec(
                (1, n_rows, LANES), lambda s: (pick(s) // n_hp, 0, off // LANES + pick(s) % n_hp))
            shared = pl.BlockSpec((1, n_rows, LANES),
                                  lambda s: (pick(s) // n_hp, 0, 3 * c_a // LANES))
            par = pl.BlockSpec((1, LANES), lambda s: (0, pick(s) % n_hp))
            up = pl.BlockSpec((DECAY_RANK + ICLR_RANK, 2 * LANES), lambda s: (0, pick(s) % n_hp))
            return seq, shared, par, up

        cur_seq, cur_wa, cur_par, cur_up = pair_specs(cur, t_len)
        nxt_seq, nxt_wa, nxt_par, nxt_up = pair_specs(nxt, g_rows)
        lag_seq, _, lag_par, _ = pair_specs(lag, t_len)
        rwkv_params = (up_full, row2(decay_base[layer]), row2(iclr_base[layer]),
                       row2(key_norm_scale[layer]), row2(key_iclr_mix[layer]), row2(bonus[layer]))
        ya = pl.pallas_call(
            _rwkv_kernel,
            grid=(n_pairs + 1,),
            in_specs=[cur_seq(0), cur_seq(c_a), cur_seq(2 * c_a), cur_wa, cur_up] + [cur_par] * 5
                     + [nxt_seq(0), nxt_seq(c_a), nxt_seq(2 * c_a), nxt_wa, nxt_up] + [nxt_par] * 5
                     + [lag_seq(o_za), lag_par, lag_par],
            out_specs=lag_seq(0),
            out_shape=jax.ShapeDtypeStruct((bsz, t_len, c_a), BF16),
            scratch_shapes=_rwkv_scratch(t_len),
            compiler_params=pltpu.CompilerParams(dimension_semantics=("arbitrary",),
                                                 vmem_limit_bytes=VMEM_LIMIT),
            name="rwkv",
        )(p, p, p, p, *rwkv_params, p, p, p, p, *rwkv_params, p,
          row2(gn_gain[layer]), row2(gn_bias[layer]))

        yb = pl.pallas_call(
            _attn_kernel,
            grid=(bsz, c_b // LANES),
            in_specs=[seq_spec(o_q), seq_spec(o_k), seq_spec(o_v), seq_spec(o_zb)],
            out_specs=pl.BlockSpec((1, t_len, LANES), lambda b, h: (b, 0, h)),
            out_shape=jax.ShapeDtypeStruct((bsz, t_len, c_b), BF16),
            scratch_shapes=[pltpu.VMEM((t_len, LANES), BF16),
                            pltpu.VMEM((t_len, LANES), BF16),
                            pltpu.VMEM((len(DILATION_PATTERNS) - 1, t_len, LANES), F32),
                            pltpu.VMEM((len(DILATION_PATTERNS) - 1, t_len, LANES), F32),
                            pltpu.VMEM((len(DILATION_PATTERNS) - 1, t_len, LANES), F32)],
            compiler_params=pltpu.CompilerParams(dimension_semantics=("arbitrary", "arbitrary"),
                                                 vmem_limit_bytes=VMEM_LIMIT),
            name="attention",
        )(p, p, p, p)

        assert depth == 1
        tmo = OUT_PROJ_ROWS
        wo = w_out[layer].astype(BF16)
        x = pl.pallas_call(
            _out_proj_kernel,
            grid=(rows // tmo,),
            in_specs=[pl.BlockSpec((tmo, c_a), lambda i: (i, 0)),
                      pl.BlockSpec((tmo, c_b), lambda i: (i, 0)),
                      pl.BlockSpec((c_a, d_model), lambda i: (0, 0)),
                      pl.BlockSpec((c_b, d_model), lambda i: (0, 0)),
                      pl.BlockSpec((tmo, d_model), lambda i: (i, 0)),
                      pl.BlockSpec((1, d_model), lambda i: (0, 0))],
            out_specs=pl.BlockSpec((tmo, d_model), lambda i: (i, 0)),
            out_shape=jax.ShapeDtypeStruct((rows, d_model), F32),
            compiler_params=pltpu.CompilerParams(dimension_semantics=("arbitrary",),
                                                 vmem_limit_bytes=VMEM_LIMIT),
            name="out_proj",
        )(ya.reshape(rows, c_a), yb.reshape(rows, c_b), wo[:c_a], wo[c_a:],
          x.reshape(rows, d_model), row2(final_gain)).reshape(bsz, t_len, d_model)
    return x
```

```python
import functools

import jax
import jax.numpy as jnp
from jax import lax
from jax.experimental import pallas as pl
from jax.experimental.pallas import tpu as pltpu

HEAD_DIM = 64
LANES = 128
SUBLANES = 8
DECAY_RANK = 64
ICLR_RANK = 64
DILATION_PATTERNS = ((128, 1), (512, 4), (2048, 16))
ATTN_BLOCK = 128
ATTN_GROUP = 4
ATTN_UNROLL = 8
ROPE_THETA = 500000.0
ROPE_DIM = HEAD_DIM // 4
NORM_EPS = 1e-6
GN_EPS = 64e-5
CHUNK = 64
SUB = 16
PACK = LANES // (2 * SUB)
RWKV_GROUP = 8
PREP_STEPS = (2, 4, 6)
IN_PROJ_ROWS = 1024
OUT_PROJ_ROWS = 2048
DECAY_SCALE = 0.6065306597126334
LOG2_E = 1.4426950408889634
ATTN_Q_SCALE = HEAD_DIM ** -0.5 * LOG2_E
NEG_BIG = -1e30
VMEM_LIMIT = 56 * 1024 * 1024

F32 = jnp.float32
BF16 = jnp.bfloat16


def _dot(a, b, **kw):
    return jnp.dot(a, b, preferred_element_type=F32, **kw)


def _dot_nt(a, b):
    return lax.dot_general(a, b, (((1,), (1,)), ((), ())), preferred_element_type=F32)


def _split2(x):
    hi = x.astype(BF16)
    return hi, (x - hi.astype(F32)).astype(BF16)


def _dot_tn(a, b):
    return lax.dot_general(a, b, (((0,), (0,)), ((), ())), preferred_element_type=F32)


def _in_proj_kernel(x_ref, g_ref, w_ref, mix_ref, cos_ref, sa_ref, sb_ref, o_ref, carry_ref,
                    *, segs, shift_w, rope_lo, q_hi, rope_hi, tiles_per_seq):
    i = pl.program_id(0)

    @pl.when(i == 0)
    def _():
        carry_ref[...] = jnp.zeros(carry_ref.shape, F32)

    x = x_ref[...]
    h = x * lax.rsqrt(jnp.mean(x * x, axis=-1, keepdims=True) + NORM_EPS) * g_ref[...]
    hb = h.astype(BF16)
    tm = x.shape[0]
    first = (i % tiles_per_seq) == 0
    row0 = lax.broadcasted_iota(jnp.int32, (tm, 1), 0) == 0
    for lo, hi in segs:
        p = _dot(hb, w_ref[:, lo:hi])
        if hi <= shift_w:
            old = jnp.where(first, 0.0, carry_ref[0:1, lo:hi])
            carry_ref[0:1, lo:hi] = p[tm - 1:tm, :]
            prev = jnp.where(row0, old, pltpu.roll(p, 1, axis=0))
            p = p + (prev - p) * mix_ref[:, lo:hi]
            o_ref[:, lo:hi] = p
        elif lo >= rope_lo and hi <= rope_hi:
            if hi <= q_hi:
                p = p * ATTN_Q_SCALE
            for j in range(lo, hi, LANES):
                t = p[:, j - lo:j - lo + LANES]
                t = (t * cos_ref[...] + pltpu.roll(t, LANES - ROPE_DIM // 2, axis=1) * sa_ref[...]
                     + pltpu.roll(t, ROPE_DIM // 2, axis=1) * sb_ref[...])
                o_ref[:, j:j + LANES] = t
        else:
            o_ref[:, lo:hi] = p


def _spread_matrix():
    j = lax.broadcasted_iota(jnp.int32, (LANES, (SUB - 1) * LANES), 0)
    col = lax.broadcasted_iota(jnp.int32, (LANES, (SUB - 1) * LANES), 1)
    sel = j == SUB * ((col % LANES) // SUB) + col // LANES
    return jnp.where(sel, 1.0, 0.0).astype(BF16)


def _fold_matrices():
    r = lax.broadcasted_iota(jnp.int32, (LANES, LANES), 0)
    l = lax.broadcasted_iota(jnp.int32, (LANES, LANES), 1)
    per_chunk = LANES // PACK
    match = ((l % per_chunk) // SUB == r // HEAD_DIM) & (l % SUB == r % SUB)
    fold = [jnp.where(match & (l // per_chunk == c), 1.0, 0.0).astype(BF16) for c in range(PACK)]
    rr = lax.broadcasted_iota(jnp.int32, (LANES, PACK * LANES), 0)
    cc = lax.broadcasted_iota(jnp.int32, (LANES, PACK * LANES), 1)
    src = per_chunk * (cc // LANES) + SUB * ((cc % LANES) // HEAD_DIM) + cc % SUB
    unfold = jnp.where(rr == src, 1.0, 0.0).astype(BF16)
    return fold, unfold


def _substitute(coef):
    rows = coef.shape[0]
    ng = rows // SUBLANES
    gpb = SUB // SUBLANES
    row = lax.broadcasted_iota(jnp.int32, (SUBLANES, LANES), 0)
    col = lax.broadcasted_iota(jnp.int32, (SUBLANES, LANES), 1) % SUB
    xs = [jnp.where(col == (row + g * SUBLANES) % SUB, 1.0, 0.0).astype(F32) for g in range(ng)]
    for ss in range(SUB - 1):
        gs, rs = divmod(ss, SUBLANES)
        for blk in range(ng // gpb):
            xrow = xs[blk * gpb + gs][rs:rs + 1, :]
            for g in range(blk * gpb + (ss + 1) // SUBLANES, (blk + 1) * gpb):
                cf = coef[g * SUBLANES:(g + 1) * SUBLANES, ss * LANES:(ss + 1) * LANES]
                xs[g] = xs[g] - cf * xrow
    return jnp.concatenate(xs, axis=0)


def _rwkv_scratch(t_len):
    n_chunks = t_len // CHUNK
    return [pltpu.VMEM((LANES, (SUB - 1) * LANES), BF16),
            pltpu.VMEM((PACK * LANES, LANES), BF16),
            pltpu.VMEM((LANES, PACK * LANES), BF16),
            pltpu.VMEM((t_len, LANES), BF16),
            pltpu.VMEM((t_len, LANES), F32),
            pltpu.VMEM((t_len, LANES), F32),
            pltpu.VMEM((n_chunks, LANES, LANES), BF16),
            pltpu.VMEM((n_chunks, HEAD_DIM, LANES), F32),
            pltpu.VMEM((n_chunks, SUBLANES, LANES), F32),
            pltpu.VMEM((2, RWKV_GROUP, 2 * CHUNK, LANES), BF16),
            pltpu.VMEM((2, RWKV_GROUP, 4 * CHUNK, LANES), BF16),
            pltpu.VMEM((2, RWKV_GROUP, 2 * CHUNK, LANES), BF16),
            pltpu.VMEM((2, RWKV_GROUP, 2 * CHUNK, LANES), BF16),
            pltpu.VMEM((2, RWKV_GROUP * CHUNK, LANES), F32),
            pltpu.VMEM((2, RWKV_GROUP * CHUNK, LANES), BF16),
            pltpu.VMEM((2, RWKV_GROUP * CHUNK, LANES), BF16),
            pltpu.VMEM((2, RWKV_GROUP * CHUNK, LANES), BF16),
            pltpu.VMEM((2, RWKV_GROUP * CHUNK, LANES), F32),
            pltpu.VMEM((2, RWKV_GROUP, SUBLANES, LANES), F32)]


def _rwkv_kernel(r_ref, k_ref, v_ref, wa_ref, up_ref, dbase_ref, abase_ref, kns_ref, kim_ref,
                 bonus_ref, rn_ref, kn_ref, vn_ref, wan_ref, upn_ref, dbasen_ref, abasen_ref,
                 knsn_ref, kimn_ref, bonusn_ref, z_ref, gng_ref, gnb_ref, o_ref,
                 spread_ref, fold_ref, unfold_ref, qt_ref, yloc_ref, bon_ref, mt_ref, nn_ref,
                 pe_ref, lhs_ref, rhs_ref, vbd_ref, kbd_ref, rd_ref, vb_ref, kdl_ref, bdl_ref,
                 bonr_ref, per_ref):
    t_len = r_ref.shape[1]
    c = CHUNK
    grp = RWKV_GROUP
    rows = grp * c
    n_chunks = t_len // c
    lane = lax.broadcasted_iota(jnp.int32, (c, LANES), 1)
    is_h0 = lane < HEAD_DIM
    assert grp % PACK == 0
    trow = lax.broadcasted_iota(jnp.int32, (c, LANES), 0)
    scol = lane % HEAD_DIM
    tb, sb = trow // SUB, scol // SUB
    strict = trow > scol
    incl = trow >= scol
    in_block = tb == sb
    ltri = (lax.broadcasted_iota(jnp.int32, (c, c), 0)
            >= lax.broadcasted_iota(jnp.int32, (c, c), 1)).astype(BF16)
    sq_r = lax.broadcasted_iota(jnp.int32, (LANES, LANES), 0)
    sq_c = lax.broadcasted_iota(jnp.int32, (LANES, LANES), 1)
    same_head = sq_r // HEAD_DIM == sq_c // HEAD_DIM

    gng, gnb = gng_ref[...], gnb_ref[...]

    def prep_source(refs, up, dbase, abase, kns, kim, bonus):
        up_hi, up_lo = _split2(up[...])
        return (refs, jnp.concatenate([up_hi, up_hi], axis=0), up_lo,
                dbase[...], abase[...], kns[...], kim[...], bonus[...])

    src_cur = prep_source((r_ref, k_ref, v_ref, wa_ref), up_ref, dbase_ref, abase_ref, kns_ref,
                          kim_ref, bonus_ref)
    src_next = prep_source((rn_ref, kn_ref, vn_ref, wan_ref), upn_ref, dbasen_ref, abasen_ref,
                           knsn_ref, kimn_ref, bonusn_ref)

    def head_sum(x):
        h0 = lax.broadcasted_iota(jnp.int32, x.shape, 1) < HEAD_DIM
        s0 = jnp.sum(jnp.where(h0, x, 0.0), axis=1, keepdims=True)
        s1 = jnp.sum(jnp.where(h0, 0.0, x), axis=1, keepdims=True)
        return jnp.where(h0, s0, s1)

    def block_diag(x):
        zero = jnp.zeros_like(x)
        return jnp.concatenate([jnp.where(is_h0, x, zero), jnp.where(is_h0, zero, x)],
                               axis=0).astype(BF16)

    n = grp
    js = range(n)
    cs = [slice(j * c, (j + 1) * c) for j in js]

    def prep(gi, slot, src=src_cur):
        (rs_ref, ks_ref, vs_ref, was_ref), up_hh, up_lo, dbase, abase, kns, kim, bonus = src
        sl = pl.ds(pl.multiple_of(gi * n * c, n * c), n * c)
        r = rs_ref[0, sl, :]
        k = ks_ref[0, sl, :]
        v = vs_ref[0, sl, :]
        wa = was_ref[0, sl, :]
        is_h0_s = lax.broadcasted_iota(jnp.int32, (n * c, LANES), 1) < HEAD_DIM
        x_hi, x_lo = _split2(jnp.where(is_h0_s, jnp.tanh(wa), wa))
        lin = _dot(jnp.concatenate([x_hi, x_lo], axis=1), up_hh) + _dot(x_hi, up_lo)
        yield
        logw = -DECAY_SCALE * jax.nn.sigmoid(dbase + lin[:, :LANES])
        a = jax.nn.sigmoid(abase + lin[:, LANES:])
        kk = k * kns
        kk = kk * lax.rsqrt(jnp.maximum(head_sum(kk * kk), 1e-24))
        kmod = k * (1.0 + (a - 1.0) * kim)
        b = kk * a
        bonr_ref[slot] = head_sum(r * kmod * bonus) * v

        parts = jnp.concatenate(_split2(logw), axis=1)
        gs = [_dot(ltri, parts[cs[j]]) for j in js]
        yield
        g = jnp.concatenate([x[:, LANES:] + x[:, :LANES] for x in gs], axis=0)
        p_end = [jnp.exp(g[(j + 1) * c - 1:(j + 1) * c, :]) for j in js]
        e_inv = jnp.exp(-g)
        kkd = kk * jnp.exp(g - logw)
        rd = r * jnp.exp(g)
        bi = b * e_inv
        ki = kmod * e_inv
        e_end = e_inv * jnp.concatenate([jnp.broadcast_to(x, (c, LANES)) for x in p_end], axis=0)
        rd_ref[slot] = rd
        vb_ref[slot] = v.astype(BF16)
        kdl_ref[slot] = (kmod * e_end).astype(BF16)
        bdl_ref[slot] = (b * e_end).astype(BF16)
        for j in js:
            lhs_ref[slot, j] = jnp.concatenate([kkd[cs[j]], rd[cs[j]]], axis=0).astype(BF16)
            rhs_ref[slot, j] = jnp.concatenate([block_diag(bi[cs[j]]), block_diag(ki[cs[j]])],
                                               axis=0)
            vbd_ref[slot, j] = block_diag(v[cs[j]])
            kbd_ref[slot, j] = block_diag(kkd[cs[j]])
            per_ref[slot, j] = jnp.broadcast_to(p_end[j], (SUBLANES, LANES))

    def pass_a(first_chunk, slot):
        aab, aak, arb, ark = [], [], [], []
        for j in js:
            pair = _dot_nt(lhs_ref[slot, j], rhs_ref[slot, j])
            aab.append(jnp.where(strict, pair[:c, :LANES], 0.0))
            aak.append(jnp.where(strict, pair[:c, LANES:], 0.0))
            arb.append(jnp.where(incl, pair[c:, :LANES], 0.0))
            ark.append(jnp.where(incl, pair[c:, LANES:], 0.0))
        yield
        akv = [_dot(jnp.concatenate([aak[j], ark[j]], axis=0).astype(BF16), vbd_ref[slot, j])
               for j in js]
        aakv = [x[:c] for x in akv]
        arkv = [x[c:] for x in akv]
        n_diag = jnp.concatenate(
            [jnp.concatenate([jnp.where(in_block, aab[q * PACK + i], 0.0).astype(BF16)
                              for i in range(PACK)], axis=1)
             for q in range(n // PACK)], axis=0)
        compact = _dot(n_diag, fold_ref[...])
        yield
        coef = _dot(compact.astype(BF16), spread_ref[...])
        t_diag = _substitute(coef)
        yield
        spread_out = _dot(t_diag.astype(BF16), unfold_ref[...])
        tinv = [jnp.where(in_block, spread_out[(j // PACK) * c:(j // PACK + 1) * c,
                                               (j % PACK) * LANES:(j % PACK + 1) * LANES], 0.0)
                for j in js]
        yield
        span = 1
        while span * SUB < c:
            low_mask = (tb // span == sb // span + 1) & ((tb // span) % 2 == 1)
            inner = [_dot(jnp.where(low_mask, aab[j], 0.0).astype(BF16),
                          block_diag(tinv[j])) for j in js]
            yield
            tinv = [tinv[j] - _dot(tinv[j].astype(BF16), block_diag(inner[j])) for j in js]
            yield
            span *= 2
        wu = [_dot(tinv[j].astype(BF16),
                   jnp.concatenate([kbd_ref[slot, j], block_diag(aakv[j])], axis=1))
              for j in js]
        yield
        w = [x[:, :LANES] for x in wu]
        uloc = [x[:, LANES:] for x in wu]
        arb_b = [arb[j].astype(BF16) for j in js]
        bdl_b = [bdl_ref[slot, cs[j], :] for j in js]
        awu = [_dot(arb_b[j], jnp.concatenate([block_diag(w[j]), block_diag(uloc[j])], axis=1))
               for j in js]
        qt = [rd_ref[slot, cs[j], :] - awu[j][:, :LANES] for j in js]
        yloc = [arkv[j] - awu[j][:, LANES:] for j in js]
        wtb = [_dot_tn(w[j].astype(BF16), bdl_b[j]) for j in js]
        nn = [_dot_tn(jnp.concatenate([vb_ref[slot, cs[j], :], (-uloc[j]).astype(BF16)], axis=0),
                      jnp.concatenate([kdl_ref[slot, cs[j], :], bdl_b[j]], axis=0))
              for j in js]
        yield
        for j in js:
            ci = first_chunk + j
            csl = pl.ds(pl.multiple_of(ci * c, c), c)
            qt_ref[csl, :] = qt[j].astype(BF16)
            yloc_ref[csl, :] = yloc[j]
            mt_ref[ci] = jnp.where(same_head, wtb[j], 0.0).astype(BF16)
            nn_ref[ci] = jnp.where(is_h0, nn[j][:HEAD_DIM], nn[j][HEAD_DIM:])
            pe_ref[ci] = per_ref[slot, j]
        gsl = pl.ds(pl.multiple_of(first_chunk * c, n * c), n * c)
        bon_ref[gsl, :] = bonr_ref[slot]

    def chunk_step(ci, state):
        sb = state.astype(BF16)
        sl = pl.ds(pl.multiple_of(ci * c, c), c)
        y = _dot_nt(qt_ref[sl, :], block_diag(sb)) + yloc_ref[sl, :]
        new_state = state * pe_ref[ci, 0:1, :] + nn_ref[ci] - _dot(sb, mt_ref[ci])
        mu = head_sum(y) * (1.0 / HEAD_DIM)
        d = y - mu
        var = head_sum(d * d) * (1.0 / HEAD_DIM)
        yn = d * lax.rsqrt(var + GN_EPS) * gng + gnb + bon_ref[sl, :]
        z = z_ref[0, sl, :]
        o_ref[0, sl, :] = (yn * (z * jax.nn.sigmoid(z))).astype(o_ref.dtype)
        return new_state

    class Chain:
        def __init__(self, gi, state):
            self.gi, self.state, self.done = gi, state, 0

        def tick(self):
            if self.done < grp:
                self.state = chunk_step(self.gi * grp + self.done, self.state)
                self.done += 1

    n_groups = n_chunks // grp

    step_id = pl.program_id(0)
    last_step = pl.num_programs(0) - 1
    zero_state = jnp.zeros((HEAD_DIM, LANES), F32)

    @pl.when(step_id == 0)
    def _():
        spread_ref[...] = _spread_matrix()
        fold, unfold = _fold_matrices()
        fold_ref[...] = jnp.concatenate(fold, axis=0)
        unfold_ref[...] = unfold
        for ref in (qt_ref, yloc_ref, bon_ref, mt_ref, nn_ref, pe_ref):
            ref[...] = jnp.zeros(ref.shape, ref.dtype)
        for _ in prep(0, 0):
            pass

    def run_group(gi, slot, state, nxt):
        chain = Chain(gi, state)
        for step, _ in enumerate(pass_a(gi * grp, slot)):
            if step in PREP_STEPS:
                next(nxt, None)
            chain.tick()
        assert chain.done == grp
        for _ in nxt:
            pass
        return chain.state

    def body(pair, state):
        state = run_group(2 * pair, 0, state, prep(2 * pair + 1, 1))
        return run_group(2 * pair + 1, 1, state, prep(2 * pair + 2, 0))

    assert n_groups % 2 == 0

    @pl.when(step_id < last_step)
    def _():
        state = lax.fori_loop(0, (n_groups - 2) // 2, body, zero_state)
        state = run_group(n_groups - 2, 0, state, prep(n_groups - 1, 1))
        run_group(n_groups - 1, 1, state, prep(0, 0, src_next))

    @pl.when(step_id == last_step)
    def _():
        def chain_only(gi, state):
            chain = Chain(gi, state)
            for _ in range(grp):
                chain.tick()
            return chain.state

        lax.fori_loop(0, n_groups, chain_only, zero_state)


def _attn_kernel(q_ref, k_ref, v_ref, z_ref, o_ref, kb_ref, vb_ref, acc_ref, m_ref, l_ref):
    t_len = q_ref.shape[1]
    blk = ATTN_BLOCK
    kb_ref[...] = k_ref[0].astype(BF16)
    vb_ref[...] = v_ref[0].astype(BF16)

    lane = lax.broadcasted_iota(jnp.int32, (blk, LANES), 1)
    is_h0 = lane < HEAD_DIM
    qi = lax.broadcasted_iota(jnp.int32, (blk, 2 * blk), 0)
    kj = lax.broadcasted_iota(jnp.int32, (blk, 2 * blk), 1)
    nb = ATTN_GROUP
    heads = (is_h0, ~is_h0)
    patterns = tuple(reversed(DILATION_PATTERNS))
    assert patterns[-1][1] == 1
    bias_causal = jnp.where(kj <= qi, 0.0, NEG_BIG).astype(F32)

    bodies = []
    for pi, (window, dil) in enumerate(patterns):
        span = window // dil
        nblk = t_len // (dil * blk)
        bias_band = jnp.where((qi + blk - kj >= 0) & (qi + blk - kj <= span), 0.0,
                              NEG_BIG).astype(F32)
        last = pi == len(patterns) - 1

        def body(it, pi=pi, dil=dil, nblk=nblk, bias_band=bias_band, last=last):
            rows, q, kw, vw, biases = [], [], [], [], []
            if nblk == 2 and nb % 2 == 0:
                for jr in range(nb // 2):
                    res = it * (nb // 2) + jr
                    seq = pl.ds(res, 2 * blk, stride=dil)
                    q_seq = q_ref[0, seq, :]
                    k_seq = k_ref[0, seq, :].astype(BF16)
                    v_seq = v_ref[0, seq, :].astype(BF16)
                    for n in range(2):
                        rows.append(pl.ds(res + dil * blk * n, blk, stride=dil))
                        q.append(q_seq[n * blk:(n + 1) * blk])
                        kw.append(k_seq)
                        vw.append(v_seq)
                        biases.append(bias_band if n else bias_causal)
            else:
                for j in range(nb):
                    res = (it * nb + j) // nblk
                    n = (it * nb + j) % nblk
                    maybe_first = j == 0 or nblk % nb != 0
                    n_prev = jnp.maximum(n - 1, 0) if maybe_first else n - 1
                    q0 = res + dil * blk * n
                    k0 = res + dil * blk * n_prev
                    if dil == 1:
                        rows.append(pl.ds(pl.multiple_of(q0, blk), blk))
                        krows = pl.ds(pl.multiple_of(k0, blk), 2 * blk)
                        kw.append(kb_ref[krows, :])
                        vw.append(vb_ref[krows, :])
                    else:
                        rows.append(pl.ds(q0, blk, stride=dil))
                        krows = pl.ds(k0, 2 * blk, stride=dil)
                        kw.append(k_ref[0, krows, :].astype(BF16))
                        vw.append(v_ref[0, krows, :].astype(BF16))
                    q.append(q_ref[0, rows[j], :])
                    biases.append(jnp.where(n == 0, bias_causal, bias_band) if maybe_first
                                  else bias_band)
            s = [[_dot_nt(jnp.where(h, q[j], 0.0).astype(BF16), kw[j]) + biases[j] for h in heads]
                 for j in range(nb)]
            m = [[jnp.max(x, axis=1, keepdims=True) for x in sj] for sj in s]
            p = [[jnp.exp2(s[j][h] - m[j][h]) for h in range(2)] for j in range(nb)]
            l = [[jnp.sum(x, axis=1, keepdims=True) for x in pj] for pj in p]
            pv = [[_dot(p[j][h].astype(BF16), vw[j]) for h in range(2)] for j in range(nb)]
            for j in range(nb):
                acc = jnp.where(is_h0, pv[j][0], pv[j][1])
                mj = jnp.where(is_h0, m[j][0], m[j][1])
                lj = jnp.where(is_h0, l[j][0], l[j][1])
                if last:
                    ms = [mj] + [m_ref[pp, rows[j], :] for pp in range(pi)]
                    ls = [lj] + [l_ref[pp, rows[j], :] for pp in range(pi)]
                    accs = [acc] + [acc_ref[pp, rows[j], :] for pp in range(pi)]
                    m_all = functools.reduce(jnp.maximum, ms)
                    ws = [jnp.exp2(mm - m_all) for mm in ms]
                    num = functools.reduce(jnp.add, [w * a for w, a in zip(ws, accs)])
                    den = functools.reduce(jnp.add, [w * x for w, x in zip(ws, ls)])
                    z = z_ref[0, rows[j], :]
                    o_ref[0, rows[j], :] = (num / den * (z * jax.nn.sigmoid(z))).astype(o_ref.dtype)
                else:
                    acc_ref[pi, rows[j], :] = acc
                    m_ref[pi, rows[j], :] = mj
                    l_ref[pi, rows[j], :] = lj

        bodies.append(body)

    unroll = ATTN_UNROLL

    def strided_body(trip, carry):
        for u in range(unroll):
            for body in bodies[:-1]:
                body(trip * unroll + u)
        return carry

    def contiguous_body(trip, carry):
        for u in range(unroll):
            bodies[-1](trip * unroll + u)
        return carry

    n_it = t_len // (blk * nb)
    assert n_it % unroll == 0
    lax.fori_loop(0, n_it // unroll, strided_body, 0)
    lax.fori_loop(0, n_it // unroll, contiguous_body, 0)


def _out_proj_kernel(ya_ref, yb_ref, wa_ref, wb_ref, x_ref, g_ref, o_ref):
    y = x_ref[...] + _dot(ya_ref[...], wa_ref[...]) + _dot(yb_ref[...], wb_ref[...])
    o_ref[...] = y * lax.rsqrt(jnp.mean(y * y, axis=-1, keepdims=True) + NORM_EPS) * g_ref[...]


def _rope_tables(t_len):
    half = ROPE_DIM // 2
    inv = ROPE_THETA ** (-jnp.arange(half, dtype=F32) * 2.0 / ROPE_DIM)
    ang = jnp.arange(t_len, dtype=jnp.int32).astype(F32)[:, None] * inv[None, :]
    cos, sin = jnp.cos(ang), jnp.sin(ang)
    ones = jnp.ones((t_len, HEAD_DIM - ROPE_DIM), F32)
    zeros = jnp.zeros((t_len, HEAD_DIM - ROPE_DIM), F32)
    zh = jnp.zeros((t_len, half), F32)
    tile = lambda a: jnp.tile(a, (1, LANES // HEAD_DIM))
    cos_t = tile(jnp.concatenate([cos, cos, ones], axis=1))
    sa_t = tile(jnp.concatenate([-sin, zh, zeros], axis=1))
    sb_t = tile(jnp.concatenate([zh, sin, zeros], axis=1))
    return cos_t, sa_t, sb_t


def kernel(x, norm_gain, w_in, shift_mix, decay_base, decay_up, iclr_base, iclr_up, key_norm_scale,
           key_iclr_mix, bonus, gn_gain, gn_bias, w_out, final_gain):
    bsz, t_len, d_model = x.shape
    depth = w_in.shape[0]
    c_a = decay_base.shape[1]
    n_hp = c_a // LANES
    shift_w = 3 * c_a + DECAY_RANK + ICLR_RANK
    in_w = w_in.shape[2]
    c_b = (in_w - shift_w - c_a) // 4
    assert c_a % LANES == 0 and c_b == c_a
    assert all(t_len % (2 * dil * ATTN_BLOCK) == 0 for _, dil in DILATION_PATTERNS)
    o_za = shift_w
    o_q = o_za + c_a
    o_k, o_v, o_zb = o_q + c_b, o_q + 2 * c_b, o_q + 3 * c_b
    segs = ((0, c_a), (c_a, 2 * c_a), (2 * c_a, 3 * c_a), (3 * c_a, shift_w), (o_za, o_q),
            (o_q, o_k), (o_k, o_v), (o_v, o_zb), (o_zb, in_w))
    cos_t, sa_t, sb_t = _rope_tables(t_len)
    rows = bsz * t_len
    tm = IN_PROJ_ROWS
    tiles_per_seq = t_len // tm
    row2 = lambda a: a.reshape(1, -1).astype(F32)

    for layer in range(depth):
        p = pl.pallas_call(
            functools.partial(_in_proj_kernel, segs=segs, shift_w=shift_w, rope_lo=o_q, q_hi=o_k,
                              rope_hi=o_v, tiles_per_seq=tiles_per_seq),
            grid=(rows // tm,),
            in_specs=[
                pl.BlockSpec((tm, d_model), lambda i: (i, 0)),
                pl.BlockSpec((1, d_model), lambda i: (0, 0)),
                pl.BlockSpec((d_model, in_w), lambda i: (0, 0), pipeline_mode=pl.Buffered(1)),
                pl.BlockSpec((1, shift_w), lambda i: (0, 0)),
                pl.BlockSpec((tm, LANES), lambda i: (i % tiles_per_seq, 0)),
                pl.BlockSpec((tm, LANES), lambda i: (i % tiles_per_seq, 0)),
                pl.BlockSpec((tm, LANES), lambda i: (i % tiles_per_seq, 0)),
            ],
            out_specs=pl.BlockSpec((tm, in_w), lambda i: (i, 0)),
            out_shape=jax.ShapeDtypeStruct((rows, in_w), F32),
            scratch_shapes=[pltpu.VMEM((SUBLANES, shift_w), F32)],
            compiler_params=pltpu.CompilerParams(dimension_semantics=("arbitrary",),
                                                 vmem_limit_bytes=VMEM_LIMIT),
            name="in_proj",
        )(x.reshape(rows, d_model), row2(norm_gain[layer]), w_in[layer].astype(BF16),
          row2(shift_mix[layer]), cos_t, sa_t, sb_t)
        p = p.reshape(bsz, t_len, in_w)

        zeros_up = jnp.zeros((DECAY_RANK, c_a), F32)
        up_full = jnp.concatenate(
            [jnp.concatenate([decay_up[layer].astype(F32), zeros_up], axis=0).reshape(
                DECAY_RANK + ICLR_RANK, n_hp, 1, LANES),
             jnp.concatenate([zeros_up, iclr_up[layer].astype(F32)], axis=0).reshape(
                 DECAY_RANK + ICLR_RANK, n_hp, 1, LANES)], axis=2).reshape(
                     DECAY_RANK + ICLR_RANK, n_hp * 2 * LANES)
        col = lambda off: (lambda b, h: (b, 0, off // LANES + h))
        seq_spec = lambda off: pl.BlockSpec((1, t_len, LANES), col(off))
        n_pairs = bsz * n_hp
        g_rows = RWKV_GROUP * CHUNK
        cur = lambda s: jnp.minimum(s, n_pairs - 1)
        nxt = lambda s: jnp.minimum(s + 1, n_pairs - 1)
        lag = lambda s: jnp.maximum(s - 1, 0)

        def pair_specs(pick, n_rows):
            seq = lambda off: pl.BlockSpec(
                (1, n_rows, LANES), lambda s: (pick(s) // n_hp, 0, off // LANES + pick(s) % n_hp))
            shared = pl.BlockSpec((1, n_rows, LANES),
                                  lambda s: (pick(s) // n_hp, 0, 3 * c_a // LANES))
            par = pl.BlockSpec((1, LANES), lambda s: (0, pick(s) % n_hp))
            up = pl.BlockSpec((DECAY_RANK + ICLR_RANK, 2 * LANES), lambda s: (0, pick(s) % n_hp))
            return seq, shared, par, up

        cur_seq, cur_wa, cur_par, cur_up = pair_specs(cur, t_len)
        nxt_seq, nxt_wa, nxt_par, nxt_up = pair_specs(nxt, g_rows)
        lag_seq, _, lag_par, _ = pair_specs(lag, t_len)
        rwkv_params = (up_full, row2(decay_base[layer]), row2(iclr_base[layer]),
                       row2(key_norm_scale[layer]), row2(key_iclr_mix[layer]), row2(bonus[layer]))
        ya = pl.pallas_call(
            _rwkv_kernel,
            grid=(n_pairs + 1,),
            in_specs=[cur_seq(0), cur_seq(c_a), cur_seq(2 * c_a), cur_wa, cur_up] + [cur_par] * 5
                     + [nxt_seq(0), nxt_seq(c_a), nxt_seq(2 * c_a), nxt_wa, nxt_up] + [nxt_par] * 5
                     + [lag_seq(o_za), lag_par, lag_par],
            out_specs=lag_seq(0),
            out_shape=jax.ShapeDtypeStruct((bsz, t_len, c_a), BF16),
            scratch_shapes=_rwkv_scratch(t_len),
            compiler_params=pltpu.CompilerParams(dimension_semantics=("arbitrary",),
                                                 vmem_limit_bytes=VMEM_LIMIT),
            name="rwkv",
        )(p, p, p, p, *rwkv_params, p, p, p, p, *rwkv_params, p,
          row2(gn_gain[layer]), row2(gn_bias[layer]))

        yb = pl.pallas_call(
            _attn_kernel,
            grid=(bsz, c_b // LANES),
            in_specs=[seq_spec(o_q), seq_spec(o_k), seq_spec(o_v), seq_spec(o_zb)],
            out_specs=pl.BlockSpec((1, t_len, LANES), lambda b, h: (b, 0, h)),
            out_shape=jax.ShapeDtypeStruct((bsz, t_len, c_b), BF16),
            scratch_shapes=[pltpu.VMEM((t_len, LANES), BF16),
                            pltpu.VMEM((t_len, LANES), BF16),
                            pltpu.VMEM((len(DILATION_PATTERNS) - 1, t_len, LANES), F32),
                            pltpu.VMEM((len(DILATION_PATTERNS) - 1, t_len, LANES), F32),
                            pltpu.VMEM((len(DILATION_PATTERNS) - 1, t_len, LANES), F32)],
            compiler_params=pltpu.CompilerParams(dimension_semantics=("arbitrary", "arbitrary"),
                                                 vmem_limit_bytes=VMEM_LIMIT),
            name="attention",
        )(p, p, p, p)

        assert depth == 1
        tmo = OUT_PROJ_ROWS
        wo = w_out[layer].astype(BF16)
        x = pl.pallas_call(
            _out_proj_kernel,
            grid=(rows // tmo,),
            in_specs=[pl.BlockSpec((tmo, c_a), lambda i: (i, 0)),
                      pl.BlockSpec((tmo, c_b), lambda i: (i, 0)),
                      pl.BlockSpec((c_a, d_model), lambda i: (0, 0)),
                      pl.BlockSpec((c_b, d_model), lambda i: (0, 0)),
                      pl.BlockSpec((tmo, d_model), lambda i: (i, 0)),
                      pl.BlockSpec((1, d_model), lambda i: (0, 0))],
            out_specs=pl.BlockSpec((tmo, d_model), lambda i: (i, 0)),
            out_shape=jax.ShapeDtypeStruct((rows, d_model), F32),
            compiler_params=pltpu.CompilerParams(dimension_semantics=("arbitrary",),
                                                 vmem_limit_bytes=VMEM_LIMIT),
            name="out_proj",
        )(ya.reshape(rows, c_a), yb.reshape(rows, c_b), wo[:c_a], wo[c_a:],
          x.reshape(rows, d_model), row2(final_gain)).reshape(bsz, t_len, d_model)
    return x
```
